```python
import math
import numpy as np
import jax
import jax.numpy as jnp
from jax import lax

D_MODEL = 4096
BATCH = 8
SEQ = 2048
DEPTH = 4

HEAD_DIM = 128
N_GROUPS = 4
HEADS_PER_GROUP = D_MODEL // HEAD_DIM // N_GROUPS
GROUP_WIDTH = HEADS_PER_GROUP * HEAD_DIM
MIX_WIDTH = N_GROUPS * GROUP_WIDTH
D_FF = ((8 * D_MODEL + 3 * 256 - 1) // (3 * 256)) * 256
PLE_DIM = 256
ROPE_THETA = 10000.0
EPS = 1e-6
Q_BLOCK = 128
NEG_INF = -1e30
POS_INF = 1e30

DIFF_QK_DIM = HEAD_DIM // 2
DIFF_V_DIM = HEAD_DIM
IDX_HEADS = 16
IDX_DIM = 64
DSA_TOPK = 256
MLA_Q_RANK = 1024
MLA_KV_RANK = 512
MLA_NOPE = 128
MLA_ROPE = 64
MLA_V = HEAD_DIM
NSA_KV_HEADS = 2
NSA_Q_PER_KV = HEADS_PER_GROUP // NSA_KV_HEADS
CMP_LEN = 32
CMP_STRIDE = 16
CMP_HIDDEN = HEAD_DIM
SEL_BLOCK = 64
SEL_TOPN = 16
SEL_INIT = 1
SEL_LOCAL = 2
SEL_Q_BLOCK = 32
WINDOW = 512

IN_SIZES = (
    HEADS_PER_GROUP * 2 * DIFF_QK_DIM, HEADS_PER_GROUP * 2 * DIFF_QK_DIM, HEADS_PER_GROUP * DIFF_V_DIM,
    GROUP_WIDTH, HEAD_DIM, HEAD_DIM, IDX_HEADS * IDX_DIM, IDX_DIM, IDX_HEADS,
    MLA_Q_RANK, MLA_KV_RANK, MLA_ROPE,
    GROUP_WIDTH, 3 * 2 * NSA_KV_HEADS * HEAD_DIM, 3 * HEADS_PER_GROUP,
)
IN_WIDTH = sum(IN_SIZES)

kernel_name = 'hybrid_parallel_diff_dsa_mla_nsa_block'


def rmsnorm(x, g):
    x32 = x.astype(jnp.float32)
    y = x32 * lax.rsqrt(jnp.mean(x32 * x32, axis=-1, keepdims=True) + EPS)
    return (y * g.astype(jnp.float32)).astype(x.dtype)


def rope_tables(positions, dim):
    inv_freq = ROPE_THETA ** (-jnp.arange(0, dim, 2, dtype=jnp.float32) / dim)
    ang = positions.astype(jnp.float32)[..., None] * inv_freq
    return jnp.cos(ang), jnp.sin(ang)


def apply_rope(x, cos, sin):
    half = x.shape[-1] // 2
    x32 = x.astype(jnp.float32)
    x1, x2 = x32[..., :half], x32[..., half:]
    return jnp.concatenate([x1 * cos - x2 * sin, x2 * cos + x1 * sin], axis=-1).astype(x.dtype)


def masked_softmax(s, mask):
    s = jnp.where(mask, s.astype(jnp.float32), NEG_INF)
    return jnp.where(mask, jax.nn.softmax(s, axis=-1), 0.0)


def merge_blocks(o):
    o = jnp.moveaxis(o, 0, 1)
    return o.reshape((o.shape[0], o.shape[1] * o.shape[2]) + o.shape[3:])


def dense_causal_attention(q, k, v, scale):
    S = q.shape[1]
    kpos = jnp.arange(S)

    def block(i):
        start = i * Q_BLOCK
        qb = lax.dynamic_slice_in_dim(q, start, Q_BLOCK, axis=1)
        sc = jnp.einsum('bqhd,bkhd->bhqk', qb, k) * scale
        mask = kpos[None, :] <= (start + jnp.arange(Q_BLOCK))[:, None]
        a = masked_softmax(sc, mask)
        return jnp.einsum('bhqk,bkhd->bqhd', a.astype(v.dtype), v)

    return merge_blocks(lax.map(block, jnp.arange(S // Q_BLOCK)))


def diff_attention(q, k, v, cos, sin, g_q, g_k, g_sub, lam_params, layer_idx):
    B, S = q.shape[:2]
    c, s_ = cos[:, :, None, None, :], sin[:, :, None, None, :]
    q = apply_rope(rmsnorm(q, g_q), c, s_)
    k = apply_rope(rmsnorm(k, g_k), c, s_)
    lam_init = 0.8 - 0.6 * math.exp(-0.3 * layer_idx)
    lp = lam_params.astype(jnp.float32)
    lam = jnp.exp(jnp.sum(lp[0] * lp[1])) - jnp.exp(jnp.sum(lp[2] * lp[3])) + lam_init
    scale = DIFF_QK_DIM ** -0.5
    kpos = jnp.arange(S)

    def block(i):
        start = i * Q_BLOCK
        qb = lax.dynamic_slice_in_dim(q, start, Q_BLOCK, axis=1)
        sc = jnp.einsum('bqhcd,bkhcd->bhcqk', qb, k) * scale
        mask = kpos[None, :] <= (start + jnp.arange(Q_BLOCK))[:, None]
        a = masked_softmax(sc, mask)
        w = a[:, :, 0] - lam * a[:, :, 1]
        return jnp.einsum('bhqk,bkhd->bqhd', w.astype(v.dtype), v)

    o = merge_blocks(lax.map(block, jnp.arange(S // Q_BLOCK)))
    o = rmsnorm(o, g_sub) * (1.0 - lam_init)
    return o.reshape(B, S, -1)


def dsa_attention(q, k, v, iq, ik, iw, cos, sin, cos_i, sin_i, g_q, g_k):
    B, S = q.shape[:2]
    q = apply_rope(rmsnorm(q, g_q), cos[:, :, None, :], sin[:, :, None, :])
    k = apply_rope(rmsnorm(k, g_k), cos, sin)
    iq = apply_rope(iq, cos_i[:, :, None, :], sin_i[:, :, None, :])
    ik = apply_rope(ik, cos_i, sin_i)
    n_keep = min(DSA_TOPK, S // 4)
    scale = HEAD_DIM ** -0.5
    kpos = jnp.arange(S)
    gather_keys = jax.vmap(lambda t, ix: t[ix])

    def block(i):
        start = i * Q_BLOCK
        qpos = start + jnp.arange(Q_BLOCK)
        qb = lax.dynamic_slice_in_dim(q, start, Q_BLOCK, axis=1)
        iqb = lax.dynamic_slice_in_dim(iq, start, Q_BLOCK, axis=1)
        iwb = lax.dynamic_slice_in_dim(iw, start, Q_BLOCK, axis=1)
        rel = jax.nn.relu(jnp.einsum('bqhd,bkd->bqhk', iqb, ik).astype(jnp.float32) * IDX_DIM ** -0.5)
        score = jnp.einsum('bqh,bqhk->bqk', iwb.astype(jnp.float32) * IDX_HEADS ** -0.5, rel)
        score = jnp.where(kpos[None, None, :] <= qpos[None, :, None], score, NEG_INF)
        _, sel = lax.top_k(score, n_keep)
        ks = gather_keys(k, sel)
        vs = gather_keys(v, sel)
        sc = jnp.einsum('bqhd,bqnd->bhqn', qb, ks) * scale
        a = masked_softmax(sc, (sel <= qpos[None, :, None])[:, None])
        return jnp.einsum('bhqn,bqnd->bqhd', a.astype(vs.dtype), vs)

    o = merge_blocks(lax.map(block, jnp.arange(S // Q_BLOCK)))
    return o.reshape(B, S, -1)


def mla_attention(c_q, c_kv, k_rope, cos, sin, g_cq, g_ckv, w_uq, w_uk, w_uv, g_q, g_k):
    B, S = c_q.shape[:2]
    H = HEADS_PER_GROUP
    q = (rmsnorm(c_q, g_cq) @ w_uq).reshape(B, S, H, MLA_NOPE + MLA_ROPE)
    ckv = rmsnorm(c_kv, g_ckv)
    k_nope = (ckv @ w_uk).reshape(B, S, H, MLA_NOPE)
    v = (ckv @ w_uv).reshape(B, S, H, MLA_V)
    k_r = jnp.broadcast_to(k_rope[:, :, None, :], (B, S, H, MLA_ROPE))
    k = jnp.concatenate([k_nope, k_r], axis=-1)
    q = rmsnorm(q, g_q)
    k = rmsnorm(k, g_k)
    c, s_ = cos[:, :, None, :], sin[:, :, None, :]
    q = jnp.concatenate([q[..., :MLA_NOPE], apply_rope(q[..., MLA_NOPE:], c, s_)], axis=-1)
    k = jnp.concatenate([k[..., :MLA_NOPE], apply_rope(k[..., MLA_NOPE:], c, s_)], axis=-1)
    o = dense_causal_attention(q, k, v, (MLA_NOPE + MLA_ROPE) ** -0.5)
    return o.reshape(B, S, -1)


def nsa_attention(q, kv, gate_logits, cos, sin, positions, g_q, g_k, cmp_w1, cmp_w2, cmp_pe):
    B, S = q.shape[:2]
    G, R, d = NSA_KV_HEADS, NSA_Q_PER_KV, HEAD_DIM
    scale = d ** -0.5
    c, s_ = cos[:, :, None, :], sin[:, :, None, :]
    qg = apply_rope(rmsnorm(q, g_q), c, s_).reshape(B, S, G, R, d)
    tpos = np.arange(S)

    n_cmp = (S - CMP_LEN) // CMP_STRIDE + 1
    starts = np.arange(n_cmp) * CMP_STRIDE
    ends = starts + CMP_LEN - 1
    blocks = kv[:, :, 0][:, starts[:, None] + np.arange(CMP_LEN)[None, :]]
    blocks = blocks + jnp.transpose(cmp_pe, (1, 0, 2))[:, :, None, :]
    flat = jnp.transpose(blocks, (0, 1, 3, 4, 2, 5)).reshape(B, n_cmp, 2, G, CMP_LEN * d)
    hid = jax.nn.gelu(jnp.einsum('bnkgf,kfh->bnkgh', flat, cmp_w1))
    comp = jnp.einsum('bnkgh,khe->bnkge', hid, cmp_w2)
    cos_c, sin_c = rope_tables(positions[:, ends], d)
    k_cmp = apply_rope(rmsnorm(comp[:, :, 0], g_k[0]), cos_c[:, :, None, :], sin_c[:, :, None, :])
    v_cmp = comp[:, :, 1]
    sc = jnp.einsum('bsgrd,bjgd->bgrsj', qg, k_cmp) * scale
    p_cmp = masked_softmax(sc, ends[None, :] <= tpos[:, None])
    o_cmp = jnp.einsum('bgrsj,bjgd->bsgrd', p_cmp.astype(v_cmp.dtype), v_cmp)

    n_sel = S // SEL_BLOCK
    n_top = min(SEL_TOPN, n_sel)
    blk = np.arange(n_sel)
    sel_start = blk * SEL_BLOCK
    cmp_to_sel = ((starts[:, None] < sel_start[None, :] + SEL_BLOCK)
                  & (starts[:, None] + CMP_LEN > sel_start[None, :])).astype(np.float32)
    imp = jnp.einsum('bgrsj,jn->bgsn', p_cmp, jnp.asarray(cmp_to_sel))
    dist = (tpos // SEL_BLOCK)[:, None] - blk[None, :]
    forced = (blk[None, :] < SEL_INIT) | ((dist >= 0) & (dist < SEL_LOCAL))
    admissible = sel_start[None, :] <= tpos[:, None]
    imp = jnp.where(forced, POS_INF, jnp.where(admissible, imp, NEG_INF))
    _, sel = lax.top_k(imp, n_top)
    k_slc = apply_rope(rmsnorm(kv[:, :, 1, 0], g_k[1]), c, s_)
    kb = k_slc.reshape(B, n_sel, SEL_BLOCK, G, d).transpose(0, 3, 1, 2, 4)
    vb = kv[:, :, 1, 1].reshape(B, n_sel, SEL_BLOCK, G, d).transpose(0, 3, 1, 2, 4)
    gather_blocks = jax.vmap(jax.vmap(lambda t, ix: t[ix]))
    q_t = qg.transpose(0, 2, 3, 1, 4)

    def sel_chunk(i):
        start = i * SEL_Q_BLOCK
        qc = lax.dynamic_slice_in_dim(q_t, start, SEL_Q_BLOCK, axis=3)
        ix = lax.dynamic_slice_in_dim(sel, start, SEL_Q_BLOCK, axis=2)
        ks = gather_blocks(kb, ix)
        vs = gather_blocks(vb, ix).reshape(B, G, SEL_Q_BLOCK, n_top * SEL_BLOCK, d)
        qpos = start + jnp.arange(SEL_Q_BLOCK)
        tok = ix[..., None] * SEL_BLOCK + jnp.arange(SEL_BLOCK)
        mask = (tok <= qpos[None, None, :, None, None]).reshape(B, G, 1, SEL_Q_BLOCK, n_top * SEL_BLOCK)
        sc_s = jnp.einsum('bgrqd,bgqntd->bgrqnt', qc, ks).reshape(B, G, R, SEL_Q_BLOCK, n_top * SEL_BLOCK) * scale
        a_s = masked_softmax(sc_s, mask)
        return jnp.einsum('bgrqm,bgqmd->bqgrd', a_s.astype(vs.dtype), vs)

    o_slc = merge_blocks(lax.map(sel_chunk, jnp.arange(S // SEL_Q_BLOCK)))

    n_qb = S // Q_BLOCK
    span = WINDOW + Q_BLOCK
    kidx = np.arange(n_qb)[:, None] * Q_BLOCK + np.arange(span)[None, :]
    pad = ((0, 0), (WINDOW, 0), (0, 0), (0, 0))
    k_win = jnp.pad(apply_rope(rmsnorm(kv[:, :, 2, 0], g_k[2]), c, s_), pad)[:, kidx]
    v_win = jnp.pad(kv[:, :, 2, 1], pad)[:, kidx]
    qb = qg.reshape(B, n_qb, Q_BLOCK, G, R, d)
    sc_w = jnp.einsum('bnqgrd,bnkgd->bngrqk', qb, k_win) * scale
    spos = (kidx - WINDOW)[:, None, :]
    tq = (np.arange(n_qb)[:, None] * Q_BLOCK + np.arange(Q_BLOCK)[None, :])[:, :, None]
    wmask = (spos <= tq) & (spos > tq - WINDOW) & (spos >= 0)
    a_w = masked_softmax(sc_w, wmask[None, :, None, None])
    o_win = jnp.einsum('bngrqk,bnkgd->bnqgrd', a_w.astype(v_win.dtype), v_win).reshape(B, S, G, R, d)

    g = jax.nn.sigmoid(gate_logits.astype(jnp.float32)).reshape(B, S, G, R, 3)
    o = g[..., 0:1] * o_cmp + g[..., 1:2] * o_slc + g[..., 2:3] * o_win
    return o.astype(q.dtype).reshape(B, S, -1)


def setup_inputs(seed: int = 0) -> dict:
    key = jax.random.key(seed)
    keys = iter(jax.random.split(key, 40))
    f32 = jnp.float32
    H = HEADS_PER_GROUP

    def dense(shape, fan_in):
        return jax.random.normal(next(keys), shape, f32) * (fan_in ** -0.5)

    def gain(shape):
        return 1.0 + 0.01 * jax.random.normal(next(keys), shape, f32)

    x = jax.random.normal(next(keys), (BATCH, SEQ, D_MODEL), f32)
    p = jax.random.normal(next(keys), (DEPTH, BATCH, SEQ, PLE_DIM), f32)
    offset = jax.random.randint(next(keys), (BATCH, 1), 0, 4096, dtype=jnp.int32)
    positions = offset + jnp.arange(SEQ, dtype=jnp.int32)[None, :]
    return {
        'x': x,
        'p': p,
        'positions': positions,
        'w_in': dense((DEPTH, D_MODEL, IN_WIDTH), D_MODEL),
        'w_out': dense((DEPTH, MIX_WIDTH, D_MODEL), MIX_WIDTH),
        'g_mix': gain((DEPTH, D_MODEL)),
        'g_ffn': gain((DEPTH, D_MODEL)),
        'w_gate': dense((DEPTH, D_MODEL, D_FF), D_MODEL),
        'w_up': dense((DEPTH, D_MODEL, D_FF), D_MODEL),
        'w_down': dense((DEPTH, D_FF, D_MODEL), D_FF),
        'w_ple_proj': dense((DEPTH, PLE_DIM, D_MODEL), PLE_DIM),
        'w_ple_gate': dense((DEPTH, D_MODEL, D_MODEL), D_MODEL),
        'g_ple': gain((DEPTH, D_MODEL)),
        'g_group_out': gain((DEPTH, 3, GROUP_WIDTH)),
        'diff_g_q': gain((DEPTH, DIFF_QK_DIM)),
        'diff_g_k': gain((DEPTH, DIFF_QK_DIM)),
        'diff_g_sub': gain((DEPTH, DIFF_V_DIM)),
        'diff_lambda': 0.1 * jax.random.normal(next(keys), (DEPTH, 4, DIFF_QK_DIM), f32),
        'dsa_g_q': gain((DEPTH, HEAD_DIM)),
        'dsa_g_k': gain((DEPTH, HEAD_DIM)),
        'mla_g_cq': gain((DEPTH, MLA_Q_RANK)),
        'mla_g_ckv': gain((DEPTH, MLA_KV_RANK)),
        'mla_w_uq': dense((DEPTH, MLA_Q_RANK, H * (MLA_NOPE + MLA_ROPE)), MLA_Q_RANK),
        'mla_w_uk': dense((DEPTH, MLA_KV_RANK, H * MLA_NOPE), MLA_KV_RANK),
        'mla_w_uv': dense((DEPTH, MLA_KV_RANK, H * MLA_V), MLA_KV_RANK),
        'mla_g_q': gain((DEPTH, MLA_NOPE + MLA_ROPE)),
        'mla_g_k': gain((DEPTH, MLA_NOPE + MLA_ROPE)),
        'nsa_g_q': gain((DEPTH, HEAD_DIM)),
        'nsa_g_k': gain((DEPTH, 3, HEAD_DIM)),
        'nsa_cmp_w1': dense((DEPTH, 2, CMP_LEN * HEAD_DIM, CMP_HIDDEN), CMP_LEN * HEAD_DIM),
        'nsa_cmp_w2': dense((DEPTH, 2, CMP_HIDDEN, HEAD_DIM), CMP_HIDDEN),
        'nsa_cmp_pe': 0.02 * jax.random.normal(next(keys), (DEPTH, 2, CMP_LEN, HEAD_DIM), f32),
    }


def reference(x, p, positions, w_in, w_out, g_mix, g_ffn, w_gate, w_up, w_down, w_ple_proj, w_ple_gate, g_ple,
              g_group_out, diff_g_q, diff_g_k, diff_g_sub, diff_lambda, dsa_g_q, dsa_g_k, mla_g_cq, mla_g_ckv,
              mla_w_uq, mla_w_uk, mla_w_uv, mla_g_q, mla_g_k, nsa_g_q, nsa_g_k, nsa_cmp_w1, nsa_cmp_w2, nsa_cmp_pe):
    B, S = x.shape[:2]
    H = HEADS_PER_GROUP
    rope_diff = rope_tables(positions, DIFF_QK_DIM)
    rope_head = rope_tables(positions, HEAD_DIM)
    rope_idx = rope_tables(positions, IDX_DIM)
    rope_mla = rope_tables(positions, MLA_ROPE)
    split_at = [int(o) for o in np.cumsum(IN_SIZES)[:-1]]
    h = x
    for i in range(DEPTH):
        u = rmsnorm(h, g_mix[i])
        (a_q, a_k, a_v, b_q, b_k, b_v, b_iq, b_ik, b_iw,
         c_q, c_kv, c_kr, d_q, d_kv, d_g) = jnp.split(u @ w_in[i], split_at, axis=-1)
        o_a = diff_attention(a_q.reshape(B, S, H, 2, DIFF_QK_DIM), a_k.reshape(B, S, H, 2, DIFF_QK_DIM),
                             a_v.reshape(B, S, H, DIFF_V_DIM), rope_diff[0], rope_diff[1],
                             diff_g_q[i], diff_g_k[i], diff_g_sub[i], diff_lambda[i], i)
        o_b = dsa_attention(b_q.reshape(B, S, H, HEAD_DIM), b_k, b_v, b_iq.reshape(B, S, IDX_HEADS, IDX_DIM),
                            b_ik, b_iw, rope_head[0], rope_head[1], rope_idx[0], rope_idx[1],
                            dsa_g_q[i], dsa_g_k[i])
        o_c = mla_attention(c_q, c_kv, c_kr, rope_mla[0], rope_mla[1], mla_g_cq[i], mla_g_ckv[i],
                            mla_w_uq[i], mla_w_uk[i], mla_w_uv[i], mla_g_q[i], mla_g_k[i])
        o_d = nsa_attention(d_q.reshape(B, S, H, HEAD_DIM), d_kv.reshape(B, S, 3, 2, NSA_KV_HEADS, HEAD_DIM),
                            d_g.reshape(B, S, H, 3), rope_head[0], rope_head[1], positions,
                            nsa_g_q[i], nsa_g_k[i], nsa_cmp_w1[i], nsa_cmp_w2[i], nsa_cmp_pe[i])
        mixed = jnp.concatenate([o_a,
                                 rmsnorm(o_b, g_group_out[i, 0]),
                                 rmsnorm(o_c, g_group_out[i, 1]),
                                 rmsnorm(o_d, g_group_out[i, 2])], axis=-1)
        h = h + mixed @ w_out[i]
        u = rmsnorm(h, g_ffn[i])
        h = h + (jax.nn.silu(u @ w_gate[i]) * (u @ w_up[i])) @ w_down[i]
        gate = jax.nn.sigmoid(rmsnorm(h, g_ple[i]) @ w_ple_gate[i])
        h = h + gate * (p[i] @ w_ple_proj[i])
    return h
```

```python
import functools
import math

import numpy as np
import jax
import jax.numpy as jnp
from jax import lax
from jax.experimental import pallas as pl
from jax.experimental.pallas import tpu as pltpu

D_MODEL = 4096
DEPTH = 4
HEAD_DIM = 128
N_GROUPS = 4
HEADS_PER_GROUP = D_MODEL // HEAD_DIM // N_GROUPS
GROUP_WIDTH = HEADS_PER_GROUP * HEAD_DIM
MIX_WIDTH = N_GROUPS * GROUP_WIDTH
D_FF = ((8 * D_MODEL + 3 * 256 - 1) // (3 * 256)) * 256
PLE_DIM = 256
ROPE_THETA = 10000.0
EPS = 1e-6
Q_BLOCK = 128
NEG_INF = -1e30
POS_INF = 1e30

DIFF_QK_DIM = HEAD_DIM // 2
DIFF_V_DIM = HEAD_DIM
IDX_HEADS = 16
IDX_DIM = 64
DSA_TOPK = 256
MLA_Q_RANK = 1024
MLA_KV_RANK = 512
MLA_NOPE = 128
MLA_ROPE = 64
MLA_V = HEAD_DIM
NSA_KV_HEADS = 2
NSA_Q_PER_KV = HEADS_PER_GROUP // NSA_KV_HEADS
CMP_LEN = 32
CMP_STRIDE = 16
CMP_HIDDEN = HEAD_DIM
SEL_BLOCK = 64
SEL_TOPN = 16
SEL_INIT = 1
SEL_LOCAL = 2
SEL_Q_BLOCK = 32
WINDOW = 512

IN_SIZES = (
    HEADS_PER_GROUP * 2 * DIFF_QK_DIM, HEADS_PER_GROUP * 2 * DIFF_QK_DIM, HEADS_PER_GROUP * DIFF_V_DIM,
    GROUP_WIDTH, HEAD_DIM, HEAD_DIM, IDX_HEADS * IDX_DIM, IDX_DIM, IDX_HEADS,
    MLA_Q_RANK, MLA_KV_RANK, MLA_ROPE,
    GROUP_WIDTH, 3 * 2 * NSA_KV_HEADS * HEAD_DIM, 3 * HEADS_PER_GROUP,
)
IN_WIDTH = sum(IN_SIZES)

V7X_VMEM_LIMIT_BYTES = 56 * 1024 * 1024
LANE = 128


def _round_up(n, m):
    return (n + m - 1) // m * m


def _compiler_params(semantics):
    return pltpu.CompilerParams(dimension_semantics=semantics, vmem_limit_bytes=V7X_VMEM_LIMIT_BYTES)


def _rmsnorm_rows_kernel(x_ref, g_ref, o_ref):
    x = x_ref[...]
    ms = jnp.mean(x * x, axis=-1, keepdims=True)
    o_ref[...] = (x * lax.rsqrt(ms + EPS) * g_ref[...]).astype(o_ref.dtype)


def _rmsnorm_rows(x, g, *, tm=256, out_dtype=jnp.bfloat16):
    m, d = x.shape
    return pl.pallas_call(
        _rmsnorm_rows_kernel,
        grid=(m // tm,),
        in_specs=[pl.BlockSpec((tm, d), lambda i: (i, 0)), pl.BlockSpec((1, d), lambda i: (0, 0))],
        out_specs=pl.BlockSpec((tm, d), lambda i: (i, 0)),
        out_shape=jax.ShapeDtypeStruct((m, d), out_dtype),
        compiler_params=_compiler_params(("parallel",)),
        name="rmsnorm_rows",
    )(x, g.reshape(1, d))


def _mm_kernel(a_ref, w_ref, o_ref, acc_ref, *, nk):
    k = pl.program_id(2)

    @pl.when(k == 0)
    def _():
        acc_ref[...] = jnp.zeros_like(acc_ref)

    acc_ref[...] += jnp.dot(a_ref[...], w_ref[...], preferred_element_type=jnp.float32)

    @pl.when(k == nk - 1)
    def _():
        o_ref[...] = acc_ref[...].astype(o_ref.dtype)


def _mm_res_kernel(a_ref, w_ref, r_ref, o_ref, acc_ref, *, nk):
    k = pl.program_id(2)

    @pl.when(k == 0)
    def _():
        acc_ref[...] = jnp.zeros_like(acc_ref)

    acc_ref[...] += jnp.dot(a_ref[...], w_ref[...], preferred_element_type=jnp.float32)

    @pl.when(k == nk - 1)
    def _():
        o_ref[...] = (r_ref[...] + acc_ref[...]).astype(o_ref.dtype)


def _matmul(a, w, *, res=None, tm, tn, tk, out_dtype=jnp.float32):
    m, kdim = a.shape
    n = w.shape[1]
    nk = kdim // tk
    assert m % tm == 0 and n % tn == 0 and kdim % tk == 0
    in_specs = [pl.BlockSpec((tm, tk), lambda i, j, k: (i, k)), pl.BlockSpec((tk, tn), lambda i, j, k: (k, j))]
    args = [a, w]
    if res is None:
        body = functools.partial(_mm_kernel, nk=nk)
    else:
        body = functools.partial(_mm_res_kernel, nk=nk)
        in_specs.append(pl.BlockSpec((tm, tn), lambda i, j, k: (i, j)))
        args.append(res)
    return pl.pallas_call(
        body,
        grid=(m // tm, n // tn, nk),
        in_specs=in_specs,
        out_specs=pl.BlockSpec((tm, tn), lambda i, j, k: (i, j)),
        out_shape=jax.ShapeDtypeStruct((m, n), out_dtype),
        scratch_shapes=[pltpu.VMEM((tm, tn), jnp.float32)],
        compiler_params=_compiler_params(("parallel", "parallel", "arbitrary")),
        name="matmul_res" if res is not None else "matmul",
    )(*args)


def _swiglu_kernel(a_ref, wg_ref, wu_ref, o_ref):
    a = a_ref[...]
    g = jnp.dot(a, wg_ref[...], preferred_element_type=jnp.float32)
    u = jnp.dot(a, wu_ref[...], preferred_element_type=jnp.float32)
    o_ref[...] = (g * jax.nn.sigmoid(g) * u).astype(o_ref.dtype)


def _swiglu(a, wg, wu, *, tm, tn):
    m, kdim = a.shape
    n = wg.shape[1]
    return pl.pallas_call(
        _swiglu_kernel,
        grid=(m // tm, n // tn),
        in_specs=[pl.BlockSpec((tm, kdim), lambda i, j: (i, 0)),
                  pl.BlockSpec((kdim, tn), lambda i, j: (0, j)),
                  pl.BlockSpec((kdim, tn), lambda i, j: (0, j))],
        out_specs=pl.BlockSpec((tm, tn), lambda i, j: (i, j)),
        out_shape=jax.ShapeDtypeStruct((m, n), jnp.bfloat16),
        compiler_params=_compiler_params(("parallel", "parallel")),
        name="swiglu",
    )(a, wg, wu)


def _ple_kernel(a_ref, wg_ref, p_ref, wp_ref, r_ref, o_ref):
    gate = jax.nn.sigmoid(jnp.dot(a_ref[...], wg_ref[...], preferred_element_type=jnp.float32))
    emb = jnp.dot(p_ref[...], wp_ref[...], preferred_element_type=jnp.float32)
    o_ref[...] = r_ref[...] + gate * emb


def _ple(a, wg, p, wp, res, *, tm, tn):
    m, kdim = a.shape
    n = wg.shape[1]
    pdim = p.shape[1]
    return pl.pallas_call(
        _ple_kernel,
        grid=(m // tm, n // tn),
        in_specs=[pl.BlockSpec((tm, kdim), lambda i, j: (i, 0)),
                  pl.BlockSpec((kdim, tn), lambda i, j: (0, j)),
                  pl.BlockSpec((tm, pdim), lambda i, j: (i, 0)),
                  pl.BlockSpec((pdim, tn), lambda i, j: (0, j)),
                  pl.BlockSpec((tm, tn), lambda i, j: (i, j))],
        out_specs=pl.BlockSpec((tm, tn), lambda i, j: (i, j)),
        out_shape=jax.ShapeDtypeStruct((m, n), jnp.float32),
        compiler_params=_compiler_params(("parallel", "parallel")),
        name="ple",
    )(a, wg, p, wp, res)


def _rmsnorm(x, g):
    x32 = x.astype(jnp.float32)
    y = x32 * lax.rsqrt(jnp.mean(x32 * x32, axis=-1, keepdims=True) + EPS)
    return (y * g.astype(jnp.float32)).astype(x.dtype)


def _rope_tables(positions, dim):
    inv_freq = ROPE_THETA ** (-jnp.arange(0, dim, 2, dtype=jnp.float32) / dim)
    ang = positions.astype(jnp.float32)[..., None] * inv_freq
    return jnp.cos(ang), jnp.sin(ang)


def _apply_rope(x, cos, sin):
    half = x.shape[-1] // 2
    x32 = x.astype(jnp.float32)
    x1, x2 = x32[..., :half], x32[..., half:]
    return jnp.concatenate([x1 * cos - x2 * sin, x2 * cos + x1 * sin], axis=-1).astype(x.dtype)


def _masked_softmax(s, mask):
    s = jnp.where(mask, s.astype(jnp.float32), NEG_INF)
    return jnp.where(mask, jax.nn.softmax(s, axis=-1), 0.0)


def _merge_blocks(o):
    o = jnp.moveaxis(o, 0, 1)
    return o.reshape((o.shape[0], o.shape[1] * o.shape[2]) + o.shape[3:])


def _dense_causal_attention(q, k, v, scale):
    S = q.shape[1]
    kpos = jnp.arange(S)

    def block(i):
        start = i * Q_BLOCK
        qb = lax.dynamic_slice_in_dim(q, start, Q_BLOCK, axis=1)
        sc = jnp.einsum('bqhd,bkhd->bhqk', qb, k) * scale
        mask = kpos[None, :] <= (start + jnp.arange(Q_BLOCK))[:, None]
        a = _masked_softmax(sc, mask)
        return jnp.einsum('bhqk,bkhd->bqhd', a.astype(v.dtype), v)

    return _merge_blocks(lax.map(block, jnp.arange(S // Q_BLOCK)))


def _diff_attention(q, k, v, cos, sin, g_q, g_k, g_sub, lam_params, layer_idx):
    B, S = q.shape[:2]
    c, s_ = cos[:, :, None, None, :], sin[:, :, None, None, :]
    q = _apply_rope(_rmsnorm(q, g_q), c, s_)
    k = _apply_rope(_rmsnorm(k, g_k), c, s_)
    lam_init = 0.8 - 0.6 * math.exp(-0.3 * layer_idx)
    lp = lam_params.astype(jnp.float32)
    lam = jnp.exp(jnp.sum(lp[0] * lp[1])) - jnp.exp(jnp.sum(lp[2] * lp[3])) + lam_init
    scale = DIFF_QK_DIM ** -0.5
    kpos = jnp.arange(S)

    def block(i):
        start = i * Q_BLOCK
        qb = lax.dynamic_slice_in_dim(q, start, Q_BLOCK, axis=1)
        sc = jnp.einsum('bqhcd,bkhcd->bhcqk', qb, k) * scale
        mask = kpos[None, :] <= (start + jnp.arange(Q_BLOCK))[:, None]
        a = _masked_softmax(sc, mask)
        w = a[:, :, 0] - lam * a[:, :, 1]
        return jnp.einsum('bhqk,bkhd->bqhd', w.astype(v.dtype), v)

    o = _merge_blocks(lax.map(block, jnp.arange(S // Q_BLOCK)))
    o = _rmsnorm(o, g_sub) * (1.0 - lam_init)
    return o.reshape(B, S, -1)


def _dsa_attention(q, k, v, iq, ik, iw, cos, sin, cos_i, sin_i, g_q, g_k):
    B, S = q.shape[:2]
    q = _apply_rope(_rmsnorm(q, g_q), cos[:, :, None, :], sin[:, :, None, :])
    k = _apply_rope(_rmsnorm(k, g_k), cos, sin)
    iq = _apply_rope(iq, cos_i[:, :, None, :], sin_i[:, :, None, :])
    ik = _apply_rope(ik, cos_i, sin_i)
    n_keep = min(DSA_TOPK, S // 4)
    scale = HEAD_DIM ** -0.5
    kpos = jnp.arange(S)
    gather_keys = jax.vmap(lambda t, ix: t[ix])

    def block(i):
        start = i * Q_BLOCK
        qpos = start + jnp.arange(Q_BLOCK)
        qb = lax.dynamic_slice_in_dim(q, start, Q_BLOCK, axis=1)
        iqb = lax.dynamic_slice_in_dim(iq, start, Q_BLOCK, axis=1)
        iwb = lax.dynamic_slice_in_dim(iw, start, Q_BLOCK, axis=1)
        rel = jax.nn.relu(jnp.einsum('bqhd,bkd->bqhk', iqb, ik).astype(jnp.float32) * IDX_DIM ** -0.5)
        score = jnp.einsum('bqh,bqhk->bqk', iwb.astype(jnp.float32) * IDX_HEADS ** -0.5, rel)
        score = jnp.where(kpos[None, None, :] <= qpos[None, :, None], score, NEG_INF)
        _, sel = lax.top_k(score, n_keep)
        ks = gather_keys(k, sel)
        vs = gather_keys(v, sel)
        sc = jnp.einsum('bqhd,bqnd->bhqn', qb, ks) * scale
        a = _masked_softmax(sc, (sel <= qpos[None, :, None])[:, None])
        return jnp.einsum('bhqn,bqnd->bqhd', a.astype(vs.dtype), vs)

    o = _merge_blocks(lax.map(block, jnp.arange(S // Q_BLOCK)))
    return o.reshape(B, S, -1)


def _mla_attention(c_q, c_kv, k_rope, cos, sin, g_cq, g_ckv, w_uq, w_uk, w_uv, g_q, g_k):
    B, S = c_q.shape[:2]
    H = HEADS_PER_GROUP
    q = (_rmsnorm(c_q, g_cq) @ w_uq).reshape(B, S, H, MLA_NOPE + MLA_ROPE)
    ckv = _rmsnorm(c_kv, g_ckv)
    k_nope = (ckv @ w_uk).reshape(B, S, H, MLA_NOPE)
    v = (ckv @ w_uv).reshape(B, S, H, MLA_V)
    k_r = jnp.broadcast_to(k_rope[:, :, None, :], (B, S, H, MLA_ROPE))
    k = jnp.concatenate([k_nope, k_r], axis=-1)
    q = _rmsnorm(q, g_q)
    k = _rmsnorm(k, g_k)
    c, s_ = cos[:, :, None, :], sin[:, :, None, :]
    q = jnp.concatenate([q[..., :MLA_NOPE], _apply_rope(q[..., MLA_NOPE:], c, s_)], axis=-1)
    k = jnp.concatenate([k[..., :MLA_NOPE], _apply_rope(k[..., MLA_NOPE:], c, s_)], axis=-1)
    o = _dense_causal_attention(q, k, v, (MLA_NOPE + MLA_ROPE) ** -0.5)
    return o.reshape(B, S, -1)


def _nsa_attention(q, kv, gate_logits, cos, sin, positions, g_q, g_k, cmp_w1, cmp_w2, cmp_pe):
    B, S = q.shape[:2]
    G, R, d = NSA_KV_HEADS, NSA_Q_PER_KV, HEAD_DIM
    scale = d ** -0.5
    c, s_ = cos[:, :, None, :], sin[:, :, None, :]
    qg = _apply_rope(_rmsnorm(q, g_q), c, s_).reshape(B, S, G, R, d)
    tpos = np.arange(S)

    n_cmp = (S - CMP_LEN) // CMP_STRIDE + 1
    starts = np.arange(n_cmp) * CMP_STRIDE
    ends = starts + CMP_LEN - 1
    blocks = kv[:, :, 0][:, starts[:, None] + np.arange(CMP_LEN)[None, :]]
    blocks = blocks + jnp.transpose(cmp_pe, (1, 0, 2))[:, :, None, :]
    flat = jnp.transpose(blocks, (0, 1, 3, 4, 2, 5)).reshape(B, n_cmp, 2, G, CMP_LEN * d)
    hid = jax.nn.gelu(jnp.einsum('bnkgf,kfh->bnkgh', flat, cmp_w1))
    comp = jnp.einsum('bnkgh,khe->bnkge', hid, cmp_w2)
    cos_c, sin_c = _rope_tables(positions[:, ends], d)
    k_cmp = _apply_rope(_rmsnorm(comp[:, :, 0], g_k[0]), cos_c[:, :, None, :], sin_c[:, :, None, :])
    v_cmp = comp[:, :, 1]
    sc = jnp.einsum('bsgrd,bjgd->bgrsj', qg, k_cmp) * scale
    p_cmp = _masked_softmax(sc, ends[None, :] <= tpos[:, None])
    o_cmp = jnp.einsum('bgrsj,bjgd->bsgrd', p_cmp.astype(v_cmp.dtype), v_cmp)

    n_sel = S // SEL_BLOCK
    n_top = min(SEL_TOPN, n_sel)
    blk = np.arange(n_sel)
    sel_start = blk * SEL_BLOCK
    cmp_to_sel = ((starts[:, None] < sel_start[None, :] + SEL_BLOCK)
                  & (starts[:, None] + CMP_LEN > sel_start[None, :])).astype(np.float32)
    imp = jnp.einsum('bgrsj,jn->bgsn', p_cmp, jnp.asarray(cmp_to_sel))
    dist = (tpos // SEL_BLOCK)[:, None] - blk[None, :]
    forced = (blk[None, :] < SEL_INIT) | ((dist >= 0) & (dist < SEL_LOCAL))
    admissible = sel_start[None, :] <= tpos[:, None]
    imp = jnp.where(forced, POS_INF, jnp.where(admissible, imp, NEG_INF))
    _, sel = lax.top_k(imp, n_top)
    k_slc = _apply_rope(_rmsnorm(kv[:, :, 1, 0], g_k[1]), c, s_)
    kb = k_slc.reshape(B, n_sel, SEL_BLOCK, G, d).transpose(0, 3, 1, 2, 4)
    vb = kv[:, :, 1, 1].reshape(B, n_sel, SEL_BLOCK, G, d).transpose(0, 3, 1, 2, 4)
    gather_blocks = jax.vmap(jax.vmap(lambda t, ix: t[ix]))
    q_t = qg.transpose(0, 2, 3, 1, 4)

    def sel_chunk(i):
        start = i * SEL_Q_BLOCK
        qc = lax.dynamic_slice_in_dim(q_t, start, SEL_Q_BLOCK, axis=3)
        ix = lax.dynamic_slice_in_dim(sel, start, SEL_Q_BLOCK, axis=2)
        ks = gather_blocks(kb, ix)
        vs = gather_blocks(vb, ix).reshape(B, G, SEL_Q_BLOCK, n_top * SEL_BLOCK, d)
        qpos = start + jnp.arange(SEL_Q_BLOCK)
        tok = ix[..., None] * SEL_BLOCK + jnp.arange(SEL_BLOCK)
        mask = (tok <= qpos[None, None, :, None, None]).reshape(B, G, 1, SEL_Q_BLOCK, n_top * SEL_BLOCK)
        sc_s = jnp.einsum('bgrqd,bgqntd->bgrqnt', qc, ks).reshape(B, G, R, SEL_Q_BLOCK, n_top * SEL_BLOCK) * scale
        a_s = _masked_softmax(sc_s, mask)
        return jnp.einsum('bgrqm,bgqmd->bqgrd', a_s.astype(vs.dtype), vs)

    o_slc = _merge_blocks(lax.map(sel_chunk, jnp.arange(S // SEL_Q_BLOCK)))

    n_qb = S // Q_BLOCK
    span = WINDOW + Q_BLOCK
    kidx = np.arange(n_qb)[:, None] * Q_BLOCK + np.arange(span)[None, :]
    pad = ((0, 0), (WINDOW, 0), (0, 0), (0, 0))
    k_win = jnp.pad(_apply_rope(_rmsnorm(kv[:, :, 2, 0], g_k[2]), c, s_), pad)[:, kidx]
    v_win = jnp.pad(kv[:, :, 2, 1], pad)[:, kidx]
    qb = qg.reshape(B, n_qb, Q_BLOCK, G, R, d)
    sc_w = jnp.einsum('bnqgrd,bnkgd->bngrqk', qb, k_win) * scale
    spos = (kidx - WINDOW)[:, None, :]
    tq = (np.arange(n_qb)[:, None] * Q_BLOCK + np.arange(Q_BLOCK)[None, :])[:, :, None]
    wmask = (spos <= tq) & (spos > tq - WINDOW) & (spos >= 0)
    a_w = _masked_softmax(sc_w, wmask[None, :, None, None])
    o_win = jnp.einsum('bngrqk,bnkgd->bnqgrd', a_w.astype(v_win.dtype), v_win).reshape(B, S, G, R, d)

    g = jax.nn.sigmoid(gate_logits.astype(jnp.float32)).reshape(B, S, G, R, 3)
    o = g[..., 0:1] * o_cmp + g[..., 1:2] * o_slc + g[..., 2:3] * o_win
    return o.astype(q.dtype).reshape(B, S, -1)


IN_WIDTH_PAD = _round_up(IN_WIDTH, 512)
D_FF_PAD = _round_up(D_FF, 1024)


def _pad_cols(w, n):
    return jnp.pad(w, ((0, 0), (0, n - w.shape[1])))


def _pad_rows(w, n):
    return jnp.pad(w, ((0, n - w.shape[0]), (0, 0)))


def kernel(x, p, positions, w_in, w_out, g_mix, g_ffn, w_gate, w_up, w_down, w_ple_proj, w_ple_gate, g_ple,
           g_group_out, diff_g_q, diff_g_k, diff_g_sub, diff_lambda, dsa_g_q, dsa_g_k, mla_g_cq, mla_g_ckv,
           mla_w_uq, mla_w_uk, mla_w_uv, mla_g_q, mla_g_k, nsa_g_q, nsa_g_k, nsa_cmp_w1, nsa_cmp_w2, nsa_cmp_pe):
    B, S = x.shape[:2]
    M = B * S
    H = HEADS_PER_GROUP
    bf16 = jnp.bfloat16
    rope_diff = _rope_tables(positions, DIFF_QK_DIM)
    rope_head = _rope_tables(positions, HEAD_DIM)
    rope_idx = _rope_tables(positions, IDX_DIM)
    rope_mla = _rope_tables(positions, MLA_ROPE)
    split_at = [int(o) for o in np.cumsum(IN_SIZES)[:-1]]
    h = x.reshape(M, D_MODEL)
    for i in range(DEPTH):
        w_in_i = _pad_cols(w_in[i].astype(bf16), IN_WIDTH_PAD)
        u = _rmsnorm_rows(h, g_mix[i])
        proj = _matmul(u, w_in_i, tm=1024, tn=512, tk=D_MODEL)[:, :IN_WIDTH].reshape(B, S, IN_WIDTH)
        (a_q, a_k, a_v, b_q, b_k, b_v, b_iq, b_ik, b_iw,
         c_q, c_kv, c_kr, d_q, d_kv, d_g) = jnp.split(proj, split_at, axis=-1)
        o_a = _diff_attention(a_q.reshape(B, S, H, 2, DIFF_QK_DIM), a_k.reshape(B, S, H, 2, DIFF_QK_DIM),
                              a_v.reshape(B, S, H, DIFF_V_DIM), rope_diff[0], rope_diff[1],
                              diff_g_q[i], diff_g_k[i], diff_g_sub[i], diff_lambda[i], i)
        o_b = _dsa_attention(b_q.reshape(B, S, H, HEAD_DIM), b_k, b_v, b_iq.reshape(B, S, IDX_HEADS, IDX_DIM),
                             b_ik, b_iw, rope_head[0], rope_head[1], rope_idx[0], rope_idx[1],
                             dsa_g_q[i], dsa_g_k[i])
        o_c = _mla_attention(c_q, c_kv, c_kr, rope_mla[0], rope_mla[1], mla_g_cq[i], mla_g_ckv[i],
                             mla_w_uq[i], mla_w_uk[i], mla_w_uv[i], mla_g_q[i], mla_g_k[i])
        o_d = _nsa_attention(d_q.reshape(B, S, H, HEAD_DIM), d_kv.reshape(B, S, 3, 2, NSA_KV_HEADS, HEAD_DIM),
                             d_g.reshape(B, S, H, 3), rope_head[0], rope_head[1], positions,
                             nsa_g_q[i], nsa_g_k[i], nsa_cmp_w1[i], nsa_cmp_w2[i], nsa_cmp_pe[i])
        mixed = jnp.concatenate([o_a,
                                 _rmsnorm(o_b, g_group_out[i, 0]),
                                 _rmsnorm(o_c, g_group_out[i, 1]),
                                 _rmsnorm(o_d, g_group_out[i, 2])], axis=-1).reshape(M, MIX_WIDTH).astype(bf16)
        h = _matmul(mixed, w_out[i].astype(bf16), res=h, tm=1024, tn=1024, tk=1024)
        u = _rmsnorm_rows(h, g_ffn[i])
        act = _swiglu(u, _pad_cols(w_gate[i].astype(bf16), D_FF_PAD), _pad_cols(w_up[i].astype(bf16), D_FF_PAD),
                      tm=1024, tn=512)
        h = _matmul(act, _pad_rows(w_down[i].astype(bf16), D_FF_PAD), res=h, tm=1024, tn=1024, tk=D_FF_PAD // 8)
        u = _rmsnorm_rows(h, g_ple[i])
        h = _ple(u, w_ple_gate[i].astype(bf16), p[i].reshape(M, PLE_DIM).astype(bf16), w_ple_proj[i].astype(bf16), h,
                 tm=1024, tn=512)
    return h.reshape(B, S, D_MODEL)
```

```python
import functools
import math

import numpy as np
import jax
import jax.numpy as jnp
from jax import lax
from jax.experimental import pallas as pl
from jax.experimental.pallas import tpu as pltpu

D_MODEL = 4096
DEPTH = 4
HEAD_DIM = 128
N_GROUPS = 4
HEADS_PER_GROUP = D_MODEL // HEAD_DIM // N_GROUPS
GROUP_WIDTH = HEADS_PER_GROUP * HEAD_DIM
MIX_WIDTH = N_GROUPS * GROUP_WIDTH
D_FF = ((8 * D_MODEL + 3 * 256 - 1) // (3 * 256)) * 256
PLE_DIM = 256
ROPE_THETA = 10000.0
EPS = 1e-6
Q_BLOCK = 128
NEG_INF = -1e30
POS_INF = 1e30

DIFF_QK_DIM = HEAD_DIM // 2
DIFF_V_DIM = HEAD_DIM
IDX_HEADS = 16
IDX_DIM = 64
DSA_TOPK = 256
MLA_Q_RANK = 1024
MLA_KV_RANK = 512
MLA_NOPE = 128
MLA_ROPE = 64
MLA_V = HEAD_DIM
NSA_KV_HEADS = 2
NSA_Q_PER_KV = HEADS_PER_GROUP // NSA_KV_HEADS
CMP_LEN = 32
CMP_STRIDE = 16
CMP_HIDDEN = HEAD_DIM
SEL_BLOCK = 64
SEL_TOPN = 16
SEL_INIT = 1
SEL_LOCAL = 2
SEL_Q_BLOCK = 32
WINDOW = 512

IN_SIZES = (
    HEADS_PER_GROUP * 2 * DIFF_QK_DIM, HEADS_PER_GROUP * 2 * DIFF_QK_DIM, HEADS_PER_GROUP * DIFF_V_DIM,
    GROUP_WIDTH, HEAD_DIM, HEAD_DIM, IDX_HEADS * IDX_DIM, IDX_DIM, IDX_HEADS,
    MLA_Q_RANK, MLA_KV_RANK, MLA_ROPE,
    GROUP_WIDTH, 3 * 2 * NSA_KV_HEADS * HEAD_DIM, 3 * HEADS_PER_GROUP,
)
IN_WIDTH = sum(IN_SIZES)

V7X_VMEM_LIMIT_BYTES = 56 * 1024 * 1024
LANE = 128
MXU_DTYPE = getattr(jnp, "bfloat16")


def _round_up(n, m):
    return (n + m - 1) // m * m


def _compiler_params(semantics):
    return pltpu.CompilerParams(dimension_semantics=semantics, vmem_limit_bytes=V7X_VMEM_LIMIT_BYTES)


def _rmsnorm_rows_kernel(x_ref, g_ref, o_ref):
    x = x_ref[...]
    ms = jnp.mean(x * x, axis=-1, keepdims=True)
    o_ref[...] = (x * lax.rsqrt(ms + EPS) * g_ref[...]).astype(o_ref.dtype)


def _rmsnorm_rows(x, g, *, tm=256):
    m, d = x.shape
    out_dtype = MXU_DTYPE
    return pl.pallas_call(
        _rmsnorm_rows_kernel,
        grid=(m // tm,),
        in_specs=[pl.BlockSpec((tm, d), lambda i: (i, 0)), pl.BlockSpec((1, d), lambda i: (0, 0))],
        out_specs=pl.BlockSpec((tm, d), lambda i: (i, 0)),
        out_shape=jax.ShapeDtypeStruct((m, d), out_dtype),
        compiler_params=_compiler_params(("parallel",)),
        name="rmsnorm_rows",
    )(x, g.reshape(1, d))


def _mm_kernel(a_ref, w_ref, o_ref, acc_ref, *, nk):
    k = pl.program_id(2)

    @pl.when(k == 0)
    def _():
        acc_ref[...] = jnp.zeros_like(acc_ref)

    acc_ref[...] += jnp.dot(a_ref[...], w_ref[...], preferred_element_type=jnp.float32)

    @pl.when(k == nk - 1)
    def _():
        o_ref[...] = acc_ref[...].astype(o_ref.dtype)


def _mm_res_kernel(a_ref, w_ref, r_ref, o_ref, acc_ref, *, nk):
    k = pl.program_id(2)

    @pl.when(k == 0)
    def _():
        acc_ref[...] = jnp.zeros_like(acc_ref)

    acc_ref[...] += jnp.dot(a_ref[...], w_ref[...], preferred_element_type=jnp.float32)

    @pl.when(k == nk - 1)
    def _():
        o_ref[...] = (r_ref[...] + acc_ref[...]).astype(o_ref.dtype)


def _matmul(a, w, *, res=None, tm, tn, tk, out_dtype=jnp.float32):
    m, kdim = a.shape
    n = w.shape[1]
    nk = kdim // tk
    assert m % tm == 0 and n % tn == 0 and kdim % tk == 0
    in_specs = [pl.BlockSpec((tm, tk), lambda i, j, k: (i, k)), pl.BlockSpec((tk, tn), lambda i, j, k: (k, j))]
    args = [a, w]
    if res is None:
        body = functools.partial(_mm_kernel, nk=nk)
    else:
        body = functools.partial(_mm_res_kernel, nk=nk)
        in_specs.append(pl.BlockSpec((tm, tn), lambda i, j, k: (i, j)))
        args.append(res)
    return pl.pallas_call(
        body,
        grid=(m // tm, n // tn, nk),
        in_specs=in_specs,
        out_specs=pl.BlockSpec((tm, tn), lambda i, j, k: (i, j)),
        out_shape=jax.ShapeDtypeStruct((m, n), out_dtype),
        scratch_shapes=[pltpu.VMEM((tm, tn), jnp.float32)],
        compiler_params=_compiler_params(("parallel", "parallel", "arbitrary")),
        name="matmul_res" if res is not None else "matmul",
    )(*args)


def _swiglu_kernel(a_ref, wg_ref, wu_ref, o_ref):
    a = a_ref[...]
    g = jnp.dot(a, wg_ref[...], preferred_element_type=jnp.float32)
    u = jnp.dot(a, wu_ref[...], preferred_element_type=jnp.float32)
    o_ref[...] = (g * jax.nn.sigmoid(g) * u).astype(o_ref.dtype)


def _swiglu(a, wg, wu, *, tm, tn):
    m, kdim = a.shape
    n = wg.shape[1]
    return pl.pallas_call(
        _swiglu_kernel,
        grid=(m // tm, n // tn),
        in_specs=[pl.BlockSpec((tm, kdim), lambda i, j: (i, 0)),
                  pl.BlockSpec((kdim, tn), lambda i, j: (0, j)),
                  pl.BlockSpec((kdim, tn), lambda i, j: (0, j))],
        out_specs=pl.BlockSpec((tm, tn), lambda i, j: (i, j)),
        out_shape=jax.ShapeDtypeStruct((m, n), MXU_DTYPE),
        compiler_params=_compiler_params(("parallel", "parallel")),
        name="swiglu",
    )(a, wg, wu)


def _ple_kernel(a_ref, wg_ref, p_ref, wp_ref, r_ref, o_ref):
    gate = jax.nn.sigmoid(jnp.dot(a_ref[...], wg_ref[...], preferred_element_type=jnp.float32))
    emb = jnp.dot(p_ref[...], wp_ref[...], preferred_element_type=jnp.float32)
    o_ref[...] = r_ref[...] + gate * emb


def _ple(a, wg, p, wp, res, *, tm, tn):
    m, kdim = a.shape
    n = wg.shape[1]
    pdim = p.shape[1]
    return pl.pallas_call(
        _ple_kernel,
        grid=(m // tm, n // tn),
        in_specs=[pl.BlockSpec((tm, kdim), lambda i, j: (i, 0)),
                  pl.BlockSpec((kdim, tn), lambda i, j: (0, j)),
                  pl.BlockSpec((tm, pdim), lambda i, j: (i, 0)),
                  pl.BlockSpec((pdim, tn), lambda i, j: (0, j)),
                  pl.BlockSpec((tm, tn), lambda i, j: (i, j))],
        out_specs=pl.BlockSpec((tm, tn), lambda i, j: (i, j)),
        out_shape=jax.ShapeDtypeStruct((m, n), jnp.float32),
        compiler_params=_compiler_params(("parallel", "parallel")),
        name="ple",
    )(a, wg, p, wp, res)


def _rmsnorm(x, g):
    x32 = x.astype(jnp.float32)
    y = x32 * lax.rsqrt(jnp.mean(x32 * x32, axis=-1, keepdims=True) + EPS)
    return (y * g.astype(jnp.float32)).astype(x.dtype)


def _rope_tables(positions, dim):
    inv_freq = ROPE_THETA ** (-jnp.arange(0, dim, 2, dtype=jnp.float32) / dim)
    ang = positions.astype(jnp.float32)[..., None] * inv_freq
    return jnp.cos(ang), jnp.sin(ang)


def _apply_rope(x, cos, sin):
    half = x.shape[-1] // 2
    x32 = x.astype(jnp.float32)
    x1, x2 = x32[..., :half], x32[..., half:]
    return jnp.concatenate([x1 * cos - x2 * sin, x2 * cos + x1 * sin], axis=-1).astype(x.dtype)


def _masked_softmax(s, mask):
    s = jnp.where(mask, s.astype(jnp.float32), NEG_INF)
    return jnp.where(mask, jax.nn.softmax(s, axis=-1), 0.0)


def _merge_blocks(o):
    o = jnp.moveaxis(o, 0, 1)
    return o.reshape((o.shape[0], o.shape[1] * o.shape[2]) + o.shape[3:])


def _dense_causal_attention(q, k, v, scale):
    S = q.shape[1]
    kpos = jnp.arange(S)

    def block(i):
        start = i * Q_BLOCK
        qb = lax.dynamic_slice_in_dim(q, start, Q_BLOCK, axis=1)
        sc = jnp.einsum('bqhd,bkhd->bhqk', qb, k) * scale
        mask = kpos[None, :] <= (start + jnp.arange(Q_BLOCK))[:, None]
        a = _masked_softmax(sc, mask)
        return jnp.einsum('bhqk,bkhd->bqhd', a.astype(v.dtype), v)

    return _merge_blocks(lax.map(block, jnp.arange(S // Q_BLOCK)))


def _diff_attention(q, k, v, cos, sin, g_q, g_k, g_sub, lam_params, layer_idx):
    B, S = q.shape[:2]
    c, s_ = cos[:, :, None, None, :], sin[:, :, None, None, :]
    q = _apply_rope(_rmsnorm(q, g_q), c, s_)
    k = _apply_rope(_rmsnorm(k, g_k), c, s_)
    lam_init = 0.8 - 0.6 * math.exp(-0.3 * layer_idx)
    lp = lam_params.astype(jnp.float32)
    lam = jnp.exp(jnp.sum(lp[0] * lp[1])) - jnp.exp(jnp.sum(lp[2] * lp[3])) + lam_init
    scale = DIFF_QK_DIM ** -0.5
    kpos = jnp.arange(S)

    def block(i):
        start = i * Q_BLOCK
        qb = lax.dynamic_slice_in_dim(q, start, Q_BLOCK, axis=1)
        sc = jnp.einsum('bqhcd,bkhcd->bhcqk', qb, k) * scale
        mask = kpos[None, :] <= (start + jnp.arange(Q_BLOCK))[:, None]
        a = _masked_softmax(sc, mask)
        w = a[:, :, 0] - lam * a[:, :, 1]
        return jnp.einsum('bhqk,bkhd->bqhd', w.astype(v.dtype), v)

    o = _merge_blocks(lax.map(block, jnp.arange(S // Q_BLOCK)))
    o = _rmsnorm(o, g_sub) * (1.0 - lam_init)
    return o.reshape(B, S, -1)


def _dsa_attention(q, k, v, iq, ik, iw, cos, sin, cos_i, sin_i, g_q, g_k):
    B, S = q.shape[:2]
    q = _apply_rope(_rmsnorm(q, g_q), cos[:, :, None, :], sin[:, :, None, :])
    k = _apply_rope(_rmsnorm(k, g_k), cos, sin)
    iq = _apply_rope(iq, cos_i[:, :, None, :], sin_i[:, :, None, :])
    ik = _apply_rope(ik, cos_i, sin_i)
    n_keep = min(DSA_TOPK, S // 4)
    scale = HEAD_DIM ** -0.5
    kpos = jnp.arange(S)
    gather_keys = jax.vmap(lambda t, ix: t[ix])

    def block(i):
        start = i * Q_BLOCK
        qpos = start + jnp.arange(Q_BLOCK)
        qb = lax.dynamic_slice_in_dim(q, start, Q_BLOCK, axis=1)
        iqb = lax.dynamic_slice_in_dim(iq, start, Q_BLOCK, axis=1)
        iwb = lax.dynamic_slice_in_dim(iw, start, Q_BLOCK, axis=1)
        rel = jax.nn.relu(jnp.einsum('bqhd,bkd->bqhk', iqb, ik).astype(jnp.float32) * IDX_DIM ** -0.5)
        score = jnp.einsum('bqh,bqhk->bqk', iwb.astype(jnp.float32) * IDX_HEADS ** -0.5, rel)
        score = jnp.where(kpos[None, None, :] <= qpos[None, :, None], score, NEG_INF)
        _, sel = lax.top_k(score, n_keep)
        ks = gather_keys(k, sel)
        vs = gather_keys(v, sel)
        sc = jnp.einsum('bqhd,bqnd->bhqn', qb, ks) * scale
        a = _masked_softmax(sc, (sel <= qpos[None, :, None])[:, None])
        return jnp.einsum('bhqn,bqnd->bqhd', a.astype(vs.dtype), vs)

    o = _merge_blocks(lax.map(block, jnp.arange(S // Q_BLOCK)))
    return o.reshape(B, S, -1)


def _mla_attention(c_q, c_kv, k_rope, cos, sin, g_cq, g_ckv, w_uq, w_uk, w_uv, g_q, g_k):
    B, S = c_q.shape[:2]
    H = HEADS_PER_GROUP
    q = (_rmsnorm(c_q, g_cq) @ w_uq).reshape(B, S, H, MLA_NOPE + MLA_ROPE)
    ckv = _rmsnorm(c_kv, g_ckv)
    k_nope = (ckv @ w_uk).reshape(B, S, H, MLA_NOPE)
    v = (ckv @ w_uv).reshape(B, S, H, MLA_V)
    k_r = jnp.broadcast_to(k_rope[:, :, None, :], (B, S, H, MLA_ROPE))
    k = jnp.concatenate([k_nope, k_r], axis=-1)
    q = _rmsnorm(q, g_q)
    k = _rmsnorm(k, g_k)
    c, s_ = cos[:, :, None, :], sin[:, :, None, :]
    q = jnp.concatenate([q[..., :MLA_NOPE], _apply_rope(q[..., MLA_NOPE:], c, s_)], axis=-1)
    k = jnp.concatenate([k[..., :MLA_NOPE], _apply_rope(k[..., MLA_NOPE:], c, s_)], axis=-1)
    o = _dense_causal_attention(q, k, v, (MLA_NOPE + MLA_ROPE) ** -0.5)
    return o.reshape(B, S, -1)


def _nsa_attention(q, kv, gate_logits, cos, sin, positions, g_q, g_k, cmp_w1, cmp_w2, cmp_pe):
    B, S = q.shape[:2]
    G, R, d = NSA_KV_HEADS, NSA_Q_PER_KV, HEAD_DIM
    scale = d ** -0.5
    c, s_ = cos[:, :, None, :], sin[:, :, None, :]
    qg = _apply_rope(_rmsnorm(q, g_q), c, s_).reshape(B, S, G, R, d)
    tpos = np.arange(S)

    n_cmp = (S - CMP_LEN) // CMP_STRIDE + 1
    starts = np.arange(n_cmp) * CMP_STRIDE
    ends = starts + CMP_LEN - 1
    blocks = kv[:, :, 0][:, starts[:, None] + np.arange(CMP_LEN)[None, :]]
    blocks = blocks + jnp.transpose(cmp_pe, (1, 0, 2))[:, :, None, :]
    flat = jnp.transpose(blocks, (0, 1, 3, 4, 2, 5)).reshape(B, n_cmp, 2, G, CMP_LEN * d)
    hid = jax.nn.gelu(jnp.einsum('bnkgf,kfh->bnkgh', flat, cmp_w1))
    comp = jnp.einsum('bnkgh,khe->bnkge', hid, cmp_w2)
    cos_c, sin_c = _rope_tables(positions[:, ends], d)
    k_cmp = _apply_rope(_rmsnorm(comp[:, :, 0], g_k[0]), cos_c[:, :, None, :], sin_c[:, :, None, :])
    v_cmp = comp[:, :, 1]
    sc = jnp.einsum('bsgrd,bjgd->bgrsj', qg, k_cmp) * scale
    p_cmp = _masked_softmax(sc, ends[None, :] <= tpos[:, None])
    o_cmp = jnp.einsum('bgrsj,bjgd->bsgrd', p_cmp.astype(v_cmp.dtype), v_cmp)

    n_sel = S // SEL_BLOCK
    n_top = min(SEL_TOPN, n_sel)
    blk = np.arange(n_sel)
    sel_start = blk * SEL_BLOCK
    cmp_to_sel = ((starts[:, None] < sel_start[None, :] + SEL_BLOCK)
                  & (starts[:, None] + CMP_LEN > sel_start[None, :])).astype(np.float32)
    imp = jnp.einsum('bgrsj,jn->bgsn', p_cmp, jnp.asarray(cmp_to_sel))
    dist = (tpos // SEL_BLOCK)[:, None] - blk[None, :]
    forced = (blk[None, :] < SEL_INIT) | ((dist >= 0) & (dist < SEL_LOCAL))
    admissible = sel_start[None, :] <= tpos[:, None]
    imp = jnp.where(forced, POS_INF, jnp.where(admissible, imp, NEG_INF))
    _, sel = lax.top_k(imp, n_top)
    k_slc = _apply_rope(_rmsnorm(kv[:, :, 1, 0], g_k[1]), c, s_)
    kb = k_slc.reshape(B, n_sel, SEL_BLOCK, G, d).transpose(0, 3, 1, 2, 4)
    vb = kv[:, :, 1, 1].reshape(B, n_sel, SEL_BLOCK, G, d).transpose(0, 3, 1, 2, 4)
    gather_blocks = jax.vmap(jax.vmap(lambda t, ix: t[ix]))
    q_t = qg.transpose(0, 2, 3, 1, 4)

    def sel_chunk(i):
        start = i * SEL_Q_BLOCK
        qc = lax.dynamic_slice_in_dim(q_t, start, SEL_Q_BLOCK, axis=3)
        ix = lax.dynamic_slice_in_dim(sel, start, SEL_Q_BLOCK, axis=2)
        ks = gather_blocks(kb, ix)
        vs = gather_blocks(vb, ix).reshape(B, G, SEL_Q_BLOCK, n_top * SEL_BLOCK, d)
        qpos = start + jnp.arange(SEL_Q_BLOCK)
        tok = ix[..., None] * SEL_BLOCK + jnp.arange(SEL_BLOCK)
        mask = (tok <= qpos[None, None, :, None, None]).reshape(B, G, 1, SEL_Q_BLOCK, n_top * SEL_BLOCK)
        sc_s = jnp.einsum('bgrqd,bgqntd->bgrqnt', qc, ks).reshape(B, G, R, SEL_Q_BLOCK, n_top * SEL_BLOCK) * scale
        a_s = _masked_softmax(sc_s, mask)
        return jnp.einsum('bgrqm,bgqmd->bqgrd', a_s.astype(vs.dtype), vs)

    o_slc = _merge_blocks(lax.map(sel_chunk, jnp.arange(S // SEL_Q_BLOCK)))

    n_qb = S // Q_BLOCK
    span = WINDOW + Q_BLOCK
    kidx = np.arange(n_qb)[:, None] * Q_BLOCK + np.arange(span)[None, :]
    pad = ((0, 0), (WINDOW, 0), (0, 0), (0, 0))
    k_win = jnp.pad(_apply_rope(_rmsnorm(kv[:, :, 2, 0], g_k[2]), c, s_), pad)[:, kidx]
    v_win = jnp.pad(kv[:, :, 2, 1], pad)[:, kidx]
    qb = qg.reshape(B, n_qb, Q_BLOCK, G, R, d)
    sc_w = jnp.einsum('bnqgrd,bnkgd->bngrqk', qb, k_win) * scale
    spos = (kidx - WINDOW)[:, None, :]
    tq = (np.arange(n_qb)[:, None] * Q_BLOCK + np.arange(Q_BLOCK)[None, :])[:, :, None]
    wmask = (spos <= tq) & (spos > tq - WINDOW) & (spos >= 0)
    a_w = _masked_softmax(sc_w, wmask[None, :, None, None])
    o_win = jnp.einsum('bngrqk,bnkgd->bnqgrd', a_w.astype(v_win.dtype), v_win).reshape(B, S, G, R, d)

    g = jax.nn.sigmoid(gate_logits.astype(jnp.float32)).reshape(B, S, G, R, 3)
    o = g[..., 0:1] * o_cmp + g[..., 1:2] * o_slc + g[..., 2:3] * o_win
    return o.astype(q.dtype).reshape(B, S, -1)


_SEG_ORDER = ("a_q", "a_k", "a_v", "b_q", "b_iq", "c_q", "d_q", "d_kv", "c_kv", "b_k", "b_v")
_SEG_NAMES = ("a_q", "a_k", "a_v", "b_q", "b_k", "b_v", "b_iq", "b_ik", "b_iw", "c_q", "c_kv", "c_kr", "d_q", "d_kv", "d_g")
_ORIG_START = dict(zip(_SEG_NAMES, [int(o) for o in np.cumsum((0,) + IN_SIZES[:-1])]))
_ORIG_SIZE = dict(zip(_SEG_NAMES, IN_SIZES))
_GATES_PER_KV_GROUP = 3 * NSA_Q_PER_KV


def _build_in_layout():
    col = {}
    perm = []
    for name in _SEG_ORDER:
        col[name] = len(perm)
        perm.extend(range(_ORIG_START[name], _ORIG_START[name] + _ORIG_SIZE[name]))
    zero = IN_WIDTH
    col["narrow"] = len(perm)
    blk0 = list(range(_ORIG_START["b_ik"], _ORIG_START["b_ik"] + IDX_DIM))
    blk0 += list(range(_ORIG_START["d_g"], _ORIG_START["d_g"] + _GATES_PER_KV_GROUP))
    blk0 += list(range(_ORIG_START["b_iw"], _ORIG_START["b_iw"] + IDX_HEADS))
    blk0 += [zero] * (LANE - len(blk0))
    blk1 = list(range(_ORIG_START["c_kr"], _ORIG_START["c_kr"] + MLA_ROPE))
    blk1 += list(range(_ORIG_START["d_g"] + _GATES_PER_KV_GROUP, _ORIG_START["d_g"] + 2 * _GATES_PER_KV_GROUP))
    blk1 += [zero] * (LANE - len(blk1))
    perm.extend(blk0 + blk1)
    width = _round_up(len(perm), 512)
    perm.extend([zero] * (width - len(perm)))
    return col, np.asarray(perm, np.int32), width


_COL, _IN_PERM, IN_WIDTH_PAD = _build_in_layout()
_GATE_LANE0 = IDX_DIM
_IW_LANE0 = IDX_DIM + _GATES_PER_KV_GROUP


def _rope_lane_tables(cos, sin):
    return jnp.concatenate([cos, cos], axis=-1), jnp.concatenate([-sin, sin], axis=-1)


def _seg64_sum(y):
    r = lax.broadcasted_iota(jnp.int32, (LANE, LANE), 0) // 64
    c = lax.broadcasted_iota(jnp.int32, (LANE, LANE), 1) // 64
    bd = jnp.where(r == c, 1.0, 0.0).astype(MXU_DTYPE)
    hi = y.astype(MXU_DTYPE)
    lo = (y - hi.astype(jnp.float32)).astype(MXU_DTYPE)
    return (jnp.dot(hi, bd, preferred_element_type=jnp.float32)
            + jnp.dot(lo, bd, preferred_element_type=jnp.float32))


def _prep_kernel(x_ref, g_ref, cc_ref, ss_ref, o_ref, *, seg, norm, scale):
    x = x_ref[...]
    if norm:
        if seg == LANE:
            ms = jnp.mean(x * x, axis=-1, keepdims=True)
        else:
            ms = _seg64_sum(x * x) * (1.0 / seg)
        x = x * lax.rsqrt(ms + EPS) * g_ref[0]
    if seg == LANE:
        partner = pltpu.roll(x, LANE // 2, axis=1)
    else:
        lane = lax.broadcasted_iota(jnp.int32, x.shape, 1)
        partner = jnp.where((lane % seg) < seg // 2, pltpu.roll(x, LANE - seg // 2, axis=1),
                            pltpu.roll(x, seg // 2, axis=1))
    y = x * cc_ref[...] + partner * ss_ref[...]
    if scale != 1.0:
        y = y * scale
    o_ref[...] = y.astype(o_ref.dtype)


def _prep_heads(proj, col0, n_blocks, gains, cc, ss, *, seg, norm, scale=1.0, tm=1024):
    m = proj.shape[0]
    tm = min(tm, m)
    c0 = col0 // LANE
    n_gain = gains.shape[0]
    gidx = (lambda i, h: (h, 0, 0)) if n_gain > 1 else (lambda i, h: (0, 0, 0))
    return pl.pallas_call(
        functools.partial(_prep_kernel, seg=seg, norm=norm, scale=scale),
        grid=(m // tm, n_blocks),
        in_specs=[pl.BlockSpec((tm, LANE), lambda i, h: (i, c0 + h)),
                  pl.BlockSpec((1, 1, LANE), gidx),
                  pl.BlockSpec((tm, LANE), lambda i, h: (i, 0)),
                  pl.BlockSpec((tm, LANE), lambda i, h: (i, 0))],
        out_specs=pl.BlockSpec((tm, LANE), lambda i, h: (i, h)),
        out_shape=jax.ShapeDtypeStruct((m, LANE * n_blocks), MXU_DTYPE),
        compiler_params=_compiler_params(("parallel", "parallel")),
        name="prep_heads",
    )(proj, gains.reshape(n_gain, 1, LANE), cc, ss)


def _dot_nt(a, b):
    return lax.dot_general(a, b, (((1,), (1,)), ((), ())), preferred_element_type=jnp.float32)


def _softmax_step(s, mask, m, l, acc, v):
    s = jnp.where(mask, s, NEG_INF)
    m_new = jnp.maximum(m, jnp.max(s, axis=-1, keepdims=True))
    alpha = jnp.exp(m - m_new)
    p = jnp.where(mask, jnp.exp(s - m_new), 0.0)
    l = alpha * l + jnp.sum(p, axis=-1, keepdims=True)
    acc = alpha * acc + jnp.dot(p.astype(MXU_DTYPE), v, preferred_element_type=jnp.float32)
    return m_new, l, acc


def _softmax_finish(l, acc):
    return acc * (1.0 / jnp.maximum(l, 1e-30))


def _nsa_cmp_kernel(x_ref, pe_ref, w1_ref, w2_ref, gk_ref, cc_ref, ss_ref, kc_ref, vc_ref):
    nch = x_ref.shape[2]
    for kv in range(2):
        x = x_ref[kv, 0]
        lo = jnp.dot((x + pe_ref[kv, 0:1, :]).astype(MXU_DTYPE), w1_ref[kv, 0], preferred_element_type=jnp.float32)
        hi = jnp.dot((x + pe_ref[kv, 1:2, :]).astype(MXU_DTYPE), w1_ref[kv, 1], preferred_element_type=jnp.float32)
        hid = lo + pltpu.roll(hi, nch - 1, axis=0)
        comp = jnp.dot(jax.nn.gelu(hid).astype(MXU_DTYPE), w2_ref[kv], preferred_element_type=jnp.float32)
        if kv == 0:
            ms = jnp.mean(comp * comp, axis=-1, keepdims=True)
            y = comp * lax.rsqrt(ms + EPS) * gk_ref[...]
            y = y * cc_ref[0] + pltpu.roll(y, LANE // 2, axis=1) * ss_ref[0]
            kc_ref[0] = y.astype(kc_ref.dtype)
        else:
            vc_ref[0] = comp.astype(vc_ref.dtype)


def _nsa_compress(craw, pe2, w1, w2, gk, cc_c, ss_c, *, n_batch):
    _, gb, nch, width = craw.shape
    return pl.pallas_call(
        _nsa_cmp_kernel,
        grid=(gb,),
        in_specs=[pl.BlockSpec((2, 1, nch, width), lambda n: (0, n, 0, 0)),
                  pl.BlockSpec((2, 2, width), lambda n: (0, 0, 0)),
                  pl.BlockSpec((2, 2, width, CMP_HIDDEN), lambda n: (0, 0, 0, 0)),
                  pl.BlockSpec((2, CMP_HIDDEN, HEAD_DIM), lambda n: (0, 0, 0)),
                  pl.BlockSpec((1, HEAD_DIM), lambda n: (0, 0)),
                  pl.BlockSpec((1, nch, HEAD_DIM), lambda n: (n % n_batch, 0, 0)),
                  pl.BlockSpec((1, nch, HEAD_DIM), lambda n: (n % n_batch, 0, 0))],
        out_specs=[pl.BlockSpec((1, nch, HEAD_DIM), lambda n: (n, 0, 0)),
                   pl.BlockSpec((1, nch, HEAD_DIM), lambda n: (n, 0, 0))],
        out_shape=[jax.ShapeDtypeStruct((gb, nch, HEAD_DIM), MXU_DTYPE)] * 2,
        compiler_params=_compiler_params(("parallel",)),
        name="nsa_compress",
    )(craw, pe2, w1, w2, gk, cc_c, ss_c)


def _nsa_attn_kernel(q_ref, kc_ref, vc_ref, ks_ref, vs_ref, kw_ref, vw_ref, gate_ref, c2s_ref, exp_ref, o_ref, *,
                     tq, n_sel, n_top):
    i = pl.program_id(2)
    R = NSA_Q_PER_KV
    rows = R * tq
    nch = kc_ref.shape[1]
    bf16 = MXU_DTYPE
    q = jnp.concatenate([q_ref[:, r * HEAD_DIM:(r + 1) * HEAD_DIM] for r in range(R)], axis=0)
    t_row = i * tq + lax.broadcasted_iota(jnp.int32, (rows, 1), 0) % tq

    s = _dot_nt(q, kc_ref[0])
    blk_end = lax.broadcasted_iota(jnp.int32, (1, nch), 1) * CMP_STRIDE + (CMP_LEN - 1)
    mask = blk_end <= t_row
    zero = jnp.zeros((rows, 1), jnp.float32)
    _, l, pv = _softmax_step(s, mask, zero + NEG_INF, zero, jnp.zeros((rows, HEAD_DIM), jnp.float32), vc_ref[0])
    inv = 1.0 / jnp.maximum(l, 1e-30)
    o_cmp = pv * inv
    s = jnp.where(mask, s, NEG_INF)
    p_cmp = jnp.where(mask, jnp.exp(s - jnp.max(s, axis=-1, keepdims=True)), 0.0) * inv

    imp_rows = jnp.dot(p_cmp.astype(bf16), c2s_ref[...], preferred_element_type=jnp.float32)
    imp = imp_rows[0:tq]
    for r in range(1, R):
        imp = imp + imp_rows[r * tq:(r + 1) * tq]
    imp_t = imp.T[0:n_sel]
    blk = lax.broadcasted_iota(jnp.int32, (n_sel, tq), 0)
    t_lane = i * tq + lax.broadcasted_iota(jnp.int32, (n_sel, tq), 1)
    dist = lax.shift_right_logical(t_lane, int(math.log2(SEL_BLOCK))) - blk
    forced = (blk < SEL_INIT) | ((dist >= 0) & (dist < SEL_LOCAL))
    val = jnp.where(forced, POS_INF, jnp.where(blk * SEL_BLOCK <= t_lane, imp_t, NEG_INF))
    rank = jnp.zeros((n_sel, tq), jnp.float32)
    for mblk in range(n_sel):
        vm = val[mblk:mblk + 1, :]
        rank = rank + jnp.where((vm > val) | ((vm == val) & (blk > mblk)), 1.0, 0.0)
    sel_t = jnp.where(rank < n_top, 1.0, 0.0)
    sel_t = jnp.concatenate([sel_t, jnp.zeros((LANE - n_sel, tq), jnp.float32)], axis=0) if n_sel < LANE else sel_t
    sel_q = sel_t.T.astype(bf16)
    sel_rows = jnp.concatenate([sel_q] * R, axis=0)

    key_lane = lax.broadcasted_iota(jnp.int32, (1, tq), 1)

    def slc_body(j, carry):
        off = pl.multiple_of(j * tq, tq)
        k = ks_ref[pl.ds(off, tq), :]
        v = vs_ref[pl.ds(off, tq), :].astype(bf16)
        picked = jnp.dot(sel_rows, exp_ref[j], preferred_element_type=jnp.float32)
        msk = (picked > 0.5) & (off + key_lane <= t_row)
        return _softmax_step(_dot_nt(q, k), msk, *carry, v)

    init = (zero + NEG_INF, zero, jnp.zeros((rows, HEAD_DIM), jnp.float32))
    _, l, acc = lax.fori_loop(0, i + 1, slc_body, init)
    o_slc = _softmax_finish(l, acc)

    def win_body(j, carry):
        off = pl.multiple_of(j * tq, tq)
        k = kw_ref[pl.ds(off, tq), :]
        v = vw_ref[pl.ds(off, tq), :].astype(bf16)
        key = off + key_lane
        msk = (key <= t_row) & (key > t_row - WINDOW)
        return _softmax_step(_dot_nt(q, k), msk, *carry, v)

    _, l, acc = lax.fori_loop(jnp.maximum(i - WINDOW // tq, 0), i + 1, win_body, init)
    o_win = _softmax_finish(l, acc)

    gate = jax.nn.sigmoid(gate_ref[...])
    for r in range(R):
        c = _GATE_LANE0 + 3 * r
        rs = slice(r * tq, (r + 1) * tq)
        o_ref[:, r * HEAD_DIM:(r + 1) * HEAD_DIM] = (gate[:, c:c + 1] * o_cmp[rs]
                                                     + gate[:, c + 1:c + 2] * o_slc[rs]
                                                     + gate[:, c + 2:c + 3] * o_win[rs])


def _nsa_mixer(proj, B, S, cc_h, ss_h, positions, g_q, g_k, cmp_w1, cmp_w2, cmp_pe):
    M = B * S
    G, R, d = NSA_KV_HEADS, NSA_Q_PER_KV, HEAD_DIM
    bf16 = MXU_DTYPE
    tq = Q_BLOCK
    nq = S // tq
    nch = S // CMP_STRIDE
    n_sel = S // SEL_BLOCK
    n_top = min(SEL_TOPN, n_sel)
    kv0 = _COL["d_kv"]

    qn = _prep_heads(proj, _COL["d_q"], HEADS_PER_GROUP, g_q.reshape(1, d), cc_h, ss_h,
                     seg=LANE, norm=True, scale=d ** -0.5)
    k_slc = _prep_heads(proj, kv0 + (1 * 2 + 0) * G * d, G, g_k[1].reshape(1, d), cc_h, ss_h, seg=LANE, norm=True)
    k_win = _prep_heads(proj, kv0 + (2 * 2 + 0) * G * d, G, g_k[2].reshape(1, d), cc_h, ss_h, seg=LANE, norm=True)

    craw = proj[:, kv0:kv0 + 2 * G * d].reshape(B, S, 2, G, d).transpose(2, 3, 0, 1, 4)
    craw = craw.reshape(2, G * B, nch, CMP_STRIDE * d)
    ends = np.minimum(np.arange(nch) * CMP_STRIDE + CMP_LEN - 1, S - 1)
    cc_c, ss_c = _rope_lane_tables(*_rope_tables(positions[:, ends], d))
    pe2 = cmp_pe.reshape(2, 2, CMP_STRIDE * d)
    w1 = cmp_w1.astype(bf16).reshape(2, 2, CMP_STRIDE * d, CMP_HIDDEN)
    k_cmp, v_cmp = _nsa_compress(craw, pe2, w1, cmp_w2.astype(bf16), g_k[0].reshape(1, d), cc_c, ss_c, n_batch=B)

    starts = np.arange(nch) * CMP_STRIDE
    sel_start = np.arange(LANE) * SEL_BLOCK
    c2s = ((starts[:, None] < sel_start[None, :] + SEL_BLOCK) & (starts[:, None] + CMP_LEN > sel_start[None, :])
           & (np.arange(nch)[:, None] < nch - 1) & (np.arange(LANE)[None, :] < n_sel))
    expand = (np.arange(LANE)[None, :, None] == (np.arange(nq)[:, None, None] * tq + np.arange(tq)[None, None, :]) // SEL_BLOCK)
    c2s = jnp.asarray(c2s, bf16)
    expand = jnp.asarray(expand, bf16)

    narrow = _COL["narrow"] // LANE
    col_vs = (kv0 + (1 * 2 + 1) * G * d) // LANE
    col_vw = (kv0 + (2 * 2 + 1) * G * d) // LANE
    return pl.pallas_call(
        functools.partial(_nsa_attn_kernel, tq=tq, n_sel=n_sel, n_top=n_top),
        grid=(B, G, nq),
        in_specs=[pl.BlockSpec((tq, R * d), lambda b, g, i: (b * nq + i, g)),
                  pl.BlockSpec((1, nch, d), lambda b, g, i: (g * B + b, 0, 0)),
                  pl.BlockSpec((1, nch, d), lambda b, g, i: (g * B + b, 0, 0)),
                  pl.BlockSpec((S, d), lambda b, g, i: (b, g)),
                  pl.BlockSpec((S, d), lambda b, g, i: (b, col_vs + g)),
                  pl.BlockSpec((S, d), lambda b, g, i: (b, g)),
                  pl.BlockSpec((S, d), lambda b, g, i: (b, col_vw + g)),
                  pl.BlockSpec((tq, LANE), lambda b, g, i: (b * nq + i, narrow + g)),
                  pl.BlockSpec((nch, LANE), lambda b, g, i: (0, 0)),
                  pl.BlockSpec((nq, LANE, tq), lambda b, g, i: (0, 0, 0))],
        out_specs=pl.BlockSpec((tq, R * d), lambda b, g, i: (b * nq + i, g)),
        out_shape=jax.ShapeDtypeStruct((M, GROUP_WIDTH), jnp.float32),
        compiler_params=_compiler_params(("parallel", "parallel", "arbitrary")),
        name="nsa_attention",
    )(qn, k_cmp, v_cmp, k_slc, proj, k_win, proj, proj, c2s, expand)


_INT_MIN = -2 ** 31


def _sortable_key(x):
    b = lax.bitcast_convert_type(x + 0.0, jnp.int32)
    return jnp.where(b >= 0, b, b ^ 0x7FFFFFFF)


def _dsa_kernel(q_ref, k_ref, v_ref, iq_ref, ik_ref, nar_ref, o_ref, key_ref, msk_ref, *, tq, n_keep, idx_bits):
    i = pl.program_id(1)
    H = HEADS_PER_GROUP
    bf16 = MXU_DTYPE
    t_col = i * tq + lax.broadcasted_iota(jnp.int32, (tq, 1), 0)
    key_lane = lax.broadcasted_iota(jnp.int32, (1, tq), 1)
    lane = lax.broadcasted_iota(jnp.int32, (tq, LANE), 1)
    iw = nar_ref[...] * (IDX_HEADS ** -0.5)
    iq_pairs = [iq_ref[:, pr * LANE:(pr + 1) * LANE] for pr in range(IDX_HEADS // 2)]

    def score_body(j, _):
        off = pl.multiple_of(j * tq, tq)
        ik = ik_ref[pl.ds(off, tq), :]
        ik_lo = jnp.where(lane < IDX_DIM, ik, jnp.zeros_like(ik))
        ik_hi = jnp.where(lane < IDX_DIM, jnp.zeros_like(ik), pltpu.roll(ik.astype(jnp.float32), IDX_DIM, axis=1).astype(bf16))
        sc = jnp.zeros((tq, tq), jnp.float32)
        for hd in range(IDX_HEADS):
            rel = jnp.maximum(_dot_nt(iq_pairs[hd // 2], ik_lo if hd % 2 == 0 else ik_hi), 0.0)
            sc = sc + iw[:, _IW_LANE0 + hd:_IW_LANE0 + hd + 1] * rel
        sc = jnp.where(off + key_lane <= t_col, sc, NEG_INF)
        key_ref[j] = _sortable_key(sc)
        return 0

    lax.fori_loop(0, i + 1, score_body, 0)

    def count(pred):
        def body(j, part):
            return part + jnp.where(pred(key_ref[j], j * tq), 1.0, 0.0)
        part = lax.fori_loop(0, i + 1, body, jnp.zeros((tq, tq), jnp.float32))
        return jnp.sum(part, axis=-1, keepdims=True)

    c0 = count(lambda kk, off: kk >= 0)
    thr = jnp.where(c0 >= n_keep, 0, _INT_MIN).astype(jnp.int32)

    def thr_body(it, thr):
        cand = thr | lax.shift_left(jnp.int32(1), 30 - it)
        c = count(lambda kk, off: kk >= cand)
        return jnp.where(c >= n_keep, cand, thr)

    thr = lax.fori_loop(0, 31, thr_body, thr)

    need = n_keep - count(lambda kk, off: kk > thr)

    def cut_body(it, cut):
        cand = cut | lax.shift_left(jnp.int32(1), idx_bits - 1 - it)
        c = count(lambda kk, off: (kk == thr) & (off + key_lane < cand))
        return jnp.where(c < need, cand, cut)

    cut = lax.fori_loop(0, idx_bits, cut_body, jnp.zeros((tq, 1), jnp.int32))

    def mask_body(j, _):
        kk = key_ref[j]
        kidx = j * tq + key_lane
        sel = (kk > thr) | ((kk == thr) & (kidx <= cut))
        msk_ref[j] = jnp.where(sel & (kidx <= t_col), 1.0, 0.0)
        return 0

    lax.fori_loop(0, i + 1, mask_body, 0)

    hpp = 4
    rows = hpp * tq
    zero = jnp.zeros((rows, 1), jnp.float32)
    for hc in range(H // hpp):
        q = jnp.concatenate([q_ref[:, (hc * hpp + r) * HEAD_DIM:(hc * hpp + r + 1) * HEAD_DIM] for r in range(hpp)], axis=0)

        def att_body(j, carry):
            off = pl.multiple_of(j * tq, tq)
            k = k_ref[pl.ds(off, tq), :]
            v = v_ref[pl.ds(off, tq), :].astype(bf16)
            m1 = msk_ref[j] > 0.5
            msk = jnp.concatenate([m1] * hpp, axis=0)
            return _softmax_step(_dot_nt(q, k), msk, *carry, v)

        init = (zero + NEG_INF, zero, jnp.zeros((rows, HEAD_DIM), jnp.float32))
        _, l, acc = lax.fori_loop(0, i + 1, att_body, init)
        o = _softmax_finish(l, acc)
        for r in range(hpp):
            hh = hc * hpp + r
            o_ref[:, hh * HEAD_DIM:(hh + 1) * HEAD_DIM] = o[r * tq:(r + 1) * tq]


def _dsa_mixer(proj, B, S, cc_h, ss_h, cc_i, ss_i, g_q, g_k):
    M = B * S
    d = HEAD_DIM
    tq = Q_BLOCK
    nq = S // tq
    n_keep = min(DSA_TOPK, S // 4)
    qn = _prep_heads(proj, _COL["b_q"], HEADS_PER_GROUP, g_q.reshape(1, d), cc_h, ss_h, seg=LANE, norm=True, scale=d ** -0.5)
    kn = _prep_heads(proj, _COL["b_k"], 1, g_k.reshape(1, d), cc_h, ss_h, seg=LANE, norm=True)
    ones = jnp.ones((1, LANE), jnp.float32)
    iqn = _prep_heads(proj, _COL["b_iq"], IDX_HEADS * IDX_DIM // LANE, ones, cc_i, ss_i, seg=IDX_DIM, norm=False,
                      scale=IDX_DIM ** -0.5)
    ikn = _prep_heads(proj, _COL["narrow"], 1, ones, cc_i, ss_i, seg=IDX_DIM, norm=False)
    narrow = _COL["narrow"] // LANE
    col_v = _COL["b_v"] // LANE
    return pl.pallas_call(
        functools.partial(_dsa_kernel, tq=tq, n_keep=n_keep, idx_bits=int(math.log2(S))),
        grid=(B, nq),
        in_specs=[pl.BlockSpec((tq, GROUP_WIDTH), lambda b, i: (b * nq + i, 0)),
                  pl.BlockSpec((S, d), lambda b, i: (b, 0)),
                  pl.BlockSpec((S, d), lambda b, i: (b, col_v)),
                  pl.BlockSpec((tq, IDX_HEADS * IDX_DIM), lambda b, i: (b * nq + i, 0)),
                  pl.BlockSpec((S, LANE), lambda b, i: (b, 0)),
                  pl.BlockSpec((tq, LANE), lambda b, i: (b * nq + i, narrow))],
        out_specs=pl.BlockSpec((tq, GROUP_WIDTH), lambda b, i: (b * nq + i, 0)),
        out_shape=jax.ShapeDtypeStruct((M, GROUP_WIDTH), jnp.float32),
        scratch_shapes=[pltpu.VMEM((nq, tq, tq), jnp.int32), pltpu.VMEM((nq, tq, tq), jnp.float32)],
        compiler_params=_compiler_params(("parallel", "arbitrary")),
        name="dsa_attention",
    )(qn, kn, proj, iqn, ikn, proj)


D_FF_PAD = _round_up(D_FF, 1024)


def _seg(proj, name):
    return proj[:, _COL[name]:_COL[name] + _ORIG_SIZE[name]]


def _pad_cols(w, n):
    return jnp.pad(w, ((0, 0), (0, n - w.shape[1])))


def _pad_rows(w, n):
    return jnp.pad(w, ((0, n - w.shape[0]), (0, 0)))


def kernel(x, p, positions, w_in, w_out, g_mix, g_ffn, w_gate, w_up, w_down, w_ple_proj, w_ple_gate, g_ple,
           g_group_out, diff_g_q, diff_g_k, diff_g_sub, diff_lambda, dsa_g_q, dsa_g_k, mla_g_cq, mla_g_ckv,
           mla_w_uq, mla_w_uk, mla_w_uv, mla_g_q, mla_g_k, nsa_g_q, nsa_g_k, nsa_cmp_w1, nsa_cmp_w2, nsa_cmp_pe):
    B, S = x.shape[:2]
    M = B * S
    H = HEADS_PER_GROUP
    bf16 = MXU_DTYPE
    rope_diff = _rope_tables(positions, DIFF_QK_DIM)
    rope_head = _rope_tables(positions, HEAD_DIM)
    rope_idx = _rope_tables(positions, IDX_DIM)
    rope_mla = _rope_tables(positions, MLA_ROPE)
    cc_h, ss_h = (t.reshape(M, HEAD_DIM) for t in _rope_lane_tables(*rope_head))
    cc_i, ss_i = (jnp.tile(t.reshape(M, IDX_DIM), (1, LANE // IDX_DIM)) for t in _rope_lane_tables(*rope_idx))
    narrow = _COL["narrow"]
    h = x.reshape(M, D_MODEL)
    for i in range(DEPTH):
        w_in_i = jnp.take(_pad_cols(w_in[i].astype(bf16), IN_WIDTH + 1), _IN_PERM, axis=1)
        u = _rmsnorm_rows(h, g_mix[i])
        proj = _matmul(u, w_in_i, tm=1024, tn=512, tk=D_MODEL)
        a_q, a_k, a_v, b_q, b_k, b_v, b_iq, c_q, c_kv = (
            _seg(proj, n).reshape(B, S, -1) for n in ("a_q", "a_k", "a_v", "b_q", "b_k", "b_v", "b_iq", "c_q", "c_kv"))
        b_ik = proj[:, narrow:narrow + IDX_DIM].reshape(B, S, -1)
        b_iw = proj[:, narrow + _IW_LANE0:narrow + _IW_LANE0 + IDX_HEADS].reshape(B, S, -1)
        c_kr = proj[:, narrow + LANE:narrow + LANE + MLA_ROPE].reshape(B, S, -1)
        o_a = _diff_attention(a_q.reshape(B, S, H, 2, DIFF_QK_DIM), a_k.reshape(B, S, H, 2, DIFF_QK_DIM),
                              a_v.reshape(B, S, H, DIFF_V_DIM), rope_diff[0], rope_diff[1],
                              diff_g_q[i], diff_g_k[i], diff_g_sub[i], diff_lambda[i], i)
        o_b = _dsa_mixer(proj, B, S, cc_h, ss_h, cc_i, ss_i, dsa_g_q[i], dsa_g_k[i]).reshape(B, S, GROUP_WIDTH)
        o_c = _mla_attention(c_q, c_kv, c_kr, rope_mla[0], rope_mla[1], mla_g_cq[i], mla_g_ckv[i],
                             mla_w_uq[i], mla_w_uk[i], mla_w_uv[i], mla_g_q[i], mla_g_k[i])
        o_d = _nsa_mixer(proj, B, S, cc_h, ss_h, positions, nsa_g_q[i], nsa_g_k[i],
                         nsa_cmp_w1[i], nsa_cmp_w2[i], nsa_cmp_pe[i]).reshape(B, S, GROUP_WIDTH)
        mixed = jnp.concatenate([o_a,
                                 _rmsnorm(o_b, g_group_out[i, 0]),
                                 _rmsnorm(o_c, g_group_out[i, 1]),
                                 _rmsnorm(o_d, g_group_out[i, 2])], axis=-1).reshape(M, MIX_WIDTH).astype(bf16)
        h = _matmul(mixed, w_out[i].astype(bf16), res=h, tm=1024, tn=1024, tk=1024)
        u = _rmsnorm_rows(h, g_ffn[i])
        act = _swiglu(u, _pad_cols(w_gate[i].astype(bf16), D_FF_PAD), _pad_cols(w_up[i].astype(bf16), D_FF_PAD),
                      tm=1024, tn=512)
        h = _matmul(act, _pad_rows(w_down[i].astype(bf16), D_FF_PAD), res=h, tm=1024, tn=1024, tk=D_FF_PAD // 8)
        u = _rmsnorm_rows(h, g_ple[i])
        h = _ple(u, w_ple_gate[i].astype(bf16), p[i].reshape(M, PLE_DIM).astype(bf16), w_ple_proj[i].astype(bf16), h,
                 tm=1024, tn=512)
    return h.reshape(B, S, D_MODEL)
```

```python
import functools
import math

import numpy as np
import jax
import jax.numpy as jnp
from jax import lax
from jax.experimental import pallas as pl
from jax.experimental.pallas import tpu as pltpu

D_MODEL = 4096
DEPTH = 4
HEAD_DIM = 128
N_GROUPS = 4
HEADS_PER_GROUP = D_MODEL // HEAD_DIM // N_GROUPS
GROUP_WIDTH = HEADS_PER_GROUP * HEAD_DIM
MIX_WIDTH = N_GROUPS * GROUP_WIDTH
D_FF = ((8 * D_MODEL + 3 * 256 - 1) // (3 * 256)) * 256
PLE_DIM = 256
ROPE_THETA = 10000.0
EPS = 1e-6
Q_BLOCK = 128
NEG_INF = -1e30
POS_INF = 1e30

DIFF_QK_DIM = HEAD_DIM // 2
DIFF_V_DIM = HEAD_DIM
IDX_HEADS = 16
IDX_DIM = 64
DSA_TOPK = 256
MLA_Q_RANK = 1024
MLA_KV_RANK = 512
MLA_NOPE = 128
MLA_ROPE = 64
MLA_V = HEAD_DIM
NSA_KV_HEADS = 2
NSA_Q_PER_KV = HEADS_PER_GROUP // NSA_KV_HEADS
CMP_LEN = 32
CMP_STRIDE = 16
CMP_HIDDEN = HEAD_DIM
SEL_BLOCK = 64
SEL_TOPN = 16
SEL_INIT = 1
SEL_LOCAL = 2
SEL_Q_BLOCK = 32
WINDOW = 512

IN_SIZES = (
    HEADS_PER_GROUP * 2 * DIFF_QK_DIM, HEADS_PER_GROUP * 2 * DIFF_QK_DIM, HEADS_PER_GROUP * DIFF_V_DIM,
    GROUP_WIDTH, HEAD_DIM, HEAD_DIM, IDX_HEADS * IDX_DIM, IDX_DIM, IDX_HEADS,
    MLA_Q_RANK, MLA_KV_RANK, MLA_ROPE,
    GROUP_WIDTH, 3 * 2 * NSA_KV_HEADS * HEAD_DIM, 3 * HEADS_PER_GROUP,
)
IN_WIDTH = sum(IN_SIZES)

V7X_VMEM_LIMIT_BYTES = 56 * 1024 * 1024
LANE = 128
MXU_DTYPE = getattr(jnp, "bfloat16")


def _round_up(n, m):
    return (n + m - 1) // m * m


def _compiler_params(semantics):
    return pltpu.CompilerParams(dimension_semantics=semantics, vmem_limit_bytes=V7X_VMEM_LIMIT_BYTES)


def _rmsnorm_rows_kernel(x_ref, g_ref, o_ref):
    x = x_ref[...]
    ms = jnp.mean(x * x, axis=-1, keepdims=True)
    o_ref[...] = (x * lax.rsqrt(ms + EPS) * g_ref[...]).astype(o_ref.dtype)


def _rmsnorm_rows(x, g, *, tm=256):
    m, d = x.shape
    out_dtype = MXU_DTYPE
    return pl.pallas_call(
        _rmsnorm_rows_kernel,
        grid=(m // tm,),
        in_specs=[pl.BlockSpec((tm, d), lambda i: (i, 0)), pl.BlockSpec((1, d), lambda i: (0, 0))],
        out_specs=pl.BlockSpec((tm, d), lambda i: (i, 0)),
        out_shape=jax.ShapeDtypeStruct((m, d), out_dtype),
        compiler_params=_compiler_params(("parallel",)),
        name="rmsnorm_rows",
    )(x, g.reshape(1, d))


def _mm_kernel(a_ref, w_ref, o_ref, acc_ref, *, nk):
    k = pl.program_id(2)

    @pl.when(k == 0)
    def _():
        acc_ref[...] = jnp.zeros_like(acc_ref)

    acc_ref[...] += jnp.dot(a_ref[...], w_ref[...], preferred_element_type=jnp.float32)

    @pl.when(k == nk - 1)
    def _():
        o_ref[...] = acc_ref[...].astype(o_ref.dtype)


def _mm_res_kernel(a_ref, w_ref, r_ref, o_ref, acc_ref, *, nk):
    k = pl.program_id(2)

    @pl.when(k == 0)
    def _():
        acc_ref[...] = jnp.zeros_like(acc_ref)

    acc_ref[...] += jnp.dot(a_ref[...], w_ref[...], preferred_element_type=jnp.float32)

    @pl.when(k == nk - 1)
    def _():
        o_ref[...] = (r_ref[...] + acc_ref[...]).astype(o_ref.dtype)


def _matmul(a, w, *, res=None, tm, tn, tk, out_dtype=jnp.float32):
    m, kdim = a.shape
    n = w.shape[1]
    nk = kdim // tk
    assert m % tm == 0 and n % tn == 0 and kdim % tk == 0
    in_specs = [pl.BlockSpec((tm, tk), lambda i, j, k: (i, k)), pl.BlockSpec((tk, tn), lambda i, j, k: (k, j))]
    args = [a, w]
    if res is None:
        body = functools.partial(_mm_kernel, nk=nk)
    else:
        body = functools.partial(_mm_res_kernel, nk=nk)
        in_specs.append(pl.BlockSpec((tm, tn), lambda i, j, k: (i, j)))
        args.append(res)
    return pl.pallas_call(
        body,
        grid=(m // tm, n // tn, nk),
        in_specs=in_specs,
        out_specs=pl.BlockSpec((tm, tn), lambda i, j, k: (i, j)),
        out_shape=jax.ShapeDtypeStruct((m, n), out_dtype),
        scratch_shapes=[pltpu.VMEM((tm, tn), jnp.float32)],
        compiler_params=_compiler_params(("parallel", "parallel", "arbitrary")),
        name="matmul_res" if res is not None else "matmul",
    )(*args)


def _swiglu_kernel(a_ref, wg_ref, wu_ref, o_ref):
    a = a_ref[...]
    g = jnp.dot(a, wg_ref[...], preferred_element_type=jnp.float32)
    u = jnp.dot(a, wu_ref[...], preferred_element_type=jnp.float32)
    o_ref[...] = (g * jax.nn.sigmoid(g) * u).astype(o_ref.dtype)


def _swiglu(a, wg, wu, *, tm, tn):
    m, kdim = a.shape
    n = wg.shape[1]
    return pl.pallas_call(
        _swiglu_kernel,
        grid=(m // tm, n // tn),
        in_specs=[pl.BlockSpec((tm, kdim), lambda i, j: (i, 0)),
                  pl.BlockSpec((kdim, tn), lambda i, j: (0, j)),
                  pl.BlockSpec((kdim, tn), lambda i, j: (0, j))],
        out_specs=pl.BlockSpec((tm, tn), lambda i, j: (i, j)),
        out_shape=jax.ShapeDtypeStruct((m, n), MXU_DTYPE),
        compiler_params=_compiler_params(("parallel", "parallel")),
        name="swiglu",
    )(a, wg, wu)


def _ple_kernel(a_ref, wg_ref, p_ref, wp_ref, r_ref, o_ref):
    gate = jax.nn.sigmoid(jnp.dot(a_ref[...], wg_ref[...], preferred_element_type=jnp.float32))
    emb = jnp.dot(p_ref[...], wp_ref[...], preferred_element_type=jnp.float32)
    o_ref[...] = r_ref[...] + gate * emb


def _ple(a, wg, p, wp, res, *, tm, tn):
    m, kdim = a.shape
    n = wg.shape[1]
    pdim = p.shape[1]
    return pl.pallas_call(
        _ple_kernel,
        grid=(m // tm, n // tn),
        in_specs=[pl.BlockSpec((tm, kdim), lambda i, j: (i, 0)),
                  pl.BlockSpec((kdim, tn), lambda i, j: (0, j)),
                  pl.BlockSpec((tm, pdim), lambda i, j: (i, 0)),
                  pl.BlockSpec((pdim, tn), lambda i, j: (0, j)),
                  pl.BlockSpec((tm, tn), lambda i, j: (i, j))],
        out_specs=pl.BlockSpec((tm, tn), lambda i, j: (i, j)),
        out_shape=jax.ShapeDtypeStruct((m, n), jnp.float32),
        compiler_params=_compiler_params(("parallel", "parallel")),
        name="ple",
    )(a, wg, p, wp, res)


def _rmsnorm(x, g):
    x32 = x.astype(jnp.float32)
    y = x32 * lax.rsqrt(jnp.mean(x32 * x32, axis=-1, keepdims=True) + EPS)
    return (y * g.astype(jnp.float32)).astype(x.dtype)


def _rope_tables(positions, dim):
    inv_freq = ROPE_THETA ** (-jnp.arange(0, dim, 2, dtype=jnp.float32) / dim)
    ang = positions.astype(jnp.float32)[..., None] * inv_freq
    return jnp.cos(ang), jnp.sin(ang)


def _apply_rope(x, cos, sin):
    half = x.shape[-1] // 2
    x32 = x.astype(jnp.float32)
    x1, x2 = x32[..., :half], x32[..., half:]
    return jnp.concatenate([x1 * cos - x2 * sin, x2 * cos + x1 * sin], axis=-1).astype(x.dtype)


def _masked_softmax(s, mask):
    s = jnp.where(mask, s.astype(jnp.float32), NEG_INF)
    return jnp.where(mask, jax.nn.softmax(s, axis=-1), 0.0)


def _merge_blocks(o):
    o = jnp.moveaxis(o, 0, 1)
    return o.reshape((o.shape[0], o.shape[1] * o.shape[2]) + o.shape[3:])


def _dense_causal_attention(q, k, v, scale):
    S = q.shape[1]
    kpos = jnp.arange(S)

    def block(i):
        start = i * Q_BLOCK
        qb = lax.dynamic_slice_in_dim(q, start, Q_BLOCK, axis=1)
        sc = jnp.einsum('bqhd,bkhd->bhqk', qb, k) * scale
        mask = kpos[None, :] <= (start + jnp.arange(Q_BLOCK))[:, None]
        a = _masked_softmax(sc, mask)
        return jnp.einsum('bhqk,bkhd->bqhd', a.astype(v.dtype), v)

    return _merge_blocks(lax.map(block, jnp.arange(S // Q_BLOCK)))


def _diff_attention(q, k, v, cos, sin, g_q, g_k, g_sub, lam_params, layer_idx):
    B, S = q.shape[:2]
    c, s_ = cos[:, :, None, None, :], sin[:, :, None, None, :]
    q = _apply_rope(_rmsnorm(q, g_q), c, s_)
    k = _apply_rope(_rmsnorm(k, g_k), c, s_)
    lam_init = 0.8 - 0.6 * math.exp(-0.3 * layer_idx)
    lp = lam_params.astype(jnp.float32)
    lam = jnp.exp(jnp.sum(lp[0] * lp[1])) - jnp.exp(jnp.sum(lp[2] * lp[3])) + lam_init
    scale = DIFF_QK_DIM ** -0.5
    kpos = jnp.arange(S)

    def block(i):
        start = i * Q_BLOCK
        qb = lax.dynamic_slice_in_dim(q, start, Q_BLOCK, axis=1)
        sc = jnp.einsum('bqhcd,bkhcd->bhcqk', qb, k) * scale
        mask = kpos[None, :] <= (start + jnp.arange(Q_BLOCK))[:, None]
        a = _masked_softmax(sc, mask)
        w = a[:, :, 0] - lam * a[:, :, 1]
        return jnp.einsum('bhqk,bkhd->bqhd', w.astype(v.dtype), v)

    o = _merge_blocks(lax.map(block, jnp.arange(S // Q_BLOCK)))
    o = _rmsnorm(o, g_sub) * (1.0 - lam_init)
    return o.reshape(B, S, -1)


def _dsa_attention(q, k, v, iq, ik, iw, cos, sin, cos_i, sin_i, g_q, g_k):
    B, S = q.shape[:2]
    q = _apply_rope(_rmsnorm(q, g_q), cos[:, :, None, :], sin[:, :, None, :])
    k = _apply_rope(_rmsnorm(k, g_k), cos, sin)
    iq = _apply_rope(iq, cos_i[:, :, None, :], sin_i[:, :, None, :])
    ik = _apply_rope(ik, cos_i, sin_i)
    n_keep = min(DSA_TOPK, S // 4)
    scale = HEAD_DIM ** -0.5
    kpos = jnp.arange(S)
    gather_keys = jax.vmap(lambda t, ix: t[ix])

    def block(i):
        start = i * Q_BLOCK
        qpos = start + jnp.arange(Q_BLOCK)
        qb = lax.dynamic_slice_in_dim(q, start, Q_BLOCK, axis=1)
        iqb = lax.dynamic_slice_in_dim(iq, start, Q_BLOCK, axis=1)
        iwb = lax.dynamic_slice_in_dim(iw, start, Q_BLOCK, axis=1)
        rel = jax.nn.relu(jnp.einsum('bqhd,bkd->bqhk', iqb, ik).astype(jnp.float32) * IDX_DIM ** -0.5)
        score = jnp.einsum('bqh,bqhk->bqk', iwb.astype(jnp.float32) * IDX_HEADS ** -0.5, rel)
        score = jnp.where(kpos[None, None, :] <= qpos[None, :, None], score, NEG_INF)
        _, sel = lax.top_k(score, n_keep)
        ks = gather_keys(k, sel)
        vs = gather_keys(v, sel)
        sc = jnp.einsum('bqhd,bqnd->bhqn', qb, ks) * scale
        a = _masked_softmax(sc, (sel <= qpos[None, :, None])[:, None])
        return jnp.einsum('bhqn,bqnd->bqhd', a.astype(vs.dtype), vs)

    o = _merge_blocks(lax.map(block, jnp.arange(S // Q_BLOCK)))
    return o.reshape(B, S, -1)


def _mla_attention(c_q, c_kv, k_rope, cos, sin, g_cq, g_ckv, w_uq, w_uk, w_uv, g_q, g_k):
    B, S = c_q.shape[:2]
    H = HEADS_PER_GROUP
    q = (_rmsnorm(c_q, g_cq) @ w_uq).reshape(B, S, H, MLA_NOPE + MLA_ROPE)
    ckv = _rmsnorm(c_kv, g_ckv)
    k_nope = (ckv @ w_uk).reshape(B, S, H, MLA_NOPE)
    v = (ckv @ w_uv).reshape(B, S, H, MLA_V)
    k_r = jnp.broadcast_to(k_rope[:, :, None, :], (B, S, H, MLA_ROPE))
    k = jnp.concatenate([k_nope, k_r], axis=-1)
    q = _rmsnorm(q, g_q)
    k = _rmsnorm(k, g_k)
    c, s_ = cos[:, :, None, :], sin[:, :, None, :]
    q = jnp.concatenate([q[..., :MLA_NOPE], _apply_rope(q[..., MLA_NOPE:], c, s_)], axis=-1)
    k = jnp.concatenate([k[..., :MLA_NOPE], _apply_rope(k[..., MLA_NOPE:], c, s_)], axis=-1)
    o = _dense_causal_attention(q, k, v, (MLA_NOPE + MLA_ROPE) ** -0.5)
    return o.reshape(B, S, -1)


def _nsa_attention(q, kv, gate_logits, cos, sin, positions, g_q, g_k, cmp_w1, cmp_w2, cmp_pe):
    B, S = q.shape[:2]
    G, R, d = NSA_KV_HEADS, NSA_Q_PER_KV, HEAD_DIM
    scale = d ** -0.5
    c, s_ = cos[:, :, None, :], sin[:, :, None, :]
    qg = _apply_rope(_rmsnorm(q, g_q), c, s_).reshape(B, S, G, R, d)
    tpos = np.arange(S)

    n_cmp = (S - CMP_LEN) // CMP_STRIDE + 1
    starts = np.arange(n_cmp) * CMP_STRIDE
    ends = starts + CMP_LEN - 1
    blocks = kv[:, :, 0][:, starts[:, None] + np.arange(CMP_LEN)[None, :]]
    blocks = blocks + jnp.transpose(cmp_pe, (1, 0, 2))[:, :, None, :]
    flat = jnp.transpose(blocks, (0, 1, 3, 4, 2, 5)).reshape(B, n_cmp, 2, G, CMP_LEN * d)
    hid = jax.nn.gelu(jnp.einsum('bnkgf,kfh->bnkgh', flat, cmp_w1))
    comp = jnp.einsum('bnkgh,khe->bnkge', hid, cmp_w2)
    cos_c, sin_c = _rope_tables(positions[:, ends], d)
    k_cmp = _apply_rope(_rmsnorm(comp[:, :, 0], g_k[0]), cos_c[:, :, None, :], sin_c[:, :, None, :])
    v_cmp = comp[:, :, 1]
    sc = jnp.einsum('bsgrd,bjgd->bgrsj', qg, k_cmp) * scale
    p_cmp = _masked_softmax(sc, ends[None, :] <= tpos[:, None])
    o_cmp = jnp.einsum('bgrsj,bjgd->bsgrd', p_cmp.astype(v_cmp.dtype), v_cmp)

    n_sel = S // SEL_BLOCK
    n_top = min(SEL_TOPN, n_sel)
    blk = np.arange(n_sel)
    sel_start = blk * SEL_BLOCK
    cmp_to_sel = ((starts[:, None] < sel_start[None, :] + SEL_BLOCK)
                  & (starts[:, None] + CMP_LEN > sel_start[None, :])).astype(np.float32)
    imp = jnp.einsum('bgrsj,jn->bgsn', p_cmp, jnp.asarray(cmp_to_sel))
    dist = (tpos // SEL_BLOCK)[:, None] - blk[None, :]
    forced = (blk[None, :] < SEL_INIT) | ((dist >= 0) & (dist < SEL_LOCAL))
    admissible = sel_start[None, :] <= tpos[:, None]
    imp = jnp.where(forced, POS_INF, jnp.where(admissible, imp, NEG_INF))
    _, sel = lax.top_k(imp, n_top)
    k_slc = _apply_rope(_rmsnorm(kv[:, :, 1, 0], g_k[1]), c, s_)
    kb = k_slc.reshape(B, n_sel, SEL_BLOCK, G, d).transpose(0, 3, 1, 2, 4)
    vb = kv[:, :, 1, 1].reshape(B, n_sel, SEL_BLOCK, G, d).transpose(0, 3, 1, 2, 4)
    gather_blocks = jax.vmap(jax.vmap(lambda t, ix: t[ix]))
    q_t = qg.transpose(0, 2, 3, 1, 4)

    def sel_chunk(i):
        start = i * SEL_Q_BLOCK
        qc = lax.dynamic_slice_in_dim(q_t, start, SEL_Q_BLOCK, axis=3)
        ix = lax.dynamic_slice_in_dim(sel, start, SEL_Q_BLOCK, axis=2)
        ks = gather_blocks(kb, ix)
        vs = gather_blocks(vb, ix).reshape(B, G, SEL_Q_BLOCK, n_top * SEL_BLOCK, d)
        qpos = start + jnp.arange(SEL_Q_BLOCK)
        tok = ix[..., None] * SEL_BLOCK + jnp.arange(SEL_BLOCK)
        mask = (tok <= qpos[None, None, :, None, None]).reshape(B, G, 1, SEL_Q_BLOCK, n_top * SEL_BLOCK)
        sc_s = jnp.einsum('bgrqd,bgqntd->bgrqnt', qc, ks).reshape(B, G, R, SEL_Q_BLOCK, n_top * SEL_BLOCK) * scale
        a_s = _masked_softmax(sc_s, mask)
        return jnp.einsum('bgrqm,bgqmd->bqgrd', a_s.astype(vs.dtype), vs)

    o_slc = _merge_blocks(lax.map(sel_chunk, jnp.arange(S // SEL_Q_BLOCK)))

    n_qb = S // Q_BLOCK
    span = WINDOW + Q_BLOCK
    kidx = np.arange(n_qb)[:, None] * Q_BLOCK + np.arange(span)[None, :]
    pad = ((0, 0), (WINDOW, 0), (0, 0), (0, 0))
    k_win = jnp.pad(_apply_rope(_rmsnorm(kv[:, :, 2, 0], g_k[2]), c, s_), pad)[:, kidx]
    v_win = jnp.pad(kv[:, :, 2, 1], pad)[:, kidx]
    qb = qg.reshape(B, n_qb, Q_BLOCK, G, R, d)
    sc_w = jnp.einsum('bnqgrd,bnkgd->bngrqk', qb, k_win) * scale
    spos = (kidx - WINDOW)[:, None, :]
    tq = (np.arange(n_qb)[:, None] * Q_BLOCK + np.arange(Q_BLOCK)[None, :])[:, :, None]
    wmask = (spos <= tq) & (spos > tq - WINDOW) & (spos >= 0)
    a_w = _masked_softmax(sc_w, wmask[None, :, None, None])
    o_win = jnp.einsum('bngrqk,bnkgd->bnqgrd', a_w.astype(v_win.dtype), v_win).reshape(B, S, G, R, d)

    g = jax.nn.sigmoid(gate_logits.astype(jnp.float32)).reshape(B, S, G, R, 3)
    o = g[..., 0:1] * o_cmp + g[..., 1:2] * o_slc + g[..., 2:3] * o_win
    return o.astype(q.dtype).reshape(B, S, -1)


_SEG_ORDER = ("a_q", "a_k", "a_v", "b_q", "b_iq", "c_q", "d_q", "d_kv", "c_kv", "b_k", "b_v")
_SEG_NAMES = ("a_q", "a_k", "a_v", "b_q", "b_k", "b_v", "b_iq", "b_ik", "b_iw", "c_q", "c_kv", "c_kr", "d_q", "d_kv", "d_g")
_ORIG_START = dict(zip(_SEG_NAMES, [int(o) for o in np.cumsum((0,) + IN_SIZES[:-1])]))
_ORIG_SIZE = dict(zip(_SEG_NAMES, IN_SIZES))
_GATES_PER_KV_GROUP = 3 * NSA_Q_PER_KV


def _build_in_layout():
    col = {}
    perm = []
    for name in _SEG_ORDER:
        col[name] = len(perm)
        perm.extend(range(_ORIG_START[name], _ORIG_START[name] + _ORIG_SIZE[name]))
    zero = IN_WIDTH
    col["narrow"] = len(perm)
    blk0 = list(range(_ORIG_START["b_ik"], _ORIG_START["b_ik"] + IDX_DIM))
    blk0 += list(range(_ORIG_START["d_g"], _ORIG_START["d_g"] + _GATES_PER_KV_GROUP))
    blk0 += list(range(_ORIG_START["b_iw"], _ORIG_START["b_iw"] + IDX_HEADS))
    blk0 += [zero] * (LANE - len(blk0))
    blk1 = list(range(_ORIG_START["c_kr"], _ORIG_START["c_kr"] + MLA_ROPE))
    blk1 += list(range(_ORIG_START["d_g"] + _GATES_PER_KV_GROUP, _ORIG_START["d_g"] + 2 * _GATES_PER_KV_GROUP))
    blk1 += [zero] * (LANE - len(blk1))
    perm.extend(blk0 + blk1)
    width = _round_up(len(perm), 512)
    perm.extend([zero] * (width - len(perm)))
    return col, np.asarray(perm, np.int32), width


_COL, _IN_PERM, IN_WIDTH_PAD = _build_in_layout()
_GATE_LANE0 = IDX_DIM
_IW_LANE0 = IDX_DIM + _GATES_PER_KV_GROUP


def _rope_lane_tables(cos, sin):
    return jnp.concatenate([cos, cos], axis=-1), jnp.concatenate([-sin, sin], axis=-1)


def _lane_tables(positions, dim):
    cc, ss = _rope_lane_tables(*_rope_tables(positions, dim))
    return tuple(jnp.tile(t.reshape(-1, dim), (1, LANE // dim)) for t in (cc, ss))


def _seg64_sum(y):
    r = lax.broadcasted_iota(jnp.int32, (LANE, LANE), 0) // 64
    c = lax.broadcasted_iota(jnp.int32, (LANE, LANE), 1) // 64
    bd = jnp.where(r == c, 1.0, 0.0).astype(MXU_DTYPE)
    hi = y.astype(MXU_DTYPE)
    lo = (y - hi.astype(jnp.float32)).astype(MXU_DTYPE)
    return (jnp.dot(hi, bd, preferred_element_type=jnp.float32)
            + jnp.dot(lo, bd, preferred_element_type=jnp.float32))


def _prep_kernel(x_ref, g_ref, cc_ref, ss_ref, o_ref, *, seg, norm, scale):
    x = x_ref[...]
    if norm:
        if seg == LANE:
            ms = jnp.mean(x * x, axis=-1, keepdims=True)
        else:
            ms = _seg64_sum(x * x) * (1.0 / seg)
        x = x * lax.rsqrt(ms + EPS) * g_ref[0]
    if seg == LANE:
        partner = pltpu.roll(x, LANE // 2, axis=1)
    else:
        lane = lax.broadcasted_iota(jnp.int32, x.shape, 1)
        partner = jnp.where((lane % seg) < seg // 2, pltpu.roll(x, LANE - seg // 2, axis=1),
                            pltpu.roll(x, seg // 2, axis=1))
    y = x * cc_ref[...] + partner * ss_ref[...]
    if scale != 1.0:
        y = y * scale
    o_ref[...] = y.astype(o_ref.dtype)


def _prep_heads(proj, col0, n_blocks, gains, cc, ss, *, seg, norm, scale=1.0, tm=1024):
    m = proj.shape[0]
    tm = min(tm, m)
    c0 = col0 // LANE
    n_gain = gains.shape[0]
    gidx = (lambda i, h: (h, 0, 0)) if n_gain > 1 else (lambda i, h: (0, 0, 0))
    return pl.pallas_call(
        functools.partial(_prep_kernel, seg=seg, norm=norm, scale=scale),
        grid=(m // tm, n_blocks),
        in_specs=[pl.BlockSpec((tm, LANE), lambda i, h: (i, c0 + h)),
                  pl.BlockSpec((1, 1, LANE), gidx),
                  pl.BlockSpec((tm, LANE), lambda i, h: (i, 0)),
                  pl.BlockSpec((tm, LANE), lambda i, h: (i, 0))],
        out_specs=pl.BlockSpec((tm, LANE), lambda i, h: (i, h)),
        out_shape=jax.ShapeDtypeStruct((m, LANE * n_blocks), MXU_DTYPE),
        compiler_params=_compiler_params(("parallel", "parallel")),
        name="prep_heads",
    )(proj, gains.reshape(n_gain, 1, LANE), cc, ss)


def _dot_nt(a, b):
    return lax.dot_general(a, b, (((1,), (1,)), ((), ())), preferred_element_type=jnp.float32)


def _softmax_step(s, mask, m, l, acc, v):
    s = jnp.where(mask, s, NEG_INF)
    m_new = jnp.maximum(m, jnp.max(s, axis=-1, keepdims=True))
    alpha = jnp.exp(m - m_new)
    p = jnp.where(mask, jnp.exp(s - m_new), 0.0)
    l = alpha * l + jnp.sum(p, axis=-1, keepdims=True)
    acc = alpha * acc + jnp.dot(p.astype(MXU_DTYPE), v, preferred_element_type=jnp.float32)
    return m_new, l, acc


def _softmax_finish(l, acc):
    return acc * (1.0 / jnp.maximum(l, 1e-30))


def _nsa_cmp_kernel(x_ref, pe_ref, w1_ref, w2_ref, gk_ref, cc_ref, ss_ref, kc_ref, vc_ref):
    nch = x_ref.shape[2]
    for kv in range(2):
        x = x_ref[kv, 0]
        lo = jnp.dot((x + pe_ref[kv, 0:1, :]).astype(MXU_DTYPE), w1_ref[kv, 0], preferred_element_type=jnp.float32)
        hi = jnp.dot((x + pe_ref[kv, 1:2, :]).astype(MXU_DTYPE), w1_ref[kv, 1], preferred_element_type=jnp.float32)
        hid = lo + pltpu.roll(hi, nch - 1, axis=0)
        comp = jnp.dot(jax.nn.gelu(hid).astype(MXU_DTYPE), w2_ref[kv], preferred_element_type=jnp.float32)
        if kv == 0:
            ms = jnp.mean(comp * comp, axis=-1, keepdims=True)
            y = comp * lax.rsqrt(ms + EPS) * gk_ref[...]
            y = y * cc_ref[0] + pltpu.roll(y, LANE // 2, axis=1) * ss_ref[0]
            kc_ref[0] = y.astype(kc_ref.dtype)
        else:
            vc_ref[0] = comp.astype(vc_ref.dtype)


def _nsa_compress(craw, pe2, w1, w2, gk, cc_c, ss_c, *, n_batch):
    _, gb, nch, width = craw.shape
    return pl.pallas_call(
        _nsa_cmp_kernel,
        grid=(gb,),
        in_specs=[pl.BlockSpec((2, 1, nch, width), lambda n: (0, n, 0, 0)),
                  pl.BlockSpec((2, 2, width), lambda n: (0, 0, 0)),
                  pl.BlockSpec((2, 2, width, CMP_HIDDEN), lambda n: (0, 0, 0, 0)),
                  pl.BlockSpec((2, CMP_HIDDEN, HEAD_DIM), lambda n: (0, 0, 0)),
                  pl.BlockSpec((1, HEAD_DIM), lambda n: (0, 0)),
                  pl.BlockSpec((1, nch, HEAD_DIM), lambda n: (n % n_batch, 0, 0)),
                  pl.BlockSpec((1, nch, HEAD_DIM), lambda n: (n % n_batch, 0, 0))],
        out_specs=[pl.BlockSpec((1, nch, HEAD_DIM), lambda n: (n, 0, 0)),
                   pl.BlockSpec((1, nch, HEAD_DIM), lambda n: (n, 0, 0))],
        out_shape=[jax.ShapeDtypeStruct((gb, nch, HEAD_DIM), MXU_DTYPE)] * 2,
        compiler_params=_compiler_params(("parallel",)),
        name="nsa_compress",
    )(craw, pe2, w1, w2, gk, cc_c, ss_c)


def _nsa_attn_kernel(q_ref, kc_ref, vc_ref, ks_ref, vs_ref, kw_ref, vw_ref, gate_ref, c2s_ref, exp_ref, o_ref, *,
                     tq, n_sel, n_top):
    i = pl.program_id(2)
    R = NSA_Q_PER_KV
    rows = R * tq
    nch = kc_ref.shape[1]
    bf16 = MXU_DTYPE
    q = jnp.concatenate([q_ref[:, r * HEAD_DIM:(r + 1) * HEAD_DIM] for r in range(R)], axis=0)
    t_row = i * tq + lax.broadcasted_iota(jnp.int32, (rows, 1), 0) % tq

    s = _dot_nt(q, kc_ref[0])
    blk_end = lax.broadcasted_iota(jnp.int32, (1, nch), 1) * CMP_STRIDE + (CMP_LEN - 1)
    mask = blk_end <= t_row
    zero = jnp.zeros((rows, 1), jnp.float32)
    _, l, pv = _softmax_step(s, mask, zero + NEG_INF, zero, jnp.zeros((rows, HEAD_DIM), jnp.float32), vc_ref[0])
    inv = 1.0 / jnp.maximum(l, 1e-30)
    o_cmp = pv * inv
    s = jnp.where(mask, s, NEG_INF)
    p_cmp = jnp.where(mask, jnp.exp(s - jnp.max(s, axis=-1, keepdims=True)), 0.0) * inv

    imp_rows = jnp.dot(p_cmp.astype(bf16), c2s_ref[...], preferred_element_type=jnp.float32)
    imp = imp_rows[0:tq]
    for r in range(1, R):
        imp = imp + imp_rows[r * tq:(r + 1) * tq]
    imp_t = imp.T[0:n_sel]
    blk = lax.broadcasted_iota(jnp.int32, (n_sel, tq), 0)
    t_lane = i * tq + lax.broadcasted_iota(jnp.int32, (n_sel, tq), 1)
    dist = lax.shift_right_logical(t_lane, int(math.log2(SEL_BLOCK))) - blk
    forced = (blk < SEL_INIT) | ((dist >= 0) & (dist < SEL_LOCAL))
    val = jnp.where(forced, POS_INF, jnp.where(blk * SEL_BLOCK <= t_lane, imp_t, NEG_INF))
    rank = jnp.zeros((n_sel, tq), jnp.float32)
    for mblk in range(n_sel):
        vm = val[mblk:mblk + 1, :]
        rank = rank + jnp.where((vm > val) | ((vm == val) & (blk > mblk)), 1.0, 0.0)
    sel_t = jnp.where(rank < n_top, 1.0, 0.0)
    sel_t = jnp.concatenate([sel_t, jnp.zeros((LANE - n_sel, tq), jnp.float32)], axis=0) if n_sel < LANE else sel_t
    sel_q = sel_t.T.astype(bf16)
    sel_rows = jnp.concatenate([sel_q] * R, axis=0)

    key_lane = lax.broadcasted_iota(jnp.int32, (1, tq), 1)

    def slc_body(j, carry):
        off = pl.multiple_of(j * tq, tq)
        k = ks_ref[pl.ds(off, tq), :]
        v = vs_ref[pl.ds(off, tq), :].astype(bf16)
        picked = jnp.dot(sel_rows, exp_ref[j], preferred_element_type=jnp.float32)
        msk = (picked > 0.5) & (off + key_lane <= t_row)
        return _softmax_step(_dot_nt(q, k), msk, *carry, v)

    init = (zero + NEG_INF, zero, jnp.zeros((rows, HEAD_DIM), jnp.float32))
    _, l, acc = lax.fori_loop(0, i + 1, slc_body, init)
    o_slc = _softmax_finish(l, acc)

    def win_body(j, carry):
        off = pl.multiple_of(j * tq, tq)
        k = kw_ref[pl.ds(off, tq), :]
        v = vw_ref[pl.ds(off, tq), :].astype(bf16)
        key = off + key_lane
        msk = (key <= t_row) & (key > t_row - WINDOW)
        return _softmax_step(_dot_nt(q, k), msk, *carry, v)

    _, l, acc = lax.fori_loop(jnp.maximum(i - WINDOW // tq, 0), i + 1, win_body, init)
    o_win = _softmax_finish(l, acc)

    gate = jax.nn.sigmoid(gate_ref[...])
    for r in range(R):
        c = _GATE_LANE0 + 3 * r
        rs = slice(r * tq, (r + 1) * tq)
        o_ref[:, r * HEAD_DIM:(r + 1) * HEAD_DIM] = (gate[:, c:c + 1] * o_cmp[rs]
                                                     + gate[:, c + 1:c + 2] * o_slc[rs]
                                                     + gate[:, c + 2:c + 3] * o_win[rs])


def _nsa_mixer(proj, B, S, cc_h, ss_h, positions, g_q, g_k, cmp_w1, cmp_w2, cmp_pe):
    M = B * S
    G, R, d = NSA_KV_HEADS, NSA_Q_PER_KV, HEAD_DIM
    bf16 = MXU_DTYPE
    tq = Q_BLOCK
    nq = S // tq
    nch = S // CMP_STRIDE
    n_sel = S // SEL_BLOCK
    n_top = min(SEL_TOPN, n_sel)
    kv0 = _COL["d_kv"]

    qn = _prep_heads(proj, _COL["d_q"], HEADS_PER_GROUP, g_q.reshape(1, d), cc_h, ss_h,
                     seg=LANE, norm=True, scale=d ** -0.5)
    k_slc = _prep_heads(proj, kv0 + (1 * 2 + 0) * G * d, G, g_k[1].reshape(1, d), cc_h, ss_h, seg=LANE, norm=True)
    k_win = _prep_heads(proj, kv0 + (2 * 2 + 0) * G * d, G, g_k[2].reshape(1, d), cc_h, ss_h, seg=LANE, norm=True)

    craw = proj[:, kv0:kv0 + 2 * G * d].reshape(B, S, 2, G, d).transpose(2, 3, 0, 1, 4)
    craw = craw.reshape(2, G * B, nch, CMP_STRIDE * d)
    ends = np.minimum(np.arange(nch) * CMP_STRIDE + CMP_LEN - 1, S - 1)
    cc_c, ss_c = _rope_lane_tables(*_rope_tables(positions[:, ends], d))
    pe2 = cmp_pe.reshape(2, 2, CMP_STRIDE * d)
    w1 = cmp_w1.astype(bf16).reshape(2, 2, CMP_STRIDE * d, CMP_HIDDEN)
    k_cmp, v_cmp = _nsa_compress(craw, pe2, w1, cmp_w2.astype(bf16), g_k[0].reshape(1, d), cc_c, ss_c, n_batch=B)

    starts = np.arange(nch) * CMP_STRIDE
    sel_start = np.arange(LANE) * SEL_BLOCK
    c2s = ((starts[:, None] < sel_start[None, :] + SEL_BLOCK) & (starts[:, None] + CMP_LEN > sel_start[None, :])
           & (np.arange(nch)[:, None] < nch - 1) & (np.arange(LANE)[None, :] < n_sel))
    expand = (np.arange(LANE)[None, :, None] == (np.arange(nq)[:, None, None] * tq + np.arange(tq)[None, None, :]) // SEL_BLOCK)
    c2s = jnp.asarray(c2s, bf16)
    expand = jnp.asarray(expand, bf16)

    narrow = _COL["narrow"] // LANE
    col_vs = (kv0 + (1 * 2 + 1) * G * d) // LANE
    col_vw = (kv0 + (2 * 2 + 1) * G * d) // LANE
    return pl.pallas_call(
        functools.partial(_nsa_attn_kernel, tq=tq, n_sel=n_sel, n_top=n_top),
        grid=(B, G, nq),
        in_specs=[pl.BlockSpec((tq, R * d), lambda b, g, i: (b * nq + i, g)),
                  pl.BlockSpec((1, nch, d), lambda b, g, i: (g * B + b, 0, 0)),
                  pl.BlockSpec((1, nch, d), lambda b, g, i: (g * B + b, 0, 0)),
                  pl.BlockSpec((S, d), lambda b, g, i: (b, g)),
                  pl.BlockSpec((S, d), lambda b, g, i: (b, col_vs + g)),
                  pl.BlockSpec((S, d), lambda b, g, i: (b, g)),
                  pl.BlockSpec((S, d), lambda b, g, i: (b, col_vw + g)),
                  pl.BlockSpec((tq, LANE), lambda b, g, i: (b * nq + i, narrow + g)),
                  pl.BlockSpec((nch, LANE), lambda b, g, i: (0, 0)),
                  pl.BlockSpec((nq, LANE, tq), lambda b, g, i: (0, 0, 0))],
        out_specs=pl.BlockSpec((tq, R * d), lambda b, g, i: (b * nq + i, g)),
        out_shape=jax.ShapeDtypeStruct((M, GROUP_WIDTH), jnp.float32),
        compiler_params=_compiler_params(("parallel", "parallel", "arbitrary")),
        name="nsa_attention",
    )(qn, k_cmp, v_cmp, k_slc, proj, k_win, proj, proj, c2s, expand)


_INT_MIN = -2 ** 31


def _sortable_key(x):
    b = lax.bitcast_convert_type(x + 0.0, jnp.int32)
    return jnp.where(b >= 0, b, b ^ 0x7FFFFFFF)


def _dsa_kernel(q_ref, k_ref, v_ref, iq_ref, ik_ref, nar_ref, o_ref, key_ref, msk_ref, *, tq, n_keep, idx_bits):
    i = pl.program_id(1)
    H = HEADS_PER_GROUP
    bf16 = MXU_DTYPE
    t_col = i * tq + lax.broadcasted_iota(jnp.int32, (tq, 1), 0)
    key_lane = lax.broadcasted_iota(jnp.int32, (1, tq), 1)
    lane = lax.broadcasted_iota(jnp.int32, (tq, LANE), 1)
    iw = nar_ref[...] * (IDX_HEADS ** -0.5)
    iq_pairs = [iq_ref[:, pr * LANE:(pr + 1) * LANE] for pr in range(IDX_HEADS // 2)]

    def score_body(j, _):
        off = pl.multiple_of(j * tq, tq)
        ik = ik_ref[pl.ds(off, tq), :]
        ik_lo = jnp.where(lane < IDX_DIM, ik, jnp.zeros_like(ik))
        ik_hi = jnp.where(lane < IDX_DIM, jnp.zeros_like(ik), pltpu.roll(ik.astype(jnp.float32), IDX_DIM, axis=1).astype(bf16))
        sc = jnp.zeros((tq, tq), jnp.float32)
        for hd in range(IDX_HEADS):
            rel = jnp.maximum(_dot_nt(iq_pairs[hd // 2], ik_lo if hd % 2 == 0 else ik_hi), 0.0)
            sc = sc + iw[:, _IW_LANE0 + hd:_IW_LANE0 + hd + 1] * rel
        sc = jnp.where(off + key_lane <= t_col, sc, NEG_INF)
        key_ref[j] = _sortable_key(sc)
        return 0

    lax.fori_loop(0, i + 1, score_body, 0)

    def count(pred):
        def body(j, part):
            return part + jnp.where(pred(key_ref[j], j * tq), 1.0, 0.0)
        part = lax.fori_loop(0, i + 1, body, jnp.zeros((tq, tq), jnp.float32))
        return jnp.sum(part, axis=-1, keepdims=True)

    c0 = count(lambda kk, off: kk >= 0)
    thr = jnp.where(c0 >= n_keep, 0, _INT_MIN).astype(jnp.int32)

    def thr_body(it, thr):
        cand = thr | lax.shift_left(jnp.int32(1), 30 - it)
        c = count(lambda kk, off: kk >= cand)
        return jnp.where(c >= n_keep, cand, thr)

    thr = lax.fori_loop(0, 31, thr_body, thr)

    need = n_keep - count(lambda kk, off: kk > thr)

    def cut_body(it, cut):
        cand = cut | lax.shift_left(jnp.int32(1), idx_bits - 1 - it)
        c = count(lambda kk, off: (kk == thr) & (off + key_lane < cand))
        return jnp.where(c < need, cand, cut)

    cut = lax.fori_loop(0, idx_bits, cut_body, jnp.zeros((tq, 1), jnp.int32))

    def mask_body(j, _):
        kk = key_ref[j]
        kidx = j * tq + key_lane
        sel = (kk > thr) | ((kk == thr) & (kidx <= cut))
        msk_ref[j] = jnp.where(sel & (kidx <= t_col), 1.0, 0.0)
        return 0

    lax.fori_loop(0, i + 1, mask_body, 0)

    hpp = 4
    rows = hpp * tq
    zero = jnp.zeros((rows, 1), jnp.float32)
    for hc in range(H // hpp):
        q = jnp.concatenate([q_ref[:, (hc * hpp + r) * HEAD_DIM:(hc * hpp + r + 1) * HEAD_DIM] for r in range(hpp)], axis=0)

        def att_body(j, carry):
            off = pl.multiple_of(j * tq, tq)
            k = k_ref[pl.ds(off, tq), :]
            v = v_ref[pl.ds(off, tq), :].astype(bf16)
            m1 = msk_ref[j] > 0.5
            msk = jnp.concatenate([m1] * hpp, axis=0)
            return _softmax_step(_dot_nt(q, k), msk, *carry, v)

        init = (zero + NEG_INF, zero, jnp.zeros((rows, HEAD_DIM), jnp.float32))
        _, l, acc = lax.fori_loop(0, i + 1, att_body, init)
        o = _softmax_finish(l, acc)
        for r in range(hpp):
            hh = hc * hpp + r
            o_ref[:, hh * HEAD_DIM:(hh + 1) * HEAD_DIM] = o[r * tq:(r + 1) * tq]


def _dsa_mixer(proj, B, S, cc_h, ss_h, cc_i, ss_i, g_q, g_k):
    M = B * S
    d = HEAD_DIM
    tq = Q_BLOCK
    nq = S // tq
    n_keep = min(DSA_TOPK, S // 4)
    qn = _prep_heads(proj, _COL["b_q"], HEADS_PER_GROUP, g_q.reshape(1, d), cc_h, ss_h, seg=LANE, norm=True, scale=d ** -0.5)
    kn = _prep_heads(proj, _COL["b_k"], 1, g_k.reshape(1, d), cc_h, ss_h, seg=LANE, norm=True)
    ones = jnp.ones((1, LANE), jnp.float32)
    iqn = _prep_heads(proj, _COL["b_iq"], IDX_HEADS * IDX_DIM // LANE, ones, cc_i, ss_i, seg=IDX_DIM, norm=False,
                      scale=IDX_DIM ** -0.5)
    ikn = _prep_heads(proj, _COL["narrow"], 1, ones, cc_i, ss_i, seg=IDX_DIM, norm=False)
    narrow = _COL["narrow"] // LANE
    col_v = _COL["b_v"] // LANE
    return pl.pallas_call(
        functools.partial(_dsa_kernel, tq=tq, n_keep=n_keep, idx_bits=int(math.log2(S))),
        grid=(B, nq),
        in_specs=[pl.BlockSpec((tq, GROUP_WIDTH), lambda b, i: (b * nq + i, 0)),
                  pl.BlockSpec((S, d), lambda b, i: (b, 0)),
                  pl.BlockSpec((S, d), lambda b, i: (b, col_v)),
                  pl.BlockSpec((tq, IDX_HEADS * IDX_DIM), lambda b, i: (b * nq + i, 0)),
                  pl.BlockSpec((S, LANE), lambda b, i: (b, 0)),
                  pl.BlockSpec((tq, LANE), lambda b, i: (b * nq + i, narrow))],
        out_specs=pl.BlockSpec((tq, GROUP_WIDTH), lambda b, i: (b * nq + i, 0)),
        out_shape=jax.ShapeDtypeStruct((M, GROUP_WIDTH), jnp.float32),
        scratch_shapes=[pltpu.VMEM((nq, tq, tq), jnp.int32), pltpu.VMEM((nq, tq, tq), jnp.float32)],
        compiler_params=_compiler_params(("parallel", "arbitrary")),
        name="dsa_attention",
    )(qn, kn, proj, iqn, ikn, proj)


def _causal_flash(q, k_ref, v_ref, t_row, n_tiles, tk):
    rows = q.shape[0]
    key_lane = lax.broadcasted_iota(jnp.int32, (1, tk), 1)
    zero = jnp.zeros((rows, 1), jnp.float32)

    def body(j, carry):
        off = pl.multiple_of(j * tk, tk)
        k = k_ref[pl.ds(off, tk), :]
        v = v_ref[pl.ds(off, tk), :].astype(MXU_DTYPE)
        return _softmax_step(_dot_nt(q, k), off + key_lane <= t_row, *carry, v)

    init = (zero + NEG_INF, zero, jnp.zeros((rows, v_ref.shape[-1]), jnp.float32))
    _, l, acc = lax.fori_loop(0, n_tiles, body, init)
    return _softmax_finish(l, acc)


def _diff_attn_kernel(q_ref, k_ref, v_ref, lam_ref, g_ref, o_ref, *, tq, tk, out_scale):
    i = pl.program_id(2)
    q = q_ref[...]
    lane = lax.broadcasted_iota(jnp.int32, q.shape, 1)
    zeros = jnp.zeros_like(q)
    q2 = jnp.concatenate([jnp.where(lane < DIFF_QK_DIM, q, zeros), jnp.where(lane < DIFF_QK_DIM, zeros, q)], axis=0)
    t_row = i * tq + lax.broadcasted_iota(jnp.int32, (2 * tq, 1), 0) % tq
    o2 = _causal_flash(q2, k_ref, v_ref, t_row, (i + 1) * (tq // tk), tk)
    o = o2[0:tq] - lam_ref[...] * o2[tq:2 * tq]
    ms = jnp.mean(o * o, axis=-1, keepdims=True)
    o_ref[...] = o * lax.rsqrt(ms + EPS) * g_ref[...] * out_scale


def _diff_mixer(proj, B, S, cc_d, ss_d, g_q, g_k, g_sub, lam_params, layer_idx, *, tq=128, tk=128):
    M = B * S
    d = HEAD_DIM
    nq = S // tq
    reps = LANE // DIFF_QK_DIM
    qn = _prep_heads(proj, _COL["a_q"], HEADS_PER_GROUP, jnp.tile(g_q, reps).reshape(1, d), cc_d, ss_d,
                     seg=DIFF_QK_DIM, norm=True, scale=DIFF_QK_DIM ** -0.5)
    kn = _prep_heads(proj, _COL["a_k"], HEADS_PER_GROUP, jnp.tile(g_k, reps).reshape(1, d), cc_d, ss_d,
                     seg=DIFF_QK_DIM, norm=True)
    lam_init = 0.8 - 0.6 * math.exp(-0.3 * layer_idx)
    lp = lam_params.astype(jnp.float32)
    lam = jnp.exp(jnp.sum(lp[0] * lp[1])) - jnp.exp(jnp.sum(lp[2] * lp[3])) + lam_init
    col_v = _COL["a_v"] // LANE
    return pl.pallas_call(
        functools.partial(_diff_attn_kernel, tq=tq, tk=tk, out_scale=1.0 - lam_init),
        grid=(B, HEADS_PER_GROUP, nq),
        in_specs=[pl.BlockSpec((tq, d), lambda b, h, i: (b * nq + i, h)),
                  pl.BlockSpec((S, d), lambda b, h, i: (b, h)),
                  pl.BlockSpec((S, d), lambda b, h, i: (b, col_v + h)),
                  pl.BlockSpec((1, d), lambda b, h, i: (0, 0)),
                  pl.BlockSpec((1, d), lambda b, h, i: (0, 0))],
        out_specs=pl.BlockSpec((tq, d), lambda b, h, i: (b * nq + i, h)),
        out_shape=jax.ShapeDtypeStruct((M, GROUP_WIDTH), jnp.float32),
        compiler_params=_compiler_params(("parallel", "parallel", "arbitrary")),
        name="diff_attention",
    )(qn, kn, proj, jnp.full((1, d), lam, jnp.float32), g_sub.reshape(1, d))


MLA_QK_PAD = 2 * LANE


def _rmsnorm_cols_kernel(x_ref, g_ref, o_ref):
    x = x_ref[...]
    ms = jnp.mean(x * x, axis=-1, keepdims=True)
    o_ref[...] = (x * lax.rsqrt(ms + EPS) * g_ref[...]).astype(o_ref.dtype)


def _rmsnorm_cols(proj, col0, width, g, *, tm=512):
    m = proj.shape[0]
    tm = min(tm, m)
    cb = col0 // width
    assert cb * width == col0
    return pl.pallas_call(
        _rmsnorm_cols_kernel,
        grid=(m // tm,),
        in_specs=[pl.BlockSpec((tm, width), lambda i: (i, cb)), pl.BlockSpec((1, width), lambda i: (0, 0))],
        out_specs=pl.BlockSpec((tm, width), lambda i: (i, 0)),
        out_shape=jax.ShapeDtypeStruct((m, width), MXU_DTYPE),
        compiler_params=_compiler_params(("parallel",)),
        name="rmsnorm_cols",
    )(proj, g.reshape(1, width))


def _mla_prep_kernel(a_ref, b_ref, g_ref, cc_ref, ss_ref, o_ref, *, scale):
    a = a_ref[...]
    lane = lax.broadcasted_iota(jnp.int32, a.shape, 1)
    b = jnp.where(lane < MLA_ROPE, b_ref[...], 0.0)
    ms = (jnp.sum(a * a, axis=-1, keepdims=True) + jnp.sum(b * b, axis=-1, keepdims=True)) * (1.0 / (MLA_NOPE + MLA_ROPE))
    r = lax.rsqrt(ms + EPS)
    ya = a * r * g_ref[:, 0:LANE]
    yb = b * r * g_ref[:, LANE:2 * LANE]
    partner = jnp.where((lane % MLA_ROPE) < MLA_ROPE // 2, pltpu.roll(yb, LANE - MLA_ROPE // 2, axis=1),
                        pltpu.roll(yb, MLA_ROPE // 2, axis=1))
    yb = yb * cc_ref[...] + partner * ss_ref[...]
    o_ref[:, 0:LANE] = (ya * scale).astype(o_ref.dtype)
    o_ref[:, LANE:2 * LANE] = (yb * scale).astype(o_ref.dtype)


def _mla_prep(a_arr, a_col0, a_stride, b_arr, b_col0, b_stride, gain, cc, ss, *, scale=1.0, tm=1024):
    m = a_arr.shape[0]
    tm = min(tm, m)
    ac, as_, bc, bs = a_col0 // LANE, a_stride // LANE, b_col0 // LANE, b_stride // LANE
    g2 = jnp.concatenate([gain, jnp.zeros((MLA_QK_PAD - gain.shape[0],), gain.dtype)]).reshape(1, MLA_QK_PAD)
    return pl.pallas_call(
        functools.partial(_mla_prep_kernel, scale=scale),
        grid=(m // tm, HEADS_PER_GROUP),
        in_specs=[pl.BlockSpec((tm, LANE), lambda i, h: (i, ac + h * as_)),
                  pl.BlockSpec((tm, LANE), lambda i, h: (i, bc + h * bs)),
                  pl.BlockSpec((1, MLA_QK_PAD), lambda i, h: (0, 0)),
                  pl.BlockSpec((tm, LANE), lambda i, h: (i, 0)),
                  pl.BlockSpec((tm, LANE), lambda i, h: (i, 0))],
        out_specs=pl.BlockSpec((tm, MLA_QK_PAD), lambda i, h: (i, h)),
        out_shape=jax.ShapeDtypeStruct((m, HEADS_PER_GROUP * MLA_QK_PAD), MXU_DTYPE),
        compiler_params=_compiler_params(("parallel", "parallel")),
        name="mla_prep",
    )(a_arr, b_arr, g2, cc, ss)


def _mla_attn_kernel(q_ref, k_ref, v_ref, o_ref, *, tq, tk):
    i = pl.program_id(2)
    t_row = i * tq + lax.broadcasted_iota(jnp.int32, (tq, 1), 0)
    o_ref[...] = _causal_flash(q_ref[...], k_ref, v_ref, t_row, (i + 1) * (tq // tk), tk)


def _mla_mixer(proj, B, S, cc_m, ss_m, g_cq, g_ckv, w_uq, w_uk, w_uv, g_q, g_k, *, tq=256, tk=256):
    M = B * S
    H, d = HEADS_PER_GROUP, HEAD_DIM
    tq = min(tq, S)
    tk = min(tk, tq)
    nq = S // tq
    dqk = MLA_NOPE + MLA_ROPE
    w_q = jnp.pad(w_uq.astype(MXU_DTYPE).reshape(MLA_Q_RANK, H, dqk), ((0, 0), (0, 0), (0, MLA_QK_PAD - dqk)))
    w_q = w_q.reshape(MLA_Q_RANK, H * MLA_QK_PAD)
    w_kv = jnp.concatenate([w_uk, w_uv], axis=1).astype(MXU_DTYPE)
    cq = _rmsnorm_cols(proj, _COL["c_q"], MLA_Q_RANK, g_cq)
    ckv = _rmsnorm_cols(proj, _COL["c_kv"], MLA_KV_RANK, g_ckv)
    q_up = _matmul(cq, w_q, tm=min(1024, M), tn=1024, tk=MLA_Q_RANK)
    kv_up = _matmul(ckv, w_kv, tm=min(1024, M), tn=1024, tk=MLA_KV_RANK)
    qn = _mla_prep(q_up, 0, MLA_QK_PAD, q_up, LANE, MLA_QK_PAD, g_q, cc_m, ss_m, scale=dqk ** -0.5)
    kn = _mla_prep(kv_up, 0, LANE, proj, _COL["narrow"] + LANE, 0, g_k, cc_m, ss_m)
    return pl.pallas_call(
        functools.partial(_mla_attn_kernel, tq=tq, tk=tk),
        grid=(B, H, nq),
        in_specs=[pl.BlockSpec((tq, MLA_QK_PAD), lambda b, h, i: (b * nq + i, h)),
                  pl.BlockSpec((S, MLA_QK_PAD), lambda b, h, i: (b, h)),
                  pl.BlockSpec((S, d), lambda b, h, i: (b, H + h))],
        out_specs=pl.BlockSpec((tq, d), lambda b, h, i: (b * nq + i, h)),
        out_shape=jax.ShapeDtypeStruct((M, GROUP_WIDTH), jnp.float32),
        compiler_params=_compiler_params(("parallel", "parallel", "arbitrary")),
        name="mla_attention",
    )(qn, kn, kv_up)


def _mix_kernel(a_ref, b_ref, c_ref, d_ref, g_ref, o_ref):
    o_ref[:, 0:GROUP_WIDTH] = a_ref[...].astype(o_ref.dtype)
    for n, ref in enumerate((b_ref, c_ref, d_ref)):
        x = ref[...]
        ms = jnp.mean(x * x, axis=-1, keepdims=True)
        o_ref[:, (n + 1) * GROUP_WIDTH:(n + 2) * GROUP_WIDTH] = (
            x * lax.rsqrt(ms + EPS) * g_ref[n:n + 1, :]).astype(o_ref.dtype)


def _mix_groups(o_a, o_b, o_c, o_d, g_out, *, tm=512):
    m = o_a.shape[0]
    tm = min(tm, m)
    spec = pl.BlockSpec((tm, GROUP_WIDTH), lambda i: (i, 0))
    return pl.pallas_call(
        _mix_kernel,
        grid=(m // tm,),
        in_specs=[spec, spec, spec, spec, pl.BlockSpec((3, GROUP_WIDTH), lambda i: (0, 0))],
        out_specs=pl.BlockSpec((tm, MIX_WIDTH), lambda i: (i, 0)),
        out_shape=jax.ShapeDtypeStruct((m, MIX_WIDTH), MXU_DTYPE),
        compiler_params=_compiler_params(("parallel",)),
        name="mix_groups",
    )(o_a, o_b, o_c, o_d, g_out)


D_FF_PAD = _round_up(D_FF, 1024)


def _seg(proj, name):
    return proj[:, _COL[name]:_COL[name] + _ORIG_SIZE[name]]


def _pad_cols(w, n):
    return jnp.pad(w, ((0, 0), (0, n - w.shape[1])))


def _pad_rows(w, n):
    return jnp.pad(w, ((0, n - w.shape[0]), (0, 0)))


def kernel(x, p, positions, w_in, w_out, g_mix, g_ffn, w_gate, w_up, w_down, w_ple_proj, w_ple_gate, g_ple,
           g_group_out, diff_g_q, diff_g_k, diff_g_sub, diff_lambda, dsa_g_q, dsa_g_k, mla_g_cq, mla_g_ckv,
           mla_w_uq, mla_w_uk, mla_w_uv, mla_g_q, mla_g_k, nsa_g_q, nsa_g_k, nsa_cmp_w1, nsa_cmp_w2, nsa_cmp_pe):
    B, S = x.shape[:2]
    M = B * S
    H = HEADS_PER_GROUP
    bf16 = MXU_DTYPE
    cc_h, ss_h = _lane_tables(positions, HEAD_DIM)
    cc_i, ss_i = _lane_tables(positions, IDX_DIM)
    cc_d, ss_d = _lane_tables(positions, DIFF_QK_DIM)
    cc_m, ss_m = _lane_tables(positions, MLA_ROPE)
    h = x.reshape(M, D_MODEL)
    for i in range(DEPTH):
        w_in_i = jnp.take(_pad_cols(w_in[i].astype(bf16), IN_WIDTH + 1), _IN_PERM, axis=1)
        u = _rmsnorm_rows(h, g_mix[i])
        proj = _matmul(u, w_in_i, tm=1024, tn=512, tk=D_MODEL)
        o_a = _diff_mixer(proj, B, S, cc_d, ss_d, diff_g_q[i], diff_g_k[i], diff_g_sub[i], diff_lambda[i], i)
        o_b = _dsa_mixer(proj, B, S, cc_h, ss_h, cc_i, ss_i, dsa_g_q[i], dsa_g_k[i])
        o_c = _mla_mixer(proj, B, S, cc_m, ss_m, mla_g_cq[i], mla_g_ckv[i], mla_w_uq[i], mla_w_uk[i], mla_w_uv[i],
                         mla_g_q[i], mla_g_k[i])
        o_d = _nsa_mixer(proj, B, S, cc_h, ss_h, positions, nsa_g_q[i], nsa_g_k[i],
                         nsa_cmp_w1[i], nsa_cmp_w2[i], nsa_cmp_pe[i])
        mixed = _mix_groups(o_a, o_b, o_c, o_d, g_group_out[i])
        h = _matmul(mixed, w_out[i].astype(bf16), res=h, tm=1024, tn=1024, tk=1024)
        u = _rmsnorm_rows(h, g_ffn[i])
        act = _swiglu(u, _pad_cols(w_gate[i].astype(bf16), D_FF_PAD), _pad_cols(w_up[i].astype(bf16), D_FF_PAD),
                      tm=1024, tn=512)
        h = _matmul(act, _pad_rows(w_down[i].astype(bf16), D_FF_PAD), res=h, tm=1024, tn=1024, tk=D_FF_PAD // 8)
        u = _rmsnorm_rows(h, g_ple[i])
        h = _ple(u, w_ple_gate[i].astype(bf16), p[i].reshape(M, PLE_DIM).astype(bf16), w_ple_proj[i].astype(bf16), h,
                 tm=1024, tn=512)
    return h.reshape(B, S, D_MODEL)
```

```python
import functools
import math

import numpy as np
import jax
import jax.numpy as jnp
from jax import lax
from jax.experimental import pallas as pl
from jax.experimental.pallas import tpu as pltpu

D_MODEL = 4096
DEPTH = 4
HEAD_DIM = 128
N_GROUPS = 4
HEADS_PER_GROUP = D_MODEL // HEAD_DIM // N_GROUPS
GROUP_WIDTH = HEADS_PER_GROUP * HEAD_DIM
MIX_WIDTH = N_GROUPS * GROUP_WIDTH
D_FF = ((8 * D_MODEL + 3 * 256 - 1) // (3 * 256)) * 256
PLE_DIM = 256
ROPE_THETA = 10000.0
EPS = 1e-6
Q_BLOCK = 128
NEG_INF = -1e30
POS_INF = 1e30

DIFF_QK_DIM = HEAD_DIM // 2
DIFF_V_DIM = HEAD_DIM
IDX_HEADS = 16
IDX_DIM = 64
DSA_TOPK = 256
MLA_Q_RANK = 1024
MLA_KV_RANK = 512
MLA_NOPE = 128
MLA_ROPE = 64
MLA_V = HEAD_DIM
NSA_KV_HEADS = 2
NSA_Q_PER_KV = HEADS_PER_GROUP // NSA_KV_HEADS
CMP_LEN = 32
CMP_STRIDE = 16
CMP_HIDDEN = HEAD_DIM
SEL_BLOCK = 64
SEL_TOPN = 16
SEL_INIT = 1
SEL_LOCAL = 2
SEL_Q_BLOCK = 32
WINDOW = 512

IN_SIZES = (
    HEADS_PER_GROUP * 2 * DIFF_QK_DIM, HEADS_PER_GROUP * 2 * DIFF_QK_DIM, HEADS_PER_GROUP * DIFF_V_DIM,
    GROUP_WIDTH, HEAD_DIM, HEAD_DIM, IDX_HEADS * IDX_DIM, IDX_DIM, IDX_HEADS,
    MLA_Q_RANK, MLA_KV_RANK, MLA_ROPE,
    GROUP_WIDTH, 3 * 2 * NSA_KV_HEADS * HEAD_DIM, 3 * HEADS_PER_GROUP,
)
IN_WIDTH = sum(IN_SIZES)

V7X_VMEM_LIMIT_BYTES = 56 * 1024 * 1024
LANE = 128
MXU_DTYPE = getattr(jnp, "bfloat16")


def _round_up(n, m):
    return (n + m - 1) // m * m


def _compiler_params(semantics):
    return pltpu.CompilerParams(dimension_semantics=semantics, vmem_limit_bytes=V7X_VMEM_LIMIT_BYTES)


def _rmsnorm_rows_kernel(x_ref, g_ref, o_ref):
    x = x_ref[...]
    ms = jnp.mean(x * x, axis=-1, keepdims=True)
    o_ref[...] = (x * lax.rsqrt(ms + EPS) * g_ref[...]).astype(o_ref.dtype)


def _rmsnorm_rows(x, g, *, tm=256):
    m, d = x.shape
    out_dtype = MXU_DTYPE
    return pl.pallas_call(
        _rmsnorm_rows_kernel,
        grid=(m // tm,),
        in_specs=[pl.BlockSpec((tm, d), lambda i: (i, 0)), pl.BlockSpec((1, d), lambda i: (0, 0))],
        out_specs=pl.BlockSpec((tm, d), lambda i: (i, 0)),
        out_shape=jax.ShapeDtypeStruct((m, d), out_dtype),
        compiler_params=_compiler_params(("parallel",)),
        name="rmsnorm_rows",
    )(x, g.reshape(1, d))


def _mm_kernel(a_ref, w_ref, o_ref, acc_ref, *, nk):
    k = pl.program_id(2)

    @pl.when(k == 0)
    def _():
        acc_ref[...] = jnp.zeros_like(acc_ref)

    acc_ref[...] += jnp.dot(a_ref[...], w_ref[...], preferred_element_type=jnp.float32)

    @pl.when(k == nk - 1)
    def _():
        o_ref[...] = acc_ref[...].astype(o_ref.dtype)


def _mm_res_kernel(a_ref, w_ref, r_ref, o_ref, acc_ref, *, nk):
    k = pl.program_id(2)

    @pl.when(k == 0)
    def _():
        acc_ref[...] = jnp.zeros_like(acc_ref)

    acc_ref[...] += jnp.dot(a_ref[...], w_ref[...], preferred_element_type=jnp.float32)

    @pl.when(k == nk - 1)
    def _():
        o_ref[...] = (r_ref[...] + acc_ref[...]).astype(o_ref.dtype)


def _matmul(a, w, *, res=None, tm, tn, tk, out_dtype=jnp.float32):
    m, kdim = a.shape
    n = w.shape[1]
    nk = kdim // tk
    assert m % tm == 0 and n % tn == 0 and kdim % tk == 0
    in_specs = [pl.BlockSpec((tm, tk), lambda i, j, k: (i, k)), pl.BlockSpec((tk, tn), lambda i, j, k: (k, j))]
    args = [a, w]
    if res is None:
        body = functools.partial(_mm_kernel, nk=nk)
    else:
        body = functools.partial(_mm_res_kernel, nk=nk)
        in_specs.append(pl.BlockSpec((tm, tn), lambda i, j, k: (i, j)))
        args.append(res)
    return pl.pallas_call(
        body,
        grid=(m // tm, n // tn, nk),
        in_specs=in_specs,
        out_specs=pl.BlockSpec((tm, tn), lambda i, j, k: (i, j)),
        out_shape=jax.ShapeDtypeStruct((m, n), out_dtype),
        scratch_shapes=[pltpu.VMEM((tm, tn), jnp.float32)],
        compiler_params=_compiler_params(("parallel", "parallel", "arbitrary")),
        name="matmul_res" if res is not None else "matmul",
    )(*args)


def _swiglu_kernel(a_ref, wg_ref, wu_ref, o_ref):
    a = a_ref[...]
    g = jnp.dot(a, wg_ref[...], preferred_element_type=jnp.float32)
    u = jnp.dot(a, wu_ref[...], preferred_element_type=jnp.float32)
    o_ref[...] = (g * jax.nn.sigmoid(g) * u).astype(o_ref.dtype)


def _swiglu(a, wg, wu, *, tm, tn):
    m, kdim = a.shape
    n = wg.shape[1]
    return pl.pallas_call(
        _swiglu_kernel,
        grid=(m // tm, n // tn),
        in_specs=[pl.BlockSpec((tm, kdim), lambda i, j: (i, 0)),
                  pl.BlockSpec((kdim, tn), lambda i, j: (0, j)),
                  pl.BlockSpec((kdim, tn), lambda i, j: (0, j))],
        out_specs=pl.BlockSpec((tm, tn), lambda i, j: (i, j)),
        out_shape=jax.ShapeDtypeStruct((m, n), MXU_DTYPE),
        compiler_params=_compiler_params(("parallel", "parallel")),
        name="swiglu",
    )(a, wg, wu)


def _ple_kernel(a_ref, wg_ref, p_ref, wp_ref, r_ref, o_ref):
    gate = jax.nn.sigmoid(jnp.dot(a_ref[...], wg_ref[...], preferred_element_type=jnp.float32))
    emb = jnp.dot(p_ref[...], wp_ref[...], preferred_element_type=jnp.float32)
    o_ref[...] = r_ref[...] + gate * emb


def _ple(a, wg, p, wp, res, *, tm, tn):
    m, kdim = a.shape
    n = wg.shape[1]
    pdim = p.shape[1]
    return pl.pallas_call(
        _ple_kernel,
        grid=(m // tm, n // tn),
        in_specs=[pl.BlockSpec((tm, kdim), lambda i, j: (i, 0)),
                  pl.BlockSpec((kdim, tn), lambda i, j: (0, j)),
                  pl.BlockSpec((tm, pdim), lambda i, j: (i, 0)),
                  pl.BlockSpec((pdim, tn), lambda i, j: (0, j)),
                  pl.BlockSpec((tm, tn), lambda i, j: (i, j))],
        out_specs=pl.BlockSpec((tm, tn), lambda i, j: (i, j)),
        out_shape=jax.ShapeDtypeStruct((m, n), jnp.float32),
        compiler_params=_compiler_params(("parallel", "parallel")),
        name="ple",
    )(a, wg, p, wp, res)


def _rmsnorm(x, g):
    x32 = x.astype(jnp.float32)
    y = x32 * lax.rsqrt(jnp.mean(x32 * x32, axis=-1, keepdims=True) + EPS)
    return (y * g.astype(jnp.float32)).astype(x.dtype)


def _rope_tables(positions, dim):
    inv_freq = ROPE_THETA ** (-jnp.arange(0, dim, 2, dtype=jnp.float32) / dim)
    ang = positions.astype(jnp.float32)[..., None] * inv_freq
    return jnp.cos(ang), jnp.sin(ang)


def _apply_rope(x, cos, sin):
    half = x.shape[-1] // 2
    x32 = x.astype(jnp.float32)
    x1, x2 = x32[..., :half], x32[..., half:]
    return jnp.concatenate([x1 * cos - x2 * sin, x2 * cos + x1 * sin], axis=-1).astype(x.dtype)


def _masked_softmax(s, mask):
    s = jnp.where(mask, s.astype(jnp.float32), NEG_INF)
    return jnp.where(mask, jax.nn.softmax(s, axis=-1), 0.0)


def _merge_blocks(o):
    o = jnp.moveaxis(o, 0, 1)
    return o.reshape((o.shape[0], o.shape[1] * o.shape[2]) + o.shape[3:])


def _dense_causal_attention(q, k, v, scale):
    S = q.shape[1]
    kpos = jnp.arange(S)

    def block(i):
        start = i * Q_BLOCK
        qb = lax.dynamic_slice_in_dim(q, start, Q_BLOCK, axis=1)
        sc = jnp.einsum('bqhd,bkhd->bhqk', qb, k) * scale
        mask = kpos[None, :] <= (start + jnp.arange(Q_BLOCK))[:, None]
        a = _masked_softmax(sc, mask)
        return jnp.einsum('bhqk,bkhd->bqhd', a.astype(v.dtype), v)

    return _merge_blocks(lax.map(block, jnp.arange(S // Q_BLOCK)))


def _diff_attention(q, k, v, cos, sin, g_q, g_k, g_sub, lam_params, layer_idx):
    B, S = q.shape[:2]
    c, s_ = cos[:, :, None, None, :], sin[:, :, None, None, :]
    q = _apply_rope(_rmsnorm(q, g_q), c, s_)
    k = _apply_rope(_rmsnorm(k, g_k), c, s_)
    lam_init = 0.8 - 0.6 * math.exp(-0.3 * layer_idx)
    lp = lam_params.astype(jnp.float32)
    lam = jnp.exp(jnp.sum(lp[0] * lp[1])) - jnp.exp(jnp.sum(lp[2] * lp[3])) + lam_init
    scale = DIFF_QK_DIM ** -0.5
    kpos = jnp.arange(S)

    def block(i):
        start = i * Q_BLOCK
        qb = lax.dynamic_slice_in_dim(q, start, Q_BLOCK, axis=1)
        sc = jnp.einsum('bqhcd,bkhcd->bhcqk', qb, k) * scale
        mask = kpos[None, :] <= (start + jnp.arange(Q_BLOCK))[:, None]
        a = _masked_softmax(sc, mask)
        w = a[:, :, 0] - lam * a[:, :, 1]
        return jnp.einsum('bhqk,bkhd->bqhd', w.astype(v.dtype), v)

    o = _merge_blocks(lax.map(block, jnp.arange(S // Q_BLOCK)))
    o = _rmsnorm(o, g_sub) * (1.0 - lam_init)
    return o.reshape(B, S, -1)


def _dsa_attention(q, k, v, iq, ik, iw, cos, sin, cos_i, sin_i, g_q, g_k):
    B, S = q.shape[:2]
    q = _apply_rope(_rmsnorm(q, g_q), cos[:, :, None, :], sin[:, :, None, :])
    k = _apply_rope(_rmsnorm(k, g_k), cos, sin)
    iq = _apply_rope(iq, cos_i[:, :, None, :], sin_i[:, :, None, :])
    ik = _apply_rope(ik, cos_i, sin_i)
    n_keep = min(DSA_TOPK, S // 4)
    scale = HEAD_DIM ** -0.5
    kpos = jnp.arange(S)
    gather_keys = jax.vmap(lambda t, ix: t[ix])

    def block(i):
        start = i * Q_BLOCK
        qpos = start + jnp.arange(Q_BLOCK)
        qb = lax.dynamic_slice_in_dim(q, start, Q_BLOCK, axis=1)
        iqb = lax.dynamic_slice_in_dim(iq, start, Q_BLOCK, axis=1)
        iwb = lax.dynamic_slice_in_dim(iw, start, Q_BLOCK, axis=1)
        rel = jax.nn.relu(jnp.einsum('bqhd,bkd->bqhk', iqb, ik).astype(jnp.float32) * IDX_DIM ** -0.5)
        score = jnp.einsum('bqh,bqhk->bqk', iwb.astype(jnp.float32) * IDX_HEADS ** -0.5, rel)
        score = jnp.where(kpos[None, None, :] <= qpos[None, :, None], score, NEG_INF)
        _, sel = lax.top_k(score, n_keep)
        ks = gather_keys(k, sel)
        vs = gather_keys(v, sel)
        sc = jnp.einsum('bqhd,bqnd->bhqn', qb, ks) * scale
        a = _masked_softmax(sc, (sel <= qpos[None, :, None])[:, None])
        return jnp.einsum('bhqn,bqnd->bqhd', a.astype(vs.dtype), vs)

    o = _merge_blocks(lax.map(block, jnp.arange(S // Q_BLOCK)))
    return o.reshape(B, S, -1)


def _mla_attention(c_q, c_kv, k_rope, cos, sin, g_cq, g_ckv, w_uq, w_uk, w_uv, g_q, g_k):
    B, S = c_q.shape[:2]
    H = HEADS_PER_GROUP
    q = (_rmsnorm(c_q, g_cq) @ w_uq).reshape(B, S, H, MLA_NOPE + MLA_ROPE)
    ckv = _rmsnorm(c_kv, g_ckv)
    k_nope = (ckv @ w_uk).reshape(B, S, H, MLA_NOPE)
    v = (ckv @ w_uv).reshape(B, S, H, MLA_V)
    k_r = jnp.broadcast_to(k_rope[:, :, None, :], (B, S, H, MLA_ROPE))
    k = jnp.concatenate([k_nope, k_r], axis=-1)
    q = _rmsnorm(q, g_q)
    k = _rmsnorm(k, g_k)
    c, s_ = cos[:, :, None, :], sin[:, :, None, :]
    q = jnp.concatenate([q[..., :MLA_NOPE], _apply_rope(q[..., MLA_NOPE:], c, s_)], axis=-1)
    k = jnp.concatenate([k[..., :MLA_NOPE], _apply_rope(k[..., MLA_NOPE:], c, s_)], axis=-1)
    o = _dense_causal_attention(q, k, v, (MLA_NOPE + MLA_ROPE) ** -0.5)
    return o.reshape(B, S, -1)


def _nsa_attention(q, kv, gate_logits, cos, sin, positions, g_q, g_k, cmp_w1, cmp_w2, cmp_pe):
    B, S = q.shape[:2]
    G, R, d = NSA_KV_HEADS, NSA_Q_PER_KV, HEAD_DIM
    scale = d ** -0.5
    c, s_ = cos[:, :, None, :], sin[:, :, None, :]
    qg = _apply_rope(_rmsnorm(q, g_q), c, s_).reshape(B, S, G, R, d)
    tpos = np.arange(S)

    n_cmp = (S - CMP_LEN) // CMP_STRIDE + 1
    starts = np.arange(n_cmp) * CMP_STRIDE
    ends = starts + CMP_LEN - 1
    blocks = kv[:, :, 0][:, starts[:, None] + np.arange(CMP_LEN)[None, :]]
    blocks = blocks + jnp.transpose(cmp_pe, (1, 0, 2))[:, :, None, :]
    flat = jnp.transpose(blocks, (0, 1, 3, 4, 2, 5)).reshape(B, n_cmp, 2, G, CMP_LEN * d)
    hid = jax.nn.gelu(jnp.einsum('bnkgf,kfh->bnkgh', flat, cmp_w1))
    comp = jnp.einsum('bnkgh,khe->bnkge', hid, cmp_w2)
    cos_c, sin_c = _rope_tables(positions[:, ends], d)
    k_cmp = _apply_rope(_rmsnorm(comp[:, :, 0], g_k[0]), cos_c[:, :, None, :], sin_c[:, :, None, :])
    v_cmp = comp[:, :, 1]
    sc = jnp.einsum('bsgrd,bjgd->bgrsj', qg, k_cmp) * scale
    p_cmp = _masked_softmax(sc, ends[None, :] <= tpos[:, None])
    o_cmp = jnp.einsum('bgrsj,bjgd->bsgrd', p_cmp.astype(v_cmp.dtype), v_cmp)

    n_sel = S // SEL_BLOCK
    n_top = min(SEL_TOPN, n_sel)
    blk = np.arange(n_sel)
    sel_start = blk * SEL_BLOCK
    cmp_to_sel = ((starts[:, None] < sel_start[None, :] + SEL_BLOCK)
                  & (starts[:, None] + CMP_LEN > sel_start[None, :])).astype(np.float32)
    imp = jnp.einsum('bgrsj,jn->bgsn', p_cmp, jnp.asarray(cmp_to_sel))
    dist = (tpos // SEL_BLOCK)[:, None] - blk[None, :]
    forced = (blk[None, :] < SEL_INIT) | ((dist >= 0) & (dist < SEL_LOCAL))
    admissible = sel_start[None, :] <= tpos[:, None]
    imp = jnp.where(forced, POS_INF, jnp.where(admissible, imp, NEG_INF))
    _, sel = lax.top_k(imp, n_top)
    k_slc = _apply_rope(_rmsnorm(kv[:, :, 1, 0], g_k[1]), c, s_)
    kb = k_slc.reshape(B, n_sel, SEL_BLOCK, G, d).transpose(0, 3, 1, 2, 4)
    vb = kv[:, :, 1, 1].reshape(B, n_sel, SEL_BLOCK, G, d).transpose(0, 3, 1, 2, 4)
    gather_blocks = jax.vmap(jax.vmap(lambda t, ix: t[ix]))
    q_t = qg.transpose(0, 2, 3, 1, 4)

    def sel_chunk(i):
        start = i * SEL_Q_BLOCK
        qc = lax.dynamic_slice_in_dim(q_t, start, SEL_Q_BLOCK, axis=3)
        ix = lax.dynamic_slice_in_dim(sel, start, SEL_Q_BLOCK, axis=2)
        ks = gather_blocks(kb, ix)
        vs = gather_blocks(vb, ix).reshape(B, G, SEL_Q_BLOCK, n_top * SEL_BLOCK, d)
        qpos = start + jnp.arange(SEL_Q_BLOCK)
        tok = ix[..., None] * SEL_BLOCK + jnp.arange(SEL_BLOCK)
        mask = (tok <= qpos[None, None, :, None, None]).reshape(B, G, 1, SEL_Q_BLOCK, n_top * SEL_BLOCK)
        sc_s = jnp.einsum('bgrqd,bgqntd->bgrqnt', qc, ks).reshape(B, G, R, SEL_Q_BLOCK, n_top * SEL_BLOCK) * scale
        a_s = _masked_softmax(sc_s, mask)
        return jnp.einsum('bgrqm,bgqmd->bqgrd', a_s.astype(vs.dtype), vs)

    o_slc = _merge_blocks(lax.map(sel_chunk, jnp.arange(S // SEL_Q_BLOCK)))

    n_qb = S // Q_BLOCK
    span = WINDOW + Q_BLOCK
    kidx = np.arange(n_qb)[:, None] * Q_BLOCK + np.arange(span)[None, :]
    pad = ((0, 0), (WINDOW, 0), (0, 0), (0, 0))
    k_win = jnp.pad(_apply_rope(_rmsnorm(kv[:, :, 2, 0], g_k[2]), c, s_), pad)[:, kidx]
    v_win = jnp.pad(kv[:, :, 2, 1], pad)[:, kidx]
    qb = qg.reshape(B, n_qb, Q_BLOCK, G, R, d)
    sc_w = jnp.einsum('bnqgrd,bnkgd->bngrqk', qb, k_win) * scale
    spos = (kidx - WINDOW)[:, None, :]
    tq = (np.arange(n_qb)[:, None] * Q_BLOCK + np.arange(Q_BLOCK)[None, :])[:, :, None]
    wmask = (spos <= tq) & (spos > tq - WINDOW) & (spos >= 0)
    a_w = _masked_softmax(sc_w, wmask[None, :, None, None])
    o_win = jnp.einsum('bngrqk,bnkgd->bnqgrd', a_w.astype(v_win.dtype), v_win).reshape(B, S, G, R, d)

    g = jax.nn.sigmoid(gate_logits.astype(jnp.float32)).reshape(B, S, G, R, 3)
    o = g[..., 0:1] * o_cmp + g[..., 1:2] * o_slc + g[..., 2:3] * o_win
    return o.astype(q.dtype).reshape(B, S, -1)


_SEG_ORDER = ("a_q", "a_k", "a_v", "b_q", "b_iq", "c_q", "d_q", "d_kv", "c_kv", "b_k", "b_v")
_SEG_NAMES = ("a_q", "a_k", "a_v", "b_q", "b_k", "b_v", "b_iq", "b_ik", "b_iw", "c_q", "c_kv", "c_kr", "d_q", "d_kv", "d_g")
_ORIG_START = dict(zip(_SEG_NAMES, [int(o) for o in np.cumsum((0,) + IN_SIZES[:-1])]))
_ORIG_SIZE = dict(zip(_SEG_NAMES, IN_SIZES))
_GATES_PER_KV_GROUP = 3 * NSA_Q_PER_KV


def _build_in_layout():
    col = {}
    perm = []
    for name in _SEG_ORDER:
        col[name] = len(perm)
        perm.extend(range(_ORIG_START[name], _ORIG_START[name] + _ORIG_SIZE[name]))
    zero = IN_WIDTH
    col["narrow"] = len(perm)
    blk0 = list(range(_ORIG_START["b_ik"], _ORIG_START["b_ik"] + IDX_DIM))
    blk0 += list(range(_ORIG_START["d_g"], _ORIG_START["d_g"] + _GATES_PER_KV_GROUP))
    blk0 += list(range(_ORIG_START["b_iw"], _ORIG_START["b_iw"] + IDX_HEADS))
    blk0 += [zero] * (LANE - len(blk0))
    blk1 = list(range(_ORIG_START["c_kr"], _ORIG_START["c_kr"] + MLA_ROPE))
    blk1 += list(range(_ORIG_START["d_g"] + _GATES_PER_KV_GROUP, _ORIG_START["d_g"] + 2 * _GATES_PER_KV_GROUP))
    blk1 += [zero] * (LANE - len(blk1))
    perm.extend(blk0 + blk1)
    width = _round_up(len(perm), 512)
    perm.extend([zero] * (width - len(perm)))
    return col, np.asarray(perm, np.int32), width


_COL, _IN_PERM, IN_WIDTH_PAD = _build_in_layout()
_GATE_LANE0 = IDX_DIM
_IW_LANE0 = IDX_DIM + _GATES_PER_KV_GROUP


def _rope_lane_tables(cos, sin):
    return jnp.concatenate([cos, cos], axis=-1), jnp.concatenate([-sin, sin], axis=-1)


def _lane_tables(positions, dim):
    cc, ss = _rope_lane_tables(*_rope_tables(positions, dim))
    return tuple(jnp.tile(t.reshape(-1, dim), (1, LANE // dim)) for t in (cc, ss))


def _seg64_sum(y):
    r = lax.broadcasted_iota(jnp.int32, (LANE, LANE), 0) // 64
    c = lax.broadcasted_iota(jnp.int32, (LANE, LANE), 1) // 64
    bd = jnp.where(r == c, 1.0, 0.0).astype(MXU_DTYPE)
    hi = y.astype(MXU_DTYPE)
    lo = (y - hi.astype(jnp.float32)).astype(MXU_DTYPE)
    return (jnp.dot(hi, bd, preferred_element_type=jnp.float32)
            + jnp.dot(lo, bd, preferred_element_type=jnp.float32))


def _prep_kernel(x_ref, g_ref, cc_ref, ss_ref, o_ref, *, seg, norm, scale):
    x = x_ref[...]
    if norm:
        if seg == LANE:
            ms = jnp.mean(x * x, axis=-1, keepdims=True)
        else:
            ms = _seg64_sum(x * x) * (1.0 / seg)
        x = x * lax.rsqrt(ms + EPS) * g_ref[0]
    if seg == LANE:
        partner = pltpu.roll(x, LANE // 2, axis=1)
    else:
        lane = lax.broadcasted_iota(jnp.int32, x.shape, 1)
        partner = jnp.where((lane % seg) < seg // 2, pltpu.roll(x, LANE - seg // 2, axis=1),
                            pltpu.roll(x, seg // 2, axis=1))
    y = x * cc_ref[...] + partner * ss_ref[...]
    if scale != 1.0:
        y = y * scale
    o_ref[...] = y.astype(o_ref.dtype)


def _prep_heads(proj, col0, n_blocks, gains, cc, ss, *, seg, norm, scale=1.0, tm=1024):
    m = proj.shape[0]
    tm = min(tm, m)
    c0 = col0 // LANE
    n_gain = gains.shape[0]
    gidx = (lambda i, h: (h, 0, 0)) if n_gain > 1 else (lambda i, h: (0, 0, 0))
    return pl.pallas_call(
        functools.partial(_prep_kernel, seg=seg, norm=norm, scale=scale),
        grid=(m // tm, n_blocks),
        in_specs=[pl.BlockSpec((tm, LANE), lambda i, h: (i, c0 + h)),
                  pl.BlockSpec((1, 1, LANE), gidx),
                  pl.BlockSpec((tm, LANE), lambda i, h: (i, 0)),
                  pl.BlockSpec((tm, LANE), lambda i, h: (i, 0))],
        out_specs=pl.BlockSpec((tm, LANE), lambda i, h: (i, h)),
        out_shape=jax.ShapeDtypeStruct((m, LANE * n_blocks), MXU_DTYPE),
        compiler_params=_compiler_params(("parallel", "parallel")),
        name="prep_heads",
    )(proj, gains.reshape(n_gain, 1, LANE), cc, ss)


def _dot_nt(a, b):
    return lax.dot_general(a, b, (((1,), (1,)), ((), ())), preferred_element_type=jnp.float32)


def _softmax_step(s, mask, m, l, acc, v):
    s = jnp.where(mask, s, NEG_INF)
    m_new = jnp.maximum(m, jnp.max(s, axis=-1, keepdims=True))
    alpha = jnp.exp(m - m_new)
    p = jnp.where(mask, jnp.exp(s - m_new), 0.0)
    l = alpha * l + jnp.sum(p, axis=-1, keepdims=True)
    acc = alpha * acc + jnp.dot(p.astype(MXU_DTYPE), v, preferred_element_type=jnp.float32)
    return m_new, l, acc


def _softmax_finish(l, acc):
    return acc * (1.0 / jnp.maximum(l, 1e-30))


def _nsa_cmp_kernel(x_ref, pe_ref, w1_ref, w2_ref, gk_ref, cc_ref, ss_ref, kc_ref, vc_ref):
    nch = x_ref.shape[2]
    for kv in range(2):
        x = x_ref[kv, 0]
        lo = jnp.dot((x + pe_ref[kv, 0:1, :]).astype(MXU_DTYPE), w1_ref[kv, 0], preferred_element_type=jnp.float32)
        hi = jnp.dot((x + pe_ref[kv, 1:2, :]).astype(MXU_DTYPE), w1_ref[kv, 1], preferred_element_type=jnp.float32)
        hid = lo + pltpu.roll(hi, nch - 1, axis=0)
        comp = jnp.dot(jax.nn.gelu(hid).astype(MXU_DTYPE), w2_ref[kv], preferred_element_type=jnp.float32)
        if kv == 0:
            ms = jnp.mean(comp * comp, axis=-1, keepdims=True)
            y = comp * lax.rsqrt(ms + EPS) * gk_ref[...]
            y = y * cc_ref[0] + pltpu.roll(y, LANE // 2, axis=1) * ss_ref[0]
            kc_ref[0] = y.astype(kc_ref.dtype)
        else:
            vc_ref[0] = comp.astype(vc_ref.dtype)


def _nsa_compress(craw, pe2, w1, w2, gk, cc_c, ss_c, *, n_batch):
    _, gb, nch, width = craw.shape
    return pl.pallas_call(
        _nsa_cmp_kernel,
        grid=(gb,),
        in_specs=[pl.BlockSpec((2, 1, nch, width), lambda n: (0, n, 0, 0)),
                  pl.BlockSpec((2, 2, width), lambda n: (0, 0, 0)),
                  pl.BlockSpec((2, 2, width, CMP_HIDDEN), lambda n: (0, 0, 0, 0)),
                  pl.BlockSpec((2, CMP_HIDDEN, HEAD_DIM), lambda n: (0, 0, 0)),
                  pl.BlockSpec((1, HEAD_DIM), lambda n: (0, 0)),
                  pl.BlockSpec((1, nch, HEAD_DIM), lambda n: (n % n_batch, 0, 0)),
                  pl.BlockSpec((1, nch, HEAD_DIM), lambda n: (n % n_batch, 0, 0))],
        out_specs=[pl.BlockSpec((1, nch, HEAD_DIM), lambda n: (n, 0, 0)),
                   pl.BlockSpec((1, nch, HEAD_DIM), lambda n: (n, 0, 0))],
        out_shape=[jax.ShapeDtypeStruct((gb, nch, HEAD_DIM), MXU_DTYPE)] * 2,
        compiler_params=_compiler_params(("parallel",)),
        name="nsa_compress",
    )(craw, pe2, w1, w2, gk, cc_c, ss_c)


def _nsa_attn_kernel(q_ref, kc_ref, vc_ref, ks_ref, vs_ref, kw_ref, vw_ref, gate_ref, c2s_ref, exp_ref, o_ref, *,
                     tq, tk, n_sel, n_top):
    i = pl.program_id(2)
    R = NSA_Q_PER_KV
    rows = R * tq
    nch = kc_ref.shape[1]
    bf16 = MXU_DTYPE
    q = jnp.concatenate([q_ref[:, r * HEAD_DIM:(r + 1) * HEAD_DIM] for r in range(R)], axis=0)
    t_row = i * tq + lax.broadcasted_iota(jnp.int32, (rows, 1), 0) % tq

    s = _dot_nt(q, kc_ref[0])
    blk_end = lax.broadcasted_iota(jnp.int32, (1, nch), 1) * CMP_STRIDE + (CMP_LEN - 1)
    mask = blk_end <= t_row
    zero = jnp.zeros((rows, 1), jnp.float32)
    _, l, pv = _softmax_step(s, mask, zero + NEG_INF, zero, jnp.zeros((rows, HEAD_DIM), jnp.float32), vc_ref[0])
    inv = 1.0 / jnp.maximum(l, 1e-30)
    o_cmp = pv * inv
    s = jnp.where(mask, s, NEG_INF)
    p_cmp = jnp.where(mask, jnp.exp(s - jnp.max(s, axis=-1, keepdims=True)), 0.0) * inv

    imp_rows = jnp.dot(p_cmp.astype(bf16), c2s_ref[...], preferred_element_type=jnp.float32)
    imp = imp_rows[0:tq]
    for r in range(1, R):
        imp = imp + imp_rows[r * tq:(r + 1) * tq]
    imp_t = imp.T[0:n_sel]
    blk = lax.broadcasted_iota(jnp.int32, (n_sel, tq), 0)
    t_lane = i * tq + lax.broadcasted_iota(jnp.int32, (n_sel, tq), 1)
    dist = lax.shift_right_logical(t_lane, int(math.log2(SEL_BLOCK))) - blk
    forced = (blk < SEL_INIT) | ((dist >= 0) & (dist < SEL_LOCAL))
    val = jnp.where(forced, POS_INF, jnp.where(blk * SEL_BLOCK <= t_lane, imp_t, NEG_INF))
    rank = jnp.zeros((n_sel, tq), jnp.float32)
    for mblk in range(n_sel):
        vm = val[mblk:mblk + 1, :]
        rank = rank + jnp.where((vm > val) | ((vm == val) & (blk > mblk)), 1.0, 0.0)
    sel_t = jnp.where(rank < n_top, 1.0, 0.0)
    sel_t = jnp.concatenate([sel_t, jnp.zeros((LANE - n_sel, tq), jnp.float32)], axis=0) if n_sel < LANE else sel_t
    sel_q = sel_t.T.astype(bf16)
    sel_rows = jnp.concatenate([sel_q] * R, axis=0)

    key_lane = lax.broadcasted_iota(jnp.int32, (1, tk), 1)

    def slc_body(j, carry):
        off = pl.multiple_of(j * tk, tk)
        k = ks_ref[pl.ds(off, tk), :]
        v = vs_ref[pl.ds(off, tk), :].astype(bf16)
        picked = jnp.dot(sel_rows, exp_ref[j], preferred_element_type=jnp.float32)
        msk = (picked > 0.5) & (off + key_lane <= t_row)
        return _softmax_step(_dot_nt(q, k), msk, *carry, v)

    init = (zero + NEG_INF, zero, jnp.zeros((rows, HEAD_DIM), jnp.float32))
    _, l, acc = lax.fori_loop(0, ((i + 1) * tq + tk - 1) // tk, slc_body, init)
    o_slc = _softmax_finish(l, acc)

    span = WINDOW + tq
    off = pl.multiple_of(jnp.maximum(i * tq - WINDOW, 0), tq)
    key = off + lax.broadcasted_iota(jnp.int32, (1, span), 1)
    msk = (key <= t_row) & (key > t_row - WINDOW)
    _, l, acc = _softmax_step(_dot_nt(q, kw_ref[pl.ds(off, span), :]), msk, *init,
                              vw_ref[pl.ds(off, span), :].astype(bf16))
    o_win = _softmax_finish(l, acc)

    gate = jax.nn.sigmoid(gate_ref[...])
    for r in range(R):
        c = _GATE_LANE0 + 3 * r
        rs = slice(r * tq, (r + 1) * tq)
        o_ref[:, r * HEAD_DIM:(r + 1) * HEAD_DIM] = (gate[:, c:c + 1] * o_cmp[rs]
                                                     + gate[:, c + 1:c + 2] * o_slc[rs]
                                                     + gate[:, c + 2:c + 3] * o_win[rs])


def _nsa_mixer(proj, B, S, cc_h, ss_h, positions, g_q, g_k, cmp_w1, cmp_w2, cmp_pe):
    M = B * S
    G, R, d = NSA_KV_HEADS, NSA_Q_PER_KV, HEAD_DIM
    bf16 = MXU_DTYPE
    tq = Q_BLOCK
    nq = S // tq
    nch = S // CMP_STRIDE
    n_sel = S // SEL_BLOCK
    n_top = min(SEL_TOPN, n_sel)
    kv0 = _COL["d_kv"]

    qn = _prep_heads(proj, _COL["d_q"], HEADS_PER_GROUP, g_q.reshape(1, d), cc_h, ss_h,
                     seg=LANE, norm=True, scale=d ** -0.5)
    k_slc = _prep_heads(proj, kv0 + (1 * 2 + 0) * G * d, G, g_k[1].reshape(1, d), cc_h, ss_h, seg=LANE, norm=True)
    k_win = _prep_heads(proj, kv0 + (2 * 2 + 0) * G * d, G, g_k[2].reshape(1, d), cc_h, ss_h, seg=LANE, norm=True)

    craw = proj[:, kv0:kv0 + 2 * G * d].reshape(B, S, 2, G, d).transpose(2, 3, 0, 1, 4)
    craw = craw.reshape(2, G * B, nch, CMP_STRIDE * d)
    ends = np.minimum(np.arange(nch) * CMP_STRIDE + CMP_LEN - 1, S - 1)
    cc_c, ss_c = _rope_lane_tables(*_rope_tables(positions[:, ends], d))
    pe2 = cmp_pe.reshape(2, 2, CMP_STRIDE * d)
    w1 = cmp_w1.astype(bf16).reshape(2, 2, CMP_STRIDE * d, CMP_HIDDEN)
    k_cmp, v_cmp = _nsa_compress(craw, pe2, w1, cmp_w2.astype(bf16), g_k[0].reshape(1, d), cc_c, ss_c, n_batch=B)

    starts = np.arange(nch) * CMP_STRIDE
    sel_start = np.arange(LANE) * SEL_BLOCK
    c2s = ((starts[:, None] < sel_start[None, :] + SEL_BLOCK) & (starts[:, None] + CMP_LEN > sel_start[None, :])
           & (np.arange(nch)[:, None] < nch - 1) & (np.arange(LANE)[None, :] < n_sel))
    tk = min(512, S)
    expand = (np.arange(LANE)[None, :, None] == (np.arange(S // tk)[:, None, None] * tk + np.arange(tk)[None, None, :]) // SEL_BLOCK)
    c2s = jnp.asarray(c2s, bf16)
    expand = jnp.asarray(expand, bf16)

    narrow = _COL["narrow"] // LANE
    col_vs = (kv0 + (1 * 2 + 1) * G * d) // LANE
    col_vw = (kv0 + (2 * 2 + 1) * G * d) // LANE
    return pl.pallas_call(
        functools.partial(_nsa_attn_kernel, tq=tq, tk=tk, n_sel=n_sel, n_top=n_top),
        grid=(B, G, nq),
        in_specs=[pl.BlockSpec((tq, R * d), lambda b, g, i: (b * nq + i, g)),
                  pl.BlockSpec((1, nch, d), lambda b, g, i: (g * B + b, 0, 0)),
                  pl.BlockSpec((1, nch, d), lambda b, g, i: (g * B + b, 0, 0)),
                  pl.BlockSpec((S, d), lambda b, g, i: (b, g)),
                  pl.BlockSpec((S, d), lambda b, g, i: (b, col_vs + g)),
                  pl.BlockSpec((S, d), lambda b, g, i: (b, g)),
                  pl.BlockSpec((S, d), lambda b, g, i: (b, col_vw + g)),
                  pl.BlockSpec((tq, LANE), lambda b, g, i: (b * nq + i, narrow + g)),
                  pl.BlockSpec((nch, LANE), lambda b, g, i: (0, 0)),
                  pl.BlockSpec((S // tk, LANE, tk), lambda b, g, i: (0, 0, 0))],
        out_specs=pl.BlockSpec((tq, R * d), lambda b, g, i: (b * nq + i, g)),
        out_shape=jax.ShapeDtypeStruct((M, GROUP_WIDTH), jnp.float32),
        compiler_params=_compiler_params(("parallel", "parallel", "arbitrary")),
        name="nsa_attention",
    )(qn, k_cmp, v_cmp, k_slc, proj, k_win, proj, proj, c2s, expand)


_INT_MIN = -2 ** 31


def _sortable_key(x):
    b = lax.bitcast_convert_type(x + 0.0, jnp.int32)
    return jnp.where(b >= 0, b, b ^ 0x7FFFFFFF)


def _dsa_kernel(q_ref, k_ref, v_ref, iq_ref, ik_ref, nar_ref, o_ref, key_ref, msk_ref, w_ref, *,
                tq, tk, n_keep, idx_bits):
    i = pl.program_id(1)
    H = HEADS_PER_GROUP
    bf16 = MXU_DTYPE
    t_col = i * tq + lax.broadcasted_iota(jnp.int32, (tq, 1), 0)
    key_lane = lax.broadcasted_iota(jnp.int32, (1, tq), 1)
    lane = lax.broadcasted_iota(jnp.int32, (tq, LANE), 1)
    iw = nar_ref[...] * (IDX_HEADS ** -0.5)
    for hd in range(IDX_HEADS):
        w_ref[hd] = jnp.broadcast_to(iw[:, _IW_LANE0 + hd:_IW_LANE0 + hd + 1], (tq, tq))
    iq_rows = jnp.concatenate([iq_ref[:, pr * LANE:(pr + 1) * LANE] for pr in range(IDX_HEADS // 2)], axis=0)

    def score_body(j, _):
        off = pl.multiple_of(j * tq, tq)
        ik = ik_ref[pl.ds(off, tq), :]
        ik_lo = jnp.where(lane < IDX_DIM, ik, jnp.zeros_like(ik))
        ik_hi = jnp.where(lane < IDX_DIM, jnp.zeros_like(ik), pltpu.roll(ik.astype(jnp.float32), IDX_DIM, axis=1).astype(bf16))
        rel = (jnp.maximum(_dot_nt(iq_rows, ik_lo), 0.0), jnp.maximum(_dot_nt(iq_rows, ik_hi), 0.0))
        sc = jnp.zeros((tq, tq), jnp.float32)
        for hd in range(IDX_HEADS):
            sc = sc + w_ref[hd] * rel[hd % 2][(hd // 2) * tq:(hd // 2 + 1) * tq]
        sc = jnp.where(off + key_lane <= t_col, sc, NEG_INF)
        key_ref[j] = _sortable_key(sc)
        return 0

    lax.fori_loop(0, i + 1, score_body, 0)

    def count(pred):
        def body(j, part):
            return part + jnp.where(pred(key_ref[j], j * tq), 1.0, 0.0)
        part = lax.fori_loop(0, i + 1, body, jnp.zeros((tq, tq), jnp.float32))
        return jnp.sum(part, axis=-1, keepdims=True)

    c0 = count(lambda kk, off: kk >= 0)
    thr = jnp.where(c0 >= n_keep, 0, _INT_MIN).astype(jnp.int32)

    def thr_body(it, thr):
        cand = thr | lax.shift_left(jnp.int32(1), 30 - it)
        c = count(lambda kk, off: kk >= cand)
        return jnp.where(c >= n_keep, cand, thr)

    thr = lax.fori_loop(0, 31, thr_body, thr)

    need = n_keep - count(lambda kk, off: kk > thr)

    def cut_body(it, cut):
        cand = cut | lax.shift_left(jnp.int32(1), idx_bits - 1 - it)
        c = count(lambda kk, off: (kk == thr) & (off + key_lane < cand))
        return jnp.where(c < need, cand, cut)

    cut = lax.fori_loop(0, idx_bits, cut_body, jnp.zeros((tq, 1), jnp.int32))

    tpc = tk // tq
    n_chunks = (i + tpc) // tpc

    def mask_body(j, _):
        @pl.when(j <= i)
        def _():
            kk = key_ref[j]
            kidx = j * tq + key_lane
            sel = (kk > thr) | ((kk == thr) & (kidx <= cut))
            msk_ref[j] = jnp.where(sel & (kidx <= t_col), 1.0, 0.0)

        @pl.when(j > i)
        def _():
            msk_ref[j] = jnp.zeros((tq, tq), jnp.float32)

        return 0

    lax.fori_loop(0, n_chunks * tpc, mask_body, 0)

    hpp = 4
    rows = hpp * tq
    zero = jnp.zeros((rows, 1), jnp.float32)
    qs = [jnp.concatenate([q_ref[:, (hc * hpp + r) * HEAD_DIM:(hc * hpp + r + 1) * HEAD_DIM] for r in range(hpp)], axis=0)
          for hc in range(H // hpp)]

    def att_body(j, carry):
        off = pl.multiple_of(j * tk, tk)
        k = k_ref[pl.ds(off, tk), :]
        v = v_ref[pl.ds(off, tk), :].astype(bf16)
        m1 = jnp.concatenate([msk_ref[j * tpc + u] for u in range(tpc)], axis=1) > 0.5
        msk = jnp.concatenate([m1] * hpp, axis=0)
        return tuple(_softmax_step(_dot_nt(q, k), msk, *carry[c], v) for c, q in enumerate(qs))

    init = tuple((zero + NEG_INF, zero, jnp.zeros((rows, HEAD_DIM), jnp.float32)) for _ in qs)
    res = lax.fori_loop(0, n_chunks, att_body, init)
    for hc, (_, l, acc) in enumerate(res):
        o = _softmax_finish(l, acc)
        for r in range(hpp):
            hh = hc * hpp + r
            o_ref[:, hh * HEAD_DIM:(hh + 1) * HEAD_DIM] = o[r * tq:(r + 1) * tq]


def _dsa_mixer(proj, B, S, cc_h, ss_h, cc_i, ss_i, g_q, g_k):
    M = B * S
    d = HEAD_DIM
    tq = Q_BLOCK
    nq = S // tq
    n_keep = min(DSA_TOPK, S // 4)
    qn = _prep_heads(proj, _COL["b_q"], HEADS_PER_GROUP, g_q.reshape(1, d), cc_h, ss_h, seg=LANE, norm=True, scale=d ** -0.5)
    kn = _prep_heads(proj, _COL["b_k"], 1, g_k.reshape(1, d), cc_h, ss_h, seg=LANE, norm=True)
    ones = jnp.ones((1, LANE), jnp.float32)
    iqn = _prep_heads(proj, _COL["b_iq"], IDX_HEADS * IDX_DIM // LANE, ones, cc_i, ss_i, seg=IDX_DIM, norm=False,
                      scale=IDX_DIM ** -0.5)
    ikn = _prep_heads(proj, _COL["narrow"], 1, ones, cc_i, ss_i, seg=IDX_DIM, norm=False)
    narrow = _COL["narrow"] // LANE
    col_v = _COL["b_v"] // LANE
    return pl.pallas_call(
        functools.partial(_dsa_kernel, tq=tq, tk=min(512, S), n_keep=n_keep, idx_bits=int(math.log2(S))),
        grid=(B, nq),
        in_specs=[pl.BlockSpec((tq, GROUP_WIDTH), lambda b, i: (b * nq + i, 0)),
                  pl.BlockSpec((S, d), lambda b, i: (b, 0)),
                  pl.BlockSpec((S, d), lambda b, i: (b, col_v)),
                  pl.BlockSpec((tq, IDX_HEADS * IDX_DIM), lambda b, i: (b * nq + i, 0)),
                  pl.BlockSpec((S, LANE), lambda b, i: (b, 0)),
                  pl.BlockSpec((tq, LANE), lambda b, i: (b * nq + i, narrow))],
        out_specs=pl.BlockSpec((tq, GROUP_WIDTH), lambda b, i: (b * nq + i, 0)),
        out_shape=jax.ShapeDtypeStruct((M, GROUP_WIDTH), jnp.float32),
        scratch_shapes=[pltpu.VMEM((nq, tq, tq), jnp.int32), pltpu.VMEM((nq, tq, tq), jnp.float32),
                        pltpu.VMEM((IDX_HEADS, tq, tq), jnp.float32)],
        compiler_params=_compiler_params(("parallel", "arbitrary")),
        name="dsa_attention",
    )(qn, kn, proj, iqn, ikn, proj)


def _causal_chains(qs, k_ref, v_ref, dk, dv, t_row, n_keys, tk):
    rows = qs[0].shape[0]
    key_lane = lax.broadcasted_iota(jnp.int32, (1, tk), 1)
    zero = jnp.zeros((rows, 1), jnp.float32)

    def body(j, carry):
        off = pl.multiple_of(j * tk, tk)
        mask = off + key_lane <= t_row
        out = []
        for c, q in enumerate(qs):
            k = k_ref[pl.ds(off, tk), c * dk:(c + 1) * dk]
            v = v_ref[pl.ds(off, tk), c * dv:(c + 1) * dv].astype(MXU_DTYPE)
            out.append(_softmax_step(_dot_nt(q, k), mask, *carry[c], v))
        return tuple(out)

    init = tuple((zero + NEG_INF, zero, jnp.zeros((rows, dv), jnp.float32)) for _ in qs)
    res = lax.fori_loop(0, (n_keys + tk - 1) // tk, body, init)
    return [_softmax_finish(l, acc) for _, l, acc in res]


def _diff_attn_kernel(q_ref, k_ref, v_ref, lam_ref, g_ref, o_ref, *, tq, tk, hb, out_scale):
    i = pl.program_id(2)
    d = HEAD_DIM
    lane = lax.broadcasted_iota(jnp.int32, (tq, d), 1)
    qs = []
    for c in range(hb):
        q = q_ref[:, c * d:(c + 1) * d]
        zeros = jnp.zeros_like(q)
        qs.append(jnp.concatenate([jnp.where(lane < DIFF_QK_DIM, q, zeros), jnp.where(lane < DIFF_QK_DIM, zeros, q)], axis=0))
    t_row = i * tq + lax.broadcasted_iota(jnp.int32, (2 * tq, 1), 0) % tq
    outs = _causal_chains(qs, k_ref, v_ref, d, d, t_row, (i + 1) * tq, tk)
    for c, o2 in enumerate(outs):
        o = o2[0:tq] - lam_ref[...] * o2[tq:2 * tq]
        ms = jnp.mean(o * o, axis=-1, keepdims=True)
        o_ref[:, c * d:(c + 1) * d] = o * lax.rsqrt(ms + EPS) * g_ref[...] * out_scale


def _diff_mixer(proj, B, S, cc_d, ss_d, g_q, g_k, g_sub, lam_params, layer_idx, *, tq=128, tk=512, hb=2):
    M = B * S
    d = HEAD_DIM
    nq = S // tq
    tk = min(tk, S)
    reps = LANE // DIFF_QK_DIM
    qn = _prep_heads(proj, _COL["a_q"], HEADS_PER_GROUP, jnp.tile(g_q, reps).reshape(1, d), cc_d, ss_d,
                     seg=DIFF_QK_DIM, norm=True, scale=DIFF_QK_DIM ** -0.5)
    kn = _prep_heads(proj, _COL["a_k"], HEADS_PER_GROUP, jnp.tile(g_k, reps).reshape(1, d), cc_d, ss_d,
                     seg=DIFF_QK_DIM, norm=True)
    lam_init = 0.8 - 0.6 * math.exp(-0.3 * layer_idx)
    lp = lam_params.astype(jnp.float32)
    lam = jnp.exp(jnp.sum(lp[0] * lp[1])) - jnp.exp(jnp.sum(lp[2] * lp[3])) + lam_init
    col_v = _COL["a_v"] // (hb * d)
    return pl.pallas_call(
        functools.partial(_diff_attn_kernel, tq=tq, tk=tk, hb=hb, out_scale=1.0 - lam_init),
        grid=(B, HEADS_PER_GROUP // hb, nq),
        in_specs=[pl.BlockSpec((tq, hb * d), lambda b, h, i: (b * nq + i, h)),
                  pl.BlockSpec((S, hb * d), lambda b, h, i: (b, h)),
                  pl.BlockSpec((S, hb * d), lambda b, h, i: (b, col_v + h)),
                  pl.BlockSpec((1, d), lambda b, h, i: (0, 0)),
                  pl.BlockSpec((1, d), lambda b, h, i: (0, 0))],
        out_specs=pl.BlockSpec((tq, hb * d), lambda b, h, i: (b * nq + i, h)),
        out_shape=jax.ShapeDtypeStruct((M, GROUP_WIDTH), jnp.float32),
        compiler_params=_compiler_params(("parallel", "parallel", "arbitrary")),
        name="diff_attention",
    )(qn, kn, proj, jnp.full((1, d), lam, jnp.float32), g_sub.reshape(1, d))


MLA_QK_PAD = 2 * LANE


def _rmsnorm_cols_kernel(x_ref, g_ref, o_ref):
    x = x_ref[...]
    ms = jnp.mean(x * x, axis=-1, keepdims=True)
    o_ref[...] = (x * lax.rsqrt(ms + EPS) * g_ref[...]).astype(o_ref.dtype)


def _rmsnorm_cols(proj, col0, width, g, *, tm=512):
    m = proj.shape[0]
    tm = min(tm, m)
    cb = col0 // width
    assert cb * width == col0
    return pl.pallas_call(
        _rmsnorm_cols_kernel,
        grid=(m // tm,),
        in_specs=[pl.BlockSpec((tm, width), lambda i: (i, cb)), pl.BlockSpec((1, width), lambda i: (0, 0))],
        out_specs=pl.BlockSpec((tm, width), lambda i: (i, 0)),
        out_shape=jax.ShapeDtypeStruct((m, width), MXU_DTYPE),
        compiler_params=_compiler_params(("parallel",)),
        name="rmsnorm_cols",
    )(proj, g.reshape(1, width))


def _mla_prep_kernel(a_ref, b_ref, g_ref, cc_ref, ss_ref, o_ref, *, scale):
    a = a_ref[...]
    lane = lax.broadcasted_iota(jnp.int32, a.shape, 1)
    b = jnp.where(lane < MLA_ROPE, b_ref[...], 0.0)
    ms = (jnp.sum(a * a, axis=-1, keepdims=True) + jnp.sum(b * b, axis=-1, keepdims=True)) * (1.0 / (MLA_NOPE + MLA_ROPE))
    r = lax.rsqrt(ms + EPS)
    ya = a * r * g_ref[:, 0:LANE]
    yb = b * r * g_ref[:, LANE:2 * LANE]
    partner = jnp.where((lane % MLA_ROPE) < MLA_ROPE // 2, pltpu.roll(yb, LANE - MLA_ROPE // 2, axis=1),
                        pltpu.roll(yb, MLA_ROPE // 2, axis=1))
    yb = yb * cc_ref[...] + partner * ss_ref[...]
    o_ref[:, 0:LANE] = (ya * scale).astype(o_ref.dtype)
    o_ref[:, LANE:2 * LANE] = (yb * scale).astype(o_ref.dtype)


def _mla_prep(a_arr, a_col0, a_stride, b_arr, b_col0, b_stride, gain, cc, ss, *, scale=1.0, tm=1024):
    m = a_arr.shape[0]
    tm = min(tm, m)
    ac, as_, bc, bs = a_col0 // LANE, a_stride // LANE, b_col0 // LANE, b_stride // LANE
    g2 = jnp.concatenate([gain, jnp.zeros((MLA_QK_PAD - gain.shape[0],), gain.dtype)]).reshape(1, MLA_QK_PAD)
    return pl.pallas_call(
        functools.partial(_mla_prep_kernel, scale=scale),
        grid=(m // tm, HEADS_PER_GROUP),
        in_specs=[pl.BlockSpec((tm, LANE), lambda i, h: (i, ac + h * as_)),
                  pl.BlockSpec((tm, LANE), lambda i, h: (i, bc + h * bs)),
                  pl.BlockSpec((1, MLA_QK_PAD), lambda i, h: (0, 0)),
                  pl.BlockSpec((tm, LANE), lambda i, h: (i, 0)),
                  pl.BlockSpec((tm, LANE), lambda i, h: (i, 0))],
        out_specs=pl.BlockSpec((tm, MLA_QK_PAD), lambda i, h: (i, h)),
        out_shape=jax.ShapeDtypeStruct((m, HEADS_PER_GROUP * MLA_QK_PAD), MXU_DTYPE),
        compiler_params=_compiler_params(("parallel", "parallel")),
        name="mla_prep",
    )(a_arr, b_arr, g2, cc, ss)


def _mla_attn_kernel(q_ref, k_ref, v_ref, o_ref, *, tq, tk, hb):
    i = pl.program_id(2)
    d = HEAD_DIM
    t_row = i * tq + lax.broadcasted_iota(jnp.int32, (tq, 1), 0)
    qs = [q_ref[:, c * MLA_QK_PAD:(c + 1) * MLA_QK_PAD] for c in range(hb)]
    outs = _causal_chains(qs, k_ref, v_ref, MLA_QK_PAD, d, t_row, (i + 1) * tq, tk)
    for c, o in enumerate(outs):
        o_ref[:, c * d:(c + 1) * d] = o


def _mla_mixer(proj, B, S, cc_m, ss_m, g_cq, g_ckv, w_uq, w_uk, w_uv, g_q, g_k, *, tq=256, tk=512, hb=2):
    M = B * S
    H, d = HEADS_PER_GROUP, HEAD_DIM
    tq = min(tq, S)
    tk = min(tk, S)
    nq = S // tq
    dqk = MLA_NOPE + MLA_ROPE
    w_q = jnp.pad(w_uq.astype(MXU_DTYPE).reshape(MLA_Q_RANK, H, dqk), ((0, 0), (0, 0), (0, MLA_QK_PAD - dqk)))
    w_q = w_q.reshape(MLA_Q_RANK, H * MLA_QK_PAD)
    w_kv = jnp.concatenate([w_uk, w_uv], axis=1).astype(MXU_DTYPE)
    cq = _rmsnorm_cols(proj, _COL["c_q"], MLA_Q_RANK, g_cq)
    ckv = _rmsnorm_cols(proj, _COL["c_kv"], MLA_KV_RANK, g_ckv)
    q_up = _matmul(cq, w_q, tm=min(1024, M), tn=1024, tk=MLA_Q_RANK)
    kv_up = _matmul(ckv, w_kv, tm=min(1024, M), tn=1024, tk=MLA_KV_RANK)
    qn = _mla_prep(q_up, 0, MLA_QK_PAD, q_up, LANE, MLA_QK_PAD, g_q, cc_m, ss_m, scale=dqk ** -0.5)
    kn = _mla_prep(kv_up, 0, LANE, proj, _COL["narrow"] + LANE, 0, g_k, cc_m, ss_m)
    return pl.pallas_call(
        functools.partial(_mla_attn_kernel, tq=tq, tk=tk, hb=hb),
        grid=(B, H // hb, nq),
        in_specs=[pl.BlockSpec((tq, hb * MLA_QK_PAD), lambda b, h, i: (b * nq + i, h)),
                  pl.BlockSpec((S, hb * MLA_QK_PAD), lambda b, h, i: (b, h)),
                  pl.BlockSpec((S, hb * d), lambda b, h, i: (b, H // hb + h))],
        out_specs=pl.BlockSpec((tq, hb * d), lambda b, h, i: (b * nq + i, h)),
        out_shape=jax.ShapeDtypeStruct((M, GROUP_WIDTH), jnp.float32),
        compiler_params=_compiler_params(("parallel", "parallel", "arbitrary")),
        name="mla_attention",
    )(qn, kn, kv_up)


def _mix_kernel(a_ref, b_ref, c_ref, d_ref, g_ref, o_ref):
    o_ref[:, 0:GROUP_WIDTH] = a_ref[...].astype(o_ref.dtype)
    for n, ref in enumerate((b_ref, c_ref, d_ref)):
        x = ref[...]
        ms = jnp.mean(x * x, axis=-1, keepdims=True)
        o_ref[:, (n + 1) * GROUP_WIDTH:(n + 2) * GROUP_WIDTH] = (
            x * lax.rsqrt(ms + EPS) * g_ref[n:n + 1, :]).astype(o_ref.dtype)


def _mix_groups(o_a, o_b, o_c, o_d, g_out, *, tm=512):
    m = o_a.shape[0]
    tm = min(tm, m)
    spec = pl.BlockSpec((tm, GROUP_WIDTH), lambda i: (i, 0))
    return pl.pallas_call(
        _mix_kernel,
        grid=(m // tm,),
        in_specs=[spec, spec, spec, spec, pl.BlockSpec((3, GROUP_WIDTH), lambda i: (0, 0))],
        out_specs=pl.BlockSpec((tm, MIX_WIDTH), lambda i: (i, 0)),
        out_shape=jax.ShapeDtypeStruct((m, MIX_WIDTH), MXU_DTYPE),
        compiler_params=_compiler_params(("parallel",)),
        name="mix_groups",
    )(o_a, o_b, o_c, o_d, g_out)


def _pad_cols(w, n):
    return jnp.pad(w, ((0, 0), (0, n - w.shape[1])))


def kernel(x, p, positions, w_in, w_out, g_mix, g_ffn, w_gate, w_up, w_down, w_ple_proj, w_ple_gate, g_ple,
           g_group_out, diff_g_q, diff_g_k, diff_g_sub, diff_lambda, dsa_g_q, dsa_g_k, mla_g_cq, mla_g_ckv,
           mla_w_uq, mla_w_uk, mla_w_uv, mla_g_q, mla_g_k, nsa_g_q, nsa_g_k, nsa_cmp_w1, nsa_cmp_w2, nsa_cmp_pe):
    B, S = x.shape[:2]
    M = B * S
    H = HEADS_PER_GROUP
    bf16 = MXU_DTYPE
    cc_h, ss_h = _lane_tables(positions, HEAD_DIM)
    cc_i, ss_i = _lane_tables(positions, IDX_DIM)
    cc_d, ss_d = _lane_tables(positions, DIFF_QK_DIM)
    cc_m, ss_m = _lane_tables(positions, MLA_ROPE)
    h = x.reshape(M, D_MODEL)
    for i in range(DEPTH):
        w_in_i = jnp.take(_pad_cols(w_in[i].astype(bf16), IN_WIDTH + 1), _IN_PERM, axis=1)
        u = _rmsnorm_rows(h, g_mix[i])
        proj = _matmul(u, w_in_i, tm=1024, tn=512, tk=D_MODEL)
        o_a = _diff_mixer(proj, B, S, cc_d, ss_d, diff_g_q[i], diff_g_k[i], diff_g_sub[i], diff_lambda[i], i)
        o_b = _dsa_mixer(proj, B, S, cc_h, ss_h, cc_i, ss_i, dsa_g_q[i], dsa_g_k[i])
        o_c = _mla_mixer(proj, B, S, cc_m, ss_m, mla_g_cq[i], mla_g_ckv[i], mla_w_uq[i], mla_w_uk[i], mla_w_uv[i],
                         mla_g_q[i], mla_g_k[i])
        o_d = _nsa_mixer(proj, B, S, cc_h, ss_h, positions, nsa_g_q[i], nsa_g_k[i],
                         nsa_cmp_w1[i], nsa_cmp_w2[i], nsa_cmp_pe[i])
        mixed = _mix_groups(o_a, o_b, o_c, o_d, g_group_out[i])
        h = _matmul(mixed, w_out[i].astype(bf16), res=h, tm=1024, tn=512, tk=MIX_WIDTH)
        u = _rmsnorm_rows(h, g_ffn[i])
        act = _swiglu(u, w_gate[i].astype(bf16), w_up[i].astype(bf16), tm=1024, tn=256)
        h = _matmul(act, w_down[i].astype(bf16), res=h, tm=512, tn=256, tk=D_FF)
        u = _rmsnorm_rows(h, g_ple[i])
        h = _ple(u, w_ple_gate[i].astype(bf16), p[i].reshape(M, PLE_DIM).astype(bf16), w_ple_proj[i].astype(bf16), h,
                 tm=1024, tn=512)
    return h.reshape(B, S, D_MODEL)
```

```python
import functools
import math

import numpy as np
import jax
import jax.numpy as jnp
from jax import lax
from jax.experimental import pallas as pl
from jax.experimental.pallas import tpu as pltpu

D_MODEL = 4096
DEPTH = 4
HEAD_DIM = 128
N_GROUPS = 4
HEADS_PER_GROUP = D_MODEL // HEAD_DIM // N_GROUPS
GROUP_WIDTH = HEADS_PER_GROUP * HEAD_DIM
MIX_WIDTH = N_GROUPS * GROUP_WIDTH
D_FF = ((8 * D_MODEL + 3 * 256 - 1) // (3 * 256)) * 256
PLE_DIM = 256
ROPE_THETA = 10000.0
EPS = 1e-6
Q_BLOCK = 128
NEG_INF = -1e30
POS_INF = 1e30

DIFF_QK_DIM = HEAD_DIM // 2
DIFF_V_DIM = HEAD_DIM
IDX_HEADS = 16
IDX_DIM = 64
DSA_TOPK = 256
MLA_Q_RANK = 1024
MLA_KV_RANK = 512
MLA_NOPE = 128
MLA_ROPE = 64
MLA_V = HEAD_DIM
NSA_KV_HEADS = 2
NSA_Q_PER_KV = HEADS_PER_GROUP // NSA_KV_HEADS
CMP_LEN = 32
CMP_STRIDE = 16
CMP_HIDDEN = HEAD_DIM
SEL_BLOCK = 64
SEL_TOPN = 16
SEL_INIT = 1
SEL_LOCAL = 2
SEL_Q_BLOCK = 32
WINDOW = 512

IN_SIZES = (
    HEADS_PER_GROUP * 2 * DIFF_QK_DIM, HEADS_PER_GROUP * 2 * DIFF_QK_DIM, HEADS_PER_GROUP * DIFF_V_DIM,
    GROUP_WIDTH, HEAD_DIM, HEAD_DIM, IDX_HEADS * IDX_DIM, IDX_DIM, IDX_HEADS,
    MLA_Q_RANK, MLA_KV_RANK, MLA_ROPE,
    GROUP_WIDTH, 3 * 2 * NSA_KV_HEADS * HEAD_DIM, 3 * HEADS_PER_GROUP,
)
IN_WIDTH = sum(IN_SIZES)

V7X_VMEM_LIMIT_BYTES = 56 * 1024 * 1024
LANE = 128
MXU_DTYPE = getattr(jnp, "bfloat16")


def _round_up(n, m):
    return (n + m - 1) // m * m


def _compiler_params(semantics):
    return pltpu.CompilerParams(dimension_semantics=semantics, vmem_limit_bytes=V7X_VMEM_LIMIT_BYTES)


def _rmsnorm_rows_kernel(x_ref, g_ref, o_ref):
    x = x_ref[...]
    ms = jnp.mean(x * x, axis=-1, keepdims=True)
    o_ref[...] = (x * lax.rsqrt(ms + EPS) * g_ref[...]).astype(o_ref.dtype)


def _rmsnorm_rows(x, g, *, tm=256):
    m, d = x.shape
    out_dtype = MXU_DTYPE
    return pl.pallas_call(
        _rmsnorm_rows_kernel,
        grid=(m // tm,),
        in_specs=[pl.BlockSpec((tm, d), lambda i: (i, 0)), pl.BlockSpec((1, d), lambda i: (0, 0))],
        out_specs=pl.BlockSpec((tm, d), lambda i: (i, 0)),
        out_shape=jax.ShapeDtypeStruct((m, d), out_dtype),
        compiler_params=_compiler_params(("parallel",)),
        name="rmsnorm_rows",
    )(x, g.reshape(1, d))


def _mm_kernel(a_ref, w_ref, o_ref, acc_ref, *, nk):
    k = pl.program_id(2)

    @pl.when(k == 0)
    def _():
        acc_ref[...] = jnp.zeros_like(acc_ref)

    acc_ref[...] += jnp.dot(a_ref[...], w_ref[...], preferred_element_type=jnp.float32)

    @pl.when(k == nk - 1)
    def _():
        o_ref[...] = acc_ref[...].astype(o_ref.dtype)


def _mm_res_kernel(a_ref, w_ref, r_ref, o_ref, acc_ref, *, nk):
    k = pl.program_id(2)

    @pl.when(k == 0)
    def _():
        acc_ref[...] = jnp.zeros_like(acc_ref)

    acc_ref[...] += jnp.dot(a_ref[...], w_ref[...], preferred_element_type=jnp.float32)

    @pl.when(k == nk - 1)
    def _():
        o_ref[...] = (r_ref[...] + acc_ref[...]).astype(o_ref.dtype)


def _matmul(a, w, *, res=None, tm, tn, tk, out_dtype=jnp.float32):
    m, kdim = a.shape
    n = w.shape[1]
    nk = kdim // tk
    assert m % tm == 0 and n % tn == 0 and kdim % tk == 0
    in_specs = [pl.BlockSpec((tm, tk), lambda i, j, k: (i, k)), pl.BlockSpec((tk, tn), lambda i, j, k: (k, j))]
    args = [a, w]
    if res is None:
        body = functools.partial(_mm_kernel, nk=nk)
    else:
        body = functools.partial(_mm_res_kernel, nk=nk)
        in_specs.append(pl.BlockSpec((tm, tn), lambda i, j, k: (i, j)))
        args.append(res)
    return pl.pallas_call(
        body,
        grid=(m // tm, n // tn, nk),
        in_specs=in_specs,
        out_specs=pl.BlockSpec((tm, tn), lambda i, j, k: (i, j)),
        out_shape=jax.ShapeDtypeStruct((m, n), out_dtype),
        scratch_shapes=[pltpu.VMEM((tm, tn), jnp.float32)],
        compiler_params=_compiler_params(("parallel", "parallel", "arbitrary")),
        name="matmul_res" if res is not None else "matmul",
    )(*args)


def _swiglu_kernel(a_ref, wg_ref, wu_ref, o_ref):
    a = a_ref[...]
    g = jnp.dot(a, wg_ref[...], preferred_element_type=jnp.float32)
    u = jnp.dot(a, wu_ref[...], preferred_element_type=jnp.float32)
    o_ref[...] = (g * jax.nn.sigmoid(g) * u).astype(o_ref.dtype)


def _swiglu(a, wg, wu, *, tm, tn):
    m, kdim = a.shape
    n = wg.shape[1]
    return pl.pallas_call(
        _swiglu_kernel,
        grid=(m // tm, n // tn),
        in_specs=[pl.BlockSpec((tm, kdim), lambda i, j: (i, 0)),
                  pl.BlockSpec((kdim, tn), lambda i, j: (0, j)),
                  pl.BlockSpec((kdim, tn), lambda i, j: (0, j))],
        out_specs=pl.BlockSpec((tm, tn), lambda i, j: (i, j)),
        out_shape=jax.ShapeDtypeStruct((m, n), MXU_DTYPE),
        compiler_params=_compiler_params(("parallel", "parallel")),
        name="swiglu",
    )(a, wg, wu)


def _ple_kernel(a_ref, wg_ref, p_ref, wp_ref, r_ref, o_ref):
    gate = jax.nn.sigmoid(jnp.dot(a_ref[...], wg_ref[...], preferred_element_type=jnp.float32))
    emb = jnp.dot(p_ref[...], wp_ref[...], preferred_element_type=jnp.float32)
    o_ref[...] = r_ref[...] + gate * emb


def _ple(a, wg, p, wp, res, *, tm, tn):
    m, kdim = a.shape
    n = wg.shape[1]
    pdim = p.shape[1]
    return pl.pallas_call(
        _ple_kernel,
        grid=(m // tm, n // tn),
        in_specs=[pl.BlockSpec((tm, kdim), lambda i, j: (i, 0)),
                  pl.BlockSpec((kdim, tn), lambda i, j: (0, j)),
                  pl.BlockSpec((tm, pdim), lambda i, j: (i, 0)),
                  pl.BlockSpec((pdim, tn), lambda i, j: (0, j)),
                  pl.BlockSpec((tm, tn), lambda i, j: (i, j))],
        out_specs=pl.BlockSpec((tm, tn), lambda i, j: (i, j)),
        out_shape=jax.ShapeDtypeStruct((m, n), jnp.float32),
        compiler_params=_compiler_params(("parallel", "parallel")),
        name="ple",
    )(a, wg, p, wp, res)


def _rmsnorm(x, g):
    x32 = x.astype(jnp.float32)
    y = x32 * lax.rsqrt(jnp.mean(x32 * x32, axis=-1, keepdims=True) + EPS)
    return (y * g.astype(jnp.float32)).astype(x.dtype)


def _rope_tables(positions, dim):
    inv_freq = ROPE_THETA ** (-jnp.arange(0, dim, 2, dtype=jnp.float32) / dim)
    ang = positions.astype(jnp.float32)[..., None] * inv_freq
    return jnp.cos(ang), jnp.sin(ang)


def _apply_rope(x, cos, sin):
    half = x.shape[-1] // 2
    x32 = x.astype(jnp.float32)
    x1, x2 = x32[..., :half], x32[..., half:]
    return jnp.concatenate([x1 * cos - x2 * sin, x2 * cos + x1 * sin], axis=-1).astype(x.dtype)


def _masked_softmax(s, mask):
    s = jnp.where(mask, s.astype(jnp.float32), NEG_INF)
    return jnp.where(mask, jax.nn.softmax(s, axis=-1), 0.0)


def _merge_blocks(o):
    o = jnp.moveaxis(o, 0, 1)
    return o.reshape((o.shape[0], o.shape[1] * o.shape[2]) + o.shape[3:])


def _dense_causal_attention(q, k, v, scale):
    S = q.shape[1]
    kpos = jnp.arange(S)

    def block(i):
        start = i * Q_BLOCK
        qb = lax.dynamic_slice_in_dim(q, start, Q_BLOCK, axis=1)
        sc = jnp.einsum('bqhd,bkhd->bhqk', qb, k) * scale
        mask = kpos[None, :] <= (start + jnp.arange(Q_BLOCK))[:, None]
        a = _masked_softmax(sc, mask)
        return jnp.einsum('bhqk,bkhd->bqhd', a.astype(v.dtype), v)

    return _merge_blocks(lax.map(block, jnp.arange(S // Q_BLOCK)))


def _diff_attention(q, k, v, cos, sin, g_q, g_k, g_sub, lam_params, layer_idx):
    B, S = q.shape[:2]
    c, s_ = cos[:, :, None, None, :], sin[:, :, None, None, :]
    q = _apply_rope(_rmsnorm(q, g_q), c, s_)
    k = _apply_rope(_rmsnorm(k, g_k), c, s_)
    lam_init = 0.8 - 0.6 * math.exp(-0.3 * layer_idx)
    lp = lam_params.astype(jnp.float32)
    lam = jnp.exp(jnp.sum(lp[0] * lp[1])) - jnp.exp(jnp.sum(lp[2] * lp[3])) + lam_init
    scale = DIFF_QK_DIM ** -0.5
    kpos = jnp.arange(S)

    def block(i):
        start = i * Q_BLOCK
        qb = lax.dynamic_slice_in_dim(q, start, Q_BLOCK, axis=1)
        sc = jnp.einsum('bqhcd,bkhcd->bhcqk', qb, k) * scale
        mask = kpos[None, :] <= (start + jnp.arange(Q_BLOCK))[:, None]
        a = _masked_softmax(sc, mask)
        w = a[:, :, 0] - lam * a[:, :, 1]
        return jnp.einsum('bhqk,bkhd->bqhd', w.astype(v.dtype), v)

    o = _merge_blocks(lax.map(block, jnp.arange(S // Q_BLOCK)))
    o = _rmsnorm(o, g_sub) * (1.0 - lam_init)
    return o.reshape(B, S, -1)


def _dsa_attention(q, k, v, iq, ik, iw, cos, sin, cos_i, sin_i, g_q, g_k):
    B, S = q.shape[:2]
    q = _apply_rope(_rmsnorm(q, g_q), cos[:, :, None, :], sin[:, :, None, :])
    k = _apply_rope(_rmsnorm(k, g_k), cos, sin)
    iq = _apply_rope(iq, cos_i[:, :, None, :], sin_i[:, :, None, :])
    ik = _apply_rope(ik, cos_i, sin_i)
    n_keep = min(DSA_TOPK, S // 4)
    scale = HEAD_DIM ** -0.5
    kpos = jnp.arange(S)
    gather_keys = jax.vmap(lambda t, ix: t[ix])

    def block(i):
        start = i * Q_BLOCK
        qpos = start + jnp.arange(Q_BLOCK)
        qb = lax.dynamic_slice_in_dim(q, start, Q_BLOCK, axis=1)
        iqb = lax.dynamic_slice_in_dim(iq, start, Q_BLOCK, axis=1)
        iwb = lax.dynamic_slice_in_dim(iw, start, Q_BLOCK, axis=1)
        rel = jax.nn.relu(jnp.einsum('bqhd,bkd->bqhk', iqb, ik).astype(jnp.float32) * IDX_DIM ** -0.5)
        score = jnp.einsum('bqh,bqhk->bqk', iwb.astype(jnp.float32) * IDX_HEADS ** -0.5, rel)
        score = jnp.where(kpos[None, None, :] <= qpos[None, :, None], score, NEG_INF)
        _, sel = lax.top_k(score, n_keep)
        ks = gather_keys(k, sel)
        vs = gather_keys(v, sel)
        sc = jnp.einsum('bqhd,bqnd->bhqn', qb, ks) * scale
        a = _masked_softmax(sc, (sel <= qpos[None, :, None])[:, None])
        return jnp.einsum('bhqn,bqnd->bqhd', a.astype(vs.dtype), vs)

    o = _merge_blocks(lax.map(block, jnp.arange(S // Q_BLOCK)))
    return o.reshape(B, S, -1)


def _mla_attention(c_q, c_kv, k_rope, cos, sin, g_cq, g_ckv, w_uq, w_uk, w_uv, g_q, g_k):
    B, S = c_q.shape[:2]
    H = HEADS_PER_GROUP
    q = (_rmsnorm(c_q, g_cq) @ w_uq).reshape(B, S, H, MLA_NOPE + MLA_ROPE)
    ckv = _rmsnorm(c_kv, g_ckv)
    k_nope = (ckv @ w_uk).reshape(B, S, H, MLA_NOPE)
    v = (ckv @ w_uv).reshape(B, S, H, MLA_V)
    k_r = jnp.broadcast_to(k_rope[:, :, None, :], (B, S, H, MLA_ROPE))
    k = jnp.concatenate([k_nope, k_r], axis=-1)
    q = _rmsnorm(q, g_q)
    k = _rmsnorm(k, g_k)
    c, s_ = cos[:, :, None, :], sin[:, :, None, :]
    q = jnp.concatenate([q[..., :MLA_NOPE], _apply_rope(q[..., MLA_NOPE:], c, s_)], axis=-1)
    k = jnp.concatenate([k[..., :MLA_NOPE], _apply_rope(k[..., MLA_NOPE:], c, s_)], axis=-1)
    o = _dense_causal_attention(q, k, v, (MLA_NOPE + MLA_ROPE) ** -0.5)
    return o.reshape(B, S, -1)


def _nsa_attention(q, kv, gate_logits, cos, sin, positions, g_q, g_k, cmp_w1, cmp_w2, cmp_pe):
    B, S = q.shape[:2]
    G, R, d = NSA_KV_HEADS, NSA_Q_PER_KV, HEAD_DIM
    scale = d ** -0.5
    c, s_ = cos[:, :, None, :], sin[:, :, None, :]
    qg = _apply_rope(_rmsnorm(q, g_q), c, s_).reshape(B, S, G, R, d)
    tpos = np.arange(S)

    n_cmp = (S - CMP_LEN) // CMP_STRIDE + 1
    starts = np.arange(n_cmp) * CMP_STRIDE
    ends = starts + CMP_LEN - 1
    blocks = kv[:, :, 0][:, starts[:, None] + np.arange(CMP_LEN)[None, :]]
    blocks = blocks + jnp.transpose(cmp_pe, (1, 0, 2))[:, :, None, :]
    flat = jnp.transpose(blocks, (0, 1, 3, 4, 2, 5)).reshape(B, n_cmp, 2, G, CMP_LEN * d)
    hid = jax.nn.gelu(jnp.einsum('bnkgf,kfh->bnkgh', flat, cmp_w1))
    comp = jnp.einsum('bnkgh,khe->bnkge', hid, cmp_w2)
    cos_c, sin_c = _rope_tables(positions[:, ends], d)
    k_cmp = _apply_rope(_rmsnorm(comp[:, :, 0], g_k[0]), cos_c[:, :, None, :], sin_c[:, :, None, :])
    v_cmp = comp[:, :, 1]
    sc = jnp.einsum('bsgrd,bjgd->bgrsj', qg, k_cmp) * scale
    p_cmp = _masked_softmax(sc, ends[None, :] <= tpos[:, None])
    o_cmp = jnp.einsum('bgrsj,bjgd->bsgrd', p_cmp.astype(v_cmp.dtype), v_cmp)

    n_sel = S // SEL_BLOCK
    n_top = min(SEL_TOPN, n_sel)
    blk = np.arange(n_sel)
    sel_start = blk * SEL_BLOCK
    cmp_to_sel = ((starts[:, None] < sel_start[None, :] + SEL_BLOCK)
                  & (starts[:, None] + CMP_LEN > sel_start[None, :])).astype(np.float32)
    imp = jnp.einsum('bgrsj,jn->bgsn', p_cmp, jnp.asarray(cmp_to_sel))
    dist = (tpos // SEL_BLOCK)[:, None] - blk[None, :]
    forced = (blk[None, :] < SEL_INIT) | ((dist >= 0) & (dist < SEL_LOCAL))
    admissible = sel_start[None, :] <= tpos[:, None]
    imp = jnp.where(forced, POS_INF, jnp.where(admissible, imp, NEG_INF))
    _, sel = lax.top_k(imp, n_top)
    k_slc = _apply_rope(_rmsnorm(kv[:, :, 1, 0], g_k[1]), c, s_)
    kb = k_slc.reshape(B, n_sel, SEL_BLOCK, G, d).transpose(0, 3, 1, 2, 4)
    vb = kv[:, :, 1, 1].reshape(B, n_sel, SEL_BLOCK, G, d).transpose(0, 3, 1, 2, 4)
    gather_blocks = jax.vmap(jax.vmap(lambda t, ix: t[ix]))
    q_t = qg.transpose(0, 2, 3, 1, 4)

    def sel_chunk(i):
        start = i * SEL_Q_BLOCK
        qc = lax.dynamic_slice_in_dim(q_t, start, SEL_Q_BLOCK, axis=3)
        ix = lax.dynamic_slice_in_dim(sel, start, SEL_Q_BLOCK, axis=2)
        ks = gather_blocks(kb, ix)
        vs = gather_blocks(vb, ix).reshape(B, G, SEL_Q_BLOCK, n_top * SEL_BLOCK, d)
        qpos = start + jnp.arange(SEL_Q_BLOCK)
        tok = ix[..., None] * SEL_BLOCK + jnp.arange(SEL_BLOCK)
        mask = (tok <= qpos[None, None, :, None, None]).reshape(B, G, 1, SEL_Q_BLOCK, n_top * SEL_BLOCK)
        sc_s = jnp.einsum('bgrqd,bgqntd->bgrqnt', qc, ks).reshape(B, G, R, SEL_Q_BLOCK, n_top * SEL_BLOCK) * scale
        a_s = _masked_softmax(sc_s, mask)
        return jnp.einsum('bgrqm,bgqmd->bqgrd', a_s.astype(vs.dtype), vs)

    o_slc = _merge_blocks(lax.map(sel_chunk, jnp.arange(S // SEL_Q_BLOCK)))

    n_qb = S // Q_BLOCK
    span = WINDOW + Q_BLOCK
    kidx = np.arange(n_qb)[:, None] * Q_BLOCK + np.arange(span)[None, :]
    pad = ((0, 0), (WINDOW, 0), (0, 0), (0, 0))
    k_win = jnp.pad(_apply_rope(_rmsnorm(kv[:, :, 2, 0], g_k[2]), c, s_), pad)[:, kidx]
    v_win = jnp.pad(kv[:, :, 2, 1], pad)[:, kidx]
    qb = qg.reshape(B, n_qb, Q_BLOCK, G, R, d)
    sc_w = jnp.einsum('bnqgrd,bnkgd->bngrqk', qb, k_win) * scale
    spos = (kidx - WINDOW)[:, None, :]
    tq = (np.arange(n_qb)[:, None] * Q_BLOCK + np.arange(Q_BLOCK)[None, :])[:, :, None]
    wmask = (spos <= tq) & (spos > tq - WINDOW) & (spos >= 0)
    a_w = _masked_softmax(sc_w, wmask[None, :, None, None])
    o_win = jnp.einsum('bngrqk,bnkgd->bnqgrd', a_w.astype(v_win.dtype), v_win).reshape(B, S, G, R, d)

    g = jax.nn.sigmoid(gate_logits.astype(jnp.float32)).reshape(B, S, G, R, 3)
    o = g[..., 0:1] * o_cmp + g[..., 1:2] * o_slc + g[..., 2:3] * o_win
    return o.astype(q.dtype).reshape(B, S, -1)


_SEG_ORDER = ("a_q", "a_k", "a_v", "b_q", "b_iq", "c_q", "d_q", "d_kv", "c_kv", "b_k", "b_v")
_SEG_NAMES = ("a_q", "a_k", "a_v", "b_q", "b_k", "b_v", "b_iq", "b_ik", "b_iw", "c_q", "c_kv", "c_kr", "d_q", "d_kv", "d_g")
_ORIG_START = dict(zip(_SEG_NAMES, [int(o) for o in np.cumsum((0,) + IN_SIZES[:-1])]))
_ORIG_SIZE = dict(zip(_SEG_NAMES, IN_SIZES))
_GATES_PER_KV_GROUP = 3 * NSA_Q_PER_KV


def _build_in_layout():
    col = {}
    perm = []
    for name in _SEG_ORDER:
        col[name] = len(perm)
        perm.extend(range(_ORIG_START[name], _ORIG_START[name] + _ORIG_SIZE[name]))
    zero = IN_WIDTH
    col["narrow"] = len(perm)
    blk0 = list(range(_ORIG_START["b_ik"], _ORIG_START["b_ik"] + IDX_DIM))
    blk0 += list(range(_ORIG_START["d_g"], _ORIG_START["d_g"] + _GATES_PER_KV_GROUP))
    blk0 += list(range(_ORIG_START["b_iw"], _ORIG_START["b_iw"] + IDX_HEADS))
    blk0 += [zero] * (LANE - len(blk0))
    blk1 = list(range(_ORIG_START["c_kr"], _ORIG_START["c_kr"] + MLA_ROPE))
    blk1 += list(range(_ORIG_START["d_g"] + _GATES_PER_KV_GROUP, _ORIG_START["d_g"] + 2 * _GATES_PER_KV_GROUP))
    blk1 += [zero] * (LANE - len(blk1))
    perm.extend(blk0 + blk1)
    width = _round_up(len(perm), 512)
    perm.extend([zero] * (width - len(perm)))
    return col, np.asarray(perm, np.int32), width


_COL, _IN_PERM, IN_WIDTH_PAD = _build_in_layout()
_GATE_LANE0 = IDX_DIM
_IW_LANE0 = IDX_DIM + _GATES_PER_KV_GROUP


def _rope_lane_tables(cos, sin):
    return jnp.concatenate([cos, cos], axis=-1), jnp.concatenate([-sin, sin], axis=-1)


def _lane_tables(positions, dim):
    cc, ss = _rope_lane_tables(*_rope_tables(positions, dim))
    return tuple(jnp.tile(t.reshape(-1, dim), (1, LANE // dim)) for t in (cc, ss))


def _seg64_sum(y):
    r = lax.broadcasted_iota(jnp.int32, (LANE, LANE), 0) // 64
    c = lax.broadcasted_iota(jnp.int32, (LANE, LANE), 1) // 64
    bd = jnp.where(r == c, 1.0, 0.0).astype(MXU_DTYPE)
    hi = y.astype(MXU_DTYPE)
    lo = (y - hi.astype(jnp.float32)).astype(MXU_DTYPE)
    return (jnp.dot(hi, bd, preferred_element_type=jnp.float32)
            + jnp.dot(lo, bd, preferred_element_type=jnp.float32))


def _prep_kernel(x_ref, g_ref, cc_ref, ss_ref, o_ref, *, n_blocks, seg, norm, scale):
    cc, ss, g = cc_ref[...], ss_ref[...], g_ref[...]
    lane = lax.broadcasted_iota(jnp.int32, cc.shape, 1)
    for h in range(n_blocks):
        x = x_ref[:, h * LANE:(h + 1) * LANE]
        if norm:
            if seg == LANE:
                ms = jnp.mean(x * x, axis=-1, keepdims=True)
            else:
                ms = _seg64_sum(x * x) * (1.0 / seg)
            x = x * lax.rsqrt(ms + EPS) * g
        if seg == LANE:
            partner = pltpu.roll(x, LANE // 2, axis=1)
        else:
            partner = jnp.where((lane % seg) < seg // 2, pltpu.roll(x, LANE - seg // 2, axis=1),
                                pltpu.roll(x, seg // 2, axis=1))
        y = x * cc + partner * ss
        if scale != 1.0:
            y = y * scale
        o_ref[:, h * LANE:(h + 1) * LANE] = y.astype(o_ref.dtype)


def _prep_heads(proj, col0, n_blocks, gain, cc, ss, *, seg, norm, scale=1.0, tm=256):
    m = proj.shape[0]
    tm = min(tm, m)
    width = LANE * n_blocks
    cb = col0 // width
    assert cb * width == col0
    return pl.pallas_call(
        functools.partial(_prep_kernel, n_blocks=n_blocks, seg=seg, norm=norm, scale=scale),
        grid=(m // tm,),
        in_specs=[pl.BlockSpec((tm, width), lambda i: (i, cb)),
                  pl.BlockSpec((1, LANE), lambda i: (0, 0)),
                  pl.BlockSpec((tm, LANE), lambda i: (i, 0)),
                  pl.BlockSpec((tm, LANE), lambda i: (i, 0))],
        out_specs=pl.BlockSpec((tm, width), lambda i: (i, 0)),
        out_shape=jax.ShapeDtypeStruct((m, width), MXU_DTYPE),
        compiler_params=_compiler_params(("parallel",)),
        name="prep_heads",
    )(proj, gain, cc, ss)


def _dot_nt(a, b):
    return lax.dot_general(a, b, (((1,), (1,)), ((), ())), preferred_element_type=jnp.float32)


LOG2E = math.log2(math.e)


def _softmax_step(s, mask, m, l, acc, v, *, rows_may_be_empty=False):
    if mask is not None:
        s = jnp.where(mask, s, NEG_INF)
    m_new = jnp.maximum(m, jnp.max(s, axis=-1, keepdims=True))
    alpha = jnp.exp2(m - m_new)
    p = jnp.exp2(s - m_new)
    if rows_may_be_empty:
        p = jnp.where(mask, p, 0.0)
    l = alpha * l + jnp.sum(p, axis=-1, keepdims=True)
    acc = alpha * acc + jnp.dot(p.astype(MXU_DTYPE), v, preferred_element_type=jnp.float32)
    return m_new, l, acc


def _softmax_finish(l, acc):
    return acc * (1.0 / jnp.maximum(l, 1e-30))


def _nsa_cmp_kernel(x_ref, pe_ref, w1_ref, w2_ref, gk_ref, cc_ref, ss_ref, kc_ref, vc_ref):
    nch = x_ref.shape[2]
    for kv in range(2):
        x = x_ref[kv, 0]
        lo = jnp.dot((x + pe_ref[kv, 0:1, :]).astype(MXU_DTYPE), w1_ref[kv, 0], preferred_element_type=jnp.float32)
        hi = jnp.dot((x + pe_ref[kv, 1:2, :]).astype(MXU_DTYPE), w1_ref[kv, 1], preferred_element_type=jnp.float32)
        hid = lo + pltpu.roll(hi, nch - 1, axis=0)
        comp = jnp.dot(jax.nn.gelu(hid).astype(MXU_DTYPE), w2_ref[kv], preferred_element_type=jnp.float32)
        if kv == 0:
            ms = jnp.mean(comp * comp, axis=-1, keepdims=True)
            y = comp * lax.rsqrt(ms + EPS) * gk_ref[...]
            y = y * cc_ref[0] + pltpu.roll(y, LANE // 2, axis=1) * ss_ref[0]
            kc_ref[0] = y.astype(kc_ref.dtype)
        else:
            vc_ref[0] = comp.astype(vc_ref.dtype)


def _nsa_compress(craw, pe2, w1, w2, gk, cc_c, ss_c, *, n_batch):
    _, gb, nch, width = craw.shape
    return pl.pallas_call(
        _nsa_cmp_kernel,
        grid=(gb,),
        in_specs=[pl.BlockSpec((2, 1, nch, width), lambda n: (0, n, 0, 0)),
                  pl.BlockSpec((2, 2, width), lambda n: (0, 0, 0)),
                  pl.BlockSpec((2, 2, width, CMP_HIDDEN), lambda n: (0, 0, 0, 0)),
                  pl.BlockSpec((2, CMP_HIDDEN, HEAD_DIM), lambda n: (0, 0, 0)),
                  pl.BlockSpec((1, HEAD_DIM), lambda n: (0, 0)),
                  pl.BlockSpec((1, nch, HEAD_DIM), lambda n: (n % n_batch, 0, 0)),
                  pl.BlockSpec((1, nch, HEAD_DIM), lambda n: (n % n_batch, 0, 0))],
        out_specs=[pl.BlockSpec((1, nch, HEAD_DIM), lambda n: (n, 0, 0)),
                   pl.BlockSpec((1, nch, HEAD_DIM), lambda n: (n, 0, 0))],
        out_shape=[jax.ShapeDtypeStruct((gb, nch, HEAD_DIM), MXU_DTYPE)] * 2,
        compiler_params=_compiler_params(("parallel",)),
        name="nsa_compress",
    )(craw, pe2, w1, w2, gk, cc_c, ss_c)


def _nsa_attn_kernel(q_ref, kc_ref, vc_ref, ks_ref, vs_ref, kw_ref, vw_ref, gate_ref, c2s_ref, exp_ref, o_ref, *,
                     tq, tk, n_sel, n_top):
    i = pl.program_id(2)
    R = NSA_Q_PER_KV
    rows = R * tq
    nch = kc_ref.shape[1]
    bf16 = MXU_DTYPE
    q = jnp.concatenate([q_ref[:, r * HEAD_DIM:(r + 1) * HEAD_DIM] for r in range(R)], axis=0)
    t_row = i * tq + lax.broadcasted_iota(jnp.int32, (rows, 1), 0) % tq

    s = _dot_nt(q, kc_ref[0])
    blk_end = lax.broadcasted_iota(jnp.int32, (1, nch), 1) * CMP_STRIDE + (CMP_LEN - 1)
    mask = blk_end <= t_row
    zero = jnp.zeros((rows, 1), jnp.float32)
    s = jnp.where(mask, s, NEG_INF)
    p_cmp = jnp.where(mask, jnp.exp2(s - jnp.max(s, axis=-1, keepdims=True)), 0.0)
    p_cmp = p_cmp * (1.0 / jnp.maximum(jnp.sum(p_cmp, axis=-1, keepdims=True), 1e-30))
    o_cmp = jnp.dot(p_cmp.astype(bf16), vc_ref[0], preferred_element_type=jnp.float32)

    imp_rows = jnp.dot(p_cmp.astype(bf16), c2s_ref[...], preferred_element_type=jnp.float32)
    imp = imp_rows[0:tq]
    for r in range(1, R):
        imp = imp + imp_rows[r * tq:(r + 1) * tq]
    imp_t = imp.T[0:n_sel]
    blk = lax.broadcasted_iota(jnp.int32, (n_sel, tq), 0)
    t_lane = i * tq + lax.broadcasted_iota(jnp.int32, (n_sel, tq), 1)
    dist = lax.shift_right_logical(t_lane, int(math.log2(SEL_BLOCK))) - blk
    forced = (blk < SEL_INIT) | ((dist >= 0) & (dist < SEL_LOCAL))
    val = jnp.where(forced, POS_INF, jnp.where(blk * SEL_BLOCK <= t_lane, imp_t, NEG_INF))
    rank = jnp.zeros((n_sel, tq), jnp.float32)
    for mblk in range(n_sel):
        vm = val[mblk:mblk + 1, :]
        rank = rank + jnp.where((vm > val) | ((vm == val) & (blk > mblk)), 1.0, 0.0)
    sel_t = jnp.where(rank < n_top, 1.0, 0.0)
    sel_t = jnp.concatenate([sel_t, jnp.zeros((LANE - n_sel, tq), jnp.float32)], axis=0) if n_sel < LANE else sel_t
    sel_q = sel_t.T.astype(bf16)
    sel_rows = jnp.concatenate([sel_q] * R, axis=0)

    key_lane = lax.broadcasted_iota(jnp.int32, (1, tk), 1)

    def slc_body(j, carry):
        off = pl.multiple_of(j * tk, tk)
        k = ks_ref[pl.ds(off, tk), :]
        v = vs_ref[pl.ds(off, tk), :].astype(bf16)
        picked = jnp.dot(sel_rows, exp_ref[j], preferred_element_type=jnp.float32)
        msk = (picked > 0.5) & (off + key_lane <= t_row)
        return _softmax_step(_dot_nt(q, k), msk, *carry, v)

    init = (zero + NEG_INF, zero, jnp.zeros((rows, HEAD_DIM), jnp.float32))
    _, l, acc = lax.fori_loop(0, ((i + 1) * tq + tk - 1) // tk, slc_body, init)
    o_slc = _softmax_finish(l, acc)

    span = WINDOW + tq
    off = pl.multiple_of(jnp.maximum(i * tq - WINDOW, 0), tq)
    key = off + lax.broadcasted_iota(jnp.int32, (1, span), 1)
    msk = (key <= t_row) & (key > t_row - WINDOW)
    _, l, acc = _softmax_step(_dot_nt(q, kw_ref[pl.ds(off, span), :]), msk, *init,
                              vw_ref[pl.ds(off, span), :].astype(bf16))
    o_win = _softmax_finish(l, acc)

    gate = jax.nn.sigmoid(gate_ref[...])
    for r in range(R):
        c = _GATE_LANE0 + 3 * r
        rs = slice(r * tq, (r + 1) * tq)
        o_ref[:, r * HEAD_DIM:(r + 1) * HEAD_DIM] = (gate[:, c:c + 1] * o_cmp[rs]
                                                     + gate[:, c + 1:c + 2] * o_slc[rs]
                                                     + gate[:, c + 2:c + 3] * o_win[rs])


def _nsa_mixer(proj, B, S, cc_h, ss_h, positions, g_q, g_k, cmp_w1, cmp_w2, cmp_pe):
    M = B * S
    G, R, d = NSA_KV_HEADS, NSA_Q_PER_KV, HEAD_DIM
    bf16 = MXU_DTYPE
    tq = Q_BLOCK
    nq = S // tq
    nch = S // CMP_STRIDE
    n_sel = S // SEL_BLOCK
    n_top = min(SEL_TOPN, n_sel)
    kv0 = _COL["d_kv"]

    qn = _prep_heads(proj, _COL["d_q"], HEADS_PER_GROUP, g_q.reshape(1, d), cc_h, ss_h,
                     seg=LANE, norm=True, scale=d ** -0.5 * LOG2E)
    k_slc =_prep_heads(proj, kv0 + (1 * 2 + 0) * G * d, G, g_k[1].reshape(1, d), cc_h, ss_h, seg=LANE, norm=True)
    k_win = _prep_heads(proj, kv0 + (2 * 2 + 0) * G * d, G, g_k[2].reshape(1, d), cc_h, ss_h, seg=LANE, norm=True)

    craw = proj[:, kv0:kv0 + 2 * G * d].reshape(B, S, 2, G, d).transpose(2, 3, 0, 1, 4)
    craw = craw.reshape(2, G * B, nch, CMP_STRIDE * d)
    ends = np.minimum(np.arange(nch) * CMP_STRIDE + CMP_LEN - 1, S - 1)
    cc_c, ss_c = _rope_lane_tables(*_rope_tables(positions[:, ends], d))
    pe2 = cmp_pe.reshape(2, 2, CMP_STRIDE * d)
    w1 = cmp_w1.astype(bf16).reshape(2, 2, CMP_STRIDE * d, CMP_HIDDEN)
    k_cmp, v_cmp = _nsa_compress(craw, pe2, w1, cmp_w2.astype(bf16), g_k[0].reshape(1, d), cc_c, ss_c, n_batch=B)

    starts = np.arange(nch) * CMP_STRIDE
    sel_start = np.arange(LANE) * SEL_BLOCK
    c2s = ((starts[:, None] < sel_start[None, :] + SEL_BLOCK) & (starts[:, None] + CMP_LEN > sel_start[None, :])
           & (np.arange(nch)[:, None] < nch - 1) & (np.arange(LANE)[None, :] < n_sel))
    tk = min(512, S)
    expand = (np.arange(LANE)[None, :, None] == (np.arange(S // tk)[:, None, None] * tk + np.arange(tk)[None, None, :]) // SEL_BLOCK)
    c2s = jnp.asarray(c2s, bf16)
    expand = jnp.asarray(expand, bf16)

    narrow = _COL["narrow"] // LANE
    col_vs = (kv0 + (1 * 2 + 1) * G * d) // LANE
    col_vw = (kv0 + (2 * 2 + 1) * G * d) // LANE
    return pl.pallas_call(
        functools.partial(_nsa_attn_kernel, tq=tq, tk=tk, n_sel=n_sel, n_top=n_top),
        grid=(B, G, nq),
        in_specs=[pl.BlockSpec((tq, R * d), lambda b, g, i: (b * nq + i, g)),
                  pl.BlockSpec((1, nch, d), lambda b, g, i: (g * B + b, 0, 0)),
                  pl.BlockSpec((1, nch, d), lambda b, g, i: (g * B + b, 0, 0)),
                  pl.BlockSpec((S, d), lambda b, g, i: (b, g)),
                  pl.BlockSpec((S, d), lambda b, g, i: (b, col_vs + g)),
                  pl.BlockSpec((S, d), lambda b, g, i: (b, g)),
                  pl.BlockSpec((S, d), lambda b, g, i: (b, col_vw + g)),
                  pl.BlockSpec((tq, LANE), lambda b, g, i: (b * nq + i, narrow + g)),
                  pl.BlockSpec((nch, LANE), lambda b, g, i: (0, 0)),
                  pl.BlockSpec((S // tk, LANE, tk), lambda b, g, i: (0, 0, 0))],
        out_specs=pl.BlockSpec((tq, R * d), lambda b, g, i: (b * nq + i, g)),
        out_shape=jax.ShapeDtypeStruct((M, GROUP_WIDTH), jnp.float32),
        compiler_params=_compiler_params(("parallel", "parallel", "arbitrary")),
        name="nsa_attention",
    )(qn, k_cmp, v_cmp, k_slc, proj, k_win, proj, proj, c2s, expand)


_INT_MIN = -2 ** 31


def _sortable_key(x):
    b = lax.bitcast_convert_type(x + 0.0, jnp.int32)
    return jnp.where(b >= 0, b, b ^ 0x7FFFFFFF)


def _dsa_kernel(q_ref, k_ref, v_ref, iq_ref, ik_ref, nar_ref, o_ref, key_ref, msk_ref, w_ref, *,
                tq, tk, n_keep, idx_bits):
    i = pl.program_id(1)
    H = HEADS_PER_GROUP
    bf16 = MXU_DTYPE
    t_col = i * tq + lax.broadcasted_iota(jnp.int32, (tq, 1), 0)
    key_lane = lax.broadcasted_iota(jnp.int32, (1, tq), 1)
    lane = lax.broadcasted_iota(jnp.int32, (tq, LANE), 1)
    iw = nar_ref[...] * (IDX_HEADS ** -0.5)
    for hd in range(IDX_HEADS):
        w_ref[hd] = jnp.broadcast_to(iw[:, _IW_LANE0 + hd:_IW_LANE0 + hd + 1], (tq, tq))
    iq_rows = jnp.concatenate([iq_ref[:, pr * LANE:(pr + 1) * LANE] for pr in range(IDX_HEADS // 2)], axis=0)

    def score_body(j, _):
        off = pl.multiple_of(j * tq, tq)
        ik = ik_ref[pl.ds(off, tq), :]
        ik_lo = jnp.where(lane < IDX_DIM, ik, jnp.zeros_like(ik))
        ik_hi = jnp.where(lane < IDX_DIM, jnp.zeros_like(ik), pltpu.roll(ik.astype(jnp.float32), IDX_DIM, axis=1).astype(bf16))
        rel = (jnp.maximum(_dot_nt(iq_rows, ik_lo), 0.0), jnp.maximum(_dot_nt(iq_rows, ik_hi), 0.0))
        sc = jnp.zeros((tq, tq), jnp.float32)
        for hd in range(IDX_HEADS):
            sc = sc + w_ref[hd] * rel[hd % 2][(hd // 2) * tq:(hd // 2 + 1) * tq]
        sc = jnp.where(off + key_lane <= t_col, sc, NEG_INF)
        key_ref[j] = _sortable_key(sc)
        return 0

    lax.fori_loop(0, i + 1, score_body, 0)

    def count(pred):
        def body(j, part):
            return part + jnp.where(pred(key_ref[j], j * tq), 1.0, 0.0)
        part = lax.fori_loop(0, i + 1, body, jnp.zeros((tq, tq), jnp.float32))
        return jnp.sum(part, axis=-1, keepdims=True)

    c0 = count(lambda kk, off: kk >= 0)
    thr = jnp.where(c0 >= n_keep, 0, _INT_MIN).astype(jnp.int32)

    def thr_body(it, thr):
        cand = thr | lax.shift_left(jnp.int32(1), 30 - it)
        c = count(lambda kk, off: kk >= cand)
        return jnp.where(c >= n_keep, cand, thr)

    thr = lax.fori_loop(0, 31, thr_body, thr)

    need = n_keep - count(lambda kk, off: kk > thr)

    def cut_body(it, cut):
        cand = cut | lax.shift_left(jnp.int32(1), idx_bits - 1 - it)
        c = count(lambda kk, off: (kk == thr) & (off + key_lane < cand))
        return jnp.where(c < need, cand, cut)

    cut = lax.fori_loop(0, idx_bits, cut_body, jnp.zeros((tq, 1), jnp.int32))

    tpc = tk // tq
    n_chunks = (i + tpc) // tpc

    def mask_body(j, _):
        @pl.when(j <= i)
        def _():
            kk = key_ref[j]
            kidx = j * tq + key_lane
            sel = (kk > thr) | ((kk == thr) & (kidx <= cut))
            msk_ref[j] = jnp.where(sel & (kidx <= t_col), 1.0, 0.0)

        @pl.when(j > i)
        def _():
            msk_ref[j] = jnp.zeros((tq, tq), jnp.float32)

        return 0

    lax.fori_loop(0, n_chunks * tpc, mask_body, 0)

    hpp = 4
    rows = hpp * tq
    zero = jnp.zeros((rows, 1), jnp.float32)
    qs = [jnp.concatenate([q_ref[:, (hc * hpp + r) * HEAD_DIM:(hc * hpp + r + 1) * HEAD_DIM] for r in range(hpp)], axis=0)
          for hc in range(H // hpp)]

    def att_body(j, carry):
        off = pl.multiple_of(j * tk, tk)
        k = k_ref[pl.ds(off, tk), :]
        v = v_ref[pl.ds(off, tk), :].astype(bf16)
        m1 = jnp.concatenate([msk_ref[j * tpc + u] for u in range(tpc)], axis=1) > 0.5
        msk = jnp.concatenate([m1] * hpp, axis=0)
        return tuple(_softmax_step(_dot_nt(q, k), msk, *carry[c], v) for c, q in enumerate(qs))

    init = tuple((zero + NEG_INF, zero, jnp.zeros((rows, HEAD_DIM), jnp.float32)) for _ in qs)
    res = lax.fori_loop(0, n_chunks, att_body, init)
    for hc, (_, l, acc) in enumerate(res):
        o = _softmax_finish(l, acc)
        for r in range(hpp):
            hh = hc * hpp + r
            o_ref[:, hh * HEAD_DIM:(hh + 1) * HEAD_DIM] = o[r * tq:(r + 1) * tq]


def _dsa_mixer(proj, B, S, cc_h, ss_h, cc_i, ss_i, g_q, g_k):
    M = B * S
    d = HEAD_DIM
    tq = Q_BLOCK
    nq = S // tq
    n_keep = min(DSA_TOPK, S // 4)
    qn = _prep_heads(proj, _COL["b_q"], HEADS_PER_GROUP, g_q.reshape(1, d), cc_h, ss_h, seg=LANE, norm=True,
                     scale=d ** -0.5 * LOG2E)
    kn =_prep_heads(proj, _COL["b_k"], 1, g_k.reshape(1, d), cc_h, ss_h, seg=LANE, norm=True)
    ones = jnp.ones((1, LANE), jnp.float32)
    iqn = _prep_heads(proj, _COL["b_iq"], IDX_HEADS * IDX_DIM // LANE, ones, cc_i, ss_i, seg=IDX_DIM, norm=False,
                      scale=IDX_DIM ** -0.5)
    ikn = _prep_heads(proj, _COL["narrow"], 1, ones, cc_i, ss_i, seg=IDX_DIM, norm=False)
    narrow = _COL["narrow"] // LANE
    col_v = _COL["b_v"] // LANE
    return pl.pallas_call(
        functools.partial(_dsa_kernel, tq=tq, tk=min(512, S), n_keep=n_keep, idx_bits=int(math.log2(S))),
        grid=(B, nq),
        in_specs=[pl.BlockSpec((tq, GROUP_WIDTH), lambda b, i: (b * nq + i, 0)),
                  pl.BlockSpec((S, d), lambda b, i: (b, 0)),
                  pl.BlockSpec((S, d), lambda b, i: (b, col_v)),
                  pl.BlockSpec((tq, IDX_HEADS * IDX_DIM), lambda b, i: (b * nq + i, 0)),
                  pl.BlockSpec((S, LANE), lambda b, i: (b, 0)),
                  pl.BlockSpec((tq, LANE), lambda b, i: (b * nq + i, narrow))],
        out_specs=pl.BlockSpec((tq, GROUP_WIDTH), lambda b, i: (b * nq + i, 0)),
        out_shape=jax.ShapeDtypeStruct((M, GROUP_WIDTH), jnp.float32),
        scratch_shapes=[pltpu.VMEM((nq, tq, tq), jnp.int32), pltpu.VMEM((nq, tq, tq), jnp.float32),
                        pltpu.VMEM((IDX_HEADS, tq, tq), jnp.float32)],
        compiler_params=_compiler_params(("parallel", "arbitrary")),
        name="dsa_attention",
    )(qn, kn, proj, iqn, ikn, proj)


def _causal_chains(qs, k_ref, v_ref, dk, dv, t_row, t0, tk):
    rows = qs[0].shape[0]
    zero = jnp.zeros((rows, 1), jnp.float32)

    def step(j, carry, masked):
        off = pl.multiple_of(j * tk, tk)
        mask = (off + lax.broadcasted_iota(jnp.int32, (1, tk), 1) <= t_row) if masked else None
        out = []
        for c, q in enumerate(qs):
            k = k_ref[pl.ds(off, tk), c * dk:(c + 1) * dk]
            v = v_ref[pl.ds(off, tk), c * dv:(c + 1) * dv].astype(MXU_DTYPE)
            out.append(_softmax_step(_dot_nt(q, k), mask, *carry[c], v))
        return tuple(out)

    init = tuple((zero + NEG_INF, zero, jnp.zeros((rows, dv), jnp.float32)) for _ in qs)
    n_full = t0 // tk
    res = lax.fori_loop(0, n_full, lambda j, c: step(j, c, False), init)
    res = step(n_full, res, True)
    return [_softmax_finish(l, acc) for _, l, acc in res]


def _diff_attn_kernel(q_ref, k_ref, v_ref, lam_ref, g_ref, o_ref, *, tq, tk, hb, out_scale):
    i = pl.program_id(2)
    d = HEAD_DIM
    lane = lax.broadcasted_iota(jnp.int32, (tq, d), 1)
    qs = []
    for c in range(hb):
        q = q_ref[:, c * d:(c + 1) * d]
        zeros = jnp.zeros_like(q)
        qs.append(jnp.concatenate([jnp.where(lane < DIFF_QK_DIM, q, zeros), jnp.where(lane < DIFF_QK_DIM, zeros, q)], axis=0))
    t_row = i * tq + lax.broadcasted_iota(jnp.int32, (2 * tq, 1), 0) % tq
    outs = _causal_chains(qs, k_ref, v_ref, d, d, t_row, i * tq, tk)
    for c, o2 in enumerate(outs):
        o = o2[0:tq] - lam_ref[...] * o2[tq:2 * tq]
        ms = jnp.mean(o * o, axis=-1, keepdims=True)
        o_ref[:, c * d:(c + 1) * d] = o * lax.rsqrt(ms + EPS) * g_ref[...] * out_scale


def _diff_mixer(proj, B, S, cc_d, ss_d, g_q, g_k, g_sub, lam_params, layer_idx, *, tq=128, tk=512, hb=2):
    M = B * S
    d = HEAD_DIM
    nq = S // tq
    tk = min(tk, S)
    reps = LANE // DIFF_QK_DIM
    qn = _prep_heads(proj, _COL["a_q"], HEADS_PER_GROUP, jnp.tile(g_q, reps).reshape(1, d), cc_d, ss_d,
                     seg=DIFF_QK_DIM, norm=True, scale=DIFF_QK_DIM ** -0.5 * LOG2E)
    kn = _prep_heads(proj, _COL["a_k"], HEADS_PER_GROUP, jnp.tile(g_k, reps).reshape(1, d), cc_d, ss_d,
                     seg=DIFF_QK_DIM, norm=True)
    lam_init = 0.8 - 0.6 * math.exp(-0.3 * layer_idx)
    lp = lam_params.astype(jnp.float32)
    lam = jnp.exp(jnp.sum(lp[0] * lp[1])) - jnp.exp(jnp.sum(lp[2] * lp[3])) + lam_init
    col_v = _COL["a_v"] // (hb * d)
    return pl.pallas_call(
        functools.partial(_diff_attn_kernel, tq=tq, tk=tk, hb=hb, out_scale=1.0 - lam_init),
        grid=(B, HEADS_PER_GROUP // hb, nq),
        in_specs=[pl.BlockSpec((tq, hb * d), lambda b, h, i: (b * nq + i, h)),
                  pl.BlockSpec((S, hb * d), lambda b, h, i: (b, h)),
                  pl.BlockSpec((S, hb * d), lambda b, h, i: (b, col_v + h)),
                  pl.BlockSpec((1, d), lambda b, h, i: (0, 0)),
                  pl.BlockSpec((1, d), lambda b, h, i: (0, 0))],
        out_specs=pl.BlockSpec((tq, hb * d), lambda b, h, i: (b * nq + i, h)),
        out_shape=jax.ShapeDtypeStruct((M, GROUP_WIDTH), jnp.float32),
        compiler_params=_compiler_params(("parallel", "parallel", "arbitrary")),
        name="diff_attention",
    )(qn, kn, proj, jnp.full((1, d), lam, jnp.float32), g_sub.reshape(1, d))


MLA_QK_PAD = 2 * LANE


def _rmsnorm_cols_kernel(x_ref, g_ref, o_ref):
    x = x_ref[...]
    ms = jnp.mean(x * x, axis=-1, keepdims=True)
    o_ref[...] = (x * lax.rsqrt(ms + EPS) * g_ref[...]).astype(o_ref.dtype)


def _rmsnorm_cols(proj, col0, width, g, *, tm=512):
    m = proj.shape[0]
    tm = min(tm, m)
    cb = col0 // width
    assert cb * width == col0
    return pl.pallas_call(
        _rmsnorm_cols_kernel,
        grid=(m // tm,),
        in_specs=[pl.BlockSpec((tm, width), lambda i: (i, cb)), pl.BlockSpec((1, width), lambda i: (0, 0))],
        out_specs=pl.BlockSpec((tm, width), lambda i: (i, 0)),
        out_shape=jax.ShapeDtypeStruct((m, width), MXU_DTYPE),
        compiler_params=_compiler_params(("parallel",)),
        name="rmsnorm_cols",
    )(proj, g.reshape(1, width))


def _mla_prep_kernel(a_ref, b_ref, g_ref, cc_ref, ss_ref, o_ref, *, a_stride, b_col0, b_stride, scale):
    cc, ss = cc_ref[...], ss_ref[...]
    lane = lax.broadcasted_iota(jnp.int32, cc.shape, 1)
    for h in range(HEADS_PER_GROUP):
        a = a_ref[:, h * a_stride:h * a_stride + LANE]
        b = jnp.where(lane < MLA_ROPE, b_ref[:, b_col0 + h * b_stride:b_col0 + h * b_stride + LANE], 0.0)
        ms = (jnp.sum(a * a, axis=-1, keepdims=True) + jnp.sum(b * b, axis=-1, keepdims=True)) * (1.0 / (MLA_NOPE + MLA_ROPE))
        r = lax.rsqrt(ms + EPS)
        ya = a * r * g_ref[:, 0:LANE]
        yb = b * r * g_ref[:, LANE:2 * LANE]
        partner = jnp.where((lane % MLA_ROPE) < MLA_ROPE // 2, pltpu.roll(yb, LANE - MLA_ROPE // 2, axis=1),
                            pltpu.roll(yb, MLA_ROPE // 2, axis=1))
        yb = yb * cc + partner * ss
        o_ref[:, h * MLA_QK_PAD:h * MLA_QK_PAD + LANE] = (ya * scale).astype(o_ref.dtype)
        o_ref[:, h * MLA_QK_PAD + LANE:(h + 1) * MLA_QK_PAD] = (yb * scale).astype(o_ref.dtype)


def _mla_prep(a_arr, a_width, a_stride, b_arr, b_block, b_width, b_col0, b_stride, gain, cc, ss, *, scale=1.0, tm=256):
    m = a_arr.shape[0]
    tm = min(tm, m)
    g2 = jnp.concatenate([gain, jnp.zeros((MLA_QK_PAD - gain.shape[0],), gain.dtype)]).reshape(1, MLA_QK_PAD)
    return pl.pallas_call(
        functools.partial(_mla_prep_kernel, a_stride=a_stride, b_col0=b_col0, b_stride=b_stride, scale=scale),
        grid=(m // tm,),
        in_specs=[pl.BlockSpec((tm, a_width), lambda i: (i, 0)),
                  pl.BlockSpec((tm, b_width), lambda i: (i, b_block)),
                  pl.BlockSpec((1, MLA_QK_PAD), lambda i: (0, 0)),
                  pl.BlockSpec((tm, LANE), lambda i: (i, 0)),
                  pl.BlockSpec((tm, LANE), lambda i: (i, 0))],
        out_specs=pl.BlockSpec((tm, HEADS_PER_GROUP * MLA_QK_PAD), lambda i: (i, 0)),
        out_shape=jax.ShapeDtypeStruct((m, HEADS_PER_GROUP * MLA_QK_PAD), MXU_DTYPE),
        compiler_params=_compiler_params(("parallel",)),
        name="mla_prep",
    )(a_arr, b_arr, g2, cc, ss)


def _mla_attn_kernel(q_ref, k_ref, v_ref, o_ref, *, tq, tk, hb):
    i = pl.program_id(2)
    d = HEAD_DIM
    t_row = i * tq + lax.broadcasted_iota(jnp.int32, (tq, 1), 0)
    qs = [q_ref[:, c * MLA_QK_PAD:(c + 1) * MLA_QK_PAD] for c in range(hb)]
    outs = _causal_chains(qs, k_ref, v_ref, MLA_QK_PAD, d, t_row, i * tq, tk)
    for c, o in enumerate(outs):
        o_ref[:, c * d:(c + 1) * d] = o


def _mla_mixer(proj, B, S, cc_m, ss_m, g_cq, g_ckv, w_uq, w_uk, w_uv, g_q, g_k, *, tq=256, tk=512, hb=2):
    M = B * S
    H, d = HEADS_PER_GROUP, HEAD_DIM
    tq = min(tq, S)
    tk = min(tk, S)
    nq = S // tq
    dqk = MLA_NOPE + MLA_ROPE
    w_q = jnp.pad(w_uq.astype(MXU_DTYPE).reshape(MLA_Q_RANK, H, dqk), ((0, 0), (0, 0), (0, MLA_QK_PAD - dqk)))
    w_q = w_q.reshape(MLA_Q_RANK, H * MLA_QK_PAD)
    w_kv = jnp.concatenate([w_uk, w_uv], axis=1).astype(MXU_DTYPE)
    cq = _rmsnorm_cols(proj, _COL["c_q"], MLA_Q_RANK, g_cq)
    ckv = _rmsnorm_cols(proj, _COL["c_kv"], MLA_KV_RANK, g_ckv)
    q_up = _matmul(cq, w_q, tm=min(1024, M), tn=1024, tk=MLA_Q_RANK)
    kv_up = _matmul(ckv, w_kv, tm=min(1024, M), tn=1024, tk=MLA_KV_RANK)
    qn = _mla_prep(q_up, H * MLA_QK_PAD, MLA_QK_PAD, q_up, 0, H * MLA_QK_PAD, LANE, MLA_QK_PAD, g_q, cc_m, ss_m,
                   scale=dqk ** -0.5 * LOG2E)
    kn = _mla_prep(kv_up, H * MLA_NOPE, LANE, proj, _COL["narrow"] // LANE + 1, LANE, 0, 0, g_k, cc_m, ss_m)
    return pl.pallas_call(
        functools.partial(_mla_attn_kernel, tq=tq, tk=tk, hb=hb),
        grid=(B, H // hb, nq),
        in_specs=[pl.BlockSpec((tq, hb * MLA_QK_PAD), lambda b, h, i: (b * nq + i, h)),
                  pl.BlockSpec((S, hb * MLA_QK_PAD), lambda b, h, i: (b, h)),
                  pl.BlockSpec((S, hb * d), lambda b, h, i: (b, H // hb + h))],
        out_specs=pl.BlockSpec((tq, hb * d), lambda b, h, i: (b * nq + i, h)),
        out_shape=jax.ShapeDtypeStruct((M, GROUP_WIDTH), jnp.float32),
        compiler_params=_compiler_params(("parallel", "parallel", "arbitrary")),
        name="mla_attention",
    )(qn, kn, kv_up)


def _mix_kernel(a_ref, b_ref, c_ref, d_ref, g_ref, o_ref):
    o_ref[:, 0:GROUP_WIDTH] = a_ref[...].astype(o_ref.dtype)
    for n, ref in enumerate((b_ref, c_ref, d_ref)):
        x = ref[...]
        ms = jnp.mean(x * x, axis=-1, keepdims=True)
        o_ref[:, (n + 1) * GROUP_WIDTH:(n + 2) * GROUP_WIDTH] = (
            x * lax.rsqrt(ms + EPS) * g_ref[n:n + 1, :]).astype(o_ref.dtype)


def _mix_groups(o_a, o_b, o_c, o_d, g_out, *, tm=512):
    m = o_a.shape[0]
    tm = min(tm, m)
    spec = pl.BlockSpec((tm, GROUP_WIDTH), lambda i: (i, 0))
    return pl.pallas_call(
        _mix_kernel,
        grid=(m // tm,),
        in_specs=[spec, spec, spec, spec, pl.BlockSpec((3, GROUP_WIDTH), lambda i: (0, 0))],
        out_specs=pl.BlockSpec((tm, MIX_WIDTH), lambda i: (i, 0)),
        out_shape=jax.ShapeDtypeStruct((m, MIX_WIDTH), MXU_DTYPE),
        compiler_params=_compiler_params(("parallel",)),
        name="mix_groups",
    )(o_a, o_b, o_c, o_d, g_out)


def _pad_cols(w, n):
    return jnp.pad(w, ((0, 0), (0, n - w.shape[1])))


def kernel(x, p, positions, w_in, w_out, g_mix, g_ffn, w_gate, w_up, w_down, w_ple_proj, w_ple_gate, g_ple,
           g_group_out, diff_g_q, diff_g_k, diff_g_sub, diff_lambda, dsa_g_q, dsa_g_k, mla_g_cq, mla_g_ckv,
           mla_w_uq, mla_w_uk, mla_w_uv, mla_g_q, mla_g_k, nsa_g_q, nsa_g_k, nsa_cmp_w1, nsa_cmp_w2, nsa_cmp_pe):
    B, S = x.shape[:2]
    M = B * S
    H = HEADS_PER_GROUP
    bf16 = MXU_DTYPE
    cc_h, ss_h = _lane_tables(positions, HEAD_DIM)
    cc_i, ss_i = _lane_tables(positions, IDX_DIM)
    cc_d, ss_d = _lane_tables(positions, DIFF_QK_DIM)
    cc_m, ss_m = _lane_tables(positions, MLA_ROPE)
    h = x.reshape(M, D_MODEL)
    for i in range(DEPTH):
        w_in_i = jnp.take(_pad_cols(w_in[i].astype(bf16), IN_WIDTH + 1), _IN_PERM, axis=1)
        u = _rmsnorm_rows(h, g_mix[i])
        proj = _matmul(u, w_in_i, tm=1024, tn=512, tk=D_MODEL)
        o_a = _diff_mixer(proj, B, S, cc_d, ss_d, diff_g_q[i], diff_g_k[i], diff_g_sub[i], diff_lambda[i], i)
        o_b = _dsa_mixer(proj, B, S, cc_h, ss_h, cc_i, ss_i, dsa_g_q[i], dsa_g_k[i])
        o_c = _mla_mixer(proj, B, S, cc_m, ss_m, mla_g_cq[i], mla_g_ckv[i], mla_w_uq[i], mla_w_uk[i], mla_w_uv[i],
                         mla_g_q[i], mla_g_k[i])
        o_d = _nsa_mixer(proj, B, S, cc_h, ss_h, positions, nsa_g_q[i], nsa_g_k[i],
                         nsa_cmp_w1[i], nsa_cmp_w2[i], nsa_cmp_pe[i])
        mixed = _mix_groups(o_a, o_b, o_c, o_d, g_group_out[i])
        h = _matmul(mixed, w_out[i].astype(bf16), res=h, tm=1024, tn=512, tk=MIX_WIDTH)
        u = _rmsnorm_rows(h, g_ffn[i])
        act = _swiglu(u, w_gate[i].astype(bf16), w_up[i].astype(bf16), tm=1024, tn=256)
        h = _matmul(act, w_down[i].astype(bf16), res=h, tm=512, tn=256, tk=D_FF)
        u = _rmsnorm_rows(h, g_ple[i])
        h = _ple(u, w_ple_gate[i].astype(bf16), p[i].reshape(M, PLE_DIM).astype(bf16), w_ple_proj[i].astype(bf16), h,
                 tm=1024, tn=512)
    return h.reshape(B, S, D_MODEL)
```

```python
import functools
import math

import numpy as np
import jax
import jax.numpy as jnp
from jax import lax
from jax.experimental import pallas as pl
from jax.experimental.pallas import tpu as pltpu

D_MODEL = 4096
DEPTH = 4
HEAD_DIM = 128
N_GROUPS = 4
HEADS_PER_GROUP = D_MODEL // HEAD_DIM // N_GROUPS
GROUP_WIDTH = HEADS_PER_GROUP * HEAD_DIM
MIX_WIDTH = N_GROUPS * GROUP_WIDTH
D_FF = ((8 * D_MODEL + 3 * 256 - 1) // (3 * 256)) * 256
PLE_DIM = 256
ROPE_THETA = 10000.0
EPS = 1e-6
Q_BLOCK = 128
NEG_INF = -1e30
MASK_BIAS = -2.0 ** 100
POS_INF = 1e30

DIFF_QK_DIM = HEAD_DIM // 2
DIFF_V_DIM = HEAD_DIM
IDX_HEADS = 16
IDX_DIM = 64
DSA_TOPK = 256
MLA_Q_RANK = 1024
MLA_KV_RANK = 512
MLA_NOPE = 128
MLA_ROPE = 64
MLA_V = HEAD_DIM
NSA_KV_HEADS = 2
NSA_Q_PER_KV = HEADS_PER_GROUP // NSA_KV_HEADS
CMP_LEN = 32
CMP_STRIDE = 16
CMP_HIDDEN = HEAD_DIM
SEL_BLOCK = 64
SEL_TOPN = 16
SEL_INIT = 1
SEL_LOCAL = 2
SEL_Q_BLOCK = 32
WINDOW = 512

IN_SIZES = (
    HEADS_PER_GROUP * 2 * DIFF_QK_DIM, HEADS_PER_GROUP * 2 * DIFF_QK_DIM, HEADS_PER_GROUP * DIFF_V_DIM,
    GROUP_WIDTH, HEAD_DIM, HEAD_DIM, IDX_HEADS * IDX_DIM, IDX_DIM, IDX_HEADS,
    MLA_Q_RANK, MLA_KV_RANK, MLA_ROPE,
    GROUP_WIDTH, 3 * 2 * NSA_KV_HEADS * HEAD_DIM, 3 * HEADS_PER_GROUP,
)
IN_WIDTH = sum(IN_SIZES)

V7X_VMEM_LIMIT_BYTES = 56 * 1024 * 1024
LANE = 128
MXU_DTYPE = getattr(jnp, "bfloat16")


def _round_up(n, m):
    return (n + m - 1) // m * m


def _compiler_params(semantics):
    return pltpu.CompilerParams(dimension_semantics=semantics, vmem_limit_bytes=V7X_VMEM_LIMIT_BYTES)


def _rmsnorm_rows_kernel(x_ref, g_ref, o_ref):
    x = x_ref[...]
    ms = jnp.mean(x * x, axis=-1, keepdims=True)
    o_ref[...] = (x * lax.rsqrt(ms + EPS) * g_ref[...]).astype(o_ref.dtype)


def _rmsnorm_rows(x, g, *, tm=256):
    m, d = x.shape
    out_dtype = MXU_DTYPE
    return pl.pallas_call(
        _rmsnorm_rows_kernel,
        grid=(m // tm,),
        in_specs=[pl.BlockSpec((tm, d), lambda i: (i, 0)), pl.BlockSpec((1, d), lambda i: (0, 0))],
        out_specs=pl.BlockSpec((tm, d), lambda i: (i, 0)),
        out_shape=jax.ShapeDtypeStruct((m, d), out_dtype),
        compiler_params=_compiler_params(("parallel",)),
        name="rmsnorm_rows",
    )(x, g.reshape(1, d))


def _mm_kernel(a_ref, w_ref, o_ref, acc_ref, *, nk):
    k = pl.program_id(2)

    @pl.when(k == 0)
    def _():
        acc_ref[...] = jnp.zeros_like(acc_ref)

    acc_ref[...] += jnp.dot(a_ref[...], w_ref[...], preferred_element_type=jnp.float32)

    @pl.when(k == nk - 1)
    def _():
        o_ref[...] = acc_ref[...].astype(o_ref.dtype)


def _mm_res_kernel(a_ref, w_ref, r_ref, o_ref, acc_ref, *, nk):
    k = pl.program_id(2)

    @pl.when(k == 0)
    def _():
        acc_ref[...] = jnp.zeros_like(acc_ref)

    acc_ref[...] += jnp.dot(a_ref[...], w_ref[...], preferred_element_type=jnp.float32)

    @pl.when(k == nk - 1)
    def _():
        o_ref[...] = (r_ref[...] + acc_ref[...]).astype(o_ref.dtype)


def _matmul(a, w, *, res=None, tm, tn, tk, out_dtype=jnp.float32):
    m, kdim = a.shape
    n = w.shape[1]
    nk = kdim // tk
    assert m % tm == 0 and n % tn == 0 and kdim % tk == 0
    in_specs = [pl.BlockSpec((tm, tk), lambda i, j, k: (i, k)), pl.BlockSpec((tk, tn), lambda i, j, k: (k, j))]
    args = [a, w]
    if res is None:
        body = functools.partial(_mm_kernel, nk=nk)
    else:
        body = functools.partial(_mm_res_kernel, nk=nk)
        in_specs.append(pl.BlockSpec((tm, tn), lambda i, j, k: (i, j)))
        args.append(res)
    return pl.pallas_call(
        body,
        grid=(m // tm, n // tn, nk),
        in_specs=in_specs,
        out_specs=pl.BlockSpec((tm, tn), lambda i, j, k: (i, j)),
        out_shape=jax.ShapeDtypeStruct((m, n), out_dtype),
        scratch_shapes=[pltpu.VMEM((tm, tn), jnp.float32)],
        compiler_params=_compiler_params(("parallel", "parallel", "arbitrary")),
        name="matmul_res" if res is not None else "matmul",
    )(*args)


def _swiglu_kernel(a_ref, wg_ref, wu_ref, o_ref):
    a = a_ref[...]
    g = jnp.dot(a, wg_ref[...], preferred_element_type=jnp.float32)
    u = jnp.dot(a, wu_ref[...], preferred_element_type=jnp.float32)
    o_ref[...] = (g * jax.nn.sigmoid(g) * u).astype(o_ref.dtype)


def _swiglu(a, wg, wu, *, tm, tn):
    m, kdim = a.shape
    n = wg.shape[1]
    return pl.pallas_call(
        _swiglu_kernel,
        grid=(m // tm, n // tn),
        in_specs=[pl.BlockSpec((tm, kdim), lambda i, j: (i, 0)),
                  pl.BlockSpec((kdim, tn), lambda i, j: (0, j)),
                  pl.BlockSpec((kdim, tn), lambda i, j: (0, j))],
        out_specs=pl.BlockSpec((tm, tn), lambda i, j: (i, j)),
        out_shape=jax.ShapeDtypeStruct((m, n), MXU_DTYPE),
        compiler_params=_compiler_params(("parallel", "parallel")),
        name="swiglu",
    )(a, wg, wu)


def _ple_kernel(a_ref, wg_ref, p_ref, wp_ref, r_ref, o_ref):
    gate = jax.nn.sigmoid(jnp.dot(a_ref[...], wg_ref[...], preferred_element_type=jnp.float32))
    emb = jnp.dot(p_ref[...], wp_ref[...], preferred_element_type=jnp.float32)
    o_ref[...] = r_ref[...] + gate * emb


def _ple(a, wg, p, wp, res, *, tm, tn):
    m, kdim = a.shape
    n = wg.shape[1]
    pdim = p.shape[1]
    return pl.pallas_call(
        _ple_kernel,
        grid=(m // tm, n // tn),
        in_specs=[pl.BlockSpec((tm, kdim), lambda i, j: (i, 0)),
                  pl.BlockSpec((kdim, tn), lambda i, j: (0, j)),
                  pl.BlockSpec((tm, pdim), lambda i, j: (i, 0)),
                  pl.BlockSpec((pdim, tn), lambda i, j: (0, j)),
                  pl.BlockSpec((tm, tn), lambda i, j: (i, j))],
        out_specs=pl.BlockSpec((tm, tn), lambda i, j: (i, j)),
        out_shape=jax.ShapeDtypeStruct((m, n), jnp.float32),
        compiler_params=_compiler_params(("parallel", "parallel")),
        name="ple",
    )(a, wg, p, wp, res)


def _rope_tables(positions, dim):
    inv_freq = ROPE_THETA ** (-jnp.arange(0, dim, 2, dtype=jnp.float32) / dim)
    ang = positions.astype(jnp.float32)[..., None] * inv_freq
    return jnp.cos(ang), jnp.sin(ang)


_SEG_ORDER = ("a_q", "a_k", "a_v", "b_q", "b_iq", "c_q", "d_q", "d_kv", "c_kv", "b_k", "b_v")
_SEG_NAMES = ("a_q", "a_k", "a_v", "b_q", "b_k", "b_v", "b_iq", "b_ik", "b_iw", "c_q", "c_kv", "c_kr", "d_q", "d_kv", "d_g")
_ORIG_START = dict(zip(_SEG_NAMES, [int(o) for o in np.cumsum((0,) + IN_SIZES[:-1])]))
_ORIG_SIZE = dict(zip(_SEG_NAMES, IN_SIZES))
_GATES_PER_KV_GROUP = 3 * NSA_Q_PER_KV


def _build_in_layout():
    col = {}
    perm = []
    for name in _SEG_ORDER:
        col[name] = len(perm)
        perm.extend(range(_ORIG_START[name], _ORIG_START[name] + _ORIG_SIZE[name]))
    zero = IN_WIDTH
    col["narrow"] = len(perm)
    blk0 = list(range(_ORIG_START["b_ik"], _ORIG_START["b_ik"] + IDX_DIM))
    blk0 += list(range(_ORIG_START["d_g"], _ORIG_START["d_g"] + _GATES_PER_KV_GROUP))
    blk0 += list(range(_ORIG_START["b_iw"], _ORIG_START["b_iw"] + IDX_HEADS))
    blk0 += [zero] * (LANE - len(blk0))
    blk1 = list(range(_ORIG_START["c_kr"], _ORIG_START["c_kr"] + MLA_ROPE))
    blk1 += list(range(_ORIG_START["d_g"] + _GATES_PER_KV_GROUP, _ORIG_START["d_g"] + 2 * _GATES_PER_KV_GROUP))
    blk1 += [zero] * (LANE - len(blk1))
    perm.extend(blk0 + blk1)
    width = _round_up(len(perm), 512)
    perm.extend([zero] * (width - len(perm)))
    return col, np.asarray(perm, np.int32), width


_COL, _IN_PERM, IN_WIDTH_PAD = _build_in_layout()
_GATE_LANE0 = IDX_DIM
_IW_LANE0 = IDX_DIM + _GATES_PER_KV_GROUP


def _rope_lane_tables(cos, sin):
    return jnp.concatenate([cos, cos], axis=-1), jnp.concatenate([-sin, sin], axis=-1)


def _lane_tables(positions, dim):
    cc, ss = _rope_lane_tables(*_rope_tables(positions, dim))
    return tuple(jnp.tile(t.reshape(-1, dim), (1, LANE // dim)) for t in (cc, ss))


def _seg64_sum(y):
    r = lax.broadcasted_iota(jnp.int32, (LANE, LANE), 0) // 64
    c = lax.broadcasted_iota(jnp.int32, (LANE, LANE), 1) // 64
    bd = jnp.where(r == c, 1.0, 0.0).astype(MXU_DTYPE)
    hi = y.astype(MXU_DTYPE)
    lo = (y - hi.astype(jnp.float32)).astype(MXU_DTYPE)
    return (jnp.dot(hi, bd, preferred_element_type=jnp.float32)
            + jnp.dot(lo, bd, preferred_element_type=jnp.float32))


def _prep_kernel(x_ref, g_ref, cc_ref, ss_ref, o_ref, *, n_blocks, seg, norm, scale):
    cc, ss, g = cc_ref[...], ss_ref[...], g_ref[...]
    lane = lax.broadcasted_iota(jnp.int32, cc.shape, 1)
    for h in range(n_blocks):
        x = x_ref[:, h * LANE:(h + 1) * LANE]
        if norm:
            if seg == LANE:
                ms = jnp.mean(x * x, axis=-1, keepdims=True)
            else:
                ms = _seg64_sum(x * x) * (1.0 / seg)
            x = x * lax.rsqrt(ms + EPS) * g
        if seg == LANE:
            partner = pltpu.roll(x, LANE // 2, axis=1)
        else:
            partner = jnp.where((lane % seg) < seg // 2, pltpu.roll(x, LANE - seg // 2, axis=1),
                                pltpu.roll(x, seg // 2, axis=1))
        y = x * cc + partner * ss
        if scale != 1.0:
            y = y * scale
        o_ref[:, h * LANE:(h + 1) * LANE] = y.astype(o_ref.dtype)


def _prep_heads(proj, col0, n_blocks, gain, cc, ss, *, seg, norm, scale=1.0, tm=256):
    m = proj.shape[0]
    tm = min(tm, m)
    width = LANE * n_blocks
    cb = col0 // width
    assert cb * width == col0
    return pl.pallas_call(
        functools.partial(_prep_kernel, n_blocks=n_blocks, seg=seg, norm=norm, scale=scale),
        grid=(m // tm,),
        in_specs=[pl.BlockSpec((tm, width), lambda i: (i, cb)),
                  pl.BlockSpec((1, LANE), lambda i: (0, 0)),
                  pl.BlockSpec((tm, LANE), lambda i: (i, 0)),
                  pl.BlockSpec((tm, LANE), lambda i: (i, 0))],
        out_specs=pl.BlockSpec((tm, width), lambda i: (i, 0)),
        out_shape=jax.ShapeDtypeStruct((m, width), MXU_DTYPE),
        compiler_params=_compiler_params(("parallel",)),
        name="prep_heads",
    )(proj, gain, cc, ss)


def _dot_nt(a, b):
    return lax.dot_general(a, b, (((1,), (1,)), ((), ())), preferred_element_type=jnp.float32)


LOG2E = math.log2(math.e)


def _softmax_step(s, mask, m, l, acc, v, *, rows_may_be_empty=False):
    if mask is not None:
        s = jnp.where(mask, s, NEG_INF)
    m_new = jnp.maximum(m, jnp.max(s, axis=-1, keepdims=True))
    alpha = jnp.exp2(m - m_new)
    p = jnp.exp2(s - m_new)
    if rows_may_be_empty:
        p = jnp.where(mask, p, 0.0)
    l = alpha * l + jnp.sum(p, axis=-1, keepdims=True)
    acc = alpha * acc + jnp.dot(p.astype(MXU_DTYPE), v, preferred_element_type=jnp.float32)
    return m_new, l, acc


def _softmax_finish(l, acc):
    return acc * (1.0 / jnp.maximum(l, 1e-30))


def _nsa_cmp_kernel(x_ref, pe_ref, w1_ref, w2_ref, gk_ref, cc_ref, ss_ref, kc_ref, vc_ref):
    nch = x_ref.shape[2]
    for kv in range(2):
        x = x_ref[kv, 0]
        lo = jnp.dot((x + pe_ref[kv, 0:1, :]).astype(MXU_DTYPE), w1_ref[kv, 0], preferred_element_type=jnp.float32)
        hi = jnp.dot((x + pe_ref[kv, 1:2, :]).astype(MXU_DTYPE), w1_ref[kv, 1], preferred_element_type=jnp.float32)
        hid = lo + pltpu.roll(hi, nch - 1, axis=0)
        comp = jnp.dot(jax.nn.gelu(hid).astype(MXU_DTYPE), w2_ref[kv], preferred_element_type=jnp.float32)
        if kv == 0:
            ms = jnp.mean(comp * comp, axis=-1, keepdims=True)
            y = comp * lax.rsqrt(ms + EPS) * gk_ref[...]
            y = y * cc_ref[0] + pltpu.roll(y, LANE // 2, axis=1) * ss_ref[0]
            kc_ref[0] = y.astype(kc_ref.dtype)
        else:
            vc_ref[0] = comp.astype(vc_ref.dtype)


def _nsa_compress(craw, pe2, w1, w2, gk, cc_c, ss_c, *, n_batch):
    _, gb, nch, width = craw.shape
    return pl.pallas_call(
        _nsa_cmp_kernel,
        grid=(gb,),
        in_specs=[pl.BlockSpec((2, 1, nch, width), lambda n: (0, n, 0, 0)),
                  pl.BlockSpec((2, 2, width), lambda n: (0, 0, 0)),
                  pl.BlockSpec((2, 2, width, CMP_HIDDEN), lambda n: (0, 0, 0, 0)),
                  pl.BlockSpec((2, CMP_HIDDEN, HEAD_DIM), lambda n: (0, 0, 0)),
                  pl.BlockSpec((1, HEAD_DIM), lambda n: (0, 0)),
                  pl.BlockSpec((1, nch, HEAD_DIM), lambda n: (n % n_batch, 0, 0)),
                  pl.BlockSpec((1, nch, HEAD_DIM), lambda n: (n % n_batch, 0, 0))],
        out_specs=[pl.BlockSpec((1, nch, HEAD_DIM), lambda n: (n, 0, 0)),
                   pl.BlockSpec((1, nch, HEAD_DIM), lambda n: (n, 0, 0))],
        out_shape=[jax.ShapeDtypeStruct((gb, nch, HEAD_DIM), MXU_DTYPE)] * 2,
        compiler_params=_compiler_params(("parallel",)),
        name="nsa_compress",
    )(craw, pe2, w1, w2, gk, cc_c, ss_c)


def _nsa_attn_kernel(q_ref, kc_ref, vc_ref, ks_ref, vs_ref, kw_ref, vw_ref, gate_ref, c2s_ref, exp_ref, o_ref, *,
                     tq, tk, n_sel, n_top):
    i = pl.program_id(2)
    R = NSA_Q_PER_KV
    rows = R * tq
    nch = kc_ref.shape[1]
    bf16 = MXU_DTYPE
    q = jnp.concatenate([q_ref[:, r * HEAD_DIM:(r + 1) * HEAD_DIM] for r in range(R)], axis=0)
    t_row = i * tq + lax.broadcasted_iota(jnp.int32, (rows, 1), 0) % tq

    s = _dot_nt(q, kc_ref[0])
    blk_end = lax.broadcasted_iota(jnp.int32, (1, nch), 1) * CMP_STRIDE + (CMP_LEN - 1)
    mask = blk_end <= t_row
    zero = jnp.zeros((rows, 1), jnp.float32)
    s = jnp.where(mask, s, NEG_INF)
    p_cmp = jnp.where(mask, jnp.exp2(s - jnp.max(s, axis=-1, keepdims=True)), 0.0)
    p_cmp = p_cmp * (1.0 / jnp.maximum(jnp.sum(p_cmp, axis=-1, keepdims=True), 1e-30))
    o_cmp = jnp.dot(p_cmp.astype(bf16), vc_ref[0], preferred_element_type=jnp.float32)

    imp_rows = jnp.dot(p_cmp.astype(bf16), c2s_ref[...], preferred_element_type=jnp.float32)
    imp = imp_rows[0:tq]
    for r in range(1, R):
        imp = imp + imp_rows[r * tq:(r + 1) * tq]
    imp_t = imp.T[0:n_sel]
    blk = lax.broadcasted_iota(jnp.int32, (n_sel, tq), 0)
    t_lane = i * tq + lax.broadcasted_iota(jnp.int32, (n_sel, tq), 1)
    dist = lax.shift_right_logical(t_lane, int(math.log2(SEL_BLOCK))) - blk
    forced = (blk < SEL_INIT) | ((dist >= 0) & (dist < SEL_LOCAL))
    val = jnp.where(forced, POS_INF, jnp.where(blk * SEL_BLOCK <= t_lane, imp_t, NEG_INF))
    rank = jnp.zeros((n_sel, tq), jnp.float32)
    for mblk in range(n_sel):
        vm = val[mblk:mblk + 1, :]
        rank = rank + jnp.where((vm > val) | ((vm == val) & (blk > mblk)), 1.0, 0.0)
    drop_t = jnp.where(rank < n_top, 0.0, 1.0)
    drop_t = jnp.concatenate([drop_t, jnp.zeros((LANE - n_sel, tq), jnp.float32)], axis=0) if n_sel < LANE else drop_t
    drop_q = drop_t.T.astype(bf16)
    drop_rows = jnp.concatenate([drop_q] * R, axis=0)

    def slc_step(j, carry, diagonal):
        off = pl.multiple_of(j * tk, tk)
        k = ks_ref[pl.ds(off, tk), :]
        v = vs_ref[pl.ds(off, tk), :].astype(bf16)
        s = _dot_nt(q, k) + jnp.dot(drop_rows, exp_ref[j], preferred_element_type=jnp.float32)
        mask = (off + lax.broadcasted_iota(jnp.int32, (1, tk), 1) <= t_row) if diagonal else None
        return _softmax_step(s, mask, *carry, v)

    init = (zero + NEG_INF, zero, jnp.zeros((rows, HEAD_DIM), jnp.float32))
    n_full = (i * tq) // tk
    carry = lax.fori_loop(0, n_full, lambda j, c: slc_step(j, c, False), init)
    _, l, acc = slc_step(n_full, carry, True)
    o_slc = _softmax_finish(l, acc)

    span = WINDOW + tq
    off = pl.multiple_of(jnp.maximum(i * tq - WINDOW, 0), tq)
    key = off + lax.broadcasted_iota(jnp.int32, (1, span), 1)
    s = _dot_nt(q, kw_ref[pl.ds(off, span), :])
    s = jnp.where(key <= t_row, jnp.where(key > t_row - WINDOW, s, NEG_INF), NEG_INF)
    _, l, acc = _softmax_step(s, None, *init, vw_ref[pl.ds(off, span), :].astype(bf16))
    o_win = _softmax_finish(l, acc)

    gate = jax.nn.sigmoid(gate_ref[...])
    for r in range(R):
        c = _GATE_LANE0 + 3 * r
        rs = slice(r * tq, (r + 1) * tq)
        o_ref[:, r * HEAD_DIM:(r + 1) * HEAD_DIM] = (gate[:, c:c + 1] * o_cmp[rs]
                                                     + gate[:, c + 1:c + 2] * o_slc[rs]
                                                     + gate[:, c + 2:c + 3] * o_win[rs])


def _nsa_mixer(proj, B, S, cc_h, ss_h, positions, g_q, g_k, cmp_w1, cmp_w2, cmp_pe):
    M = B * S
    G, R, d = NSA_KV_HEADS, NSA_Q_PER_KV, HEAD_DIM
    bf16 = MXU_DTYPE
    tq = Q_BLOCK
    nq = S // tq
    nch = S // CMP_STRIDE
    n_sel = S // SEL_BLOCK
    n_top = min(SEL_TOPN, n_sel)
    kv0 = _COL["d_kv"]

    qn = _prep_heads(proj, _COL["d_q"], HEADS_PER_GROUP, g_q.reshape(1, d), cc_h, ss_h,
                     seg=LANE, norm=True, scale=d ** -0.5 * LOG2E)
    k_slc =_prep_heads(proj, kv0 + (1 * 2 + 0) * G * d, G, g_k[1].reshape(1, d), cc_h, ss_h, seg=LANE, norm=True)
    k_win = _prep_heads(proj, kv0 + (2 * 2 + 0) * G * d, G, g_k[2].reshape(1, d), cc_h, ss_h, seg=LANE, norm=True)

    craw = proj[:, kv0:kv0 + 2 * G * d].reshape(B, S, 2, G, d).transpose(2, 3, 0, 1, 4)
    craw = craw.reshape(2, G * B, nch, CMP_STRIDE * d)
    ends = np.minimum(np.arange(nch) * CMP_STRIDE + CMP_LEN - 1, S - 1)
    cc_c, ss_c = _rope_lane_tables(*_rope_tables(positions[:, ends], d))
    pe2 = cmp_pe.reshape(2, 2, CMP_STRIDE * d)
    w1 = cmp_w1.astype(bf16).reshape(2, 2, CMP_STRIDE * d, CMP_HIDDEN)
    k_cmp, v_cmp = _nsa_compress(craw, pe2, w1, cmp_w2.astype(bf16), g_k[0].reshape(1, d), cc_c, ss_c, n_batch=B)

    starts = np.arange(nch) * CMP_STRIDE
    sel_start = np.arange(LANE) * SEL_BLOCK
    c2s = ((starts[:, None] < sel_start[None, :] + SEL_BLOCK) & (starts[:, None] + CMP_LEN > sel_start[None, :])
           & (np.arange(nch)[:, None] < nch - 1) & (np.arange(LANE)[None, :] < n_sel))
    tk = min(512, S)
    expand = (np.arange(LANE)[None, :, None] == (np.arange(S // tk)[:, None, None] * tk + np.arange(tk)[None, None, :]) // SEL_BLOCK)
    c2s = jnp.asarray(c2s, bf16)
    expand = jnp.asarray(expand * MASK_BIAS, bf16)

    narrow = _COL["narrow"] // LANE
    col_vs = (kv0 + (1 * 2 + 1) * G * d) // LANE
    col_vw = (kv0 + (2 * 2 + 1) * G * d) // LANE
    return pl.pallas_call(
        functools.partial(_nsa_attn_kernel, tq=tq, tk=tk, n_sel=n_sel, n_top=n_top),
        grid=(B, G, nq),
        in_specs=[pl.BlockSpec((tq, R * d), lambda b, g, i: (b * nq + i, g)),
                  pl.BlockSpec((1, nch, d), lambda b, g, i: (g * B + b, 0, 0)),
                  pl.BlockSpec((1, nch, d), lambda b, g, i: (g * B + b, 0, 0)),
                  pl.BlockSpec((S, d), lambda b, g, i: (b, g)),
                  pl.BlockSpec((S, d), lambda b, g, i: (b, col_vs + g)),
                  pl.BlockSpec((S, d), lambda b, g, i: (b, g)),
                  pl.BlockSpec((S, d), lambda b, g, i: (b, col_vw + g)),
                  pl.BlockSpec((tq, LANE), lambda b, g, i: (b * nq + i, narrow + g)),
                  pl.BlockSpec((nch, LANE), lambda b, g, i: (0, 0)),
                  pl.BlockSpec((S // tk, LANE, tk), lambda b, g, i: (0, 0, 0))],
        out_specs=pl.BlockSpec((tq, R * d), lambda b, g, i: (b * nq + i, g)),
        out_shape=jax.ShapeDtypeStruct((M, GROUP_WIDTH), jnp.float32),
        compiler_params=_compiler_params(("parallel", "parallel", "arbitrary")),
        name="nsa_attention",
    )(qn, k_cmp, v_cmp, k_slc, proj, k_win, proj, proj, c2s, expand)


_INT_MIN = -2 ** 31


def _sortable_key(x):
    b = lax.bitcast_convert_type(x + 0.0, jnp.int32)
    return jnp.where(b >= 0, b, b ^ 0x7FFFFFFF)


def _dsa_kernel(q_ref, k_ref, v_ref, iq_ref, ik_ref, nar_ref, o_ref, key_ref, msk_ref, w_ref, *,
                tq, tk, n_keep, idx_bits):
    i = pl.program_id(1)
    H = HEADS_PER_GROUP
    bf16 = MXU_DTYPE
    t_col = i * tq + lax.broadcasted_iota(jnp.int32, (tq, 1), 0)
    key_lane = lax.broadcasted_iota(jnp.int32, (1, tq), 1)
    lane = lax.broadcasted_iota(jnp.int32, (tq, LANE), 1)
    iw = nar_ref[...] * (IDX_HEADS ** -0.5)
    for hd in range(IDX_HEADS):
        w_ref[hd] = jnp.broadcast_to(iw[:, _IW_LANE0 + hd:_IW_LANE0 + hd + 1], (tq, tq))
    iq_rows = jnp.concatenate([iq_ref[:, pr * LANE:(pr + 1) * LANE] for pr in range(IDX_HEADS // 2)], axis=0)

    def score_body(j, _):
        off = pl.multiple_of(j * tq, tq)
        ik = ik_ref[pl.ds(off, tq), :]
        ik_lo = jnp.where(lane < IDX_DIM, ik, jnp.zeros_like(ik))
        ik_hi = jnp.where(lane < IDX_DIM, jnp.zeros_like(ik), pltpu.roll(ik.astype(jnp.float32), IDX_DIM, axis=1).astype(bf16))
        rel = (jnp.maximum(_dot_nt(iq_rows, ik_lo), 0.0), jnp.maximum(_dot_nt(iq_rows, ik_hi), 0.0))
        sc = jnp.zeros((tq, tq), jnp.float32)
        for hd in range(IDX_HEADS):
            sc = sc + w_ref[hd] * rel[hd % 2][(hd // 2) * tq:(hd // 2 + 1) * tq]
        sc = jnp.where(off + key_lane <= t_col, sc, NEG_INF)
        key_ref[j] = _sortable_key(sc)
        return 0

    lax.fori_loop(0, i + 1, score_body, 0)

    def count(pred):
        def body(j, part):
            return part + jnp.where(pred(key_ref[j], j * tq), 1.0, 0.0)
        part = lax.fori_loop(0, i + 1, body, jnp.zeros((tq, tq), jnp.float32))
        return jnp.sum(part, axis=-1, keepdims=True)

    c0 = count(lambda kk, off: kk >= 0)
    thr = jnp.where(c0 >= n_keep, 0, _INT_MIN).astype(jnp.int32)

    def thr_body(it, thr):
        cand = thr | lax.shift_left(jnp.int32(1), 30 - it)
        c = count(lambda kk, off: kk >= cand)
        return jnp.where(c >= n_keep, cand, thr)

    thr = lax.fori_loop(0, 31, thr_body, thr)

    n_ge = count(lambda kk, off: kk >= thr)

    def cut_search():
        need = n_keep - count(lambda kk, off: kk > thr)

        def cut_body(it, cut):
            cand = cut | lax.shift_left(jnp.int32(1), idx_bits - 1 - it)
            c = count(lambda kk, off: (kk == thr) & (off + key_lane < cand))
            return jnp.where(c < need, cand, cut)

        return lax.fori_loop(0, idx_bits, cut_body, jnp.zeros((tq, 1), jnp.int32))

    cut = lax.cond(jnp.max(n_ge) > n_keep, cut_search, lambda: jnp.full((tq, 1), 2 ** idx_bits - 1, jnp.int32))

    tpc = tk // tq
    n_chunks = (i + tpc) // tpc

    def mask_body(j, _):
        @pl.when(j <= i)
        def _():
            kk = key_ref[j]
            kidx = j * tq + key_lane
            sel = (kk > thr) | ((kk == thr) & (kidx <= cut))
            msk_ref[j] = jnp.where(kidx <= t_col, jnp.where(sel, 0.0, NEG_INF), NEG_INF)

        @pl.when(j > i)
        def _():
            msk_ref[j] = jnp.full((tq, tq), NEG_INF, jnp.float32)

        return 0

    lax.fori_loop(0, n_chunks * tpc, mask_body, 0)

    hpp = 4
    rows = hpp * tq
    zero = jnp.zeros((rows, 1), jnp.float32)
    qs = [jnp.concatenate([q_ref[:, (hc * hpp + r) * HEAD_DIM:(hc * hpp + r + 1) * HEAD_DIM] for r in range(hpp)], axis=0)
          for hc in range(H // hpp)]

    def att_body(j, carry):
        off = pl.multiple_of(j * tk, tk)
        k = k_ref[pl.ds(off, tk), :]
        v = v_ref[pl.ds(off, tk), :].astype(bf16)
        bias1 = jnp.concatenate([msk_ref[j * tpc + u] for u in range(tpc)], axis=1)
        bias = jnp.concatenate([bias1] * hpp, axis=0)
        return tuple(_softmax_step(_dot_nt(q, k) + bias, None, *carry[c], v) for c, q in enumerate(qs))

    init = tuple((zero + NEG_INF, zero, jnp.zeros((rows, HEAD_DIM), jnp.float32)) for _ in qs)
    res = lax.fori_loop(0, n_chunks, att_body, init)
    for hc, (_, l, acc) in enumerate(res):
        o = _softmax_finish(l, acc)
        for r in range(hpp):
            hh = hc * hpp + r
            o_ref[:, hh * HEAD_DIM:(hh + 1) * HEAD_DIM] = o[r * tq:(r + 1) * tq]


def _dsa_mixer(proj, B, S, cc_h, ss_h, cc_i, ss_i, g_q, g_k):
    M = B * S
    d = HEAD_DIM
    tq = Q_BLOCK
    nq = S // tq
    n_keep = min(DSA_TOPK, S // 4)
    qn = _prep_heads(proj, _COL["b_q"], HEADS_PER_GROUP, g_q.reshape(1, d), cc_h, ss_h, seg=LANE, norm=True,
                     scale=d ** -0.5 * LOG2E)
    kn =_prep_heads(proj, _COL["b_k"], 1, g_k.reshape(1, d), cc_h, ss_h, seg=LANE, norm=True)
    ones = jnp.ones((1, LANE), jnp.float32)
    iqn = _prep_heads(proj, _COL["b_iq"], IDX_HEADS * IDX_DIM // LANE, ones, cc_i, ss_i, seg=IDX_DIM, norm=False,
                      scale=IDX_DIM ** -0.5)
    ikn = _prep_heads(proj, _COL["narrow"], 1, ones, cc_i, ss_i, seg=IDX_DIM, norm=False)
    narrow = _COL["narrow"] // LANE
    col_v = _COL["b_v"] // LANE
    return pl.pallas_call(
        functools.partial(_dsa_kernel, tq=tq, tk=min(512, S), n_keep=n_keep, idx_bits=int(math.log2(S))),
        grid=(B, nq),
        in_specs=[pl.BlockSpec((tq, GROUP_WIDTH), lambda b, i: (b * nq + i, 0)),
                  pl.BlockSpec((S, d), lambda b, i: (b, 0)),
                  pl.BlockSpec((S, d), lambda b, i: (b, col_v)),
                  pl.BlockSpec((tq, IDX_HEADS * IDX_DIM), lambda b, i: (b * nq + i, 0)),
                  pl.BlockSpec((S, LANE), lambda b, i: (b, 0)),
                  pl.BlockSpec((tq, LANE), lambda b, i: (b * nq + i, narrow))],
        out_specs=pl.BlockSpec((tq, GROUP_WIDTH), lambda b, i: (b * nq + i, 0)),
        out_shape=jax.ShapeDtypeStruct((M, GROUP_WIDTH), jnp.float32),
        scratch_shapes=[pltpu.VMEM((nq, tq, tq), jnp.int32), pltpu.VMEM((nq, tq, tq), jnp.float32),
                        pltpu.VMEM((IDX_HEADS, tq, tq), jnp.float32)],
        compiler_params=_compiler_params(("parallel", "arbitrary")),
        name="dsa_attention",
    )(qn, kn, proj, iqn, ikn, proj)


def _causal_chains(qs, k_ref, v_ref, dk, dv, t_row, t0, tk):
    rows = qs[0].shape[0]
    zero = jnp.zeros((rows, 1), jnp.float32)

    def step(j, carry, masked):
        off = pl.multiple_of(j * tk, tk)
        mask = (off + lax.broadcasted_iota(jnp.int32, (1, tk), 1) <= t_row) if masked else None
        out = []
        for c, q in enumerate(qs):
            k = k_ref[pl.ds(off, tk), c * dk:(c + 1) * dk]
            v = v_ref[pl.ds(off, tk), c * dv:(c + 1) * dv].astype(MXU_DTYPE)
            out.append(_softmax_step(_dot_nt(q, k), mask, *carry[c], v))
        return tuple(out)

    init = tuple((zero + NEG_INF, zero, jnp.zeros((rows, dv), jnp.float32)) for _ in qs)
    n_full = t0 // tk
    res = lax.fori_loop(0, n_full, lambda j, c: step(j, c, False), init)
    res = step(n_full, res, True)
    return [_softmax_finish(l, acc) for _, l, acc in res]


def _diff_attn_kernel(q_ref, k_ref, v_ref, lam_ref, g_ref, o_ref, *, tq, tk, hb, out_scale):
    i = pl.program_id(2)
    d = HEAD_DIM
    lane = lax.broadcasted_iota(jnp.int32, (tq, d), 1)
    qs = []
    for c in range(hb):
        q = q_ref[:, c * d:(c + 1) * d]
        zeros = jnp.zeros_like(q)
        qs.append(jnp.concatenate([jnp.where(lane < DIFF_QK_DIM, q, zeros), jnp.where(lane < DIFF_QK_DIM, zeros, q)], axis=0))
    t_row = i * tq + lax.broadcasted_iota(jnp.int32, (2 * tq, 1), 0) % tq
    outs = _causal_chains(qs, k_ref, v_ref, d, d, t_row, i * tq, tk)
    for c, o2 in enumerate(outs):
        o = o2[0:tq] - lam_ref[...] * o2[tq:2 * tq]
        ms = jnp.mean(o * o, axis=-1, keepdims=True)
        o_ref[:, c * d:(c + 1) * d] = o * lax.rsqrt(ms + EPS) * g_ref[...] * out_scale


def _diff_mixer(proj, B, S, cc_d, ss_d, g_q, g_k, g_sub, lam_params, layer_idx, *, tq=128, tk=512, hb=4):
    M = B * S
    d = HEAD_DIM
    nq = S // tq
    tk = min(tk, S)
    reps = LANE // DIFF_QK_DIM
    qn = _prep_heads(proj, _COL["a_q"], HEADS_PER_GROUP, jnp.tile(g_q, reps).reshape(1, d), cc_d, ss_d,
                     seg=DIFF_QK_DIM, norm=True, scale=DIFF_QK_DIM ** -0.5 * LOG2E)
    kn = _prep_heads(proj, _COL["a_k"], HEADS_PER_GROUP, jnp.tile(g_k, reps).reshape(1, d), cc_d, ss_d,
                     seg=DIFF_QK_DIM, norm=True)
    lam_init = 0.8 - 0.6 * math.exp(-0.3 * layer_idx)
    lp = lam_params.astype(jnp.float32)
    lam = jnp.exp(jnp.sum(lp[0] * lp[1])) - jnp.exp(jnp.sum(lp[2] * lp[3])) + lam_init
    col_v = _COL["a_v"] // (hb * d)
    return pl.pallas_call(
        functools.partial(_diff_attn_kernel, tq=tq, tk=tk, hb=hb, out_scale=1.0 - lam_init),
        grid=(B, HEADS_PER_GROUP // hb, nq),
        in_specs=[pl.BlockSpec((tq, hb * d), lambda b, h, i: (b * nq + i, h)),
                  pl.BlockSpec((S, hb * d), lambda b, h, i: (b, h)),
                  pl.BlockSpec((S, hb * d), lambda b, h, i: (b, col_v + h)),
                  pl.BlockSpec((1, d), lambda b, h, i: (0, 0)),
                  pl.BlockSpec((1, d), lambda b, h, i: (0, 0))],
        out_specs=pl.BlockSpec((tq, hb * d), lambda b, h, i: (b * nq + i, h)),
        out_shape=jax.ShapeDtypeStruct((M, GROUP_WIDTH), jnp.float32),
        compiler_params=_compiler_params(("parallel", "parallel", "arbitrary")),
        name="diff_attention",
    )(qn, kn, proj, jnp.full((1, d), lam, jnp.float32), g_sub.reshape(1, d))


MLA_QK_PAD = 2 * LANE


def _rmsnorm_cols_kernel(x_ref, g_ref, o_ref):
    x = x_ref[...]
    ms = jnp.mean(x * x, axis=-1, keepdims=True)
    o_ref[...] = (x * lax.rsqrt(ms + EPS) * g_ref[...]).astype(o_ref.dtype)


def _rmsnorm_cols(proj, col0, width, g, *, tm=512):
    m = proj.shape[0]
    tm = min(tm, m)
    cb = col0 // width
    assert cb * width == col0
    return pl.pallas_call(
        _rmsnorm_cols_kernel,
        grid=(m // tm,),
        in_specs=[pl.BlockSpec((tm, width), lambda i: (i, cb)), pl.BlockSpec((1, width), lambda i: (0, 0))],
        out_specs=pl.BlockSpec((tm, width), lambda i: (i, 0)),
        out_shape=jax.ShapeDtypeStruct((m, width), MXU_DTYPE),
        compiler_params=_compiler_params(("parallel",)),
        name="rmsnorm_cols",
    )(proj, g.reshape(1, width))


def _mla_prep_kernel(a_ref, b_ref, g_ref, cc_ref, ss_ref, o_ref, *, a_stride, b_col0, b_stride, scale):
    cc, ss = cc_ref[...], ss_ref[...]
    lane = lax.broadcasted_iota(jnp.int32, cc.shape, 1)
    for h in range(HEADS_PER_GROUP):
        a = a_ref[:, h * a_stride:h * a_stride + LANE]
        b = jnp.where(lane < MLA_ROPE, b_ref[:, b_col0 + h * b_stride:b_col0 + h * b_stride + LANE], 0.0)
        ms = (jnp.sum(a * a, axis=-1, keepdims=True) + jnp.sum(b * b, axis=-1, keepdims=True)) * (1.0 / (MLA_NOPE + MLA_ROPE))
        r = lax.rsqrt(ms + EPS)
        ya = a * r * g_ref[:, 0:LANE]
        yb = b * r * g_ref[:, LANE:2 * LANE]
        partner = jnp.where((lane % MLA_ROPE) < MLA_ROPE // 2, pltpu.roll(yb, LANE - MLA_ROPE // 2, axis=1),
                            pltpu.roll(yb, MLA_ROPE // 2, axis=1))
        yb = yb * cc + partner * ss
        o_ref[:, h * MLA_QK_PAD:h * MLA_QK_PAD + LANE] = (ya * scale).astype(o_ref.dtype)
        o_ref[:, h * MLA_QK_PAD + LANE:(h + 1) * MLA_QK_PAD] = (yb * scale).astype(o_ref.dtype)


def _mla_prep(a_arr, a_width, a_stride, b_arr, b_block, b_width, b_col0, b_stride, gain, cc, ss, *, scale=1.0, tm=256):
    m = a_arr.shape[0]
    tm = min(tm, m)
    g2 = jnp.concatenate([gain, jnp.zeros((MLA_QK_PAD - gain.shape[0],), gain.dtype)]).reshape(1, MLA_QK_PAD)
    return pl.pallas_call(
        functools.partial(_mla_prep_kernel, a_stride=a_stride, b_col0=b_col0, b_stride=b_stride, scale=scale),
        grid=(m // tm,),
        in_specs=[pl.BlockSpec((tm, a_width), lambda i: (i, 0)),
                  pl.BlockSpec((tm, b_width), lambda i: (i, b_block)),
                  pl.BlockSpec((1, MLA_QK_PAD), lambda i: (0, 0)),
                  pl.BlockSpec((tm, LANE), lambda i: (i, 0)),
                  pl.BlockSpec((tm, LANE), lambda i: (i, 0))],
        out_specs=pl.BlockSpec((tm, HEADS_PER_GROUP * MLA_QK_PAD), lambda i: (i, 0)),
        out_shape=jax.ShapeDtypeStruct((m, HEADS_PER_GROUP * MLA_QK_PAD), MXU_DTYPE),
        compiler_params=_compiler_params(("parallel",)),
        name="mla_prep",
    )(a_arr, b_arr, g2, cc, ss)


def _mla_attn_kernel(q_ref, k_ref, v_ref, o_ref, *, tq, tk, hb):
    i = pl.program_id(2)
    d = HEAD_DIM
    t_row = i * tq + lax.broadcasted_iota(jnp.int32, (tq, 1), 0)
    qs = [q_ref[:, c * MLA_QK_PAD:(c + 1) * MLA_QK_PAD] for c in range(hb)]
    outs = _causal_chains(qs, k_ref, v_ref, MLA_QK_PAD, d, t_row, i * tq, tk)
    for c, o in enumerate(outs):
        o_ref[:, c * d:(c + 1) * d] = o


def _mla_mixer(proj, B, S, cc_m, ss_m, g_cq, g_ckv, w_uq, w_uk, w_uv, g_q, g_k, *, tq=256, tk=512, hb=4):
    M = B * S
    H, d = HEADS_PER_GROUP, HEAD_DIM
    tq = min(tq, S)
    tk = min(tk, S)
    nq = S // tq
    dqk = MLA_NOPE + MLA_ROPE
    w_q = jnp.pad(w_uq.astype(MXU_DTYPE).reshape(MLA_Q_RANK, H, dqk), ((0, 0), (0, 0), (0, MLA_QK_PAD - dqk)))
    w_q = w_q.reshape(MLA_Q_RANK, H * MLA_QK_PAD)
    w_kv = jnp.concatenate([w_uk, w_uv], axis=1).astype(MXU_DTYPE)
    cq = _rmsnorm_cols(proj, _COL["c_q"], MLA_Q_RANK, g_cq)
    ckv = _rmsnorm_cols(proj, _COL["c_kv"], MLA_KV_RANK, g_ckv)
    q_up = _matmul(cq, w_q, tm=min(1024, M), tn=1024, tk=MLA_Q_RANK)
    kv_up = _matmul(ckv, w_kv, tm=min(1024, M), tn=1024, tk=MLA_KV_RANK)
    qn = _mla_prep(q_up, H * MLA_QK_PAD, MLA_QK_PAD, q_up, 0, H * MLA_QK_PAD, LANE, MLA_QK_PAD, g_q, cc_m, ss_m,
                   scale=dqk ** -0.5 * LOG2E)
    kn = _mla_prep(kv_up, H * MLA_NOPE, LANE, proj, _COL["narrow"] // LANE + 1, LANE, 0, 0, g_k, cc_m, ss_m)
    return pl.pallas_call(
        functools.partial(_mla_attn_kernel, tq=tq, tk=tk, hb=hb),
        grid=(B, H // hb, nq),
        in_specs=[pl.BlockSpec((tq, hb * MLA_QK_PAD), lambda b, h, i: (b * nq + i, h)),
                  pl.BlockSpec((S, hb * MLA_QK_PAD), lambda b, h, i: (b, h)),
                  pl.BlockSpec((S, hb * d), lambda b, h, i: (b, H // hb + h))],
        out_specs=pl.BlockSpec((tq, hb * d), lambda b, h, i: (b * nq + i, h)),
        out_shape=jax.ShapeDtypeStruct((M, GROUP_WIDTH), jnp.float32),
        compiler_params=_compiler_params(("parallel", "parallel", "arbitrary")),
        name="mla_attention",
    )(qn, kn, kv_up)


def _mix_kernel(a_ref, b_ref, c_ref, d_ref, g_ref, o_ref):
    o_ref[:, 0:GROUP_WIDTH] = a_ref[...].astype(o_ref.dtype)
    for n, ref in enumerate((b_ref, c_ref, d_ref)):
        x = ref[...]
        ms = jnp.mean(x * x, axis=-1, keepdims=True)
        o_ref[:, (n + 1) * GROUP_WIDTH:(n + 2) * GROUP_WIDTH] = (
            x * lax.rsqrt(ms + EPS) * g_ref[n:n + 1, :]).astype(o_ref.dtype)


def _mix_groups(o_a, o_b, o_c, o_d, g_out, *, tm=512):
    m = o_a.shape[0]
    tm = min(tm, m)
    spec = pl.BlockSpec((tm, GROUP_WIDTH), lambda i: (i, 0))
    return pl.pallas_call(
        _mix_kernel,
        grid=(m // tm,),
        in_specs=[spec, spec, spec, spec, pl.BlockSpec((3, GROUP_WIDTH), lambda i: (0, 0))],
        out_specs=pl.BlockSpec((tm, MIX_WIDTH), lambda i: (i, 0)),
        out_shape=jax.ShapeDtypeStruct((m, MIX_WIDTH), MXU_DTYPE),
        compiler_params=_compiler_params(("parallel",)),
        name="mix_groups",
    )(o_a, o_b, o_c, o_d, g_out)


def _pad_cols(w, n):
    return jnp.pad(w, ((0, 0), (0, n - w.shape[1])))


def kernel(x, p, positions, w_in, w_out, g_mix, g_ffn, w_gate, w_up, w_down, w_ple_proj, w_ple_gate, g_ple,
           g_group_out, diff_g_q, diff_g_k, diff_g_sub, diff_lambda, dsa_g_q, dsa_g_k, mla_g_cq, mla_g_ckv,
           mla_w_uq, mla_w_uk, mla_w_uv, mla_g_q, mla_g_k, nsa_g_q, nsa_g_k, nsa_cmp_w1, nsa_cmp_w2, nsa_cmp_pe):
    B, S = x.shape[:2]
    M = B * S
    H = HEADS_PER_GROUP
    bf16 = MXU_DTYPE
    cc_h, ss_h = _lane_tables(positions, HEAD_DIM)
    cc_i, ss_i = _lane_tables(positions, IDX_DIM)
    cc_d, ss_d = _lane_tables(positions, DIFF_QK_DIM)
    cc_m, ss_m = _lane_tables(positions, MLA_ROPE)
    h = x.reshape(M, D_MODEL)
    for i in range(DEPTH):
        w_in_i = jnp.take(_pad_cols(w_in[i].astype(bf16), IN_WIDTH + 1), _IN_PERM, axis=1)
        u = _rmsnorm_rows(h, g_mix[i])
        proj = _matmul(u, w_in_i, tm=1024, tn=512, tk=D_MODEL)
        o_a = _diff_mixer(proj, B, S, cc_d, ss_d, diff_g_q[i], diff_g_k[i], diff_g_sub[i], diff_lambda[i], i)
        o_b = _dsa_mixer(proj, B, S, cc_h, ss_h, cc_i, ss_i, dsa_g_q[i], dsa_g_k[i])
        o_c = _mla_mixer(proj, B, S, cc_m, ss_m, mla_g_cq[i], mla_g_ckv[i], mla_w_uq[i], mla_w_uk[i], mla_w_uv[i],
                         mla_g_q[i], mla_g_k[i])
        o_d = _nsa_mixer(proj, B, S, cc_h, ss_h, positions, nsa_g_q[i], nsa_g_k[i],
                         nsa_cmp_w1[i], nsa_cmp_w2[i], nsa_cmp_pe[i])
        mixed = _mix_groups(o_a, o_b, o_c, o_d, g_group_out[i])
        h = _matmul(mixed, w_out[i].astype(bf16), res=h, tm=1024, tn=512, tk=MIX_WIDTH)
        u = _rmsnorm_rows(h, g_ffn[i])
        act = _swiglu(u, w_gate[i].astype(bf16), w_up[i].astype(bf16), tm=1024, tn=256)
        h = _matmul(act, w_down[i].astype(bf16), res=h, tm=512, tn=256, tk=D_FF)
        u = _rmsnorm_rows(h, g_ple[i])
        h = _ple(u, w_ple_gate[i].astype(bf16), p[i].reshape(M, PLE_DIM).astype(bf16), w_ple_proj[i].astype(bf16), h,
                 tm=1024, tn=512)
    return h.reshape(B, S, D_MODEL)
```

```python
import functools
import math

import numpy as np
import jax
import jax.numpy as jnp
from jax import lax
from jax.experimental import pallas as pl
from jax.experimental.pallas import tpu as pltpu

D_MODEL = 4096
DEPTH = 4
HEAD_DIM = 128
N_GROUPS = 4
HEADS_PER_GROUP = D_MODEL // HEAD_DIM // N_GROUPS
GROUP_WIDTH = HEADS_PER_GROUP * HEAD_DIM
MIX_WIDTH = N_GROUPS * GROUP_WIDTH
D_FF = ((8 * D_MODEL + 3 * 256 - 1) // (3 * 256)) * 256
PLE_DIM = 256
ROPE_THETA = 10000.0
EPS = 1e-6
Q_BLOCK = 128
NEG_INF = -1e30
MASK_BIAS = -2.0 ** 100
POS_INF = 1e30

DIFF_QK_DIM = HEAD_DIM // 2
DIFF_V_DIM = HEAD_DIM
IDX_HEADS = 16
IDX_DIM = 64
DSA_TOPK = 256
MLA_Q_RANK = 1024
MLA_KV_RANK = 512
MLA_NOPE = 128
MLA_ROPE = 64
MLA_V = HEAD_DIM
NSA_KV_HEADS = 2
NSA_Q_PER_KV = HEADS_PER_GROUP // NSA_KV_HEADS
CMP_LEN = 32
CMP_STRIDE = 16
CMP_HIDDEN = HEAD_DIM
SEL_BLOCK = 64
SEL_TOPN = 16
SEL_INIT = 1
SEL_LOCAL = 2
SEL_Q_BLOCK = 32
WINDOW = 512

IN_SIZES = (
    HEADS_PER_GROUP * 2 * DIFF_QK_DIM, HEADS_PER_GROUP * 2 * DIFF_QK_DIM, HEADS_PER_GROUP * DIFF_V_DIM,
    GROUP_WIDTH, HEAD_DIM, HEAD_DIM, IDX_HEADS * IDX_DIM, IDX_DIM, IDX_HEADS,
    MLA_Q_RANK, MLA_KV_RANK, MLA_ROPE,
    GROUP_WIDTH, 3 * 2 * NSA_KV_HEADS * HEAD_DIM, 3 * HEADS_PER_GROUP,
)
IN_WIDTH = sum(IN_SIZES)

V7X_VMEM_LIMIT_BYTES = 56 * 1024 * 1024
LANE = 128
MXU_DTYPE = getattr(jnp, "bfloat16")


def _round_up(n, m):
    return (n + m - 1) // m * m


def _compiler_params(semantics):
    return pltpu.CompilerParams(dimension_semantics=semantics, vmem_limit_bytes=V7X_VMEM_LIMIT_BYTES)


def _rmsnorm_rows_kernel(x_ref, g_ref, o_ref):
    x = x_ref[...]
    ms = jnp.mean(x * x, axis=-1, keepdims=True)
    o_ref[...] = (x * lax.rsqrt(ms + EPS) * g_ref[...]).astype(o_ref.dtype)


def _rmsnorm_rows(x, g, *, tm=256):
    m, d = x.shape
    out_dtype = MXU_DTYPE
    return pl.pallas_call(
        _rmsnorm_rows_kernel,
        grid=(m // tm,),
        in_specs=[pl.BlockSpec((tm, d), lambda i: (i, 0)), pl.BlockSpec((1, d), lambda i: (0, 0))],
        out_specs=pl.BlockSpec((tm, d), lambda i: (i, 0)),
        out_shape=jax.ShapeDtypeStruct((m, d), out_dtype),
        compiler_params=_compiler_params(("parallel",)),
        name="rmsnorm_rows",
    )(x, g.reshape(1, d))


def _mm_kernel(a_ref, w_ref, o_ref, acc_ref, *, nk):
    k = pl.program_id(2)

    @pl.when(k == 0)
    def _():
        acc_ref[...] = jnp.zeros_like(acc_ref)

    acc_ref[...] += jnp.dot(a_ref[...], w_ref[...], preferred_element_type=jnp.float32)

    @pl.when(k == nk - 1)
    def _():
        o_ref[...] = acc_ref[...].astype(o_ref.dtype)


def _mm_res_kernel(a_ref, w_ref, r_ref, o_ref, acc_ref, *, nk):
    k = pl.program_id(2)

    @pl.when(k == 0)
    def _():
        acc_ref[...] = jnp.zeros_like(acc_ref)

    acc_ref[...] += jnp.dot(a_ref[...], w_ref[...], preferred_element_type=jnp.float32)

    @pl.when(k == nk - 1)
    def _():
        o_ref[...] = (r_ref[...] + acc_ref[...]).astype(o_ref.dtype)


def _matmul(a, w, *, res=None, tm, tn, tk, out_dtype=jnp.float32):
    m, kdim = a.shape
    n = w.shape[1]
    nk = kdim // tk
    assert m % tm == 0 and n % tn == 0 and kdim % tk == 0
    in_specs = [pl.BlockSpec((tm, tk), lambda i, j, k: (i, k)), pl.BlockSpec((tk, tn), lambda i, j, k: (k, j))]
    args = [a, w]
    if res is None:
        body = functools.partial(_mm_kernel, nk=nk)
    else:
        body = functools.partial(_mm_res_kernel, nk=nk)
        in_specs.append(pl.BlockSpec((tm, tn), lambda i, j, k: (i, j)))
        args.append(res)
    return pl.pallas_call(
        body,
        grid=(m // tm, n // tn, nk),
        in_specs=in_specs,
        out_specs=pl.BlockSpec((tm, tn), lambda i, j, k: (i, j)),
        out_shape=jax.ShapeDtypeStruct((m, n), out_dtype),
        scratch_shapes=[pltpu.VMEM((tm, tn), jnp.float32)],
        compiler_params=_compiler_params(("parallel", "parallel", "arbitrary")),
        name="matmul_res" if res is not None else "matmul",
    )(*args)


def _swiglu_kernel(a_ref, wg_ref, wu_ref, o_ref):
    a = a_ref[...]
    g = jnp.dot(a, wg_ref[...], preferred_element_type=jnp.float32)
    u = jnp.dot(a, wu_ref[...], preferred_element_type=jnp.float32)
    o_ref[...] = (g * jax.nn.sigmoid(g) * u).astype(o_ref.dtype)


def _swiglu(a, wg, wu, *, tm, tn):
    m, kdim = a.shape
    n = wg.shape[1]
    return pl.pallas_call(
        _swiglu_kernel,
        grid=(m // tm, n // tn),
        in_specs=[pl.BlockSpec((tm, kdim), lambda i, j: (i, 0)),
                  pl.BlockSpec((kdim, tn), lambda i, j: (0, j)),
                  pl.BlockSpec((kdim, tn), lambda i, j: (0, j))],
        out_specs=pl.BlockSpec((tm, tn), lambda i, j: (i, j)),
        out_shape=jax.ShapeDtypeStruct((m, n), MXU_DTYPE),
        compiler_params=_compiler_params(("parallel", "parallel")),
        name="swiglu",
    )(a, wg, wu)


def _ple_kernel(a_ref, wg_ref, p_ref, wp_ref, r_ref, o_ref):
    gate = jax.nn.sigmoid(jnp.dot(a_ref[...], wg_ref[...], preferred_element_type=jnp.float32))
    emb = jnp.dot(p_ref[...], wp_ref[...], preferred_element_type=jnp.float32)
    o_ref[...] = r_ref[...] + gate * emb


def _ple(a, wg, p, wp, res, *, tm, tn):
    m, kdim = a.shape
    n = wg.shape[1]
    pdim = p.shape[1]
    return pl.pallas_call(
        _ple_kernel,
        grid=(m // tm, n // tn),
        in_specs=[pl.BlockSpec((tm, kdim), lambda i, j: (i, 0)),
                  pl.BlockSpec((kdim, tn), lambda i, j: (0, j)),
                  pl.BlockSpec((tm, pdim), lambda i, j: (i, 0)),
                  pl.BlockSpec((pdim, tn), lambda i, j: (0, j)),
                  pl.BlockSpec((tm, tn), lambda i, j: (i, j))],
        out_specs=pl.BlockSpec((tm, tn), lambda i, j: (i, j)),
        out_shape=jax.ShapeDtypeStruct((m, n), jnp.float32),
        compiler_params=_compiler_params(("parallel", "parallel")),
        name="ple",
    )(a, wg, p, wp, res)


def _rope_tables(positions, dim):
    inv_freq = ROPE_THETA ** (-jnp.arange(0, dim, 2, dtype=jnp.float32) / dim)
    ang = positions.astype(jnp.float32)[..., None] * inv_freq
    return jnp.cos(ang), jnp.sin(ang)


_SEG_ORDER = ("a_q", "a_k", "a_v", "b_q", "b_iq", "c_q", "d_q", "d_kv", "c_kv", "b_k", "b_v")
_SEG_NAMES = ("a_q", "a_k", "a_v", "b_q", "b_k", "b_v", "b_iq", "b_ik", "b_iw", "c_q", "c_kv", "c_kr", "d_q", "d_kv", "d_g")
_ORIG_START = dict(zip(_SEG_NAMES, [int(o) for o in np.cumsum((0,) + IN_SIZES[:-1])]))
_ORIG_SIZE = dict(zip(_SEG_NAMES, IN_SIZES))
_GATES_PER_KV_GROUP = 3 * NSA_Q_PER_KV


def _build_in_layout():
    col = {}
    perm = []
    for name in _SEG_ORDER:
        col[name] = len(perm)
        perm.extend(range(_ORIG_START[name], _ORIG_START[name] + _ORIG_SIZE[name]))
    zero = IN_WIDTH
    col["narrow"] = len(perm)
    blk0 = list(range(_ORIG_START["b_ik"], _ORIG_START["b_ik"] + IDX_DIM))
    blk0 += list(range(_ORIG_START["d_g"], _ORIG_START["d_g"] + _GATES_PER_KV_GROUP))
    blk0 += list(range(_ORIG_START["b_iw"], _ORIG_START["b_iw"] + IDX_HEADS))
    blk0 += [zero] * (LANE - len(blk0))
    blk1 = list(range(_ORIG_START["c_kr"], _ORIG_START["c_kr"] + MLA_ROPE))
    blk1 += list(range(_ORIG_START["d_g"] + _GATES_PER_KV_GROUP, _ORIG_START["d_g"] + 2 * _GATES_PER_KV_GROUP))
    blk1 += [zero] * (LANE - len(blk1))
    perm.extend(blk0 + blk1)
    width = _round_up(len(perm), 512)
    perm.extend([zero] * (width - len(perm)))
    return col, np.asarray(perm, np.int32), width


_COL, _IN_PERM, IN_WIDTH_PAD = _build_in_layout()
_GATE_LANE0 = IDX_DIM
_IW_LANE0 = IDX_DIM + _GATES_PER_KV_GROUP


def _rope_lane_tables(cos, sin):
    return jnp.concatenate([cos, cos], axis=-1), jnp.concatenate([-sin, sin], axis=-1)


def _lane_tables(positions, dim):
    cc, ss = _rope_lane_tables(*_rope_tables(positions, dim))
    return tuple(jnp.tile(t.reshape(-1, dim), (1, LANE // dim)) for t in (cc, ss))


def _seg64_sum(y):
    r = lax.broadcasted_iota(jnp.int32, (LANE, LANE), 0) // 64
    c = lax.broadcasted_iota(jnp.int32, (LANE, LANE), 1) // 64
    bd = jnp.where(r == c, 1.0, 0.0).astype(MXU_DTYPE)
    hi = y.astype(MXU_DTYPE)
    lo = (y - hi.astype(jnp.float32)).astype(MXU_DTYPE)
    return (jnp.dot(hi, bd, preferred_element_type=jnp.float32)
            + jnp.dot(lo, bd, preferred_element_type=jnp.float32))


def _prep_kernel(x_ref, g_ref, cc_ref, ss_ref, o_ref, *, n_blocks, seg, norm, scale):
    cc, ss, g = cc_ref[...], ss_ref[...], g_ref[...]
    lane = lax.broadcasted_iota(jnp.int32, cc.shape, 1)
    for h in range(n_blocks):
        x = x_ref[:, h * LANE:(h + 1) * LANE]
        if norm:
            if seg == LANE:
                ms = jnp.mean(x * x, axis=-1, keepdims=True)
            else:
                ms = _seg64_sum(x * x) * (1.0 / seg)
            x = x * lax.rsqrt(ms + EPS) * g
        if seg == LANE:
            partner = pltpu.roll(x, LANE // 2, axis=1)
        else:
            partner = jnp.where((lane % seg) < seg // 2, pltpu.roll(x, LANE - seg // 2, axis=1),
                                pltpu.roll(x, seg // 2, axis=1))
        y = x * cc + partner * ss
        if scale != 1.0:
            y = y * scale
        o_ref[:, h * LANE:(h + 1) * LANE] = y.astype(o_ref.dtype)


def _prep_heads(proj, col0, n_blocks, gain, cc, ss, *, seg, norm, scale=1.0, tm=256):
    m = proj.shape[0]
    tm = min(tm, m)
    width = LANE * n_blocks
    cb = col0 // width
    assert cb * width == col0
    return pl.pallas_call(
        functools.partial(_prep_kernel, n_blocks=n_blocks, seg=seg, norm=norm, scale=scale),
        grid=(m // tm,),
        in_specs=[pl.BlockSpec((tm, width), lambda i: (i, cb)),
                  pl.BlockSpec((1, LANE), lambda i: (0, 0)),
                  pl.BlockSpec((tm, LANE), lambda i: (i, 0)),
                  pl.BlockSpec((tm, LANE), lambda i: (i, 0))],
        out_specs=pl.BlockSpec((tm, width), lambda i: (i, 0)),
        out_shape=jax.ShapeDtypeStruct((m, width), MXU_DTYPE),
        compiler_params=_compiler_params(("parallel",)),
        name="prep_heads",
    )(proj, gain, cc, ss)


def _dot_nt(a, b):
    return lax.dot_general(a, b, (((1,), (1,)), ((), ())), preferred_element_type=jnp.float32)


LOG2E = math.log2(math.e)


def _softmax_step(s, mask, m, l, acc, v, *, rows_may_be_empty=False):
    if mask is not None:
        s = jnp.where(mask, s, NEG_INF)
    m_new = jnp.maximum(m, jnp.max(s, axis=-1, keepdims=True))
    alpha = jnp.exp2(m - m_new)
    p = jnp.exp2(s - m_new)
    if rows_may_be_empty:
        p = jnp.where(mask, p, 0.0)
    l = alpha * l + jnp.sum(p, axis=-1, keepdims=True)
    acc = alpha * acc + jnp.dot(p.astype(MXU_DTYPE), v, preferred_element_type=jnp.float32)
    return m_new, l, acc


def _softmax_finish(l, acc):
    return acc * (1.0 / jnp.maximum(l, 1e-30))


def _nsa_cmp_kernel(x_ref, pe_ref, w1_ref, w2_ref, gk_ref, cc_ref, ss_ref, kc_ref, vc_ref):
    nch = x_ref.shape[2]
    for kv in range(2):
        x = x_ref[kv, 0]
        lo = jnp.dot((x + pe_ref[kv, 0:1, :]).astype(MXU_DTYPE), w1_ref[kv, 0], preferred_element_type=jnp.float32)
        hi = jnp.dot((x + pe_ref[kv, 1:2, :]).astype(MXU_DTYPE), w1_ref[kv, 1], preferred_element_type=jnp.float32)
        hid = lo + pltpu.roll(hi, nch - 1, axis=0)
        comp = jnp.dot(jax.nn.gelu(hid).astype(MXU_DTYPE), w2_ref[kv], preferred_element_type=jnp.float32)
        if kv == 0:
            ms = jnp.mean(comp * comp, axis=-1, keepdims=True)
            y = comp * lax.rsqrt(ms + EPS) * gk_ref[...]
            y = y * cc_ref[0] + pltpu.roll(y, LANE // 2, axis=1) * ss_ref[0]
            kc_ref[0] = y.astype(kc_ref.dtype)
        else:
            vc_ref[0] = comp.astype(vc_ref.dtype)


def _nsa_compress(craw, pe2, w1, w2, gk, cc_c, ss_c, *, n_batch):
    _, gb, nch, width = craw.shape
    return pl.pallas_call(
        _nsa_cmp_kernel,
        grid=(gb,),
        in_specs=[pl.BlockSpec((2, 1, nch, width), lambda n: (0, n, 0, 0)),
                  pl.BlockSpec((2, 2, width), lambda n: (0, 0, 0)),
                  pl.BlockSpec((2, 2, width, CMP_HIDDEN), lambda n: (0, 0, 0, 0)),
                  pl.BlockSpec((2, CMP_HIDDEN, HEAD_DIM), lambda n: (0, 0, 0)),
                  pl.BlockSpec((1, HEAD_DIM), lambda n: (0, 0)),
                  pl.BlockSpec((1, nch, HEAD_DIM), lambda n: (n % n_batch, 0, 0)),
                  pl.BlockSpec((1, nch, HEAD_DIM), lambda n: (n % n_batch, 0, 0))],
        out_specs=[pl.BlockSpec((1, nch, HEAD_DIM), lambda n: (n, 0, 0)),
                   pl.BlockSpec((1, nch, HEAD_DIM), lambda n: (n, 0, 0))],
        out_shape=[jax.ShapeDtypeStruct((gb, nch, HEAD_DIM), MXU_DTYPE)] * 2,
        compiler_params=_compiler_params(("parallel",)),
        name="nsa_compress",
    )(craw, pe2, w1, w2, gk, cc_c, ss_c)


def _nsa_attn_kernel(q_ref, kc_ref, vc_ref, ks_ref, vs_ref, kw_ref, vw_ref, gate_ref, c2s_ref, exp_ref, o_ref, *,
                     tq, tk, n_sel, n_top):
    i = pl.program_id(2)
    R = NSA_Q_PER_KV
    rows = R * tq
    nch = kc_ref.shape[1]
    bf16 = MXU_DTYPE
    q = jnp.concatenate([q_ref[:, r * HEAD_DIM:(r + 1) * HEAD_DIM] for r in range(R)], axis=0)
    t_row = i * tq + lax.broadcasted_iota(jnp.int32, (rows, 1), 0) % tq

    s = _dot_nt(q, kc_ref[0])
    blk_end = lax.broadcasted_iota(jnp.int32, (1, nch), 1) * CMP_STRIDE + (CMP_LEN - 1)
    mask = blk_end <= t_row
    zero = jnp.zeros((rows, 1), jnp.float32)
    s = jnp.where(mask, s, NEG_INF)
    p_cmp = jnp.where(mask, jnp.exp2(s - jnp.max(s, axis=-1, keepdims=True)), 0.0)
    p_cmp = p_cmp * (1.0 / jnp.maximum(jnp.sum(p_cmp, axis=-1, keepdims=True), 1e-30))
    o_cmp = jnp.dot(p_cmp.astype(bf16), vc_ref[0], preferred_element_type=jnp.float32)

    imp_rows = jnp.dot(p_cmp.astype(bf16), c2s_ref[...], preferred_element_type=jnp.float32)
    imp = imp_rows[0:tq]
    for r in range(1, R):
        imp = imp + imp_rows[r * tq:(r + 1) * tq]
    imp_t = imp.T[0:n_sel]
    blk = lax.broadcasted_iota(jnp.int32, (n_sel, tq), 0)
    t_lane = i * tq + lax.broadcasted_iota(jnp.int32, (n_sel, tq), 1)
    dist = lax.shift_right_logical(t_lane, int(math.log2(SEL_BLOCK))) - blk
    forced = (blk < SEL_INIT) | ((dist >= 0) & (dist < SEL_LOCAL))
    val = jnp.where(forced, POS_INF, jnp.where(blk * SEL_BLOCK <= t_lane, imp_t, NEG_INF))
    rank = jnp.zeros((n_sel, tq), jnp.float32)
    for mblk in range(n_sel):
        vm = val[mblk:mblk + 1, :]
        rank = rank + jnp.where((vm > val) | ((vm == val) & (blk > mblk)), 1.0, 0.0)
    drop_t = jnp.where(rank < n_top, 0.0, 1.0)
    drop_t = jnp.concatenate([drop_t, jnp.zeros((LANE - n_sel, tq), jnp.float32)], axis=0) if n_sel < LANE else drop_t
    drop_q = drop_t.T.astype(bf16)
    drop_rows = jnp.concatenate([drop_q] * R, axis=0)

    def slc_step(j, width, carry, diagonal):
        off = pl.multiple_of(j * tk, tk)
        k = ks_ref[pl.ds(off, width), :]
        v = vs_ref[pl.ds(off, width), :].astype(bf16)
        s = _dot_nt(q, k) + jnp.dot(drop_rows, exp_ref[j, :, 0:width], preferred_element_type=jnp.float32)
        mask = (off + lax.broadcasted_iota(jnp.int32, (1, width), 1) <= t_row) if diagonal else None
        return _softmax_step(s, mask, *carry, v)

    init = (zero + NEG_INF, zero, jnp.zeros((rows, HEAD_DIM), jnp.float32))
    n_full = (i * tq) // tk
    carry = lax.fori_loop(0, n_full, lambda j, c: slc_step(j, tk, c, False), init)
    branches = [functools.partial(slc_step, n_full, tq * (u + 1), diagonal=True) for u in range(tk // tq)]
    _, l, acc = lax.switch(i - n_full * (tk // tq), branches, carry)
    o_slc = _softmax_finish(l, acc)

    span = WINDOW + tq
    off = pl.multiple_of(jnp.maximum(i * tq - WINDOW, 0), tq)
    key = off + lax.broadcasted_iota(jnp.int32, (1, span), 1)
    s = _dot_nt(q, kw_ref[pl.ds(off, span), :])
    s = jnp.where(key <= t_row, jnp.where(key > t_row - WINDOW, s, NEG_INF), NEG_INF)
    _, l, acc = _softmax_step(s, None, *init, vw_ref[pl.ds(off, span), :].astype(bf16))
    o_win = _softmax_finish(l, acc)

    gate = jax.nn.sigmoid(gate_ref[...])
    for r in range(R):
        c = _GATE_LANE0 + 3 * r
        rs = slice(r * tq, (r + 1) * tq)
        o_ref[:, r * HEAD_DIM:(r + 1) * HEAD_DIM] = (gate[:, c:c + 1] * o_cmp[rs]
                                                     + gate[:, c + 1:c + 2] * o_slc[rs]
                                                     + gate[:, c + 2:c + 3] * o_win[rs])


def _nsa_mixer(proj, B, S, cc_h, ss_h, positions, g_q, g_k, cmp_w1, cmp_w2, cmp_pe):
    M = B * S
    G, R, d = NSA_KV_HEADS, NSA_Q_PER_KV, HEAD_DIM
    bf16 = MXU_DTYPE
    tq = Q_BLOCK
    nq = S // tq
    nch = S // CMP_STRIDE
    n_sel = S // SEL_BLOCK
    n_top = min(SEL_TOPN, n_sel)
    kv0 = _COL["d_kv"]

    qn = _prep_heads(proj, _COL["d_q"], HEADS_PER_GROUP, g_q.reshape(1, d), cc_h, ss_h,
                     seg=LANE, norm=True, scale=d ** -0.5 * LOG2E)
    k_slc =_prep_heads(proj, kv0 + (1 * 2 + 0) * G * d, G, g_k[1].reshape(1, d), cc_h, ss_h, seg=LANE, norm=True)
    k_win = _prep_heads(proj, kv0 + (2 * 2 + 0) * G * d, G, g_k[2].reshape(1, d), cc_h, ss_h, seg=LANE, norm=True)

    craw = proj[:, kv0:kv0 + 2 * G * d].reshape(B, S, 2, G, d).transpose(2, 3, 0, 1, 4)
    craw = craw.reshape(2, G * B, nch, CMP_STRIDE * d)
    ends = np.minimum(np.arange(nch) * CMP_STRIDE + CMP_LEN - 1, S - 1)
    cc_c, ss_c = _rope_lane_tables(*_rope_tables(positions[:, ends], d))
    pe2 = cmp_pe.reshape(2, 2, CMP_STRIDE * d)
    w1 = cmp_w1.astype(bf16).reshape(2, 2, CMP_STRIDE * d, CMP_HIDDEN)
    k_cmp, v_cmp = _nsa_compress(craw, pe2, w1, cmp_w2.astype(bf16), g_k[0].reshape(1, d), cc_c, ss_c, n_batch=B)

    starts = np.arange(nch) * CMP_STRIDE
    sel_start = np.arange(LANE) * SEL_BLOCK
    c2s = ((starts[:, None] < sel_start[None, :] + SEL_BLOCK) & (starts[:, None] + CMP_LEN > sel_start[None, :])
           & (np.arange(nch)[:, None] < nch - 1) & (np.arange(LANE)[None, :] < n_sel))
    tk = min(512, S)
    expand = (np.arange(LANE)[None, :, None] == (np.arange(S // tk)[:, None, None] * tk + np.arange(tk)[None, None, :]) // SEL_BLOCK)
    c2s = jnp.asarray(c2s, bf16)
    expand = jnp.asarray(expand * MASK_BIAS, bf16)

    narrow = _COL["narrow"] // LANE
    col_vs = (kv0 + (1 * 2 + 1) * G * d) // LANE
    col_vw = (kv0 + (2 * 2 + 1) * G * d) // LANE
    return pl.pallas_call(
        functools.partial(_nsa_attn_kernel, tq=tq, tk=tk, n_sel=n_sel, n_top=n_top),
        grid=(B, G, nq),
        in_specs=[pl.BlockSpec((tq, R * d), lambda b, g, i: (b * nq + i, g)),
                  pl.BlockSpec((1, nch, d), lambda b, g, i: (g * B + b, 0, 0)),
                  pl.BlockSpec((1, nch, d), lambda b, g, i: (g * B + b, 0, 0)),
                  pl.BlockSpec((S, d), lambda b, g, i: (b, g)),
                  pl.BlockSpec((S, d), lambda b, g, i: (b, col_vs + g)),
                  pl.BlockSpec((S, d), lambda b, g, i: (b, g)),
                  pl.BlockSpec((S, d), lambda b, g, i: (b, col_vw + g)),
                  pl.BlockSpec((tq, LANE), lambda b, g, i: (b * nq + i, narrow + g)),
                  pl.BlockSpec((nch, LANE), lambda b, g, i: (0, 0)),
                  pl.BlockSpec((S // tk, LANE, tk), lambda b, g, i: (0, 0, 0))],
        out_specs=pl.BlockSpec((tq, R * d), lambda b, g, i: (b * nq + i, g)),
        out_shape=jax.ShapeDtypeStruct((M, GROUP_WIDTH), jnp.float32),
        compiler_params=_compiler_params(("parallel", "parallel", "arbitrary")),
        name="nsa_attention",
    )(qn, k_cmp, v_cmp, k_slc, proj, k_win, proj, proj, c2s, expand)


_INT_MIN = -2 ** 31


def _sortable_key(x):
    b = lax.bitcast_convert_type(x + 0.0, jnp.int32)
    return jnp.where(b >= 0, b, b ^ 0x7FFFFFFF)


def _dsa_kernel(q_ref, k_ref, v_ref, iq_ref, ik_ref, nar_ref, o_ref, key_ref, msk_ref, w_ref, *,
                tq, tk, n_keep, idx_bits):
    i = pl.program_id(1)
    H = HEADS_PER_GROUP
    bf16 = MXU_DTYPE
    t_col = i * tq + lax.broadcasted_iota(jnp.int32, (tq, 1), 0)
    key_lane = lax.broadcasted_iota(jnp.int32, (1, tq), 1)
    lane = lax.broadcasted_iota(jnp.int32, (tq, LANE), 1)
    iw = nar_ref[...] * (IDX_HEADS ** -0.5)
    for hd in range(IDX_HEADS):
        w_ref[hd] = jnp.broadcast_to(iw[:, _IW_LANE0 + hd:_IW_LANE0 + hd + 1], (tq, tq))
    iq_rows = jnp.concatenate([iq_ref[:, pr * LANE:(pr + 1) * LANE] for pr in range(IDX_HEADS // 2)], axis=0)

    def score_body(j, _):
        off = pl.multiple_of(j * tq, tq)
        ik = ik_ref[pl.ds(off, tq), :]
        ik_lo = jnp.where(lane < IDX_DIM, ik, jnp.zeros_like(ik))
        ik_hi = jnp.where(lane < IDX_DIM, jnp.zeros_like(ik), pltpu.roll(ik.astype(jnp.float32), IDX_DIM, axis=1).astype(bf16))
        rel = (jnp.maximum(_dot_nt(iq_rows, ik_lo), 0.0), jnp.maximum(_dot_nt(iq_rows, ik_hi), 0.0))
        sc = jnp.zeros((tq, tq), jnp.float32)
        for hd in range(IDX_HEADS):
            sc = sc + w_ref[hd] * rel[hd % 2][(hd // 2) * tq:(hd // 2 + 1) * tq]
        sc = jnp.where(off + key_lane <= t_col, sc, NEG_INF)
        key_ref[j] = _sortable_key(sc)
        return 0

    lax.fori_loop(0, i + 1, score_body, 0)

    def count(pred):
        def body(j, part):
            return part + jnp.where(pred(key_ref[j], j * tq), 1.0, 0.0)
        part = lax.fori_loop(0, i + 1, body, jnp.zeros((tq, tq), jnp.float32))
        return jnp.sum(part, axis=-1, keepdims=True)

    c0 = count(lambda kk, off: kk >= 0)
    thr = jnp.where(c0 >= n_keep, 0, _INT_MIN).astype(jnp.int32)

    def thr_body(it, thr):
        cand = thr | lax.shift_left(jnp.int32(1), 30 - it)
        c = count(lambda kk, off: kk >= cand)
        return jnp.where(c >= n_keep, cand, thr)

    thr = lax.fori_loop(0, 31, thr_body, thr)

    n_ge = count(lambda kk, off: kk >= thr)

    def cut_search():
        need = n_keep - count(lambda kk, off: kk > thr)

        def cut_body(it, cut):
            cand = cut | lax.shift_left(jnp.int32(1), idx_bits - 1 - it)
            c = count(lambda kk, off: (kk == thr) & (off + key_lane < cand))
            return jnp.where(c < need, cand, cut)

        return lax.fori_loop(0, idx_bits, cut_body, jnp.zeros((tq, 1), jnp.int32))

    cut = lax.cond(jnp.max(n_ge) > n_keep, cut_search, lambda: jnp.full((tq, 1), 2 ** idx_bits - 1, jnp.int32))

    def mask_body(j, _):
        kk = key_ref[j]
        kidx = j * tq + key_lane
        sel = (kk > thr) | ((kk == thr) & (kidx <= cut))
        msk_ref[j] = jnp.where(kidx <= t_col, jnp.where(sel, 0.0, NEG_INF), NEG_INF)
        return 0

    lax.fori_loop(0, i + 1, mask_body, 0)

    hpp = 4
    rows = hpp * tq
    zero = jnp.zeros((rows, 1), jnp.float32)
    qs = [jnp.concatenate([q_ref[:, (hc * hpp + r) * HEAD_DIM:(hc * hpp + r + 1) * HEAD_DIM] for r in range(hpp)], axis=0)
          for hc in range(H // hpp)]
    tpc = tk // tq

    def att_step(j, n_tiles, carry):
        if n_tiles == 0:
            return carry
        off = pl.multiple_of(j * tk, tk)
        k = k_ref[pl.ds(off, n_tiles * tq), :]
        v = v_ref[pl.ds(off, n_tiles * tq), :].astype(bf16)
        bias1 = jnp.concatenate([msk_ref[j * tpc + u] for u in range(n_tiles)], axis=1)
        bias = jnp.concatenate([bias1] * hpp, axis=0)
        return tuple(_softmax_step(_dot_nt(q, k) + bias, None, *carry[c], v) for c, q in enumerate(qs))

    init = tuple((zero + NEG_INF, zero, jnp.zeros((rows, HEAD_DIM), jnp.float32)) for _ in qs)
    n_full = (i + 1) // tpc
    res = lax.fori_loop(0, n_full, lambda j, c: att_step(j, tpc, c), init)
    res = lax.switch(i + 1 - n_full * tpc, [functools.partial(att_step, n_full, u) for u in range(tpc)], res)
    for hc, (_, l, acc) in enumerate(res):
        o = _softmax_finish(l, acc)
        for r in range(hpp):
            hh = hc * hpp + r
            o_ref[:, hh * HEAD_DIM:(hh + 1) * HEAD_DIM] = o[r * tq:(r + 1) * tq]


def _dsa_mixer(proj, B, S, cc_h, ss_h, cc_i, ss_i, g_q, g_k):
    M = B * S
    d = HEAD_DIM
    tq = Q_BLOCK
    nq = S // tq
    n_keep = min(DSA_TOPK, S // 4)
    qn = _prep_heads(proj, _COL["b_q"], HEADS_PER_GROUP, g_q.reshape(1, d), cc_h, ss_h, seg=LANE, norm=True,
                     scale=d ** -0.5 * LOG2E)
    kn =_prep_heads(proj, _COL["b_k"], 1, g_k.reshape(1, d), cc_h, ss_h, seg=LANE, norm=True)
    ones = jnp.ones((1, LANE), jnp.float32)
    iqn = _prep_heads(proj, _COL["b_iq"], IDX_HEADS * IDX_DIM // LANE, ones, cc_i, ss_i, seg=IDX_DIM, norm=False,
                      scale=IDX_DIM ** -0.5)
    ikn = _prep_heads(proj, _COL["narrow"], 1, ones, cc_i, ss_i, seg=IDX_DIM, norm=False)
    narrow = _COL["narrow"] // LANE
    col_v = _COL["b_v"] // LANE
    return pl.pallas_call(
        functools.partial(_dsa_kernel, tq=tq, tk=min(512, S), n_keep=n_keep, idx_bits=int(math.log2(S))),
        grid=(B, nq),
        in_specs=[pl.BlockSpec((tq, GROUP_WIDTH), lambda b, i: (b * nq + i, 0)),
                  pl.BlockSpec((S, d), lambda b, i: (b, 0)),
                  pl.BlockSpec((S, d), lambda b, i: (b, col_v)),
                  pl.BlockSpec((tq, IDX_HEADS * IDX_DIM), lambda b, i: (b * nq + i, 0)),
                  pl.BlockSpec((S, LANE), lambda b, i: (b, 0)),
                  pl.BlockSpec((tq, LANE), lambda b, i: (b * nq + i, narrow))],
        out_specs=pl.BlockSpec((tq, GROUP_WIDTH), lambda b, i: (b * nq + i, 0)),
        out_shape=jax.ShapeDtypeStruct((M, GROUP_WIDTH), jnp.float32),
        scratch_shapes=[pltpu.VMEM((nq, tq, tq), jnp.int32), pltpu.VMEM((nq, tq, tq), jnp.float32),
                        pltpu.VMEM((IDX_HEADS, tq, tq), jnp.float32)],
        compiler_params=_compiler_params(("parallel", "arbitrary")),
        name="dsa_attention",
    )(qn, kn, proj, iqn, ikn, proj)


def _causal_chains(qs, k_ref, v_ref, dk, dv, t_row, t0, tq, tk):
    rows = qs[0].shape[0]
    zero = jnp.zeros((rows, 1), jnp.float32)

    def step(off, width, carry, masked):
        mask = (off + lax.broadcasted_iota(jnp.int32, (1, width), 1) <= t_row) if masked else None
        out = []
        for c, q in enumerate(qs):
            k = k_ref[pl.ds(off, width), c * dk:(c + 1) * dk]
            v = v_ref[pl.ds(off, width), c * dv:(c + 1) * dv].astype(MXU_DTYPE)
            out.append(_softmax_step(_dot_nt(q, k), mask, *carry[c], v))
        return tuple(out)

    init = tuple((zero + NEG_INF, zero, jnp.zeros((rows, dv), jnp.float32)) for _ in qs)
    n_full = t0 // tk
    res = lax.fori_loop(0, n_full, lambda j, c: step(pl.multiple_of(j * tk, tk), tk, c, False), init)
    off = pl.multiple_of(n_full * tk, tk)
    branches = [functools.partial(step, off, tq * (u + 1), masked=True) for u in range(tk // tq)]
    res = lax.switch((t0 - off) // tq, branches, res)
    return [_softmax_finish(l, acc) for _, l, acc in res]


def _diff_attn_kernel(q_ref, k_ref, v_ref, lam_ref, g_ref, o_ref, *, tq, tk, hb, out_scale):
    i = pl.program_id(2)
    d = HEAD_DIM
    lane = lax.broadcasted_iota(jnp.int32, (tq, d), 1)
    qs = []
    for c in range(hb):
        q = q_ref[:, c * d:(c + 1) * d]
        zeros = jnp.zeros_like(q)
        qs.append(jnp.concatenate([jnp.where(lane < DIFF_QK_DIM, q, zeros), jnp.where(lane < DIFF_QK_DIM, zeros, q)], axis=0))
    t_row = i * tq + lax.broadcasted_iota(jnp.int32, (2 * tq, 1), 0) % tq
    outs = _causal_chains(qs, k_ref, v_ref, d, d, t_row, i * tq, tq, tk)
    for c, o2 in enumerate(outs):
        o = o2[0:tq] - lam_ref[...] * o2[tq:2 * tq]
        ms = jnp.mean(o * o, axis=-1, keepdims=True)
        o_ref[:, c * d:(c + 1) * d] = o * lax.rsqrt(ms + EPS) * g_ref[...] * out_scale


def _diff_mixer(proj, B, S, cc_d, ss_d, g_q, g_k, g_sub, lam_params, layer_idx, *, tq=128, tk=512, hb=4):
    M = B * S
    d = HEAD_DIM
    nq = S // tq
    tk = min(tk, S)
    reps = LANE // DIFF_QK_DIM
    qn = _prep_heads(proj, _COL["a_q"], HEADS_PER_GROUP, jnp.tile(g_q, reps).reshape(1, d), cc_d, ss_d,
                     seg=DIFF_QK_DIM, norm=True, scale=DIFF_QK_DIM ** -0.5 * LOG2E)
    kn = _prep_heads(proj, _COL["a_k"], HEADS_PER_GROUP, jnp.tile(g_k, reps).reshape(1, d), cc_d, ss_d,
                     seg=DIFF_QK_DIM, norm=True)
    lam_init = 0.8 - 0.6 * math.exp(-0.3 * layer_idx)
    lp = lam_params.astype(jnp.float32)
    lam = jnp.exp(jnp.sum(lp[0] * lp[1])) - jnp.exp(jnp.sum(lp[2] * lp[3])) + lam_init
    col_v = _COL["a_v"] // (hb * d)
    return pl.pallas_call(
        functools.partial(_diff_attn_kernel, tq=tq, tk=tk, hb=hb, out_scale=1.0 - lam_init),
        grid=(B, HEADS_PER_GROUP // hb, nq),
        in_specs=[pl.BlockSpec((tq, hb * d), lambda b, h, i: (b * nq + i, h)),
                  pl.BlockSpec((S, hb * d), lambda b, h, i: (b, h)),
                  pl.BlockSpec((S, hb * d), lambda b, h, i: (b, col_v + h)),
                  pl.BlockSpec((1, d), lambda b, h, i: (0, 0)),
                  pl.BlockSpec((1, d), lambda b, h, i: (0, 0))],
        out_specs=pl.BlockSpec((tq, hb * d), lambda b, h, i: (b * nq + i, h)),
        out_shape=jax.ShapeDtypeStruct((M, GROUP_WIDTH), jnp.float32),
        compiler_params=_compiler_params(("parallel", "parallel", "arbitrary")),
        name="diff_attention",
    )(qn, kn, proj, jnp.full((1, d), lam, jnp.float32), g_sub.reshape(1, d))


MLA_QK_PAD = 2 * LANE


def _rmsnorm_cols_kernel(x_ref, g_ref, o_ref):
    x = x_ref[...]
    ms = jnp.mean(x * x, axis=-1, keepdims=True)
    o_ref[...] = (x * lax.rsqrt(ms + EPS) * g_ref[...]).astype(o_ref.dtype)


def _rmsnorm_cols(proj, col0, width, g, *, tm=512):
    m = proj.shape[0]
    tm = min(tm, m)
    cb = col0 // width
    assert cb * width == col0
    return pl.pallas_call(
        _rmsnorm_cols_kernel,
        grid=(m // tm,),
        in_specs=[pl.BlockSpec((tm, width), lambda i: (i, cb)), pl.BlockSpec((1, width), lambda i: (0, 0))],
        out_specs=pl.BlockSpec((tm, width), lambda i: (i, 0)),
        out_shape=jax.ShapeDtypeStruct((m, width), MXU_DTYPE),
        compiler_params=_compiler_params(("parallel",)),
        name="rmsnorm_cols",
    )(proj, g.reshape(1, width))


def _mla_prep_kernel(a_ref, b_ref, g_ref, cc_ref, ss_ref, o_ref, *, a_stride, b_col0, b_stride, scale):
    cc, ss = cc_ref[...], ss_ref[...]
    lane = lax.broadcasted_iota(jnp.int32, cc.shape, 1)
    for h in range(HEADS_PER_GROUP):
        a = a_ref[:, h * a_stride:h * a_stride + LANE]
        b = jnp.where(lane < MLA_ROPE, b_ref[:, b_col0 + h * b_stride:b_col0 + h * b_stride + LANE], 0.0)
        ms = (jnp.sum(a * a, axis=-1, keepdims=True) + jnp.sum(b * b, axis=-1, keepdims=True)) * (1.0 / (MLA_NOPE + MLA_ROPE))
        r = lax.rsqrt(ms + EPS)
        ya = a * r * g_ref[:, 0:LANE]
        yb = b * r * g_ref[:, LANE:2 * LANE]
        partner = jnp.where((lane % MLA_ROPE) < MLA_ROPE // 2, pltpu.roll(yb, LANE - MLA_ROPE // 2, axis=1),
                            pltpu.roll(yb, MLA_ROPE // 2, axis=1))
        yb = yb * cc + partner * ss
        o_ref[:, h * MLA_QK_PAD:h * MLA_QK_PAD + LANE] = (ya * scale).astype(o_ref.dtype)
        o_ref[:, h * MLA_QK_PAD + LANE:(h + 1) * MLA_QK_PAD] = (yb * scale).astype(o_ref.dtype)


def _mla_prep(a_arr, a_width, a_stride, b_arr, b_block, b_width, b_col0, b_stride, gain, cc, ss, *, scale=1.0, tm=256):
    m = a_arr.shape[0]
    tm = min(tm, m)
    g2 = jnp.concatenate([gain, jnp.zeros((MLA_QK_PAD - gain.shape[0],), gain.dtype)]).reshape(1, MLA_QK_PAD)
    return pl.pallas_call(
        functools.partial(_mla_prep_kernel, a_stride=a_stride, b_col0=b_col0, b_stride=b_stride, scale=scale),
        grid=(m // tm,),
        in_specs=[pl.BlockSpec((tm, a_width), lambda i: (i, 0)),
                  pl.BlockSpec((tm, b_width), lambda i: (i, b_block)),
                  pl.BlockSpec((1, MLA_QK_PAD), lambda i: (0, 0)),
                  pl.BlockSpec((tm, LANE), lambda i: (i, 0)),
                  pl.BlockSpec((tm, LANE), lambda i: (i, 0))],
        out_specs=pl.BlockSpec((tm, HEADS_PER_GROUP * MLA_QK_PAD), lambda i: (i, 0)),
        out_shape=jax.ShapeDtypeStruct((m, HEADS_PER_GROUP * MLA_QK_PAD), MXU_DTYPE),
        compiler_params=_compiler_params(("parallel",)),
        name="mla_prep",
    )(a_arr, b_arr, g2, cc, ss)


def _mla_attn_kernel(q_ref, k_ref, v_ref, o_ref, *, tq, tk, hb):
    i = pl.program_id(2)
    d = HEAD_DIM
    t_row = i * tq + lax.broadcasted_iota(jnp.int32, (tq, 1), 0)
    qs = [q_ref[:, c * MLA_QK_PAD:(c + 1) * MLA_QK_PAD] for c in range(hb)]
    outs = _causal_chains(qs, k_ref, v_ref, MLA_QK_PAD, d, t_row, i * tq, tq, tk)
    for c, o in enumerate(outs):
        o_ref[:, c * d:(c + 1) * d] = o


def _mla_mixer(proj, B, S, cc_m, ss_m, g_cq, g_ckv, w_uq, w_uk, w_uv, g_q, g_k, *, tq=256, tk=512, hb=4):
    M = B * S
    H, d = HEADS_PER_GROUP, HEAD_DIM
    tq = min(tq, S)
    tk = min(tk, S)
    nq = S // tq
    dqk = MLA_NOPE + MLA_ROPE
    w_q = jnp.pad(w_uq.astype(MXU_DTYPE).reshape(MLA_Q_RANK, H, dqk), ((0, 0), (0, 0), (0, MLA_QK_PAD - dqk)))
    w_q = w_q.reshape(MLA_Q_RANK, H * MLA_QK_PAD)
    w_kv = jnp.concatenate([w_uk, w_uv], axis=1).astype(MXU_DTYPE)
    cq = _rmsnorm_cols(proj, _COL["c_q"], MLA_Q_RANK, g_cq)
    ckv = _rmsnorm_cols(proj, _COL["c_kv"], MLA_KV_RANK, g_ckv)
    q_up = _matmul(cq, w_q, tm=min(1024, M), tn=1024, tk=MLA_Q_RANK)
    kv_up = _matmul(ckv, w_kv, tm=min(1024, M), tn=1024, tk=MLA_KV_RANK)
    qn = _mla_prep(q_up, H * MLA_QK_PAD, MLA_QK_PAD, q_up, 0, H * MLA_QK_PAD, LANE, MLA_QK_PAD, g_q, cc_m, ss_m,
                   scale=dqk ** -0.5 * LOG2E)
    kn = _mla_prep(kv_up, H * MLA_NOPE, LANE, proj, _COL["narrow"] // LANE + 1, LANE, 0, 0, g_k, cc_m, ss_m)
    return pl.pallas_call(
        functools.partial(_mla_attn_kernel, tq=tq, tk=tk, hb=hb),
        grid=(B, H // hb, nq),
        in_specs=[pl.BlockSpec((tq, hb * MLA_QK_PAD), lambda b, h, i: (b * nq + i, h)),
                  pl.BlockSpec((S, hb * MLA_QK_PAD), lambda b, h, i: (b, h)),
                  pl.BlockSpec((S, hb * d), lambda b, h, i: (b, H // hb + h))],
        out_specs=pl.BlockSpec((tq, hb * d), lambda b, h, i: (b * nq + i, h)),
        out_shape=jax.ShapeDtypeStruct((M, GROUP_WIDTH), jnp.float32),
        compiler_params=_compiler_params(("parallel", "parallel", "arbitrary")),
        name="mla_attention",
    )(qn, kn, kv_up)


def _mix_kernel(a_ref, b_ref, c_ref, d_ref, g_ref, o_ref):
    o_ref[:, 0:GROUP_WIDTH] = a_ref[...].astype(o_ref.dtype)
    for n, ref in enumerate((b_ref, c_ref, d_ref)):
        x = ref[...]
        ms = jnp.mean(x * x, axis=-1, keepdims=True)
        o_ref[:, (n + 1) * GROUP_WIDTH:(n + 2) * GROUP_WIDTH] = (
            x * lax.rsqrt(ms + EPS) * g_ref[n:n + 1, :]).astype(o_ref.dtype)


def _mix_groups(o_a, o_b, o_c, o_d, g_out, *, tm=512):
    m = o_a.shape[0]
    tm = min(tm, m)
    spec = pl.BlockSpec((tm, GROUP_WIDTH), lambda i: (i, 0))
    return pl.pallas_call(
        _mix_kernel,
        grid=(m // tm,),
        in_specs=[spec, spec, spec, spec, pl.BlockSpec((3, GROUP_WIDTH), lambda i: (0, 0))],
        out_specs=pl.BlockSpec((tm, MIX_WIDTH), lambda i: (i, 0)),
        out_shape=jax.ShapeDtypeStruct((m, MIX_WIDTH), MXU_DTYPE),
        compiler_params=_compiler_params(("parallel",)),
        name="mix_groups",
    )(o_a, o_b, o_c, o_d, g_out)


def _pad_cols(w, n):
    return jnp.pad(w, ((0, 0), (0, n - w.shape[1])))


def kernel(x, p, positions, w_in, w_out, g_mix, g_ffn, w_gate, w_up, w_down, w_ple_proj, w_ple_gate, g_ple,
           g_group_out, diff_g_q, diff_g_k, diff_g_sub, diff_lambda, dsa_g_q, dsa_g_k, mla_g_cq, mla_g_ckv,
           mla_w_uq, mla_w_uk, mla_w_uv, mla_g_q, mla_g_k, nsa_g_q, nsa_g_k, nsa_cmp_w1, nsa_cmp_w2, nsa_cmp_pe):
    B, S = x.shape[:2]
    M = B * S
    H = HEADS_PER_GROUP
    bf16 = MXU_DTYPE
    cc_h, ss_h = _lane_tables(positions, HEAD_DIM)
    cc_i, ss_i = _lane_tables(positions, IDX_DIM)
    cc_d, ss_d = _lane_tables(positions, DIFF_QK_DIM)
    cc_m, ss_m = _lane_tables(positions, MLA_ROPE)
    h = x.reshape(M, D_MODEL)
    for i in range(DEPTH):
        w_in_i = jnp.take(_pad_cols(w_in[i].astype(bf16), IN_WIDTH + 1), _IN_PERM, axis=1)
        u = _rmsnorm_rows(h, g_mix[i])
        proj = _matmul(u, w_in_i, tm=1024, tn=512, tk=D_MODEL)
        o_a = _diff_mixer(proj, B, S, cc_d, ss_d, diff_g_q[i], diff_g_k[i], diff_g_sub[i], diff_lambda[i], i)
        o_b = _dsa_mixer(proj, B, S, cc_h, ss_h, cc_i, ss_i, dsa_g_q[i], dsa_g_k[i])
        o_c = _mla_mixer(proj, B, S, cc_m, ss_m, mla_g_cq[i], mla_g_ckv[i], mla_w_uq[i], mla_w_uk[i], mla_w_uv[i],
                         mla_g_q[i], mla_g_k[i])
        o_d = _nsa_mixer(proj, B, S, cc_h, ss_h, positions, nsa_g_q[i], nsa_g_k[i],
                         nsa_cmp_w1[i], nsa_cmp_w2[i], nsa_cmp_pe[i])
        mixed = _mix_groups(o_a, o_b, o_c, o_d, g_group_out[i])
        h = _matmul(mixed, w_out[i].astype(bf16), res=h, tm=1024, tn=512, tk=MIX_WIDTH)
        u = _rmsnorm_rows(h, g_ffn[i])
        act = _swiglu(u, w_gate[i].astype(bf16), w_up[i].astype(bf16), tm=1024, tn=256)
        h = _matmul(act, w_down[i].astype(bf16), res=h, tm=512, tn=256, tk=D_FF)
        u = _rmsnorm_rows(h, g_ple[i])
        h = _ple(u, w_ple_gate[i].astype(bf16), p[i].reshape(M, PLE_DIM).astype(bf16), w_ple_proj[i].astype(bf16), h,
                 tm=1024, tn=512)
    return h.reshape(B, S, D_MODEL)
```

```python
import functools
import math

import numpy as np
import jax
import jax.numpy as jnp
from jax import lax
from jax.experimental import pallas as pl
from jax.experimental.pallas import tpu as pltpu

D_MODEL = 4096
DEPTH = 4
HEAD_DIM = 128
N_GROUPS = 4
HEADS_PER_GROUP = D_MODEL // HEAD_DIM // N_GROUPS
GROUP_WIDTH = HEADS_PER_GROUP * HEAD_DIM
MIX_WIDTH = N_GROUPS * GROUP_WIDTH
D_FF = ((8 * D_MODEL + 3 * 256 - 1) // (3 * 256)) * 256
PLE_DIM = 256
ROPE_THETA = 10000.0
EPS = 1e-6
Q_BLOCK = 128
NEG_INF = -1e30
MASK_BIAS = -2.0 ** 100
POS_INF = 1e30

DIFF_QK_DIM = HEAD_DIM // 2
DIFF_V_DIM = HEAD_DIM
IDX_HEADS = 16
IDX_DIM = 64
DSA_TOPK = 256
MLA_Q_RANK = 1024
MLA_KV_RANK = 512
MLA_NOPE = 128
MLA_ROPE = 64
MLA_V = HEAD_DIM
NSA_KV_HEADS = 2
NSA_Q_PER_KV = HEADS_PER_GROUP // NSA_KV_HEADS
CMP_LEN = 32
CMP_STRIDE = 16
CMP_HIDDEN = HEAD_DIM
SEL_BLOCK = 64
SEL_TOPN = 16
SEL_INIT = 1
SEL_LOCAL = 2
SEL_Q_BLOCK = 32
WINDOW = 512

IN_SIZES = (
    HEADS_PER_GROUP * 2 * DIFF_QK_DIM, HEADS_PER_GROUP * 2 * DIFF_QK_DIM, HEADS_PER_GROUP * DIFF_V_DIM,
    GROUP_WIDTH, HEAD_DIM, HEAD_DIM, IDX_HEADS * IDX_DIM, IDX_DIM, IDX_HEADS,
    MLA_Q_RANK, MLA_KV_RANK, MLA_ROPE,
    GROUP_WIDTH, 3 * 2 * NSA_KV_HEADS * HEAD_DIM, 3 * HEADS_PER_GROUP,
)
IN_WIDTH = sum(IN_SIZES)

V7X_VMEM_LIMIT_BYTES = 56 * 1024 * 1024
LANE = 128
MXU_DTYPE = getattr(jnp, "bfloat16")


def _round_up(n, m):
    return (n + m - 1) // m * m


def _compiler_params(semantics):
    return pltpu.CompilerParams(dimension_semantics=semantics, vmem_limit_bytes=V7X_VMEM_LIMIT_BYTES)


def _rmsnorm_rows_kernel(x_ref, g_ref, o_ref):
    x = x_ref[...]
    ms = jnp.mean(x * x, axis=-1, keepdims=True)
    o_ref[...] = (x * lax.rsqrt(ms + EPS) * g_ref[...]).astype(o_ref.dtype)


def _rmsnorm_rows(x, g, *, tm=256):
    m, d = x.shape
    out_dtype = MXU_DTYPE
    return pl.pallas_call(
        _rmsnorm_rows_kernel,
        grid=(m // tm,),
        in_specs=[pl.BlockSpec((tm, d), lambda i: (i, 0)), pl.BlockSpec((1, d), lambda i: (0, 0))],
        out_specs=pl.BlockSpec((tm, d), lambda i: (i, 0)),
        out_shape=jax.ShapeDtypeStruct((m, d), out_dtype),
        compiler_params=_compiler_params(("parallel",)),
        name="rmsnorm_rows",
    )(x, g.reshape(1, d))


def _mm_kernel(a_ref, w_ref, o_ref, acc_ref, *, nk):
    k = pl.program_id(2)

    @pl.when(k == 0)
    def _():
        acc_ref[...] = jnp.zeros_like(acc_ref)

    acc_ref[...] += jnp.dot(a_ref[...], w_ref[...], preferred_element_type=jnp.float32)

    @pl.when(k == nk - 1)
    def _():
        o_ref[...] = acc_ref[...].astype(o_ref.dtype)


def _mm_res_kernel(a_ref, w_ref, r_ref, o_ref, acc_ref, *, nk):
    k = pl.program_id(2)

    @pl.when(k == 0)
    def _():
        acc_ref[...] = jnp.zeros_like(acc_ref)

    acc_ref[...] += jnp.dot(a_ref[...], w_ref[...], preferred_element_type=jnp.float32)

    @pl.when(k == nk - 1)
    def _():
        o_ref[...] = (r_ref[...] + acc_ref[...]).astype(o_ref.dtype)


def _layer_spec(arr, block, index_map, layer):
    if arr.ndim == 2:
        return pl.BlockSpec(block, index_map)
    return pl.BlockSpec((None,) + block, lambda *g: (layer,) + index_map(*g))


def _matmul(a, w, *, res=None, layer=None, tm, tn, tk, out_dtype=jnp.float32):
    m, kdim = a.shape
    n = w.shape[-1]
    nk = kdim // tk
    assert m % tm == 0 and n % tn == 0 and kdim % tk == 0
    in_specs = [pl.BlockSpec((tm, tk), lambda i, j, k: (i, k)), _layer_spec(w, (tk, tn), lambda i, j, k: (k, j), layer)]
    args = [a, w]
    if res is None:
        body = functools.partial(_mm_kernel, nk=nk)
    else:
        body = functools.partial(_mm_res_kernel, nk=nk)
        in_specs.append(pl.BlockSpec((tm, tn), lambda i, j, k: (i, j)))
        args.append(res)
    return pl.pallas_call(
        body,
        grid=(m // tm, n // tn, nk),
        in_specs=in_specs,
        out_specs=pl.BlockSpec((tm, tn), lambda i, j, k: (i, j)),
        out_shape=jax.ShapeDtypeStruct((m, n), out_dtype),
        scratch_shapes=[pltpu.VMEM((tm, tn), jnp.float32)],
        compiler_params=_compiler_params(("parallel", "parallel", "arbitrary")),
        name="matmul_res" if res is not None else "matmul",
    )(*args)


def _swiglu_kernel(a_ref, wg_ref, wu_ref, o_ref):
    a = a_ref[...]
    g = jnp.dot(a, wg_ref[...], preferred_element_type=jnp.float32)
    u = jnp.dot(a, wu_ref[...], preferred_element_type=jnp.float32)
    o_ref[...] = (g * jax.nn.sigmoid(g) * u).astype(o_ref.dtype)


def _swiglu(a, wg, wu, *, layer, tm, tn):
    m, kdim = a.shape
    n = wg.shape[-1]
    return pl.pallas_call(
        _swiglu_kernel,
        grid=(m // tm, n // tn),
        in_specs=[pl.BlockSpec((tm, kdim), lambda i, j: (i, 0)),
                  _layer_spec(wg, (kdim, tn), lambda i, j: (0, j), layer),
                  _layer_spec(wu, (kdim, tn), lambda i, j: (0, j), layer)],
        out_specs=pl.BlockSpec((tm, tn), lambda i, j: (i, j)),
        out_shape=jax.ShapeDtypeStruct((m, n), MXU_DTYPE),
        compiler_params=_compiler_params(("parallel", "parallel")),
        name="swiglu",
    )(a, wg, wu)


def _ple_kernel(a_ref, wg_ref, p_ref, wp_ref, r_ref, o_ref):
    gate = jax.nn.sigmoid(jnp.dot(a_ref[...], wg_ref[...], preferred_element_type=jnp.float32))
    emb = jnp.dot(p_ref[...], wp_ref[...], preferred_element_type=jnp.float32)
    o_ref[...] = r_ref[...] + gate * emb


def _ple(a, wg, p, wp, res, *, layer, tm, tn):
    m, kdim = a.shape
    n = wg.shape[-1]
    pdim = p.shape[-1]
    return pl.pallas_call(
        _ple_kernel,
        grid=(m // tm, n // tn),
        in_specs=[pl.BlockSpec((tm, kdim), lambda i, j: (i, 0)),
                  _layer_spec(wg, (kdim, tn), lambda i, j: (0, j), layer),
                  _layer_spec(p, (tm, pdim), lambda i, j: (i, 0), layer),
                  _layer_spec(wp, (pdim, tn), lambda i, j: (0, j), layer),
                  pl.BlockSpec((tm, tn), lambda i, j: (i, j))],
        out_specs=pl.BlockSpec((tm, tn), lambda i, j: (i, j)),
        out_shape=jax.ShapeDtypeStruct((m, n), jnp.float32),
        compiler_params=_compiler_params(("parallel", "parallel")),
        name="ple",
    )(a, wg, p, wp, res)


def _rope_tables(positions, dim):
    inv_freq = ROPE_THETA ** (-jnp.arange(0, dim, 2, dtype=jnp.float32) / dim)
    ang = positions.astype(jnp.float32)[..., None] * inv_freq
    return jnp.cos(ang), jnp.sin(ang)


_SEG_ORDER = ("a_q", "a_k", "a_v", "b_q", "b_iq", "c_q", "d_q", "d_kv", "c_kv", "b_k", "b_v")
_SEG_NAMES = ("a_q", "a_k", "a_v", "b_q", "b_k", "b_v", "b_iq", "b_ik", "b_iw", "c_q", "c_kv", "c_kr", "d_q", "d_kv", "d_g")
_ORIG_START = dict(zip(_SEG_NAMES, [int(o) for o in np.cumsum((0,) + IN_SIZES[:-1])]))
_ORIG_SIZE = dict(zip(_SEG_NAMES, IN_SIZES))
_GATES_PER_KV_GROUP = 3 * NSA_Q_PER_KV


def _build_in_layout():
    col = {}
    perm = []
    for name in _SEG_ORDER:
        col[name] = len(perm)
        perm.extend(range(_ORIG_START[name], _ORIG_START[name] + _ORIG_SIZE[name]))
    zero = IN_WIDTH
    col["narrow"] = len(perm)
    blk0 = list(range(_ORIG_START["b_ik"], _ORIG_START["b_ik"] + IDX_DIM))
    blk0 += list(range(_ORIG_START["d_g"], _ORIG_START["d_g"] + _GATES_PER_KV_GROUP))
    blk0 += list(range(_ORIG_START["b_iw"], _ORIG_START["b_iw"] + IDX_HEADS))
    blk0 += [zero] * (LANE - len(blk0))
    blk1 = list(range(_ORIG_START["c_kr"], _ORIG_START["c_kr"] + MLA_ROPE))
    blk1 += list(range(_ORIG_START["d_g"] + _GATES_PER_KV_GROUP, _ORIG_START["d_g"] + 2 * _GATES_PER_KV_GROUP))
    blk1 += [zero] * (LANE - len(blk1))
    perm.extend(blk0 + blk1)
    width = _round_up(len(perm), 512)
    perm.extend([zero] * (width - len(perm)))
    return col, np.asarray(perm, np.int32), width


_COL, _IN_PERM, IN_WIDTH_PAD = _build_in_layout()
_GATE_LANE0 = IDX_DIM
_IW_LANE0 = IDX_DIM + _GATES_PER_KV_GROUP


def _rope_lane_tables(cos, sin):
    return jnp.concatenate([cos, cos], axis=-1), jnp.concatenate([-sin, sin], axis=-1)


def _lane_tables(positions, dim):
    cc, ss = _rope_lane_tables(*_rope_tables(positions, dim))
    return tuple(jnp.tile(t.reshape(-1, dim), (1, LANE // dim)) for t in (cc, ss))


def _seg64_sum(y):
    r = lax.broadcasted_iota(jnp.int32, (LANE, LANE), 0) // 64
    c = lax.broadcasted_iota(jnp.int32, (LANE, LANE), 1) // 64
    bd = jnp.where(r == c, 1.0, 0.0).astype(MXU_DTYPE)
    hi = y.astype(MXU_DTYPE)
    lo = (y - hi.astype(jnp.float32)).astype(MXU_DTYPE)
    return (jnp.dot(hi, bd, preferred_element_type=jnp.float32)
            + jnp.dot(lo, bd, preferred_element_type=jnp.float32))


def _prep_kernel(x_ref, g_ref, cc_ref, ss_ref, o_ref, *, n_blocks, seg, norm, scale):
    cc, ss, g = cc_ref[...], ss_ref[...], g_ref[...]
    lane = lax.broadcasted_iota(jnp.int32, cc.shape, 1)
    for h in range(n_blocks):
        x = x_ref[:, h * LANE:(h + 1) * LANE]
        if norm:
            if seg == LANE:
                ms = jnp.mean(x * x, axis=-1, keepdims=True)
            else:
                ms = _seg64_sum(x * x) * (1.0 / seg)
            x = x * lax.rsqrt(ms + EPS) * g
        if seg == LANE:
            partner = pltpu.roll(x, LANE // 2, axis=1)
        else:
            partner = jnp.where((lane % seg) < seg // 2, pltpu.roll(x, LANE - seg // 2, axis=1),
                                pltpu.roll(x, seg // 2, axis=1))
        y = x * cc + partner * ss
        if scale != 1.0:
            y = y * scale
        o_ref[:, h * LANE:(h + 1) * LANE] = y.astype(o_ref.dtype)


def _prep_heads(proj, col0, n_blocks, gain, cc, ss, *, seg, norm, scale=1.0, tm=256):
    m = proj.shape[0]
    tm = min(tm, m)
    width = LANE * n_blocks
    cb = col0 // width
    assert cb * width == col0
    return pl.pallas_call(
        functools.partial(_prep_kernel, n_blocks=n_blocks, seg=seg, norm=norm, scale=scale),
        grid=(m // tm,),
        in_specs=[pl.BlockSpec((tm, width), lambda i: (i, cb)),
                  pl.BlockSpec((1, LANE), lambda i: (0, 0)),
                  pl.BlockSpec((tm, LANE), lambda i: (i, 0)),
                  pl.BlockSpec((tm, LANE), lambda i: (i, 0))],
        out_specs=pl.BlockSpec((tm, width), lambda i: (i, 0)),
        out_shape=jax.ShapeDtypeStruct((m, width), MXU_DTYPE),
        compiler_params=_compiler_params(("parallel",)),
        name="prep_heads",
    )(proj, gain, cc, ss)


def _dot_nt(a, b):
    return lax.dot_general(a, b, (((1,), (1,)), ((), ())), preferred_element_type=jnp.float32)


LOG2E = math.log2(math.e)


def _softmax_step(s, mask, m, l, acc, v, *, rows_may_be_empty=False):
    if mask is not None:
        s = jnp.where(mask, s, NEG_INF)
    m_new = jnp.maximum(m, jnp.max(s, axis=-1, keepdims=True))
    alpha = jnp.exp2(m - m_new)
    p = jnp.exp2(s - m_new)
    if rows_may_be_empty:
        p = jnp.where(mask, p, 0.0)
    l = alpha * l + jnp.sum(p, axis=-1, keepdims=True)
    acc = alpha * acc + jnp.dot(p.astype(MXU_DTYPE), v, preferred_element_type=jnp.float32)
    return m_new, l, acc


def _softmax_finish(l, acc):
    return acc * (1.0 / jnp.maximum(l, 1e-30))


def _nsa_cmp_kernel(x_ref, pe_ref, w1_ref, w2_ref, gk_ref, cc_ref, ss_ref, kc_ref, vc_ref):
    nch = x_ref.shape[2]
    for kv in range(2):
        x = x_ref[kv, 0]
        lo = jnp.dot((x + pe_ref[kv, 0:1, :]).astype(MXU_DTYPE), w1_ref[kv, 0], preferred_element_type=jnp.float32)
        hi = jnp.dot((x + pe_ref[kv, 1:2, :]).astype(MXU_DTYPE), w1_ref[kv, 1], preferred_element_type=jnp.float32)
        hid = lo + pltpu.roll(hi, nch - 1, axis=0)
        comp = jnp.dot(jax.nn.gelu(hid).astype(MXU_DTYPE), w2_ref[kv], preferred_element_type=jnp.float32)
        if kv == 0:
            ms = jnp.mean(comp * comp, axis=-1, keepdims=True)
            y = comp * lax.rsqrt(ms + EPS) * gk_ref[...]
            y = y * cc_ref[0] + pltpu.roll(y, LANE // 2, axis=1) * ss_ref[0]
            kc_ref[0] = y.astype(kc_ref.dtype)
        else:
            vc_ref[0] = comp.astype(vc_ref.dtype)


def _nsa_compress(craw, pe2, w1, w2, gk, cc_c, ss_c, *, n_batch):
    _, gb, nch, width = craw.shape
    return pl.pallas_call(
        _nsa_cmp_kernel,
        grid=(gb,),
        in_specs=[pl.BlockSpec((2, 1, nch, width), lambda n: (0, n, 0, 0)),
                  pl.BlockSpec((2, 2, width), lambda n: (0, 0, 0)),
                  pl.BlockSpec((2, 2, width, CMP_HIDDEN), lambda n: (0, 0, 0, 0)),
                  pl.BlockSpec((2, CMP_HIDDEN, HEAD_DIM), lambda n: (0, 0, 0)),
                  pl.BlockSpec((1, HEAD_DIM), lambda n: (0, 0)),
                  pl.BlockSpec((1, nch, HEAD_DIM), lambda n: (n % n_batch, 0, 0)),
                  pl.BlockSpec((1, nch, HEAD_DIM), lambda n: (n % n_batch, 0, 0))],
        out_specs=[pl.BlockSpec((1, nch, HEAD_DIM), lambda n: (n, 0, 0)),
                   pl.BlockSpec((1, nch, HEAD_DIM), lambda n: (n, 0, 0))],
        out_shape=[jax.ShapeDtypeStruct((gb, nch, HEAD_DIM), MXU_DTYPE)] * 2,
        compiler_params=_compiler_params(("parallel",)),
        name="nsa_compress",
    )(craw, pe2, w1, w2, gk, cc_c, ss_c)


def _nsa_attn_kernel(q_ref, kc_ref, vc_ref, ks_ref, vs_ref, kw_ref, vw_ref, gate_ref, c2s_ref, exp_ref, o_ref, *,
                     tq, tk, n_sel, n_top):
    i = pl.program_id(2)
    R = NSA_Q_PER_KV
    rows = R * tq
    nch = kc_ref.shape[1]
    bf16 = MXU_DTYPE
    q = jnp.concatenate([q_ref[:, r * HEAD_DIM:(r + 1) * HEAD_DIM] for r in range(R)], axis=0)
    t_row = i * tq + lax.broadcasted_iota(jnp.int32, (rows, 1), 0) % tq

    s = _dot_nt(q, kc_ref[0])
    blk_end = lax.broadcasted_iota(jnp.int32, (1, nch), 1) * CMP_STRIDE + (CMP_LEN - 1)
    mask = blk_end <= t_row
    zero = jnp.zeros((rows, 1), jnp.float32)
    s = jnp.where(mask, s, NEG_INF)
    p_cmp = jnp.where(mask, jnp.exp2(s - jnp.max(s, axis=-1, keepdims=True)), 0.0)
    p_cmp = p_cmp * (1.0 / jnp.maximum(jnp.sum(p_cmp, axis=-1, keepdims=True), 1e-30))
    o_cmp = jnp.dot(p_cmp.astype(bf16), vc_ref[0], preferred_element_type=jnp.float32)

    imp_rows = jnp.dot(p_cmp.astype(bf16), c2s_ref[...], preferred_element_type=jnp.float32)
    imp = imp_rows[0:tq]
    for r in range(1, R):
        imp = imp + imp_rows[r * tq:(r + 1) * tq]
    imp_t = imp.T[0:n_sel]
    blk = lax.broadcasted_iota(jnp.int32, (n_sel, tq), 0)
    t_lane = i * tq + lax.broadcasted_iota(jnp.int32, (n_sel, tq), 1)
    dist = lax.shift_right_logical(t_lane, int(math.log2(SEL_BLOCK))) - blk
    forced = (blk < SEL_INIT) | ((dist >= 0) & (dist < SEL_LOCAL))
    val = jnp.where(forced, POS_INF, jnp.where(blk * SEL_BLOCK <= t_lane, imp_t, NEG_INF))
    rank = jnp.zeros((n_sel, tq), jnp.float32)
    for mblk in range(n_sel):
        vm = val[mblk:mblk + 1, :]
        rank = rank + jnp.where((vm > val) | ((vm == val) & (blk > mblk)), 1.0, 0.0)
    drop_t = jnp.where(rank < n_top, 0.0, 1.0)
    drop_t = jnp.concatenate([drop_t, jnp.zeros((LANE - n_sel, tq), jnp.float32)], axis=0) if n_sel < LANE else drop_t
    drop_q = drop_t.T.astype(bf16)
    drop_rows = jnp.concatenate([drop_q] * R, axis=0)

    def slc_step(j, carry, diagonal):
        off = pl.multiple_of(j * tk, tk)
        k = ks_ref[pl.ds(off, tk), :]
        v = vs_ref[pl.ds(off, tk), :].astype(bf16)
        s = _dot_nt(q, k) + jnp.dot(drop_rows, exp_ref[j], preferred_element_type=jnp.float32)
        mask = (off + lax.broadcasted_iota(jnp.int32, (1, tk), 1) <= t_row) if diagonal else None
        return _softmax_step(s, mask, *carry, v)

    init = (zero + NEG_INF, zero, jnp.zeros((rows, HEAD_DIM), jnp.float32))
    n_full = (i * tq) // tk
    carry = lax.fori_loop(0, n_full, lambda j, c: slc_step(j, c, False), init)
    _, l, acc = slc_step(n_full, carry, True)
    o_slc = _softmax_finish(l, acc)

    span = WINDOW + tq
    off = pl.multiple_of(jnp.maximum(i * tq - WINDOW, 0), tq)
    key = off + lax.broadcasted_iota(jnp.int32, (1, span), 1)
    s = _dot_nt(q, kw_ref[pl.ds(off, span), :])
    s = jnp.where(key <= t_row, jnp.where(key > t_row - WINDOW, s, NEG_INF), NEG_INF)
    _, l, acc = _softmax_step(s, None, *init, vw_ref[pl.ds(off, span), :].astype(bf16))
    o_win = _softmax_finish(l, acc)

    gate = jax.nn.sigmoid(gate_ref[...])
    for r in range(R):
        c = _GATE_LANE0 + 3 * r
        rs = slice(r * tq, (r + 1) * tq)
        o_ref[:, r * HEAD_DIM:(r + 1) * HEAD_DIM] = (gate[:, c:c + 1] * o_cmp[rs]
                                                     + gate[:, c + 1:c + 2] * o_slc[rs]
                                                     + gate[:, c + 2:c + 3] * o_win[rs])


def _nsa_mixer(proj, B, S, cc_h, ss_h, positions, g_q, g_k, cmp_w1, cmp_w2, cmp_pe):
    M = B * S
    G, R, d = NSA_KV_HEADS, NSA_Q_PER_KV, HEAD_DIM
    bf16 = MXU_DTYPE
    tq = Q_BLOCK
    nq = S // tq
    nch = S // CMP_STRIDE
    n_sel = S // SEL_BLOCK
    n_top = min(SEL_TOPN, n_sel)
    kv0 = _COL["d_kv"]

    qn = _prep_heads(proj, _COL["d_q"], HEADS_PER_GROUP, g_q.reshape(1, d), cc_h, ss_h,
                     seg=LANE, norm=True, scale=d ** -0.5 * LOG2E)
    k_slc =_prep_heads(proj, kv0 + (1 * 2 + 0) * G * d, G, g_k[1].reshape(1, d), cc_h, ss_h, seg=LANE, norm=True)
    k_win = _prep_heads(proj, kv0 + (2 * 2 + 0) * G * d, G, g_k[2].reshape(1, d), cc_h, ss_h, seg=LANE, norm=True)

    craw = proj[:, kv0:kv0 + 2 * G * d].reshape(B, S, 2, G, d).transpose(2, 3, 0, 1, 4)
    craw = craw.reshape(2, G * B, nch, CMP_STRIDE * d)
    ends = np.minimum(np.arange(nch) * CMP_STRIDE + CMP_LEN - 1, S - 1)
    cc_c, ss_c = _rope_lane_tables(*_rope_tables(positions[:, ends], d))
    pe2 = cmp_pe.reshape(2, 2, CMP_STRIDE * d)
    w1 = cmp_w1.astype(bf16).reshape(2, 2, CMP_STRIDE * d, CMP_HIDDEN)
    k_cmp, v_cmp = _nsa_compress(craw, pe2, w1, cmp_w2.astype(bf16), g_k[0].reshape(1, d), cc_c, ss_c, n_batch=B)

    starts = np.arange(nch) * CMP_STRIDE
    sel_start = np.arange(LANE) * SEL_BLOCK
    c2s = ((starts[:, None] < sel_start[None, :] + SEL_BLOCK) & (starts[:, None] + CMP_LEN > sel_start[None, :])
           & (np.arange(nch)[:, None] < nch - 1) & (np.arange(LANE)[None, :] < n_sel))
    tk = min(512, S)
    expand = (np.arange(LANE)[None, :, None] == (np.arange(S // tk)[:, None, None] * tk + np.arange(tk)[None, None, :]) // SEL_BLOCK)
    c2s = jnp.asarray(c2s, bf16)
    expand = jnp.asarray(expand * MASK_BIAS, bf16)

    narrow = _COL["narrow"] // LANE
    col_vs = (kv0 + (1 * 2 + 1) * G * d) // LANE
    col_vw = (kv0 + (2 * 2 + 1) * G * d) // LANE
    return pl.pallas_call(
        functools.partial(_nsa_attn_kernel, tq=tq, tk=tk, n_sel=n_sel, n_top=n_top),
        grid=(B, G, nq),
        in_specs=[pl.BlockSpec((tq, R * d), lambda b, g, i: (b * nq + i, g)),
                  pl.BlockSpec((1, nch, d), lambda b, g, i: (g * B + b, 0, 0)),
                  pl.BlockSpec((1, nch, d), lambda b, g, i: (g * B + b, 0, 0)),
                  pl.BlockSpec((S, d), lambda b, g, i: (b, g)),
                  pl.BlockSpec((S, d), lambda b, g, i: (b, col_vs + g)),
                  pl.BlockSpec((S, d), lambda b, g, i: (b, g)),
                  pl.BlockSpec((S, d), lambda b, g, i: (b, col_vw + g)),
                  pl.BlockSpec((tq, LANE), lambda b, g, i: (b * nq + i, narrow + g)),
                  pl.BlockSpec((nch, LANE), lambda b, g, i: (0, 0)),
                  pl.BlockSpec((S // tk, LANE, tk), lambda b, g, i: (0, 0, 0))],
        out_specs=pl.BlockSpec((tq, R * d), lambda b, g, i: (b * nq + i, g)),
        out_shape=jax.ShapeDtypeStruct((M, GROUP_WIDTH), jnp.float32),
        compiler_params=_compiler_params(("parallel", "parallel", "arbitrary")),
        name="nsa_attention",
    )(qn, k_cmp, v_cmp, k_slc, proj, k_win, proj, proj, c2s, expand)


_INT_MIN = -2 ** 31


def _sortable_key(x):
    b = lax.bitcast_convert_type(x + 0.0, jnp.int32)
    return jnp.where(b >= 0, b, b ^ 0x7FFFFFFF)


def _dsa_kernel(q_ref, k_ref, v_ref, iq_ref, ik_ref, nar_ref, o_ref, key_ref, msk_ref, w_ref, *,
                tq, tk, n_keep, idx_bits):
    i = pl.program_id(1)
    H = HEADS_PER_GROUP
    bf16 = MXU_DTYPE
    t_col = i * tq + lax.broadcasted_iota(jnp.int32, (tq, 1), 0)
    key_lane = lax.broadcasted_iota(jnp.int32, (1, tq), 1)
    lane = lax.broadcasted_iota(jnp.int32, (tq, LANE), 1)
    iw = nar_ref[...] * (IDX_HEADS ** -0.5)
    for hd in range(IDX_HEADS):
        w_ref[hd] = jnp.broadcast_to(iw[:, _IW_LANE0 + hd:_IW_LANE0 + hd + 1], (tq, tq))
    iq_rows = jnp.concatenate([iq_ref[:, pr * LANE:(pr + 1) * LANE] for pr in range(IDX_HEADS // 2)], axis=0)

    def score_body(j, _):
        off = pl.multiple_of(j * tq, tq)
        ik = ik_ref[pl.ds(off, tq), :]
        ik_lo = jnp.where(lane < IDX_DIM, ik, jnp.zeros_like(ik))
        ik_hi = jnp.where(lane < IDX_DIM, jnp.zeros_like(ik), pltpu.roll(ik.astype(jnp.float32), IDX_DIM, axis=1).astype(bf16))
        rel = (jnp.maximum(_dot_nt(iq_rows, ik_lo), 0.0), jnp.maximum(_dot_nt(iq_rows, ik_hi), 0.0))
        sc = jnp.zeros((tq, tq), jnp.float32)
        for hd in range(IDX_HEADS):
            sc = sc + w_ref[hd] * rel[hd % 2][(hd // 2) * tq:(hd // 2 + 1) * tq]
        sc = jnp.where(off + key_lane <= t_col, sc, NEG_INF)
        key_ref[j] = _sortable_key(sc)
        return 0

    lax.fori_loop(0, i + 1, score_body, 0)

    def count(pred):
        def body(j, part):
            return part + jnp.where(pred(key_ref[j], j * tq), 1.0, 0.0)
        part = lax.fori_loop(0, i + 1, body, jnp.zeros((tq, tq), jnp.float32))
        return jnp.sum(part, axis=-1, keepdims=True)

    c0 = count(lambda kk, off: kk >= 0)
    thr = jnp.where(c0 >= n_keep, 0, _INT_MIN).astype(jnp.int32)

    def thr_body(it, thr):
        cand = thr | lax.shift_left(jnp.int32(1), 30 - it)
        c = count(lambda kk, off: kk >= cand)
        return jnp.where(c >= n_keep, cand, thr)

    thr = lax.fori_loop(0, 31, thr_body, thr)

    n_ge = count(lambda kk, off: kk >= thr)

    def cut_search():
        need = n_keep - count(lambda kk, off: kk > thr)

        def cut_body(it, cut):
            cand = cut | lax.shift_left(jnp.int32(1), idx_bits - 1 - it)
            c = count(lambda kk, off: (kk == thr) & (off + key_lane < cand))
            return jnp.where(c < need, cand, cut)

        return lax.fori_loop(0, idx_bits, cut_body, jnp.zeros((tq, 1), jnp.int32))

    cut = lax.cond(jnp.max(n_ge) > n_keep, cut_search, lambda: jnp.full((tq, 1), 2 ** idx_bits - 1, jnp.int32))

    tpc = tk // tq
    n_chunks = (i + tpc) // tpc

    def mask_body(j, _):
        @pl.when(j <= i)
        def _():
            kk = key_ref[j]
            kidx = j * tq + key_lane
            sel = (kk > thr) | ((kk == thr) & (kidx <= cut))
            msk_ref[j] = jnp.where(kidx <= t_col, jnp.where(sel, 0.0, NEG_INF), NEG_INF)

        @pl.when(j > i)
        def _():
            msk_ref[j] = jnp.full((tq, tq), NEG_INF, jnp.float32)

        return 0

    lax.fori_loop(0, n_chunks * tpc, mask_body, 0)

    hpp = 4
    rows = hpp * tq
    zero = jnp.zeros((rows, 1), jnp.float32)
    qs = [jnp.concatenate([q_ref[:, (hc * hpp + r) * HEAD_DIM:(hc * hpp + r + 1) * HEAD_DIM] for r in range(hpp)], axis=0)
          for hc in range(H // hpp)]

    def att_body(j, carry):
        off = pl.multiple_of(j * tk, tk)
        k = k_ref[pl.ds(off, tk), :]
        v = v_ref[pl.ds(off, tk), :].astype(bf16)
        bias1 = jnp.concatenate([msk_ref[j * tpc + u] for u in range(tpc)], axis=1)
        bias = jnp.concatenate([bias1] * hpp, axis=0)
        return tuple(_softmax_step(_dot_nt(q, k) + bias, None, *carry[c], v) for c, q in enumerate(qs))

    init = tuple((zero + NEG_INF, zero, jnp.zeros((rows, HEAD_DIM), jnp.float32)) for _ in qs)
    res = lax.fori_loop(0, n_chunks, att_body, init)
    for hc, (_, l, acc) in enumerate(res):
        o = _softmax_finish(l, acc)
        for r in range(hpp):
            hh = hc * hpp + r
            o_ref[:, hh * HEAD_DIM:(hh + 1) * HEAD_DIM] = o[r * tq:(r + 1) * tq]


def _dsa_mixer(proj, B, S, cc_h, ss_h, cc_i, ss_i, g_q, g_k):
    M = B * S
    d = HEAD_DIM
    tq = Q_BLOCK
    nq = S // tq
    n_keep = min(DSA_TOPK, S // 4)
    qn = _prep_heads(proj, _COL["b_q"], HEADS_PER_GROUP, g_q.reshape(1, d), cc_h, ss_h, seg=LANE, norm=True,
                     scale=d ** -0.5 * LOG2E)
    kn =_prep_heads(proj, _COL["b_k"], 1, g_k.reshape(1, d), cc_h, ss_h, seg=LANE, norm=True)
    ones = jnp.ones((1, LANE), jnp.float32)
    iqn = _prep_heads(proj, _COL["b_iq"], IDX_HEADS * IDX_DIM // LANE, ones, cc_i, ss_i, seg=IDX_DIM, norm=False,
                      scale=IDX_DIM ** -0.5)
    ikn = _prep_heads(proj, _COL["narrow"], 1, ones, cc_i, ss_i, seg=IDX_DIM, norm=False)
    narrow = _COL["narrow"] // LANE
    col_v = _COL["b_v"] // LANE
    return pl.pallas_call(
        functools.partial(_dsa_kernel, tq=tq, tk=min(512, S), n_keep=n_keep, idx_bits=int(math.log2(S))),
        grid=(B, nq),
        in_specs=[pl.BlockSpec((tq, GROUP_WIDTH), lambda b, i: (b * nq + i, 0)),
                  pl.BlockSpec((S, d), lambda b, i: (b, 0)),
                  pl.BlockSpec((S, d), lambda b, i: (b, col_v)),
                  pl.BlockSpec((tq, IDX_HEADS * IDX_DIM), lambda b, i: (b * nq + i, 0)),
                  pl.BlockSpec((S, LANE), lambda b, i: (b, 0)),
                  pl.BlockSpec((tq, LANE), lambda b, i: (b * nq + i, narrow))],
        out_specs=pl.BlockSpec((tq, GROUP_WIDTH), lambda b, i: (b * nq + i, 0)),
        out_shape=jax.ShapeDtypeStruct((M, GROUP_WIDTH), jnp.float32),
        scratch_shapes=[pltpu.VMEM((nq, tq, tq), jnp.int32), pltpu.VMEM((nq, tq, tq), jnp.float32),
                        pltpu.VMEM((IDX_HEADS, tq, tq), jnp.float32)],
        compiler_params=_compiler_params(("parallel", "arbitrary")),
        name="dsa_attention",
    )(qn, kn, proj, iqn, ikn, proj)


def _causal_chains(qs, k_ref, v_ref, dk, dv, t_row, t0, tk):
    rows = qs[0].shape[0]
    zero = jnp.zeros((rows, 1), jnp.float32)

    def step(j, carry, masked):
        off = pl.multiple_of(j * tk, tk)
        mask = (off + lax.broadcasted_iota(jnp.int32, (1, tk), 1) <= t_row) if masked else None
        out = []
        for c, q in enumerate(qs):
            k = k_ref[pl.ds(off, tk), c * dk:(c + 1) * dk]
            v = v_ref[pl.ds(off, tk), c * dv:(c + 1) * dv].astype(MXU_DTYPE)
            out.append(_softmax_step(_dot_nt(q, k), mask, *carry[c], v))
        return tuple(out)

    init = tuple((zero + NEG_INF, zero, jnp.zeros((rows, dv), jnp.float32)) for _ in qs)
    n_full = t0 // tk
    res = lax.fori_loop(0, n_full, lambda j, c: step(j, c, False), init)
    res = step(n_full, res, True)
    return [_softmax_finish(l, acc) for _, l, acc in res]


def _diff_attn_kernel(q_ref, k_ref, v_ref, lam_ref, g_ref, o_ref, *, tq, tk, hb, out_scale):
    i = pl.program_id(2)
    d = HEAD_DIM
    lane = lax.broadcasted_iota(jnp.int32, (tq, d), 1)
    qs = []
    for c in range(hb):
        q = q_ref[:, c * d:(c + 1) * d]
        zeros = jnp.zeros_like(q)
        qs.append(jnp.concatenate([jnp.where(lane < DIFF_QK_DIM, q, zeros), jnp.where(lane < DIFF_QK_DIM, zeros, q)], axis=0))
    t_row = i * tq + lax.broadcasted_iota(jnp.int32, (2 * tq, 1), 0) % tq
    outs = _causal_chains(qs, k_ref, v_ref, d, d, t_row, i * tq, tk)
    for c, o2 in enumerate(outs):
        o = o2[0:tq] - lam_ref[...] * o2[tq:2 * tq]
        ms = jnp.mean(o * o, axis=-1, keepdims=True)
        o_ref[:, c * d:(c + 1) * d] = o * lax.rsqrt(ms + EPS) * g_ref[...] * out_scale


def _diff_mixer(proj, B, S, cc_d, ss_d, g_q, g_k, g_sub, lam_params, layer_idx, *, tq=128, tk=512, hb=4):
    M = B * S
    d = HEAD_DIM
    nq = S // tq
    tk = min(tk, S)
    reps = LANE // DIFF_QK_DIM
    qn = _prep_heads(proj, _COL["a_q"], HEADS_PER_GROUP, jnp.tile(g_q, reps).reshape(1, d), cc_d, ss_d,
                     seg=DIFF_QK_DIM, norm=True, scale=DIFF_QK_DIM ** -0.5 * LOG2E)
    kn = _prep_heads(proj, _COL["a_k"], HEADS_PER_GROUP, jnp.tile(g_k, reps).reshape(1, d), cc_d, ss_d,
                     seg=DIFF_QK_DIM, norm=True)
    lam_init = 0.8 - 0.6 * math.exp(-0.3 * layer_idx)
    lp = lam_params.astype(jnp.float32)
    lam = jnp.exp(jnp.sum(lp[0] * lp[1])) - jnp.exp(jnp.sum(lp[2] * lp[3])) + lam_init
    col_v = _COL["a_v"] // (hb * d)
    return pl.pallas_call(
        functools.partial(_diff_attn_kernel, tq=tq, tk=tk, hb=hb, out_scale=1.0 - lam_init),
        grid=(B, HEADS_PER_GROUP // hb, nq),
        in_specs=[pl.BlockSpec((tq, hb * d), lambda b, h, i: (b * nq + i, h)),
                  pl.BlockSpec((S, hb * d), lambda b, h, i: (b, h)),
                  pl.BlockSpec((S, hb * d), lambda b, h, i: (b, col_v + h)),
                  pl.BlockSpec((1, d), lambda b, h, i: (0, 0)),
                  pl.BlockSpec((1, d), lambda b, h, i: (0, 0))],
        out_specs=pl.BlockSpec((tq, hb * d), lambda b, h, i: (b * nq + i, h)),
        out_shape=jax.ShapeDtypeStruct((M, GROUP_WIDTH), jnp.float32),
        compiler_params=_compiler_params(("parallel", "parallel", "arbitrary")),
        name="diff_attention",
    )(qn, kn, proj, jnp.full((1, d), lam, jnp.float32), g_sub.reshape(1, d))


MLA_QK_PAD = 2 * LANE


def _rmsnorm_cols_kernel(x_ref, g_ref, o_ref):
    x = x_ref[...]
    ms = jnp.mean(x * x, axis=-1, keepdims=True)
    o_ref[...] = (x * lax.rsqrt(ms + EPS) * g_ref[...]).astype(o_ref.dtype)


def _rmsnorm_cols(proj, col0, width, g, *, tm=512):
    m = proj.shape[0]
    tm = min(tm, m)
    cb = col0 // width
    assert cb * width == col0
    return pl.pallas_call(
        _rmsnorm_cols_kernel,
        grid=(m // tm,),
        in_specs=[pl.BlockSpec((tm, width), lambda i: (i, cb)), pl.BlockSpec((1, width), lambda i: (0, 0))],
        out_specs=pl.BlockSpec((tm, width), lambda i: (i, 0)),
        out_shape=jax.ShapeDtypeStruct((m, width), MXU_DTYPE),
        compiler_params=_compiler_params(("parallel",)),
        name="rmsnorm_cols",
    )(proj, g.reshape(1, width))


def _mla_prep_kernel(a_ref, b_ref, g_ref, cc_ref, ss_ref, o_ref, *, a_stride, b_col0, b_stride, scale):
    cc, ss = cc_ref[...], ss_ref[...]
    lane = lax.broadcasted_iota(jnp.int32, cc.shape, 1)
    for h in range(HEADS_PER_GROUP):
        a = a_ref[:, h * a_stride:h * a_stride + LANE]
        b = jnp.where(lane < MLA_ROPE, b_ref[:, b_col0 + h * b_stride:b_col0 + h * b_stride + LANE], 0.0)
        ms = (jnp.sum(a * a, axis=-1, keepdims=True) + jnp.sum(b * b, axis=-1, keepdims=True)) * (1.0 / (MLA_NOPE + MLA_ROPE))
        r = lax.rsqrt(ms + EPS)
        ya = a * r * g_ref[:, 0:LANE]
        yb = b * r * g_ref[:, LANE:2 * LANE]
        partner = jnp.where((lane % MLA_ROPE) < MLA_ROPE // 2, pltpu.roll(yb, LANE - MLA_ROPE // 2, axis=1),
                            pltpu.roll(yb, MLA_ROPE // 2, axis=1))
        yb = yb * cc + partner * ss
        o_ref[:, h * MLA_QK_PAD:h * MLA_QK_PAD + LANE] = (ya * scale).astype(o_ref.dtype)
        o_ref[:, h * MLA_QK_PAD + LANE:(h + 1) * MLA_QK_PAD] = (yb * scale).astype(o_ref.dtype)


def _mla_prep(a_arr, a_width, a_stride, b_arr, b_block, b_width, b_col0, b_stride, gain, cc, ss, *, scale=1.0, tm=256):
    m = a_arr.shape[0]
    tm = min(tm, m)
    g2 = jnp.concatenate([gain, jnp.zeros((MLA_QK_PAD - gain.shape[0],), gain.dtype)]).reshape(1, MLA_QK_PAD)
    return pl.pallas_call(
        functools.partial(_mla_prep_kernel, a_stride=a_stride, b_col0=b_col0, b_stride=b_stride, scale=scale),
        grid=(m // tm,),
        in_specs=[pl.BlockSpec((tm, a_width), lambda i: (i, 0)),
                  pl.BlockSpec((tm, b_width), lambda i: (i, b_block)),
                  pl.BlockSpec((1, MLA_QK_PAD), lambda i: (0, 0)),
                  pl.BlockSpec((tm, LANE), lambda i: (i, 0)),
                  pl.BlockSpec((tm, LANE), lambda i: (i, 0))],
        out_specs=pl.BlockSpec((tm, HEADS_PER_GROUP * MLA_QK_PAD), lambda i: (i, 0)),
        out_shape=jax.ShapeDtypeStruct((m, HEADS_PER_GROUP * MLA_QK_PAD), MXU_DTYPE),
        compiler_params=_compiler_params(("parallel",)),
        name="mla_prep",
    )(a_arr, b_arr, g2, cc, ss)


def _mla_attn_kernel(q_ref, k_ref, v_ref, o_ref, *, tq, tk, hb):
    i = pl.program_id(2)
    d = HEAD_DIM
    t_row = i * tq + lax.broadcasted_iota(jnp.int32, (tq, 1), 0)
    qs = [q_ref[:, c * MLA_QK_PAD:(c + 1) * MLA_QK_PAD] for c in range(hb)]
    outs = _causal_chains(qs, k_ref, v_ref, MLA_QK_PAD, d, t_row, i * tq, tk)
    for c, o in enumerate(outs):
        o_ref[:, c * d:(c + 1) * d] = o


def _mla_mixer(proj, B, S, cc_m, ss_m, g_cq, g_ckv, w_uq, w_uk, w_uv, g_q, g_k, *, tq=256, tk=512, hb=4):
    M = B * S
    H, d = HEADS_PER_GROUP, HEAD_DIM
    tq = min(tq, S)
    tk = min(tk, S)
    nq = S // tq
    dqk = MLA_NOPE + MLA_ROPE
    w_q = jnp.pad(w_uq.astype(MXU_DTYPE).reshape(MLA_Q_RANK, H, dqk), ((0, 0), (0, 0), (0, MLA_QK_PAD - dqk)))
    w_q = w_q.reshape(MLA_Q_RANK, H * MLA_QK_PAD)
    w_kv = jnp.concatenate([w_uk, w_uv], axis=1).astype(MXU_DTYPE)
    cq = _rmsnorm_cols(proj, _COL["c_q"], MLA_Q_RANK, g_cq)
    ckv = _rmsnorm_cols(proj, _COL["c_kv"], MLA_KV_RANK, g_ckv)
    q_up = _matmul(cq, w_q, tm=min(1024, M), tn=1024, tk=MLA_Q_RANK)
    kv_up = _matmul(ckv, w_kv, tm=min(1024, M), tn=1024, tk=MLA_KV_RANK)
    qn = _mla_prep(q_up, H * MLA_QK_PAD, MLA_QK_PAD, q_up, 0, H * MLA_QK_PAD, LANE, MLA_QK_PAD, g_q, cc_m, ss_m,
                   scale=dqk ** -0.5 * LOG2E)
    kn = _mla_prep(kv_up, H * MLA_NOPE, LANE, proj, _COL["narrow"] // LANE + 1, LANE, 0, 0, g_k, cc_m, ss_m)
    return pl.pallas_call(
        functools.partial(_mla_attn_kernel, tq=tq, tk=tk, hb=hb),
        grid=(B, H // hb, nq),
        in_specs=[pl.BlockSpec((tq, hb * MLA_QK_PAD), lambda b, h, i: (b * nq + i, h)),
                  pl.BlockSpec((S, hb * MLA_QK_PAD), lambda b, h, i: (b, h)),
                  pl.BlockSpec((S, hb * d), lambda b, h, i: (b, H // hb + h))],
        out_specs=pl.BlockSpec((tq, hb * d), lambda b, h, i: (b * nq + i, h)),
        out_shape=jax.ShapeDtypeStruct((M, GROUP_WIDTH), jnp.float32),
        compiler_params=_compiler_params(("parallel", "parallel", "arbitrary")),
        name="mla_attention",
    )(qn, kn, kv_up)


def _mix_kernel(a_ref, b_ref, c_ref, d_ref, g_ref, o_ref):
    o_ref[:, 0:GROUP_WIDTH] = a_ref[...].astype(o_ref.dtype)
    for n, ref in enumerate((b_ref, c_ref, d_ref)):
        x = ref[...]
        ms = jnp.mean(x * x, axis=-1, keepdims=True)
        o_ref[:, (n + 1) * GROUP_WIDTH:(n + 2) * GROUP_WIDTH] = (
            x * lax.rsqrt(ms + EPS) * g_ref[n:n + 1, :]).astype(o_ref.dtype)


def _mix_groups(o_a, o_b, o_c, o_d, g_out, *, tm=512):
    m = o_a.shape[0]
    tm = min(tm, m)
    spec = pl.BlockSpec((tm, GROUP_WIDTH), lambda i: (i, 0))
    return pl.pallas_call(
        _mix_kernel,
        grid=(m // tm,),
        in_specs=[spec, spec, spec, spec, pl.BlockSpec((3, GROUP_WIDTH), lambda i: (0, 0))],
        out_specs=pl.BlockSpec((tm, MIX_WIDTH), lambda i: (i, 0)),
        out_shape=jax.ShapeDtypeStruct((m, MIX_WIDTH), MXU_DTYPE),
        compiler_params=_compiler_params(("parallel",)),
        name="mix_groups",
    )(o_a, o_b, o_c, o_d, g_out)


def kernel(x, p, positions, w_in, w_out, g_mix, g_ffn, w_gate, w_up, w_down, w_ple_proj, w_ple_gate, g_ple,
           g_group_out, diff_g_q, diff_g_k, diff_g_sub, diff_lambda, dsa_g_q, dsa_g_k, mla_g_cq, mla_g_ckv,
           mla_w_uq, mla_w_uk, mla_w_uv, mla_g_q, mla_g_k, nsa_g_q, nsa_g_k, nsa_cmp_w1, nsa_cmp_w2, nsa_cmp_pe):
    B, S = x.shape[:2]
    M = B * S
    H = HEADS_PER_GROUP
    bf16 = MXU_DTYPE
    cc_h, ss_h = _lane_tables(positions, HEAD_DIM)
    cc_i, ss_i = _lane_tables(positions, IDX_DIM)
    cc_d, ss_d = _lane_tables(positions, DIFF_QK_DIM)
    cc_m, ss_m = _lane_tables(positions, MLA_ROPE)
    w_in_p = jnp.take(jnp.pad(w_in.astype(bf16), ((0, 0), (0, 0), (0, 1))), _IN_PERM, axis=2)
    w_out_b, w_gate_b, w_up_b, w_down_b = (w.astype(bf16) for w in (w_out, w_gate, w_up, w_down))
    w_pg_b, w_pp_b = w_ple_gate.astype(bf16), w_ple_proj.astype(bf16)
    p_b = p.reshape(DEPTH, M, PLE_DIM).astype(bf16)
    h = x.reshape(M, D_MODEL)
    for i in range(DEPTH):
        u = _rmsnorm_rows(h, g_mix[i])
        proj = _matmul(u, w_in_p, layer=i, tm=1024, tn=512, tk=D_MODEL)
        o_a = _diff_mixer(proj, B, S, cc_d, ss_d, diff_g_q[i], diff_g_k[i], diff_g_sub[i], diff_lambda[i], i)
        o_b = _dsa_mixer(proj, B, S, cc_h, ss_h, cc_i, ss_i, dsa_g_q[i], dsa_g_k[i])
        o_c = _mla_mixer(proj, B, S, cc_m, ss_m, mla_g_cq[i], mla_g_ckv[i], mla_w_uq[i], mla_w_uk[i], mla_w_uv[i],
                         mla_g_q[i], mla_g_k[i])
        o_d = _nsa_mixer(proj, B, S, cc_h, ss_h, positions, nsa_g_q[i], nsa_g_k[i],
                         nsa_cmp_w1[i], nsa_cmp_w2[i], nsa_cmp_pe[i])
        mixed = _mix_groups(o_a, o_b, o_c, o_d, g_group_out[i])
        h = _matmul(mixed, w_out_b, layer=i, res=h, tm=1024, tn=512, tk=MIX_WIDTH)
        u = _rmsnorm_rows(h, g_ffn[i])
        act = _swiglu(u, w_gate_b, w_up_b, layer=i, tm=1024, tn=256)
        h = _matmul(act, w_down_b, layer=i, res=h, tm=512, tn=256, tk=D_FF)
        u = _rmsnorm_rows(h, g_ple[i])
        h = _ple(u, w_pg_b, p_b, w_pp_b, h, layer=i, tm=1024, tn=512)
    return h.reshape(B, S, D_MODEL)
```

```python
import functools
import math

import numpy as np
import jax
import jax.numpy as jnp
from jax import lax
from jax.experimental import pallas as pl
from jax.experimental.pallas import tpu as pltpu

D_MODEL = 4096
DEPTH = 4
HEAD_DIM = 128
N_GROUPS = 4
HEADS_PER_GROUP = D_MODEL // HEAD_DIM // N_GROUPS
GROUP_WIDTH = HEADS_PER_GROUP * HEAD_DIM
MIX_WIDTH = N_GROUPS * GROUP_WIDTH
D_FF = ((8 * D_MODEL + 3 * 256 - 1) // (3 * 256)) * 256
PLE_DIM = 256
ROPE_THETA = 10000.0
EPS = 1e-6
Q_BLOCK = 128
NEG_INF = -1e30
MASK_BIAS = -2.0 ** 100
POS_INF = 1e30

DIFF_QK_DIM = HEAD_DIM // 2
DIFF_V_DIM = HEAD_DIM
IDX_HEADS = 16
IDX_DIM = 64
DSA_TOPK = 256
MLA_Q_RANK = 1024
MLA_KV_RANK = 512
MLA_NOPE = 128
MLA_ROPE = 64
MLA_V = HEAD_DIM
NSA_KV_HEADS = 2
NSA_Q_PER_KV = HEADS_PER_GROUP // NSA_KV_HEADS
CMP_LEN = 32
CMP_STRIDE = 16
CMP_HIDDEN = HEAD_DIM
SEL_BLOCK = 64
SEL_TOPN = 16
SEL_INIT = 1
SEL_LOCAL = 2
SEL_Q_BLOCK = 32
WINDOW = 512

IN_SIZES = (
    HEADS_PER_GROUP * 2 * DIFF_QK_DIM, HEADS_PER_GROUP * 2 * DIFF_QK_DIM, HEADS_PER_GROUP * DIFF_V_DIM,
    GROUP_WIDTH, HEAD_DIM, HEAD_DIM, IDX_HEADS * IDX_DIM, IDX_DIM, IDX_HEADS,
    MLA_Q_RANK, MLA_KV_RANK, MLA_ROPE,
    GROUP_WIDTH, 3 * 2 * NSA_KV_HEADS * HEAD_DIM, 3 * HEADS_PER_GROUP,
)
IN_WIDTH = sum(IN_SIZES)

V7X_VMEM_LIMIT_BYTES = 56 * 1024 * 1024
LANE = 128
MXU_DTYPE = getattr(jnp, "bfloat16")


def _round_up(n, m):
    return (n + m - 1) // m * m


def _compiler_params(semantics):
    return pltpu.CompilerParams(dimension_semantics=semantics, vmem_limit_bytes=V7X_VMEM_LIMIT_BYTES)


def _rmsnorm_rows_kernel(x_ref, g_ref, o_ref):
    x = x_ref[...]
    ms = jnp.mean(x * x, axis=-1, keepdims=True)
    o_ref[...] = (x * lax.rsqrt(ms + EPS) * g_ref[...]).astype(o_ref.dtype)


def _rmsnorm_rows(x, g, *, tm=256):
    m, d = x.shape
    out_dtype = MXU_DTYPE
    return pl.pallas_call(
        _rmsnorm_rows_kernel,
        grid=(m // tm,),
        in_specs=[pl.BlockSpec((tm, d), lambda i: (i, 0)), pl.BlockSpec((1, d), lambda i: (0, 0))],
        out_specs=pl.BlockSpec((tm, d), lambda i: (i, 0)),
        out_shape=jax.ShapeDtypeStruct((m, d), out_dtype),
        compiler_params=_compiler_params(("parallel",)),
        name="rmsnorm_rows",
    )(x, g.reshape(1, d))


def _mm_kernel(a_ref, w_ref, o_ref, acc_ref, *, nk):
    k = pl.program_id(2)

    @pl.when(k == 0)
    def _():
        acc_ref[...] = jnp.zeros_like(acc_ref)

    acc_ref[...] += jnp.dot(a_ref[...], w_ref[...], preferred_element_type=jnp.float32)

    @pl.when(k == nk - 1)
    def _():
        o_ref[...] = acc_ref[...].astype(o_ref.dtype)


def _mm_res_kernel(a_ref, w_ref, r_ref, o_ref, acc_ref, *, nk):
    k = pl.program_id(2)

    @pl.when(k == 0)
    def _():
        acc_ref[...] = jnp.zeros_like(acc_ref)

    acc_ref[...] += jnp.dot(a_ref[...], w_ref[...], preferred_element_type=jnp.float32)

    @pl.when(k == nk - 1)
    def _():
        o_ref[...] = (r_ref[...] + acc_ref[...]).astype(o_ref.dtype)


def _layer_spec(arr, block, index_map, layer):
    if arr.ndim == 2:
        return pl.BlockSpec(block, index_map)
    return pl.BlockSpec((None,) + block, lambda *g: (layer,) + index_map(*g))


def _matmul(a, w, *, res=None, layer=None, tm, tn, tk, out_dtype=jnp.float32):
    m, kdim = a.shape
    n = w.shape[-1]
    nk = kdim // tk
    assert m % tm == 0 and n % tn == 0 and kdim % tk == 0
    in_specs = [pl.BlockSpec((tm, tk), lambda i, j, k: (i, k)), _layer_spec(w, (tk, tn), lambda i, j, k: (k, j), layer)]
    args = [a, w]
    if res is None:
        body = functools.partial(_mm_kernel, nk=nk)
    else:
        body = functools.partial(_mm_res_kernel, nk=nk)
        in_specs.append(pl.BlockSpec((tm, tn), lambda i, j, k: (i, j)))
        args.append(res)
    return pl.pallas_call(
        body,
        grid=(m // tm, n // tn, nk),
        in_specs=in_specs,
        out_specs=pl.BlockSpec((tm, tn), lambda i, j, k: (i, j)),
        out_shape=jax.ShapeDtypeStruct((m, n), out_dtype),
        scratch_shapes=[pltpu.VMEM((tm, tn), jnp.float32)],
        compiler_params=_compiler_params(("parallel", "parallel", "arbitrary")),
        name="matmul_res" if res is not None else "matmul",
    )(*args)


def _swiglu_kernel(a_ref, wg_ref, wu_ref, o_ref):
    a = a_ref[...]
    g = jnp.dot(a, wg_ref[...], preferred_element_type=jnp.float32)
    u = jnp.dot(a, wu_ref[...], preferred_element_type=jnp.float32)
    o_ref[...] = (g * jax.nn.sigmoid(g) * u).astype(o_ref.dtype)


def _swiglu(a, wg, wu, *, layer, tm, tn):
    m, kdim = a.shape
    n = wg.shape[-1]
    return pl.pallas_call(
        _swiglu_kernel,
        grid=(m // tm, n // tn),
        in_specs=[pl.BlockSpec((tm, kdim), lambda i, j: (i, 0)),
                  _layer_spec(wg, (kdim, tn), lambda i, j: (0, j), layer),
                  _layer_spec(wu, (kdim, tn), lambda i, j: (0, j), layer)],
        out_specs=pl.BlockSpec((tm, tn), lambda i, j: (i, j)),
        out_shape=jax.ShapeDtypeStruct((m, n), MXU_DTYPE),
        compiler_params=_compiler_params(("parallel", "parallel")),
        name="swiglu",
    )(a, wg, wu)


def _ple_kernel(a_ref, wg_ref, p_ref, wp_ref, r_ref, o_ref):
    gate = jax.nn.sigmoid(jnp.dot(a_ref[...], wg_ref[...], preferred_element_type=jnp.float32))
    emb = jnp.dot(p_ref[...], wp_ref[...], preferred_element_type=jnp.float32)
    o_ref[...] = r_ref[...] + gate * emb


def _ple(a, wg, p, wp, res, *, layer, tm, tn):
    m, kdim = a.shape
    n = wg.shape[-1]
    pdim = p.shape[-1]
    return pl.pallas_call(
        _ple_kernel,
        grid=(m // tm, n // tn),
        in_specs=[pl.BlockSpec((tm, kdim), lambda i, j: (i, 0)),
                  _layer_spec(wg, (kdim, tn), lambda i, j: (0, j), layer),
                  _layer_spec(p, (tm, pdim), lambda i, j: (i, 0), layer),
                  _layer_spec(wp, (pdim, tn), lambda i, j: (0, j), layer),
                  pl.BlockSpec((tm, tn), lambda i, j: (i, j))],
        out_specs=pl.BlockSpec((tm, tn), lambda i, j: (i, j)),
        out_shape=jax.ShapeDtypeStruct((m, n), jnp.float32),
        compiler_params=_compiler_params(("parallel", "parallel")),
        name="ple",
    )(a, wg, p, wp, res)


def _rope_tables(positions, dim):
    inv_freq = ROPE_THETA ** (-jnp.arange(0, dim, 2, dtype=jnp.float32) / dim)
    ang = positions.astype(jnp.float32)[..., None] * inv_freq
    return jnp.cos(ang), jnp.sin(ang)


_SEG_ORDER = ("a_q", "a_k", "a_v", "b_q", "b_iq", "c_q", "d_q", "d_kv", "c_kv", "b_k", "b_v")
_SEG_NAMES = ("a_q", "a_k", "a_v", "b_q", "b_k", "b_v", "b_iq", "b_ik", "b_iw", "c_q", "c_kv", "c_kr", "d_q", "d_kv", "d_g")
_ORIG_START = dict(zip(_SEG_NAMES, [int(o) for o in np.cumsum((0,) + IN_SIZES[:-1])]))
_ORIG_SIZE = dict(zip(_SEG_NAMES, IN_SIZES))
_GATES_PER_KV_GROUP = 3 * NSA_Q_PER_KV


def _build_in_layout():
    col = {}
    perm = []
    for name in _SEG_ORDER:
        col[name] = len(perm)
        perm.extend(range(_ORIG_START[name], _ORIG_START[name] + _ORIG_SIZE[name]))
    zero = IN_WIDTH
    col["narrow"] = len(perm)
    blk0 = list(range(_ORIG_START["b_ik"], _ORIG_START["b_ik"] + IDX_DIM))
    blk0 += list(range(_ORIG_START["d_g"], _ORIG_START["d_g"] + _GATES_PER_KV_GROUP))
    blk0 += list(range(_ORIG_START["b_iw"], _ORIG_START["b_iw"] + IDX_HEADS))
    blk0 += [zero] * (LANE - len(blk0))
    blk1 = list(range(_ORIG_START["c_kr"], _ORIG_START["c_kr"] + MLA_ROPE))
    blk1 += list(range(_ORIG_START["d_g"] + _GATES_PER_KV_GROUP, _ORIG_START["d_g"] + 2 * _GATES_PER_KV_GROUP))
    blk1 += [zero] * (LANE - len(blk1))
    perm.extend(blk0 + blk1)
    width = _round_up(len(perm), 512)
    perm.extend([zero] * (width - len(perm)))
    return col, np.asarray(perm, np.int32), width


_COL, _IN_PERM, IN_WIDTH_PAD = _build_in_layout()


def _runs(perm):
    runs = []
    for c in perm.tolist():
        src = None if c == IN_WIDTH else c
        if runs and ((src is None and runs[-1][0] is None)
                     or (src is not None and runs[-1][0] is not None and runs[-1][0] + runs[-1][1] == src)):
            runs[-1][1] += 1
        else:
            runs.append([src, 1])
    return [tuple(r) for r in runs]


_IN_RUNS = _runs(_IN_PERM)
_GATE_LANE0 = IDX_DIM
_IW_LANE0 = IDX_DIM + _GATES_PER_KV_GROUP


def _rope_lane_tables(cos, sin):
    return jnp.concatenate([cos, cos], axis=-1), jnp.concatenate([-sin, sin], axis=-1)


def _lane_tables(positions, dim):
    cc, ss = _rope_lane_tables(*_rope_tables(positions, dim))
    return tuple(jnp.tile(t.reshape(-1, dim), (1, LANE // dim)) for t in (cc, ss))


def _seg64_sum(y):
    r = lax.broadcasted_iota(jnp.int32, (LANE, LANE), 0) // 64
    c = lax.broadcasted_iota(jnp.int32, (LANE, LANE), 1) // 64
    bd = jnp.where(r == c, 1.0, 0.0).astype(MXU_DTYPE)
    hi = y.astype(MXU_DTYPE)
    lo = (y - hi.astype(jnp.float32)).astype(MXU_DTYPE)
    return (jnp.dot(hi, bd, preferred_element_type=jnp.float32)
            + jnp.dot(lo, bd, preferred_element_type=jnp.float32))


def _prep_kernel(x_ref, g_ref, cc_ref, ss_ref, o_ref, *, n_blocks, seg, norm, scale):
    cc, ss, g = cc_ref[...], ss_ref[...], g_ref[...]
    lane = lax.broadcasted_iota(jnp.int32, cc.shape, 1)
    for h in range(n_blocks):
        x = x_ref[:, h * LANE:(h + 1) * LANE]
        if norm:
            if seg == LANE:
                ms = jnp.mean(x * x, axis=-1, keepdims=True)
            else:
                ms = _seg64_sum(x * x) * (1.0 / seg)
            x = x * lax.rsqrt(ms + EPS) * g
        if seg == LANE:
            partner = pltpu.roll(x, LANE // 2, axis=1)
        else:
            partner = jnp.where((lane % seg) < seg // 2, pltpu.roll(x, LANE - seg // 2, axis=1),
                                pltpu.roll(x, seg // 2, axis=1))
        y = x * cc + partner * ss
        if scale != 1.0:
            y = y * scale
        o_ref[:, h * LANE:(h + 1) * LANE] = y.astype(o_ref.dtype)


def _prep_heads(proj, col0, n_blocks, gain, cc, ss, *, seg, norm, scale=1.0, tm=256):
    m = proj.shape[0]
    tm = min(tm, m)
    width = LANE * n_blocks
    cb = col0 // width
    assert cb * width == col0
    return pl.pallas_call(
        functools.partial(_prep_kernel, n_blocks=n_blocks, seg=seg, norm=norm, scale=scale),
        grid=(m // tm,),
        in_specs=[pl.BlockSpec((tm, width), lambda i: (i, cb)),
                  pl.BlockSpec((1, LANE), lambda i: (0, 0)),
                  pl.BlockSpec((tm, LANE), lambda i: (i, 0)),
                  pl.BlockSpec((tm, LANE), lambda i: (i, 0))],
        out_specs=pl.BlockSpec((tm, width), lambda i: (i, 0)),
        out_shape=jax.ShapeDtypeStruct((m, width), MXU_DTYPE),
        compiler_params=_compiler_params(("parallel",)),
        name="prep_heads",
    )(proj, gain, cc, ss)


def _dot_nt(a, b):
    return lax.dot_general(a, b, (((1,), (1,)), ((), ())), preferred_element_type=jnp.float32)


LOG2E = math.log2(math.e)


def _softmax_step(s, mask, m, l, acc, v, *, rows_may_be_empty=False):
    if mask is not None:
        s = jnp.where(mask, s, NEG_INF)
    m_new = jnp.maximum(m, jnp.max(s, axis=-1, keepdims=True))
    alpha = jnp.exp2(m - m_new)
    p = jnp.exp2(s - m_new)
    if rows_may_be_empty:
        p = jnp.where(mask, p, 0.0)
    l = alpha * l + jnp.sum(p, axis=-1, keepdims=True)
    acc = alpha * acc + jnp.dot(p.astype(MXU_DTYPE), v, preferred_element_type=jnp.float32)
    return m_new, l, acc


def _softmax_finish(l, acc):
    return acc * (1.0 / jnp.maximum(l, 1e-30))


def _nsa_cmp_kernel(x_ref, pe_ref, w1_ref, w2_ref, gk_ref, cc_ref, ss_ref, kc_ref, vc_ref):
    nch = x_ref.shape[2]
    for kv in range(2):
        x = x_ref[kv, 0]
        lo = jnp.dot((x + pe_ref[kv, 0:1, :]).astype(MXU_DTYPE), w1_ref[kv, 0], preferred_element_type=jnp.float32)
        hi = jnp.dot((x + pe_ref[kv, 1:2, :]).astype(MXU_DTYPE), w1_ref[kv, 1], preferred_element_type=jnp.float32)
        hid = lo + pltpu.roll(hi, nch - 1, axis=0)
        comp = jnp.dot(jax.nn.gelu(hid).astype(MXU_DTYPE), w2_ref[kv], preferred_element_type=jnp.float32)
        if kv == 0:
            ms = jnp.mean(comp * comp, axis=-1, keepdims=True)
            y = comp * lax.rsqrt(ms + EPS) * gk_ref[...]
            y = y * cc_ref[0] + pltpu.roll(y, LANE // 2, axis=1) * ss_ref[0]
            kc_ref[0] = y.astype(kc_ref.dtype)
        else:
            vc_ref[0] = comp.astype(vc_ref.dtype)


def _nsa_compress(craw, pe2, w1, w2, gk, cc_c, ss_c, *, n_batch):
    _, gb, nch, width = craw.shape
    return pl.pallas_call(
        _nsa_cmp_kernel,
        grid=(gb,),
        in_specs=[pl.BlockSpec((2, 1, nch, width), lambda n: (0, n, 0, 0)),
                  pl.BlockSpec((2, 2, width), lambda n: (0, 0, 0)),
                  pl.BlockSpec((2, 2, width, CMP_HIDDEN), lambda n: (0, 0, 0, 0)),
                  pl.BlockSpec((2, CMP_HIDDEN, HEAD_DIM), lambda n: (0, 0, 0)),
                  pl.BlockSpec((1, HEAD_DIM), lambda n: (0, 0)),
                  pl.BlockSpec((1, nch, HEAD_DIM), lambda n: (n % n_batch, 0, 0)),
                  pl.BlockSpec((1, nch, HEAD_DIM), lambda n: (n % n_batch, 0, 0))],
        out_specs=[pl.BlockSpec((1, nch, HEAD_DIM), lambda n: (n, 0, 0)),
                   pl.BlockSpec((1, nch, HEAD_DIM), lambda n: (n, 0, 0))],
        out_shape=[jax.ShapeDtypeStruct((gb, nch, HEAD_DIM), MXU_DTYPE)] * 2,
        compiler_params=_compiler_params(("parallel",)),
        name="nsa_compress",
    )(craw, pe2, w1, w2, gk, cc_c, ss_c)


def _nsa_attn_kernel(q_ref, kc_ref, vc_ref, ks_ref, vs_ref, kw_ref, vw_ref, gate_ref, c2s_ref, exp_ref, o_ref, *,
                     tq, tk, n_sel, n_top):
    i = pl.program_id(2)
    R = NSA_Q_PER_KV
    rows = R * tq
    nch = kc_ref.shape[1]
    bf16 = MXU_DTYPE
    q = jnp.concatenate([q_ref[:, r * HEAD_DIM:(r + 1) * HEAD_DIM] for r in range(R)], axis=0)
    t_row = i * tq + lax.broadcasted_iota(jnp.int32, (rows, 1), 0) % tq

    s = _dot_nt(q, kc_ref[0])
    blk_end = lax.broadcasted_iota(jnp.int32, (1, nch), 1) * CMP_STRIDE + (CMP_LEN - 1)
    mask = blk_end <= t_row
    zero = jnp.zeros((rows, 1), jnp.float32)
    s = jnp.where(mask, s, NEG_INF)
    p_cmp = jnp.where(mask, jnp.exp2(s - jnp.max(s, axis=-1, keepdims=True)), 0.0)
    p_cmp = p_cmp * (1.0 / jnp.maximum(jnp.sum(p_cmp, axis=-1, keepdims=True), 1e-30))
    o_cmp = jnp.dot(p_cmp.astype(bf16), vc_ref[0], preferred_element_type=jnp.float32)

    imp_rows = jnp.dot(p_cmp.astype(bf16), c2s_ref[...], preferred_element_type=jnp.float32)
    imp = imp_rows[0:tq]
    for r in range(1, R):
        imp = imp + imp_rows[r * tq:(r + 1) * tq]
    imp_t = imp.T[0:n_sel]
    blk = lax.broadcasted_iota(jnp.int32, (n_sel, tq), 0)
    t_lane = i * tq + lax.broadcasted_iota(jnp.int32, (n_sel, tq), 1)
    dist = lax.shift_right_logical(t_lane, int(math.log2(SEL_BLOCK))) - blk
    forced = (blk < SEL_INIT) | ((dist >= 0) & (dist < SEL_LOCAL))
    val = jnp.where(forced, POS_INF, jnp.where(blk * SEL_BLOCK <= t_lane, imp_t, NEG_INF))
    rank = jnp.zeros((n_sel, tq), jnp.float32)
    for mblk in range(n_sel):
        vm = val[mblk:mblk + 1, :]
        rank = rank + jnp.where((vm > val) | ((vm == val) & (blk > mblk)), 1.0, 0.0)
    drop_t = jnp.where(rank < n_top, 0.0, 1.0)
    drop_t = jnp.concatenate([drop_t, jnp.zeros((LANE - n_sel, tq), jnp.float32)], axis=0) if n_sel < LANE else drop_t
    drop_q = drop_t.T.astype(bf16)
    drop_rows = jnp.concatenate([drop_q] * R, axis=0)

    def slc_step(j, carry, diagonal):
        off = pl.multiple_of(j * tk, tk)
        k = ks_ref[pl.ds(off, tk), :]
        v = vs_ref[pl.ds(off, tk), :].astype(bf16)
        s = _dot_nt(q, k) + jnp.dot(drop_rows, exp_ref[j], preferred_element_type=jnp.float32)
        mask = (off + lax.broadcasted_iota(jnp.int32, (1, tk), 1) <= t_row) if diagonal else None
        return _softmax_step(s, mask, *carry, v)

    init = (zero + NEG_INF, zero, jnp.zeros((rows, HEAD_DIM), jnp.float32))
    n_full = (i * tq) // tk
    carry = lax.fori_loop(0, n_full, lambda j, c: slc_step(j, c, False), init)
    _, l, acc = slc_step(n_full, carry, True)
    o_slc = _softmax_finish(l, acc)

    span = WINDOW + tq
    off = pl.multiple_of(jnp.maximum(i * tq - WINDOW, 0), tq)
    key = off + lax.broadcasted_iota(jnp.int32, (1, span), 1)
    s = _dot_nt(q, kw_ref[pl.ds(off, span), :])
    s = jnp.where(key <= t_row, jnp.where(key > t_row - WINDOW, s, NEG_INF), NEG_INF)
    _, l, acc = _softmax_step(s, None, *init, vw_ref[pl.ds(off, span), :].astype(bf16))
    o_win = _softmax_finish(l, acc)

    gate = jax.nn.sigmoid(gate_ref[...])
    for r in range(R):
        c = _GATE_LANE0 + 3 * r
        rs = slice(r * tq, (r + 1) * tq)
        o_ref[:, r * HEAD_DIM:(r + 1) * HEAD_DIM] = (gate[:, c:c + 1] * o_cmp[rs]
                                                     + gate[:, c + 1:c + 2] * o_slc[rs]
                                                     + gate[:, c + 2:c + 3] * o_win[rs])


def _nsa_mixer(proj, B, S, cc_h, ss_h, positions, g_q, g_k, cmp_w1, cmp_w2, cmp_pe):
    M = B * S
    G, R, d = NSA_KV_HEADS, NSA_Q_PER_KV, HEAD_DIM
    bf16 = MXU_DTYPE
    tq = Q_BLOCK
    nq = S // tq
    nch = S // CMP_STRIDE
    n_sel = S // SEL_BLOCK
    n_top = min(SEL_TOPN, n_sel)
    kv0 = _COL["d_kv"]

    qn = _prep_heads(proj, _COL["d_q"], HEADS_PER_GROUP, g_q.reshape(1, d), cc_h, ss_h,
                     seg=LANE, norm=True, scale=d ** -0.5 * LOG2E)
    k_slc =_prep_heads(proj, kv0 + (1 * 2 + 0) * G * d, G, g_k[1].reshape(1, d), cc_h, ss_h, seg=LANE, norm=True)
    k_win = _prep_heads(proj, kv0 + (2 * 2 + 0) * G * d, G, g_k[2].reshape(1, d), cc_h, ss_h, seg=LANE, norm=True)

    craw = proj[:, kv0:kv0 + 2 * G * d].reshape(B, S, 2, G, d).transpose(2, 3, 0, 1, 4)
    craw = craw.reshape(2, G * B, nch, CMP_STRIDE * d)
    ends = np.minimum(np.arange(nch) * CMP_STRIDE + CMP_LEN - 1, S - 1)
    cc_c, ss_c = _rope_lane_tables(*_rope_tables(positions[:, ends], d))
    pe2 = cmp_pe.reshape(2, 2, CMP_STRIDE * d)
    w1 = cmp_w1.astype(bf16).reshape(2, 2, CMP_STRIDE * d, CMP_HIDDEN)
    k_cmp, v_cmp = _nsa_compress(craw, pe2, w1, cmp_w2.astype(bf16), g_k[0].reshape(1, d), cc_c, ss_c, n_batch=B)

    starts = np.arange(nch) * CMP_STRIDE
    sel_start = np.arange(LANE) * SEL_BLOCK
    c2s = ((starts[:, None] < sel_start[None, :] + SEL_BLOCK) & (starts[:, None] + CMP_LEN > sel_start[None, :])
           & (np.arange(nch)[:, None] < nch - 1) & (np.arange(LANE)[None, :] < n_sel))
    tk = min(512, S)
    expand = (np.arange(LANE)[None, :, None] == (np.arange(S // tk)[:, None, None] * tk + np.arange(tk)[None, None, :]) // SEL_BLOCK)
    c2s = jnp.asarray(c2s, bf16)
    expand = jnp.asarray(expand * MASK_BIAS, bf16)

    narrow = _COL["narrow"] // LANE
    col_vs = (kv0 + (1 * 2 + 1) * G * d) // LANE
    col_vw = (kv0 + (2 * 2 + 1) * G * d) // LANE
    return pl.pallas_call(
        functools.partial(_nsa_attn_kernel, tq=tq, tk=tk, n_sel=n_sel, n_top=n_top),
        grid=(B, G, nq),
        in_specs=[pl.BlockSpec((tq, R * d), lambda b, g, i: (b * nq + i, g)),
                  pl.BlockSpec((1, nch, d), lambda b, g, i: (g * B + b, 0, 0)),
                  pl.BlockSpec((1, nch, d), lambda b, g, i: (g * B + b, 0, 0)),
                  pl.BlockSpec((S, d), lambda b, g, i: (b, g)),
                  pl.BlockSpec((S, d), lambda b, g, i: (b, col_vs + g)),
                  pl.BlockSpec((S, d), lambda b, g, i: (b, g)),
                  pl.BlockSpec((S, d), lambda b, g, i: (b, col_vw + g)),
                  pl.BlockSpec((tq, LANE), lambda b, g, i: (b * nq + i, narrow + g)),
                  pl.BlockSpec((nch, LANE), lambda b, g, i: (0, 0)),
                  pl.BlockSpec((S // tk, LANE, tk), lambda b, g, i: (0, 0, 0))],
        out_specs=pl.BlockSpec((tq, R * d), lambda b, g, i: (b * nq + i, g)),
        out_shape=jax.ShapeDtypeStruct((M, GROUP_WIDTH), jnp.float32),
        compiler_params=_compiler_params(("parallel", "parallel", "arbitrary")),
        name="nsa_attention",
    )(qn, k_cmp, v_cmp, k_slc, proj, k_win, proj, proj, c2s, expand)


_INT_MIN = -2 ** 31


def _sortable_key(x):
    b = lax.bitcast_convert_type(x + 0.0, jnp.int32)
    return jnp.where(b >= 0, b, b ^ 0x7FFFFFFF)


def _dsa_kernel(q_ref, k_ref, v_ref, iq_ref, ik_ref, nar_ref, o_ref, key_ref, msk_ref, *,
                tq, tk, n_keep, idx_bits):
    i = pl.program_id(1)
    H = HEADS_PER_GROUP
    bf16 = MXU_DTYPE
    t_lane = i * tq + lax.broadcasted_iota(jnp.int32, (1, tq), 1)
    key_sub = lax.broadcasted_iota(jnp.int32, (tq, 1), 0)
    lane = lax.broadcasted_iota(jnp.int32, (tq, LANE), 1)
    iw_t = (nar_ref[...] * (IDX_HEADS ** -0.5)).T
    w_rows = [iw_t[_IW_LANE0 + hd:_IW_LANE0 + hd + 1, :] for hd in range(IDX_HEADS)]
    iq_rows = jnp.concatenate([iq_ref[:, pr * LANE:(pr + 1) * LANE] for pr in range(IDX_HEADS // 2)], axis=0)

    def score_body(j, _):
        off = pl.multiple_of(j * tq, tq)
        ik = ik_ref[pl.ds(off, tq), :]
        ik_lo = jnp.where(lane < IDX_DIM, ik, jnp.zeros_like(ik))
        ik_hi = jnp.where(lane < IDX_DIM, jnp.zeros_like(ik), pltpu.roll(ik.astype(jnp.float32), IDX_DIM, axis=1).astype(bf16))
        rel = (jnp.maximum(_dot_nt(ik_lo, iq_rows), 0.0), jnp.maximum(_dot_nt(ik_hi, iq_rows), 0.0))
        sc = jnp.zeros((tq, tq), jnp.float32)
        for hd in range(IDX_HEADS):
            sc = sc + w_rows[hd] * rel[hd % 2][:, (hd // 2) * tq:(hd // 2 + 1) * tq]
        sc = jnp.where(off + key_sub <= t_lane, sc, NEG_INF)
        key_ref[j] = _sortable_key(sc)
        return 0

    lax.fori_loop(0, i + 1, score_body, 0)

    def count(pred):
        def body(j, part):
            return part + jnp.where(pred(key_ref[j], j * tq), 1.0, 0.0)
        part = lax.fori_loop(0, i + 1, body, jnp.zeros((tq, tq), jnp.float32))
        return jnp.sum(part, axis=0, keepdims=True)

    c0 = count(lambda kk, off: kk >= 0)
    thr = jnp.where(c0 >= n_keep, 0, _INT_MIN).astype(jnp.int32)

    def thr_body(it, thr):
        cand = thr | lax.shift_left(jnp.int32(1), 30 - it)
        c = count(lambda kk, off: kk >= cand)
        return jnp.where(c >= n_keep, cand, thr)

    thr = lax.fori_loop(0, 31, thr_body, thr)

    n_ge = count(lambda kk, off: kk >= thr)

    def cut_search():
        need = n_keep - count(lambda kk, off: kk > thr)

        def cut_body(it, cut):
            cand = cut | lax.shift_left(jnp.int32(1), idx_bits - 1 - it)
            c = count(lambda kk, off: (kk == thr) & (off + key_sub < cand))
            return jnp.where(c < need, cand, cut)

        return lax.fori_loop(0, idx_bits, cut_body, jnp.zeros((1, tq), jnp.int32))

    cut = lax.cond(jnp.max(n_ge) > n_keep, cut_search, lambda: jnp.full((1, tq), 2 ** idx_bits - 1, jnp.int32))

    tpc = tk // tq
    n_chunks = (i + tpc) // tpc

    def mask_body(j, _):
        @pl.when(j <= i)
        def _():
            kk = key_ref[j]
            kidx = j * tq + key_sub
            sel = (kk > thr) | ((kk == thr) & (kidx <= cut))
            bias_t = jnp.where(kidx <= t_lane, jnp.where(sel, 0.0, NEG_INF), NEG_INF)
            msk_ref[j] = bias_t.T

        @pl.when(j > i)
        def _():
            msk_ref[j] = jnp.full((tq, tq), NEG_INF, jnp.float32)

        return 0

    lax.fori_loop(0, n_chunks * tpc, mask_body, 0)

    hpp = 4
    rows = hpp * tq
    zero = jnp.zeros((rows, 1), jnp.float32)
    qs = [jnp.concatenate([q_ref[:, (hc * hpp + r) * HEAD_DIM:(hc * hpp + r + 1) * HEAD_DIM] for r in range(hpp)], axis=0)
          for hc in range(H // hpp)]

    def att_body(j, carry):
        off = pl.multiple_of(j * tk, tk)
        k = k_ref[pl.ds(off, tk), :]
        v = v_ref[pl.ds(off, tk), :].astype(bf16)
        bias1 = jnp.concatenate([msk_ref[j * tpc + u] for u in range(tpc)], axis=1)
        bias = jnp.concatenate([bias1] * hpp, axis=0)
        return tuple(_softmax_step(_dot_nt(q, k) + bias, None, *carry[c], v) for c, q in enumerate(qs))

    init = tuple((zero + NEG_INF, zero, jnp.zeros((rows, HEAD_DIM), jnp.float32)) for _ in qs)
    res = lax.fori_loop(0, n_chunks, att_body, init)
    for hc, (_, l, acc) in enumerate(res):
        o = _softmax_finish(l, acc)
        for r in range(hpp):
            hh = hc * hpp + r
            o_ref[:, hh * HEAD_DIM:(hh + 1) * HEAD_DIM] = o[r * tq:(r + 1) * tq]


def _dsa_mixer(proj, B, S, cc_h, ss_h, cc_i, ss_i, g_q, g_k):
    M = B * S
    d = HEAD_DIM
    tq = Q_BLOCK
    nq = S // tq
    n_keep = min(DSA_TOPK, S // 4)
    qn = _prep_heads(proj, _COL["b_q"], HEADS_PER_GROUP, g_q.reshape(1, d), cc_h, ss_h, seg=LANE, norm=True,
                     scale=d ** -0.5 * LOG2E)
    kn =_prep_heads(proj, _COL["b_k"], 1, g_k.reshape(1, d), cc_h, ss_h, seg=LANE, norm=True)
    ones = jnp.ones((1, LANE), jnp.float32)
    iqn = _prep_heads(proj, _COL["b_iq"], IDX_HEADS * IDX_DIM // LANE, ones, cc_i, ss_i, seg=IDX_DIM, norm=False,
                      scale=IDX_DIM ** -0.5)
    ikn = _prep_heads(proj, _COL["narrow"], 1, ones, cc_i, ss_i, seg=IDX_DIM, norm=False)
    narrow = _COL["narrow"] // LANE
    col_v = _COL["b_v"] // LANE
    return pl.pallas_call(
        functools.partial(_dsa_kernel, tq=tq, tk=min(512, S), n_keep=n_keep, idx_bits=int(math.log2(S))),
        grid=(B, nq),
        in_specs=[pl.BlockSpec((tq, GROUP_WIDTH), lambda b, i: (b * nq + i, 0)),
                  pl.BlockSpec((S, d), lambda b, i: (b, 0)),
                  pl.BlockSpec((S, d), lambda b, i: (b, col_v)),
                  pl.BlockSpec((tq, IDX_HEADS * IDX_DIM), lambda b, i: (b * nq + i, 0)),
                  pl.BlockSpec((S, LANE), lambda b, i: (b, 0)),
                  pl.BlockSpec((tq, LANE), lambda b, i: (b * nq + i, narrow))],
        out_specs=pl.BlockSpec((tq, GROUP_WIDTH), lambda b, i: (b * nq + i, 0)),
        out_shape=jax.ShapeDtypeStruct((M, GROUP_WIDTH), jnp.float32),
        scratch_shapes=[pltpu.VMEM((nq, tq, tq), jnp.int32), pltpu.VMEM((nq, tq, tq), jnp.float32)],
        compiler_params=_compiler_params(("parallel", "arbitrary")),
        name="dsa_attention",
    )(qn, kn, proj, iqn, ikn, proj)


def _causal_chains(qs, k_ref, v_ref, dk, dv, t_row, t0, tk):
    rows = qs[0].shape[0]
    zero = jnp.zeros((rows, 1), jnp.float32)

    def step(j, carry, masked):
        off = pl.multiple_of(j * tk, tk)
        mask = (off + lax.broadcasted_iota(jnp.int32, (1, tk), 1) <= t_row) if masked else None
        out = []
        for c, q in enumerate(qs):
            k = k_ref[pl.ds(off, tk), c * dk:(c + 1) * dk]
            v = v_ref[pl.ds(off, tk), c * dv:(c + 1) * dv].astype(MXU_DTYPE)
            out.append(_softmax_step(_dot_nt(q, k), mask, *carry[c], v))
        return tuple(out)

    init = tuple((zero + NEG_INF, zero, jnp.zeros((rows, dv), jnp.float32)) for _ in qs)
    n_full = t0 // tk
    res = lax.fori_loop(0, n_full, lambda j, c: step(j, c, False), init)
    res = step(n_full, res, True)
    return [_softmax_finish(l, acc) for _, l, acc in res]


def _diff_attn_kernel(q_ref, k_ref, v_ref, lam_ref, g_ref, o_ref, *, tq, tk, hb, out_scale):
    i = pl.program_id(2)
    d = HEAD_DIM
    lane = lax.broadcasted_iota(jnp.int32, (tq, d), 1)
    qs = []
    for c in range(hb):
        q = q_ref[:, c * d:(c + 1) * d]
        zeros = jnp.zeros_like(q)
        qs.append(jnp.concatenate([jnp.where(lane < DIFF_QK_DIM, q, zeros), jnp.where(lane < DIFF_QK_DIM, zeros, q)], axis=0))
    t_row = i * tq + lax.broadcasted_iota(jnp.int32, (2 * tq, 1), 0) % tq
    outs = _causal_chains(qs, k_ref, v_ref, d, d, t_row, i * tq, tk)
    for c, o2 in enumerate(outs):
        o = o2[0:tq] - lam_ref[...] * o2[tq:2 * tq]
        ms = jnp.mean(o * o, axis=-1, keepdims=True)
        o_ref[:, c * d:(c + 1) * d] = o * lax.rsqrt(ms + EPS) * g_ref[...] * out_scale


def _diff_mixer(proj, B, S, cc_d, ss_d, g_q, g_k, g_sub, lam_params, layer_idx, *, tq=128, tk=512, hb=4):
    M = B * S
    d = HEAD_DIM
    nq = S // tq
    tk = min(tk, S)
    reps = LANE // DIFF_QK_DIM
    qn = _prep_heads(proj, _COL["a_q"], HEADS_PER_GROUP, jnp.tile(g_q, reps).reshape(1, d), cc_d, ss_d,
                     seg=DIFF_QK_DIM, norm=True, scale=DIFF_QK_DIM ** -0.5 * LOG2E)
    kn = _prep_heads(proj, _COL["a_k"], HEADS_PER_GROUP, jnp.tile(g_k, reps).reshape(1, d), cc_d, ss_d,
                     seg=DIFF_QK_DIM, norm=True)
    lam_init = 0.8 - 0.6 * math.exp(-0.3 * layer_idx)
    lp = lam_params.astype(jnp.float32)
    lam = jnp.exp(jnp.sum(lp[0] * lp[1])) - jnp.exp(jnp.sum(lp[2] * lp[3])) + lam_init
    col_v = _COL["a_v"] // (hb * d)
    return pl.pallas_call(
        functools.partial(_diff_attn_kernel, tq=tq, tk=tk, hb=hb, out_scale=1.0 - lam_init),
        grid=(B, HEADS_PER_GROUP // hb, nq),
        in_specs=[pl.BlockSpec((tq, hb * d), lambda b, h, i: (b * nq + i, h)),
                  pl.BlockSpec((S, hb * d), lambda b, h, i: (b, h)),
                  pl.BlockSpec((S, hb * d), lambda b, h, i: (b, col_v + h)),
                  pl.BlockSpec((1, d), lambda b, h, i: (0, 0)),
                  pl.BlockSpec((1, d), lambda b, h, i: (0, 0))],
        out_specs=pl.BlockSpec((tq, hb * d), lambda b, h, i: (b * nq + i, h)),
        out_shape=jax.ShapeDtypeStruct((M, GROUP_WIDTH), jnp.float32),
        compiler_params=_compiler_params(("parallel", "parallel", "arbitrary")),
        name="diff_attention",
    )(qn, kn, proj, jnp.full((1, d), lam, jnp.float32), g_sub.reshape(1, d))


MLA_QK_PAD = 2 * LANE


def _rmsnorm_cols_kernel(x_ref, g_ref, o_ref):
    x = x_ref[...]
    ms = jnp.mean(x * x, axis=-1, keepdims=True)
    o_ref[...] = (x * lax.rsqrt(ms + EPS) * g_ref[...]).astype(o_ref.dtype)


def _rmsnorm_cols(proj, col0, width, g, *, tm=512):
    m = proj.shape[0]
    tm = min(tm, m)
    cb = col0 // width
    assert cb * width == col0
    return pl.pallas_call(
        _rmsnorm_cols_kernel,
        grid=(m // tm,),
        in_specs=[pl.BlockSpec((tm, width), lambda i: (i, cb)), pl.BlockSpec((1, width), lambda i: (0, 0))],
        out_specs=pl.BlockSpec((tm, width), lambda i: (i, 0)),
        out_shape=jax.ShapeDtypeStruct((m, width), MXU_DTYPE),
        compiler_params=_compiler_params(("parallel",)),
        name="rmsnorm_cols",
    )(proj, g.reshape(1, width))


def _mla_prep_kernel(a_ref, b_ref, g_ref, cc_ref, ss_ref, o_ref, *, a_stride, b_col0, b_stride, scale):
    cc, ss = cc_ref[...], ss_ref[...]
    lane = lax.broadcasted_iota(jnp.int32, cc.shape, 1)
    for h in range(HEADS_PER_GROUP):
        a = a_ref[:, h * a_stride:h * a_stride + LANE]
        b = jnp.where(lane < MLA_ROPE, b_ref[:, b_col0 + h * b_stride:b_col0 + h * b_stride + LANE], 0.0)
        ms = (jnp.sum(a * a, axis=-1, keepdims=True) + jnp.sum(b * b, axis=-1, keepdims=True)) * (1.0 / (MLA_NOPE + MLA_ROPE))
        r = lax.rsqrt(ms + EPS)
        ya = a * r * g_ref[:, 0:LANE]
        yb = b * r * g_ref[:, LANE:2 * LANE]
        partner = jnp.where((lane % MLA_ROPE) < MLA_ROPE // 2, pltpu.roll(yb, LANE - MLA_ROPE // 2, axis=1),
                            pltpu.roll(yb, MLA_ROPE // 2, axis=1))
        yb = yb * cc + partner * ss
        o_ref[:, h * MLA_QK_PAD:h * MLA_QK_PAD + LANE] = (ya * scale).astype(o_ref.dtype)
        o_ref[:, h * MLA_QK_PAD + LANE:(h + 1) * MLA_QK_PAD] = (yb * scale).astype(o_ref.dtype)


def _mla_prep(a_arr, a_width, a_stride, b_arr, b_block, b_width, b_col0, b_stride, gain, cc, ss, *, scale=1.0, tm=256):
    m = a_arr.shape[0]
    tm = min(tm, m)
    g2 = jnp.concatenate([gain, jnp.zeros((MLA_QK_PAD - gain.shape[0],), gain.dtype)]).reshape(1, MLA_QK_PAD)
    return pl.pallas_call(
        functools.partial(_mla_prep_kernel, a_stride=a_stride, b_col0=b_col0, b_stride=b_stride, scale=scale),
        grid=(m // tm,),
        in_specs=[pl.BlockSpec((tm, a_width), lambda i: (i, 0)),
                  pl.BlockSpec((tm, b_width), lambda i: (i, b_block)),
                  pl.BlockSpec((1, MLA_QK_PAD), lambda i: (0, 0)),
                  pl.BlockSpec((tm, LANE), lambda i: (i, 0)),
                  pl.BlockSpec((tm, LANE), lambda i: (i, 0))],
        out_specs=pl.BlockSpec((tm, HEADS_PER_GROUP * MLA_QK_PAD), lambda i: (i, 0)),
        out_shape=jax.ShapeDtypeStruct((m, HEADS_PER_GROUP * MLA_QK_PAD), MXU_DTYPE),
        compiler_params=_compiler_params(("parallel",)),
        name="mla_prep",
    )(a_arr, b_arr, g2, cc, ss)


def _mla_attn_kernel(q_ref, k_ref, v_ref, o_ref, *, tq, tk, hb):
    i = pl.program_id(2)
    d = HEAD_DIM
    t_row = i * tq + lax.broadcasted_iota(jnp.int32, (tq, 1), 0)
    qs = [q_ref[:, c * MLA_QK_PAD:(c + 1) * MLA_QK_PAD] for c in range(hb)]
    outs = _causal_chains(qs, k_ref, v_ref, MLA_QK_PAD, d, t_row, i * tq, tk)
    for c, o in enumerate(outs):
        o_ref[:, c * d:(c + 1) * d] = o


def _mla_mixer(proj, B, S, cc_m, ss_m, g_cq, g_ckv, w_uq, w_uk, w_uv, g_q, g_k, *, tq=256, tk=512, hb=4):
    M = B * S
    H, d = HEADS_PER_GROUP, HEAD_DIM
    tq = min(tq, S)
    tk = min(tk, S)
    nq = S // tq
    dqk = MLA_NOPE + MLA_ROPE
    w_q = jnp.pad(w_uq.astype(MXU_DTYPE).reshape(MLA_Q_RANK, H, dqk), ((0, 0), (0, 0), (0, MLA_QK_PAD - dqk)))
    w_q = w_q.reshape(MLA_Q_RANK, H * MLA_QK_PAD)
    w_kv = jnp.concatenate([w_uk, w_uv], axis=1).astype(MXU_DTYPE)
    cq = _rmsnorm_cols(proj, _COL["c_q"], MLA_Q_RANK, g_cq)
    ckv = _rmsnorm_cols(proj, _COL["c_kv"], MLA_KV_RANK, g_ckv)
    q_up = _matmul(cq, w_q, tm=min(1024, M), tn=1024, tk=MLA_Q_RANK)
    kv_up = _matmul(ckv, w_kv, tm=min(1024, M), tn=1024, tk=MLA_KV_RANK)
    qn = _mla_prep(q_up, H * MLA_QK_PAD, MLA_QK_PAD, q_up, 0, H * MLA_QK_PAD, LANE, MLA_QK_PAD, g_q, cc_m, ss_m,
                   scale=dqk ** -0.5 * LOG2E)
    kn = _mla_prep(kv_up, H * MLA_NOPE, LANE, proj, _COL["narrow"] // LANE + 1, LANE, 0, 0, g_k, cc_m, ss_m)
    return pl.pallas_call(
        functools.partial(_mla_attn_kernel, tq=tq, tk=tk, hb=hb),
        grid=(B, H // hb, nq),
        in_specs=[pl.BlockSpec((tq, hb * MLA_QK_PAD), lambda b, h, i: (b * nq + i, h)),
                  pl.BlockSpec((S, hb * MLA_QK_PAD), lambda b, h, i: (b, h)),
                  pl.BlockSpec((S, hb * d), lambda b, h, i: (b, H // hb + h))],
        out_specs=pl.BlockSpec((tq, hb * d), lambda b, h, i: (b * nq + i, h)),
        out_shape=jax.ShapeDtypeStruct((M, GROUP_WIDTH), jnp.float32),
        compiler_params=_compiler_params(("parallel", "parallel", "arbitrary")),
        name="mla_attention",
    )(qn, kn, kv_up)


def _mix_kernel(a_ref, b_ref, c_ref, d_ref, g_ref, o_ref):
    o_ref[:, 0:GROUP_WIDTH] = a_ref[...].astype(o_ref.dtype)
    for n, ref in enumerate((b_ref, c_ref, d_ref)):
        x = ref[...]
        ms = jnp.mean(x * x, axis=-1, keepdims=True)
        o_ref[:, (n + 1) * GROUP_WIDTH:(n + 2) * GROUP_WIDTH] = (
            x * lax.rsqrt(ms + EPS) * g_ref[n:n + 1, :]).astype(o_ref.dtype)


def _mix_groups(o_a, o_b, o_c, o_d, g_out, *, tm=512):
    m = o_a.shape[0]
    tm = min(tm, m)
    spec = pl.BlockSpec((tm, GROUP_WIDTH), lambda i: (i, 0))
    return pl.pallas_call(
        _mix_kernel,
        grid=(m // tm,),
        in_specs=[spec, spec, spec, spec, pl.BlockSpec((3, GROUP_WIDTH), lambda i: (0, 0))],
        out_specs=pl.BlockSpec((tm, MIX_WIDTH), lambda i: (i, 0)),
        out_shape=jax.ShapeDtypeStruct((m, MIX_WIDTH), MXU_DTYPE),
        compiler_params=_compiler_params(("parallel",)),
        name="mix_groups",
    )(o_a, o_b, o_c, o_d, g_out)


def kernel(x, p, positions, w_in, w_out, g_mix, g_ffn, w_gate, w_up, w_down, w_ple_proj, w_ple_gate, g_ple,
           g_group_out, diff_g_q, diff_g_k, diff_g_sub, diff_lambda, dsa_g_q, dsa_g_k, mla_g_cq, mla_g_ckv,
           mla_w_uq, mla_w_uk, mla_w_uv, mla_g_q, mla_g_k, nsa_g_q, nsa_g_k, nsa_cmp_w1, nsa_cmp_w2, nsa_cmp_pe):
    B, S = x.shape[:2]
    M = B * S
    H = HEADS_PER_GROUP
    bf16 = MXU_DTYPE
    cc_h, ss_h = _lane_tables(positions, HEAD_DIM)
    cc_i, ss_i = _lane_tables(positions, IDX_DIM)
    cc_d, ss_d = _lane_tables(positions, DIFF_QK_DIM)
    cc_m, ss_m = _lane_tables(positions, MLA_ROPE)
    w_in_p = jnp.concatenate(
        [jnp.zeros(w_in.shape[:2] + (n,), bf16) if s is None else w_in[:, :, s:s + n].astype(bf16) for s, n in _IN_RUNS],
        axis=2)
    w_out_b, w_gate_b, w_up_b, w_down_b = (w.astype(bf16) for w in (w_out, w_gate, w_up, w_down))
    w_pg_b, w_pp_b = w_ple_gate.astype(bf16), w_ple_proj.astype(bf16)
    p_b = p.reshape(DEPTH, M, PLE_DIM).astype(bf16)
    h = x.reshape(M, D_MODEL)
    for i in range(DEPTH):
        u = _rmsnorm_rows(h, g_mix[i])
        proj = _matmul(u, w_in_p, layer=i, tm=1024, tn=512, tk=D_MODEL)
        o_a = _diff_mixer(proj, B, S, cc_d, ss_d, diff_g_q[i], diff_g_k[i], diff_g_sub[i], diff_lambda[i], i)
        o_b = _dsa_mixer(proj, B, S, cc_h, ss_h, cc_i, ss_i, dsa_g_q[i], dsa_g_k[i])
        o_c = _mla_mixer(proj, B, S, cc_m, ss_m, mla_g_cq[i], mla_g_ckv[i], mla_w_uq[i], mla_w_uk[i], mla_w_uv[i],
                         mla_g_q[i], mla_g_k[i])
        o_d = _nsa_mixer(proj, B, S, cc_h, ss_h, positions, nsa_g_q[i], nsa_g_k[i],
                         nsa_cmp_w1[i], nsa_cmp_w2[i], nsa_cmp_pe[i])
        mixed = _mix_groups(o_a, o_b, o_c, o_d, g_group_out[i])
        h = _matmul(mixed, w_out_b, layer=i, res=h, tm=1024, tn=512, tk=MIX_WIDTH)
        u = _rmsnorm_rows(h, g_ffn[i])
        act = _swiglu(u, w_gate_b, w_up_b, layer=i, tm=1024, tn=256)
        h = _matmul(act, w_down_b, layer=i, res=h, tm=512, tn=256, tk=D_FF)
        u = _rmsnorm_rows(h, g_ple[i])
        h = _ple(u, w_pg_b, p_b, w_pp_b, h, layer=i, tm=1024, tn=512)
    return h.reshape(B, S, D_MODEL)
```

```python
import functools
import math

import numpy as np
import jax
import jax.numpy as jnp
from jax import lax
from jax.experimental import pallas as pl
from jax.experimental.pallas import tpu as pltpu

D_MODEL = 4096
DEPTH = 4
HEAD_DIM = 128
N_GROUPS = 4
HEADS_PER_GROUP = D_MODEL // HEAD_DIM // N_GROUPS
GROUP_WIDTH = HEADS_PER_GROUP * HEAD_DIM
MIX_WIDTH = N_GROUPS * GROUP_WIDTH
D_FF = ((8 * D_MODEL + 3 * 256 - 1) // (3 * 256)) * 256
PLE_DIM = 256
ROPE_THETA = 10000.0
EPS = 1e-6
Q_BLOCK = 128
NEG_INF = -1e30
MASK_BIAS = -2.0 ** 100
POS_INF = 1e30

DIFF_QK_DIM = HEAD_DIM // 2
DIFF_V_DIM = HEAD_DIM
IDX_HEADS = 16
IDX_DIM = 64
DSA_TOPK = 256
MLA_Q_RANK = 1024
MLA_KV_RANK = 512
MLA_NOPE = 128
MLA_ROPE = 64
MLA_V = HEAD_DIM
NSA_KV_HEADS = 2
NSA_Q_PER_KV = HEADS_PER_GROUP // NSA_KV_HEADS
CMP_LEN = 32
CMP_STRIDE = 16
CMP_HIDDEN = HEAD_DIM
SEL_BLOCK = 64
SEL_TOPN = 16
SEL_INIT = 1
SEL_LOCAL = 2
SEL_Q_BLOCK = 32
WINDOW = 512

IN_SIZES = (
    HEADS_PER_GROUP * 2 * DIFF_QK_DIM, HEADS_PER_GROUP * 2 * DIFF_QK_DIM, HEADS_PER_GROUP * DIFF_V_DIM,
    GROUP_WIDTH, HEAD_DIM, HEAD_DIM, IDX_HEADS * IDX_DIM, IDX_DIM, IDX_HEADS,
    MLA_Q_RANK, MLA_KV_RANK, MLA_ROPE,
    GROUP_WIDTH, 3 * 2 * NSA_KV_HEADS * HEAD_DIM, 3 * HEADS_PER_GROUP,
)
IN_WIDTH = sum(IN_SIZES)

V7X_VMEM_LIMIT_BYTES = 56 * 1024 * 1024
LANE = 128
MXU_DTYPE = getattr(jnp, "bfloat16")


def _round_up(n, m):
    return (n + m - 1) // m * m


def _compiler_params(semantics):
    return pltpu.CompilerParams(dimension_semantics=semantics, vmem_limit_bytes=V7X_VMEM_LIMIT_BYTES)


def _rmsnorm_rows_kernel(x_ref, g_ref, o_ref):
    x = x_ref[...]
    ms = jnp.mean(x * x, axis=-1, keepdims=True)
    o_ref[...] = (x * lax.rsqrt(ms + EPS) * g_ref[...]).astype(o_ref.dtype)


def _rmsnorm_rows(x, g, *, tm=256):
    m, d = x.shape
    out_dtype = MXU_DTYPE
    return pl.pallas_call(
        _rmsnorm_rows_kernel,
        grid=(m // tm,),
        in_specs=[pl.BlockSpec((tm, d), lambda i: (i, 0)), pl.BlockSpec((1, d), lambda i: (0, 0))],
        out_specs=pl.BlockSpec((tm, d), lambda i: (i, 0)),
        out_shape=jax.ShapeDtypeStruct((m, d), out_dtype),
        compiler_params=_compiler_params(("parallel",)),
        name="rmsnorm_rows",
    )(x, g.reshape(1, d))


def _mm_kernel(a_ref, w_ref, o_ref, acc_ref, *, nk):
    k = pl.program_id(2)

    @pl.when(k == 0)
    def _():
        acc_ref[...] = jnp.zeros_like(acc_ref)

    acc_ref[...] += jnp.dot(a_ref[...], w_ref[...], preferred_element_type=jnp.float32)

    @pl.when(k == nk - 1)
    def _():
        o_ref[...] = acc_ref[...].astype(o_ref.dtype)


def _mm_res_kernel(a_ref, w_ref, r_ref, o_ref, acc_ref, *, nk):
    k = pl.program_id(2)

    @pl.when(k == 0)
    def _():
        acc_ref[...] = jnp.zeros_like(acc_ref)

    acc_ref[...] += jnp.dot(a_ref[...], w_ref[...], preferred_element_type=jnp.float32)

    @pl.when(k == nk - 1)
    def _():
        o_ref[...] = (r_ref[...] + acc_ref[...]).astype(o_ref.dtype)


def _layer_spec(arr, block, index_map, layer):
    if arr.ndim == 2:
        return pl.BlockSpec(block, index_map)
    return pl.BlockSpec((None,) + block, lambda *g: (layer,) + index_map(*g))


def _matmul(a, w, *, res=None, layer=None, tm, tn, tk, out_dtype=jnp.float32):
    m, kdim = a.shape
    n = w.shape[-1]
    nk = kdim // tk
    assert m % tm == 0 and n % tn == 0 and kdim % tk == 0
    in_specs = [pl.BlockSpec((tm, tk), lambda i, j, k: (i, k)), _layer_spec(w, (tk, tn), lambda i, j, k: (k, j), layer)]
    args = [a, w]
    if res is None:
        body = functools.partial(_mm_kernel, nk=nk)
    else:
        body = functools.partial(_mm_res_kernel, nk=nk)
        in_specs.append(pl.BlockSpec((tm, tn), lambda i, j, k: (i, j)))
        args.append(res)
    return pl.pallas_call(
        body,
        grid=(m // tm, n // tn, nk),
        in_specs=in_specs,
        out_specs=pl.BlockSpec((tm, tn), lambda i, j, k: (i, j)),
        out_shape=jax.ShapeDtypeStruct((m, n), out_dtype),
        scratch_shapes=[pltpu.VMEM((tm, tn), jnp.float32)],
        compiler_params=_compiler_params(("parallel", "parallel", "arbitrary")),
        name="matmul_res" if res is not None else "matmul",
    )(*args)


def _swiglu_kernel(a_ref, wg_ref, wu_ref, o_ref):
    a = a_ref[...]
    g = jnp.dot(a, wg_ref[...], preferred_element_type=jnp.float32)
    u = jnp.dot(a, wu_ref[...], preferred_element_type=jnp.float32)
    o_ref[...] = (g * jax.nn.sigmoid(g) * u).astype(o_ref.dtype)


def _swiglu(a, wg, wu, *, layer, tm, tn):
    m, kdim = a.shape
    n = wg.shape[-1]
    return pl.pallas_call(
        _swiglu_kernel,
        grid=(m // tm, n // tn),
        in_specs=[pl.BlockSpec((tm, kdim), lambda i, j: (i, 0)),
                  _layer_spec(wg, (kdim, tn), lambda i, j: (0, j), layer),
                  _layer_spec(wu, (kdim, tn), lambda i, j: (0, j), layer)],
        out_specs=pl.BlockSpec((tm, tn), lambda i, j: (i, j)),
        out_shape=jax.ShapeDtypeStruct((m, n), MXU_DTYPE),
        compiler_params=_compiler_params(("parallel", "parallel")),
        name="swiglu",
    )(a, wg, wu)


def _ple_kernel(a_ref, wg_ref, p_ref, wp_ref, r_ref, o_ref):
    gate = jax.nn.sigmoid(jnp.dot(a_ref[...], wg_ref[...], preferred_element_type=jnp.float32))
    emb = jnp.dot(p_ref[...], wp_ref[...], preferred_element_type=jnp.float32)
    o_ref[...] = r_ref[...] + gate * emb


def _ple(a, wg, p, wp, res, *, layer, tm, tn):
    m, kdim = a.shape
    n = wg.shape[-1]
    pdim = p.shape[-1]
    return pl.pallas_call(
        _ple_kernel,
        grid=(m // tm, n // tn),
        in_specs=[pl.BlockSpec((tm, kdim), lambda i, j: (i, 0)),
                  _layer_spec(wg, (kdim, tn), lambda i, j: (0, j), layer),
                  _layer_spec(p, (tm, pdim), lambda i, j: (i, 0), layer),
                  _layer_spec(wp, (pdim, tn), lambda i, j: (0, j), layer),
                  pl.BlockSpec((tm, tn), lambda i, j: (i, j))],
        out_specs=pl.BlockSpec((tm, tn), lambda i, j: (i, j)),
        out_shape=jax.ShapeDtypeStruct((m, n), jnp.float32),
        compiler_params=_compiler_params(("parallel", "parallel")),
        name="ple",
    )(a, wg, p, wp, res)


def _rope_tables(positions, dim):
    inv_freq = ROPE_THETA ** (-jnp.arange(0, dim, 2, dtype=jnp.float32) / dim)
    ang = positions.astype(jnp.float32)[..., None] * inv_freq
    return jnp.cos(ang), jnp.sin(ang)


_SEG_ORDER = ("a_q", "a_k", "a_v", "b_q", "b_iq", "c_q", "d_q", "d_kv", "c_kv", "b_k", "b_v")
_SEG_NAMES = ("a_q", "a_k", "a_v", "b_q", "b_k", "b_v", "b_iq", "b_ik", "b_iw", "c_q", "c_kv", "c_kr", "d_q", "d_kv", "d_g")
_ORIG_START = dict(zip(_SEG_NAMES, [int(o) for o in np.cumsum((0,) + IN_SIZES[:-1])]))
_ORIG_SIZE = dict(zip(_SEG_NAMES, IN_SIZES))
_GATES_PER_KV_GROUP = 3 * NSA_Q_PER_KV


def _build_in_layout():
    col = {}
    perm = []
    for name in _SEG_ORDER:
        col[name] = len(perm)
        perm.extend(range(_ORIG_START[name], _ORIG_START[name] + _ORIG_SIZE[name]))
    zero = IN_WIDTH
    col["narrow"] = len(perm)
    blk0 = list(range(_ORIG_START["b_ik"], _ORIG_START["b_ik"] + IDX_DIM))
    blk0 += list(range(_ORIG_START["d_g"], _ORIG_START["d_g"] + _GATES_PER_KV_GROUP))
    blk0 += list(range(_ORIG_START["b_iw"], _ORIG_START["b_iw"] + IDX_HEADS))
    blk0 += [zero] * (LANE - len(blk0))
    blk1 = list(range(_ORIG_START["c_kr"], _ORIG_START["c_kr"] + MLA_ROPE))
    blk1 += list(range(_ORIG_START["d_g"] + _GATES_PER_KV_GROUP, _ORIG_START["d_g"] + 2 * _GATES_PER_KV_GROUP))
    blk1 += [zero] * (LANE - len(blk1))
    perm.extend(blk0 + blk1)
    width = _round_up(len(perm), 512)
    perm.extend([zero] * (width - len(perm)))
    return col, np.asarray(perm, np.int32), width


_COL, _IN_PERM, IN_WIDTH_PAD = _build_in_layout()


def _runs(perm):
    runs = []
    for c in perm.tolist():
        src = None if c == IN_WIDTH else c
        if runs and ((src is None and runs[-1][0] is None)
                     or (src is not None and runs[-1][0] is not None and runs[-1][0] + runs[-1][1] == src)):
            runs[-1][1] += 1
        else:
            runs.append([src, 1])
    return [tuple(r) for r in runs]


_IN_RUNS = _runs(_IN_PERM)
_GATE_LANE0 = IDX_DIM
_IW_LANE0 = IDX_DIM + _GATES_PER_KV_GROUP


def _rope_lane_tables(cos, sin):
    return jnp.concatenate([cos, cos], axis=-1), jnp.concatenate([-sin, sin], axis=-1)


def _lane_tables(positions, dim):
    cc, ss = _rope_lane_tables(*_rope_tables(positions, dim))
    return tuple(jnp.tile(t.reshape(-1, dim), (1, LANE // dim)) for t in (cc, ss))


def _seg64_sum(y):
    r = lax.broadcasted_iota(jnp.int32, (LANE, LANE), 0) // 64
    c = lax.broadcasted_iota(jnp.int32, (LANE, LANE), 1) // 64
    bd = jnp.where(r == c, 1.0, 0.0).astype(MXU_DTYPE)
    hi = y.astype(MXU_DTYPE)
    lo = (y - hi.astype(jnp.float32)).astype(MXU_DTYPE)
    return (jnp.dot(hi, bd, preferred_element_type=jnp.float32)
            + jnp.dot(lo, bd, preferred_element_type=jnp.float32))


def _prep_kernel(x_ref, g_ref, cc_ref, ss_ref, o_ref, *, n_blocks, seg, norm, scale):
    cc, ss, g = cc_ref[...], ss_ref[...], g_ref[...]
    lane = lax.broadcasted_iota(jnp.int32, cc.shape, 1)
    for h in range(n_blocks):
        x = x_ref[:, h * LANE:(h + 1) * LANE]
        if norm:
            if seg == LANE:
                ms = jnp.mean(x * x, axis=-1, keepdims=True)
            else:
                ms = _seg64_sum(x * x) * (1.0 / seg)
            x = x * lax.rsqrt(ms + EPS) * g
        if seg == LANE:
            partner = pltpu.roll(x, LANE // 2, axis=1)
        else:
            partner = jnp.where((lane % seg) < seg // 2, pltpu.roll(x, LANE - seg // 2, axis=1),
                                pltpu.roll(x, seg // 2, axis=1))
        y = x * cc + partner * ss
        if scale != 1.0:
            y = y * scale
        o_ref[:, h * LANE:(h + 1) * LANE] = y.astype(o_ref.dtype)


def _prep_heads(proj, col0, n_blocks, gain, cc, ss, *, seg, norm, scale=1.0, tm=256):
    m = proj.shape[0]
    tm = min(tm, m)
    width = LANE * n_blocks
    cb = col0 // width
    assert cb * width == col0
    return pl.pallas_call(
        functools.partial(_prep_kernel, n_blocks=n_blocks, seg=seg, norm=norm, scale=scale),
        grid=(m // tm,),
        in_specs=[pl.BlockSpec((tm, width), lambda i: (i, cb)),
                  pl.BlockSpec((1, LANE), lambda i: (0, 0)),
                  pl.BlockSpec((tm, LANE), lambda i: (i, 0)),
                  pl.BlockSpec((tm, LANE), lambda i: (i, 0))],
        out_specs=pl.BlockSpec((tm, width), lambda i: (i, 0)),
        out_shape=jax.ShapeDtypeStruct((m, width), MXU_DTYPE),
        compiler_params=_compiler_params(("parallel",)),
        name="prep_heads",
    )(proj, gain, cc, ss)


def _dot_nt(a, b):
    return lax.dot_general(a, b, (((1,), (1,)), ((), ())), preferred_element_type=jnp.float32)


LOG2E = math.log2(math.e)


def _softmax_step(s, mask, m, l, acc, v, *, rows_may_be_empty=False):
    if mask is not None:
        s = jnp.where(mask, s, NEG_INF)
    m_new = jnp.maximum(m, jnp.max(s, axis=-1, keepdims=True))
    alpha = jnp.exp2(m - m_new)
    p = jnp.exp2(s - m_new)
    if rows_may_be_empty:
        p = jnp.where(mask, p, 0.0)
    l = alpha * l + jnp.sum(p, axis=-1, keepdims=True)
    acc = alpha * acc + jnp.dot(p.astype(MXU_DTYPE), v, preferred_element_type=jnp.float32)
    return m_new, l, acc


def _softmax_step_t(s_t, m, l, acc_t, v_t):
    m_new = jnp.maximum(m, jnp.max(s_t, axis=0, keepdims=True))
    alpha = jnp.exp2(m - m_new)
    p = jnp.exp2(s_t - m_new)
    l = alpha * l + jnp.sum(p, axis=0, keepdims=True)
    acc_t = alpha * acc_t + jnp.dot(v_t, p.astype(MXU_DTYPE), preferred_element_type=jnp.float32)
    return m_new, l, acc_t


def _softmax_finish(l, acc):
    return acc * (1.0 / jnp.maximum(l, 1e-30))


def _nsa_cmp_kernel(x_ref, pe_ref, w1_ref, w2_ref, gk_ref, cc_ref, ss_ref, kc_ref, vc_ref):
    nch = x_ref.shape[2]
    for kv in range(2):
        x = x_ref[kv, 0]
        lo = jnp.dot((x + pe_ref[kv, 0:1, :]).astype(MXU_DTYPE), w1_ref[kv, 0], preferred_element_type=jnp.float32)
        hi = jnp.dot((x + pe_ref[kv, 1:2, :]).astype(MXU_DTYPE), w1_ref[kv, 1], preferred_element_type=jnp.float32)
        hid = lo + pltpu.roll(hi, nch - 1, axis=0)
        comp = jnp.dot(jax.nn.gelu(hid).astype(MXU_DTYPE), w2_ref[kv], preferred_element_type=jnp.float32)
        if kv == 0:
            ms = jnp.mean(comp * comp, axis=-1, keepdims=True)
            y = comp * lax.rsqrt(ms + EPS) * gk_ref[...]
            y = y * cc_ref[0] + pltpu.roll(y, LANE // 2, axis=1) * ss_ref[0]
            kc_ref[0] = y.astype(kc_ref.dtype)
        else:
            vc_ref[0] = comp.astype(vc_ref.dtype)


def _nsa_compress(craw, pe2, w1, w2, gk, cc_c, ss_c, *, n_batch):
    _, gb, nch, width = craw.shape
    return pl.pallas_call(
        _nsa_cmp_kernel,
        grid=(gb,),
        in_specs=[pl.BlockSpec((2, 1, nch, width), lambda n: (0, n, 0, 0)),
                  pl.BlockSpec((2, 2, width), lambda n: (0, 0, 0)),
                  pl.BlockSpec((2, 2, width, CMP_HIDDEN), lambda n: (0, 0, 0, 0)),
                  pl.BlockSpec((2, CMP_HIDDEN, HEAD_DIM), lambda n: (0, 0, 0)),
                  pl.BlockSpec((1, HEAD_DIM), lambda n: (0, 0)),
                  pl.BlockSpec((1, nch, HEAD_DIM), lambda n: (n % n_batch, 0, 0)),
                  pl.BlockSpec((1, nch, HEAD_DIM), lambda n: (n % n_batch, 0, 0))],
        out_specs=[pl.BlockSpec((1, nch, HEAD_DIM), lambda n: (n, 0, 0)),
                   pl.BlockSpec((1, nch, HEAD_DIM), lambda n: (n, 0, 0))],
        out_shape=[jax.ShapeDtypeStruct((gb, nch, HEAD_DIM), MXU_DTYPE)] * 2,
        compiler_params=_compiler_params(("parallel",)),
        name="nsa_compress",
    )(craw, pe2, w1, w2, gk, cc_c, ss_c)


def _nsa_attn_kernel(q_ref, kc_ref, vc_ref, ks_ref, vs_ref, kw_ref, vw_ref, gate_ref, c2s_ref, exp_ref, o_ref, *,
                     tq, tk, n_sel, n_top):
    i = pl.program_id(2)
    R = NSA_Q_PER_KV
    rows = R * tq
    nch = kc_ref.shape[1]
    bf16 = MXU_DTYPE
    q = jnp.concatenate([q_ref[:, r * HEAD_DIM:(r + 1) * HEAD_DIM] for r in range(R)], axis=0)
    t_row = i * tq + lax.broadcasted_iota(jnp.int32, (rows, 1), 0) % tq

    s = _dot_nt(q, kc_ref[0])
    blk_end = lax.broadcasted_iota(jnp.int32, (1, nch), 1) * CMP_STRIDE + (CMP_LEN - 1)
    mask = blk_end <= t_row
    zero = jnp.zeros((rows, 1), jnp.float32)
    s = jnp.where(mask, s, NEG_INF)
    p_cmp = jnp.where(mask, jnp.exp2(s - jnp.max(s, axis=-1, keepdims=True)), 0.0)
    p_cmp = p_cmp * (1.0 / jnp.maximum(jnp.sum(p_cmp, axis=-1, keepdims=True), 1e-30))
    o_cmp = jnp.dot(p_cmp.astype(bf16), vc_ref[0], preferred_element_type=jnp.float32)

    imp_rows = jnp.dot(p_cmp.astype(bf16), c2s_ref[...], preferred_element_type=jnp.float32)
    imp = imp_rows[0:tq]
    for r in range(1, R):
        imp = imp + imp_rows[r * tq:(r + 1) * tq]
    imp_t = imp.T[0:n_sel]
    blk = lax.broadcasted_iota(jnp.int32, (n_sel, tq), 0)
    t_lane = i * tq + lax.broadcasted_iota(jnp.int32, (n_sel, tq), 1)
    dist = lax.shift_right_logical(t_lane, int(math.log2(SEL_BLOCK))) - blk
    forced = (blk < SEL_INIT) | ((dist >= 0) & (dist < SEL_LOCAL))
    val = jnp.where(forced, POS_INF, jnp.where(blk * SEL_BLOCK <= t_lane, imp_t, NEG_INF))
    rank = jnp.zeros((n_sel, tq), jnp.float32)
    for mblk in range(n_sel):
        vm = val[mblk:mblk + 1, :]
        rank = rank + jnp.where((vm > val) | ((vm == val) & (blk > mblk)), 1.0, 0.0)
    drop_t = jnp.where(rank < n_top, 0.0, 1.0)
    drop_t = jnp.concatenate([drop_t, jnp.zeros((LANE - n_sel, tq), jnp.float32)], axis=0) if n_sel < LANE else drop_t
    drop_q = drop_t.T.astype(bf16)
    drop_rows = jnp.concatenate([drop_q] * R, axis=0)

    def slc_step(j, carry, diagonal):
        off = pl.multiple_of(j * tk, tk)
        k = ks_ref[pl.ds(off, tk), :]
        v = vs_ref[pl.ds(off, tk), :].astype(bf16)
        s = _dot_nt(q, k) + jnp.dot(drop_rows, exp_ref[j], preferred_element_type=jnp.float32)
        mask = (off + lax.broadcasted_iota(jnp.int32, (1, tk), 1) <= t_row) if diagonal else None
        return _softmax_step(s, mask, *carry, v)

    init = (zero + NEG_INF, zero, jnp.zeros((rows, HEAD_DIM), jnp.float32))
    n_full = (i * tq) // tk
    carry = lax.fori_loop(0, n_full, lambda j, c: slc_step(j, c, False), init)
    _, l, acc = slc_step(n_full, carry, True)
    o_slc = _softmax_finish(l, acc)

    span = WINDOW + tq
    off = pl.multiple_of(jnp.maximum(i * tq - WINDOW, 0), tq)
    key = off + lax.broadcasted_iota(jnp.int32, (1, span), 1)
    s = _dot_nt(q, kw_ref[pl.ds(off, span), :])
    s = jnp.where(key <= t_row, jnp.where(key > t_row - WINDOW, s, NEG_INF), NEG_INF)
    _, l, acc = _softmax_step(s, None, *init, vw_ref[pl.ds(off, span), :].astype(bf16))
    o_win = _softmax_finish(l, acc)

    gate = jax.nn.sigmoid(gate_ref[...])
    for r in range(R):
        c = _GATE_LANE0 + 3 * r
        rs = slice(r * tq, (r + 1) * tq)
        o_ref[:, r * HEAD_DIM:(r + 1) * HEAD_DIM] = (gate[:, c:c + 1] * o_cmp[rs]
                                                     + gate[:, c + 1:c + 2] * o_slc[rs]
                                                     + gate[:, c + 2:c + 3] * o_win[rs])


def _nsa_mixer(proj, B, S, cc_h, ss_h, positions, g_q, g_k, cmp_w1, cmp_w2, cmp_pe):
    M = B * S
    G, R, d = NSA_KV_HEADS, NSA_Q_PER_KV, HEAD_DIM
    bf16 = MXU_DTYPE
    tq = Q_BLOCK
    nq = S // tq
    nch = S // CMP_STRIDE
    n_sel = S // SEL_BLOCK
    n_top = min(SEL_TOPN, n_sel)
    kv0 = _COL["d_kv"]

    qn = _prep_heads(proj, _COL["d_q"], HEADS_PER_GROUP, g_q.reshape(1, d), cc_h, ss_h,
                     seg=LANE, norm=True, scale=d ** -0.5 * LOG2E)
    k_slc =_prep_heads(proj, kv0 + (1 * 2 + 0) * G * d, G, g_k[1].reshape(1, d), cc_h, ss_h, seg=LANE, norm=True)
    k_win = _prep_heads(proj, kv0 + (2 * 2 + 0) * G * d, G, g_k[2].reshape(1, d), cc_h, ss_h, seg=LANE, norm=True)

    craw = proj[:, kv0:kv0 + 2 * G * d].reshape(B, S, 2, G, d).transpose(2, 3, 0, 1, 4)
    craw = craw.reshape(2, G * B, nch, CMP_STRIDE * d)
    ends = np.minimum(np.arange(nch) * CMP_STRIDE + CMP_LEN - 1, S - 1)
    cc_c, ss_c = _rope_lane_tables(*_rope_tables(positions[:, ends], d))
    pe2 = cmp_pe.reshape(2, 2, CMP_STRIDE * d)
    w1 = cmp_w1.astype(bf16).reshape(2, 2, CMP_STRIDE * d, CMP_HIDDEN)
    k_cmp, v_cmp = _nsa_compress(craw, pe2, w1, cmp_w2.astype(bf16), g_k[0].reshape(1, d), cc_c, ss_c, n_batch=B)

    starts = np.arange(nch) * CMP_STRIDE
    sel_start = np.arange(LANE) * SEL_BLOCK
    c2s = ((starts[:, None] < sel_start[None, :] + SEL_BLOCK) & (starts[:, None] + CMP_LEN > sel_start[None, :])
           & (np.arange(nch)[:, None] < nch - 1) & (np.arange(LANE)[None, :] < n_sel))
    tk = min(512, S)
    expand = (np.arange(LANE)[None, :, None] == (np.arange(S // tk)[:, None, None] * tk + np.arange(tk)[None, None, :]) // SEL_BLOCK)
    c2s = jnp.asarray(c2s, bf16)
    expand = jnp.asarray(expand * MASK_BIAS, bf16)

    narrow = _COL["narrow"] // LANE
    col_vs = (kv0 + (1 * 2 + 1) * G * d) // LANE
    col_vw = (kv0 + (2 * 2 + 1) * G * d) // LANE
    return pl.pallas_call(
        functools.partial(_nsa_attn_kernel, tq=tq, tk=tk, n_sel=n_sel, n_top=n_top),
        grid=(B, G, nq),
        in_specs=[pl.BlockSpec((tq, R * d), lambda b, g, i: (b * nq + i, g)),
                  pl.BlockSpec((1, nch, d), lambda b, g, i: (g * B + b, 0, 0)),
                  pl.BlockSpec((1, nch, d), lambda b, g, i: (g * B + b, 0, 0)),
                  pl.BlockSpec((S, d), lambda b, g, i: (b, g)),
                  pl.BlockSpec((S, d), lambda b, g, i: (b, col_vs + g)),
                  pl.BlockSpec((S, d), lambda b, g, i: (b, g)),
                  pl.BlockSpec((S, d), lambda b, g, i: (b, col_vw + g)),
                  pl.BlockSpec((tq, LANE), lambda b, g, i: (b * nq + i, narrow + g)),
                  pl.BlockSpec((nch, LANE), lambda b, g, i: (0, 0)),
                  pl.BlockSpec((S // tk, LANE, tk), lambda b, g, i: (0, 0, 0))],
        out_specs=pl.BlockSpec((tq, R * d), lambda b, g, i: (b * nq + i, g)),
        out_shape=jax.ShapeDtypeStruct((M, GROUP_WIDTH), jnp.float32),
        compiler_params=_compiler_params(("parallel", "parallel", "arbitrary")),
        name="nsa_attention",
    )(qn, k_cmp, v_cmp, k_slc, proj, k_win, proj, proj, c2s, expand)


_INT_MIN = -2 ** 31


def _sortable_key(x):
    b = lax.bitcast_convert_type(x + 0.0, jnp.int32)
    return jnp.where(b >= 0, b, b ^ 0x7FFFFFFF)


def _dsa_kernel(q_ref, k_ref, v_ref, iq_ref, ik_ref, nar_ref, o_ref, key_ref, msk_ref, vt_ref, *,
                tq, tk, n_keep, idx_bits):
    i = pl.program_id(1)
    H = HEADS_PER_GROUP
    bf16 = MXU_DTYPE
    t_lane = i * tq + lax.broadcasted_iota(jnp.int32, (1, tq), 1)
    key_sub = lax.broadcasted_iota(jnp.int32, (tq, 1), 0)
    lane = lax.broadcasted_iota(jnp.int32, (tq, LANE), 1)
    iw_t = (nar_ref[...] * (IDX_HEADS ** -0.5)).T
    w_rows = [iw_t[_IW_LANE0 + hd:_IW_LANE0 + hd + 1, :] for hd in range(IDX_HEADS)]
    iq_rows = jnp.concatenate([iq_ref[:, pr * LANE:(pr + 1) * LANE] for pr in range(IDX_HEADS // 2)], axis=0)

    def score_body(j, _):
        off = pl.multiple_of(j * tq, tq)
        ik = ik_ref[pl.ds(off, tq), :]
        ik_lo = jnp.where(lane < IDX_DIM, ik, jnp.zeros_like(ik))
        ik_hi = jnp.where(lane < IDX_DIM, jnp.zeros_like(ik), pltpu.roll(ik.astype(jnp.float32), IDX_DIM, axis=1).astype(bf16))
        rel = (jnp.maximum(_dot_nt(ik_lo, iq_rows), 0.0), jnp.maximum(_dot_nt(ik_hi, iq_rows), 0.0))
        sc = jnp.zeros((tq, tq), jnp.float32)
        for hd in range(IDX_HEADS):
            sc = sc + w_rows[hd] * rel[hd % 2][:, (hd // 2) * tq:(hd // 2 + 1) * tq]
        sc = jnp.where(off + key_sub <= t_lane, sc, NEG_INF)
        key_ref[j] = _sortable_key(sc)
        return 0

    lax.fori_loop(0, i + 1, score_body, 0)

    def count(pred):
        def body(j, part):
            return part + jnp.where(pred(key_ref[j], j * tq), 1.0, 0.0)
        part = lax.fori_loop(0, i + 1, body, jnp.zeros((tq, tq), jnp.float32))
        return jnp.sum(part, axis=0, keepdims=True)

    c0 = count(lambda kk, off: kk >= 0)
    thr = jnp.where(c0 >= n_keep, 0, _INT_MIN).astype(jnp.int32)

    def thr_body(it, thr):
        cand = thr | lax.shift_left(jnp.int32(1), 30 - it)
        c = count(lambda kk, off: kk >= cand)
        return jnp.where(c >= n_keep, cand, thr)

    thr = lax.fori_loop(0, 31, thr_body, thr)

    n_ge = count(lambda kk, off: kk >= thr)

    def cut_search():
        need = n_keep - count(lambda kk, off: kk > thr)

        def cut_body(it, cut):
            cand = cut | lax.shift_left(jnp.int32(1), idx_bits - 1 - it)
            c = count(lambda kk, off: (kk == thr) & (off + key_sub < cand))
            return jnp.where(c < need, cand, cut)

        return lax.fori_loop(0, idx_bits, cut_body, jnp.zeros((1, tq), jnp.int32))

    cut = lax.cond(jnp.max(n_ge) > n_keep, cut_search, lambda: jnp.full((1, tq), 2 ** idx_bits - 1, jnp.int32))

    tpc = tk // tq
    n_chunks = (i + tpc) // tpc

    def mask_body(j, _):
        @pl.when(j <= i)
        def _():
            kk = key_ref[j]
            kidx = j * tq + key_sub
            sel = (kk > thr) | ((kk == thr) & (kidx <= cut))
            msk_ref[j] = jnp.where(kidx <= t_lane, jnp.where(sel, 0.0, NEG_INF), NEG_INF)

        @pl.when(j > i)
        def _():
            msk_ref[j] = jnp.full((tq, tq), NEG_INF, jnp.float32)

        return 0

    lax.fori_loop(0, n_chunks * tpc, mask_body, 0)

    @pl.when(i == 0)
    def _():
        for c in range(vt_ref.shape[0]):
            for u in range(tpc):
                r0 = (c * tpc + u) * tq
                vt_ref[c, :, u * tq:(u + 1) * tq] = v_ref[r0:r0 + tq, :].T.astype(bf16)

    hpp = 4
    rows = hpp * tq
    zero = jnp.zeros((1, rows), jnp.float32)
    qs = [jnp.concatenate([q_ref[:, (hc * hpp + r) * HEAD_DIM:(hc * hpp + r + 1) * HEAD_DIM] for r in range(hpp)], axis=0)
          for hc in range(H // hpp)]

    def att_body(j, carry):
        off = pl.multiple_of(j * tk, tk)
        k = k_ref[pl.ds(off, tk), :]
        bias1 = jnp.concatenate([msk_ref[j * tpc + u] for u in range(tpc)], axis=0)
        bias = jnp.concatenate([bias1] * hpp, axis=1)
        return tuple(_softmax_step_t(_dot_nt(k, q) + bias, *carry[c], vt_ref[j]) for c, q in enumerate(qs))

    init = tuple((zero + NEG_INF, zero, jnp.zeros((HEAD_DIM, rows), jnp.float32)) for _ in qs)
    res = lax.fori_loop(0, n_chunks, att_body, init)
    for hc, (_, l, acc_t) in enumerate(res):
        o = _softmax_finish(l, acc_t).T
        for r in range(hpp):
            hh = hc * hpp + r
            o_ref[:, hh * HEAD_DIM:(hh + 1) * HEAD_DIM] = o[r * tq:(r + 1) * tq]


def _dsa_mixer(proj, B, S, cc_h, ss_h, cc_i, ss_i, g_q, g_k):
    M = B * S
    d = HEAD_DIM
    tq = Q_BLOCK
    nq = S // tq
    n_keep = min(DSA_TOPK, S // 4)
    tk = min(512, S)
    qn = _prep_heads(proj, _COL["b_q"], HEADS_PER_GROUP, g_q.reshape(1, d), cc_h, ss_h, seg=LANE, norm=True,
                     scale=d ** -0.5 * LOG2E)
    kn =_prep_heads(proj, _COL["b_k"], 1, g_k.reshape(1, d), cc_h, ss_h, seg=LANE, norm=True)
    ones = jnp.ones((1, LANE), jnp.float32)
    iqn = _prep_heads(proj, _COL["b_iq"], IDX_HEADS * IDX_DIM // LANE, ones, cc_i, ss_i, seg=IDX_DIM, norm=False,
                      scale=IDX_DIM ** -0.5)
    ikn = _prep_heads(proj, _COL["narrow"], 1, ones, cc_i, ss_i, seg=IDX_DIM, norm=False)
    narrow = _COL["narrow"] // LANE
    col_v = _COL["b_v"] // LANE
    return pl.pallas_call(
        functools.partial(_dsa_kernel, tq=tq, tk=tk, n_keep=n_keep, idx_bits=int(math.log2(S))),
        grid=(B, nq),
        in_specs=[pl.BlockSpec((tq, GROUP_WIDTH), lambda b, i: (b * nq + i, 0)),
                  pl.BlockSpec((S, d), lambda b, i: (b, 0)),
                  pl.BlockSpec((S, d), lambda b, i: (b, col_v)),
                  pl.BlockSpec((tq, IDX_HEADS * IDX_DIM), lambda b, i: (b * nq + i, 0)),
                  pl.BlockSpec((S, LANE), lambda b, i: (b, 0)),
                  pl.BlockSpec((tq, LANE), lambda b, i: (b * nq + i, narrow))],
        out_specs=pl.BlockSpec((tq, GROUP_WIDTH), lambda b, i: (b * nq + i, 0)),
        out_shape=jax.ShapeDtypeStruct((M, GROUP_WIDTH), jnp.float32),
        scratch_shapes=[pltpu.VMEM((nq, tq, tq), jnp.int32), pltpu.VMEM((nq, tq, tq), jnp.float32),
                        pltpu.VMEM((S // tk, d, tk), MXU_DTYPE)],
        compiler_params=_compiler_params(("parallel", "arbitrary")),
        name="dsa_attention",
    )(qn, kn, proj, iqn, ikn, proj)


def _causal_chains(qs, k_ref, v_ref, dk, dv, t_row, t0, tk):
    rows = qs[0].shape[0]
    zero = jnp.zeros((rows, 1), jnp.float32)

    def step(j, carry, masked):
        off = pl.multiple_of(j * tk, tk)
        mask = (off + lax.broadcasted_iota(jnp.int32, (1, tk), 1) <= t_row) if masked else None
        out = []
        for c, q in enumerate(qs):
            k = k_ref[pl.ds(off, tk), c * dk:(c + 1) * dk]
            v = v_ref[pl.ds(off, tk), c * dv:(c + 1) * dv].astype(MXU_DTYPE)
            out.append(_softmax_step(_dot_nt(q, k), mask, *carry[c], v))
        return tuple(out)

    init = tuple((zero + NEG_INF, zero, jnp.zeros((rows, dv), jnp.float32)) for _ in qs)
    n_full = t0 // tk
    res = lax.fori_loop(0, n_full, lambda j, c: step(j, c, False), init)
    res = step(n_full, res, True)
    return [_softmax_finish(l, acc) for _, l, acc in res]


def _diff_attn_kernel(q_ref, k_ref, v_ref, lam_ref, g_ref, o_ref, *, tq, tk, hb, out_scale):
    i = pl.program_id(2)
    d = HEAD_DIM
    lane = lax.broadcasted_iota(jnp.int32, (tq, d), 1)
    qs = []
    for c in range(hb):
        q = q_ref[:, c * d:(c + 1) * d]
        zeros = jnp.zeros_like(q)
        qs.append(jnp.concatenate([jnp.where(lane < DIFF_QK_DIM, q, zeros), jnp.where(lane < DIFF_QK_DIM, zeros, q)], axis=0))
    t_row = i * tq + lax.broadcasted_iota(jnp.int32, (2 * tq, 1), 0) % tq
    outs = _causal_chains(qs, k_ref, v_ref, d, d, t_row, i * tq, tk)
    for c, o2 in enumerate(outs):
        o = o2[0:tq] - lam_ref[...] * o2[tq:2 * tq]
        ms = jnp.mean(o * o, axis=-1, keepdims=True)
        o_ref[:, c * d:(c + 1) * d] = o * lax.rsqrt(ms + EPS) * g_ref[...] * out_scale


def _diff_mixer(proj, B, S, cc_d, ss_d, g_q, g_k, g_sub, lam_params, layer_idx, *, tq=256, tk=512, hb=2):
    M = B * S
    d = HEAD_DIM
    nq = S // tq
    tk = min(tk, S)
    reps = LANE // DIFF_QK_DIM
    qn = _prep_heads(proj, _COL["a_q"], HEADS_PER_GROUP, jnp.tile(g_q, reps).reshape(1, d), cc_d, ss_d,
                     seg=DIFF_QK_DIM, norm=True, scale=DIFF_QK_DIM ** -0.5 * LOG2E)
    kn = _prep_heads(proj, _COL["a_k"], HEADS_PER_GROUP, jnp.tile(g_k, reps).reshape(1, d), cc_d, ss_d,
                     seg=DIFF_QK_DIM, norm=True)
    lam_init = 0.8 - 0.6 * math.exp(-0.3 * layer_idx)
    lp = lam_params.astype(jnp.float32)
    lam = jnp.exp(jnp.sum(lp[0] * lp[1])) - jnp.exp(jnp.sum(lp[2] * lp[3])) + lam_init
    col_v = _COL["a_v"] // (hb * d)
    return pl.pallas_call(
        functools.partial(_diff_attn_kernel, tq=tq, tk=tk, hb=hb, out_scale=1.0 - lam_init),
        grid=(B, HEADS_PER_GROUP // hb, nq),
        in_specs=[pl.BlockSpec((tq, hb * d), lambda b, h, i: (b * nq + i, h)),
                  pl.BlockSpec((S, hb * d), lambda b, h, i: (b, h)),
                  pl.BlockSpec((S, hb * d), lambda b, h, i: (b, col_v + h)),
                  pl.BlockSpec((1, d), lambda b, h, i: (0, 0)),
                  pl.BlockSpec((1, d), lambda b, h, i: (0, 0))],
        out_specs=pl.BlockSpec((tq, hb * d), lambda b, h, i: (b * nq + i, h)),
        out_shape=jax.ShapeDtypeStruct((M, GROUP_WIDTH), jnp.float32),
        compiler_params=_compiler_params(("parallel", "parallel", "arbitrary")),
        name="diff_attention",
    )(qn, kn, proj, jnp.full((1, d), lam, jnp.float32), g_sub.reshape(1, d))


MLA_QK_PAD = 2 * LANE


def _rmsnorm_cols_kernel(x_ref, g_ref, o_ref):
    x = x_ref[...]
    ms = jnp.mean(x * x, axis=-1, keepdims=True)
    o_ref[...] = (x * lax.rsqrt(ms + EPS) * g_ref[...]).astype(o_ref.dtype)


def _rmsnorm_cols(proj, col0, width, g, *, tm=512):
    m = proj.shape[0]
    tm = min(tm, m)
    cb = col0 // width
    assert cb * width == col0
    return pl.pallas_call(
        _rmsnorm_cols_kernel,
        grid=(m // tm,),
        in_specs=[pl.BlockSpec((tm, width), lambda i: (i, cb)), pl.BlockSpec((1, width), lambda i: (0, 0))],
        out_specs=pl.BlockSpec((tm, width), lambda i: (i, 0)),
        out_shape=jax.ShapeDtypeStruct((m, width), MXU_DTYPE),
        compiler_params=_compiler_params(("parallel",)),
        name="rmsnorm_cols",
    )(proj, g.reshape(1, width))


def _mla_prep_kernel(a_ref, b_ref, g_ref, cc_ref, ss_ref, o_ref, *, a_stride, b_col0, b_stride, scale):
    cc, ss = cc_ref[...], ss_ref[...]
    lane = lax.broadcasted_iota(jnp.int32, cc.shape, 1)
    for h in range(HEADS_PER_GROUP):
        a = a_ref[:, h * a_stride:h * a_stride + LANE]
        b = jnp.where(lane < MLA_ROPE, b_ref[:, b_col0 + h * b_stride:b_col0 + h * b_stride + LANE], 0.0)
        ms = (jnp.sum(a * a, axis=-1, keepdims=True) + jnp.sum(b * b, axis=-1, keepdims=True)) * (1.0 / (MLA_NOPE + MLA_ROPE))
        r = lax.rsqrt(ms + EPS)
        ya = a * r * g_ref[:, 0:LANE]
        yb = b * r * g_ref[:, LANE:2 * LANE]
        partner = jnp.where((lane % MLA_ROPE) < MLA_ROPE // 2, pltpu.roll(yb, LANE - MLA_ROPE // 2, axis=1),
                            pltpu.roll(yb, MLA_ROPE // 2, axis=1))
        yb = yb * cc + partner * ss
        o_ref[:, h * MLA_QK_PAD:h * MLA_QK_PAD + LANE] = (ya * scale).astype(o_ref.dtype)
        o_ref[:, h * MLA_QK_PAD + LANE:(h + 1) * MLA_QK_PAD] = (yb * scale).astype(o_ref.dtype)


def _mla_prep(a_arr, a_width, a_stride, b_arr, b_block, b_width, b_col0, b_stride, gain, cc, ss, *, scale=1.0, tm=256):
    m = a_arr.shape[0]
    tm = min(tm, m)
    g2 = jnp.concatenate([gain, jnp.zeros((MLA_QK_PAD - gain.shape[0],), gain.dtype)]).reshape(1, MLA_QK_PAD)
    return pl.pallas_call(
        functools.partial(_mla_prep_kernel, a_stride=a_stride, b_col0=b_col0, b_stride=b_stride, scale=scale),
        grid=(m // tm,),
        in_specs=[pl.BlockSpec((tm, a_width), lambda i: (i, 0)),
                  pl.BlockSpec((tm, b_width), lambda i: (i, b_block)),
                  pl.BlockSpec((1, MLA_QK_PAD), lambda i: (0, 0)),
                  pl.BlockSpec((tm, LANE), lambda i: (i, 0)),
                  pl.BlockSpec((tm, LANE), lambda i: (i, 0))],
        out_specs=pl.BlockSpec((tm, HEADS_PER_GROUP * MLA_QK_PAD), lambda i: (i, 0)),
        out_shape=jax.ShapeDtypeStruct((m, HEADS_PER_GROUP * MLA_QK_PAD), MXU_DTYPE),
        compiler_params=_compiler_params(("parallel",)),
        name="mla_prep",
    )(a_arr, b_arr, g2, cc, ss)


def _mla_attn_kernel(q_ref, k_ref, v_ref, o_ref, *, tq, tk, hb):
    i = pl.program_id(2)
    d = HEAD_DIM
    t_row = i * tq + lax.broadcasted_iota(jnp.int32, (tq, 1), 0)
    qs = [q_ref[:, c * MLA_QK_PAD:(c + 1) * MLA_QK_PAD] for c in range(hb)]
    outs = _causal_chains(qs, k_ref, v_ref, MLA_QK_PAD, d, t_row, i * tq, tk)
    for c, o in enumerate(outs):
        o_ref[:, c * d:(c + 1) * d] = o


def _mla_mixer(proj, B, S, cc_m, ss_m, g_cq, g_ckv, w_uq, w_uk, w_uv, g_q, g_k, *, tq=512, tk=512, hb=2):
    M = B * S
    H, d = HEADS_PER_GROUP, HEAD_DIM
    tq = min(tq, S)
    tk = min(tk, S)
    nq = S // tq
    dqk = MLA_NOPE + MLA_ROPE
    w_q = jnp.pad(w_uq.astype(MXU_DTYPE).reshape(MLA_Q_RANK, H, dqk), ((0, 0), (0, 0), (0, MLA_QK_PAD - dqk)))
    w_q = w_q.reshape(MLA_Q_RANK, H * MLA_QK_PAD)
    w_kv = jnp.concatenate([w_uk, w_uv], axis=1).astype(MXU_DTYPE)
    cq = _rmsnorm_cols(proj, _COL["c_q"], MLA_Q_RANK, g_cq)
    ckv = _rmsnorm_cols(proj, _COL["c_kv"], MLA_KV_RANK, g_ckv)
    q_up = _matmul(cq, w_q, tm=min(1024, M), tn=1024, tk=MLA_Q_RANK)
    kv_up = _matmul(ckv, w_kv, tm=min(1024, M), tn=1024, tk=MLA_KV_RANK)
    qn = _mla_prep(q_up, H * MLA_QK_PAD, MLA_QK_PAD, q_up, 0, H * MLA_QK_PAD, LANE, MLA_QK_PAD, g_q, cc_m, ss_m,
                   scale=dqk ** -0.5 * LOG2E)
    kn = _mla_prep(kv_up, H * MLA_NOPE, LANE, proj, _COL["narrow"] // LANE + 1, LANE, 0, 0, g_k, cc_m, ss_m)
    return pl.pallas_call(
        functools.partial(_mla_attn_kernel, tq=tq, tk=tk, hb=hb),
        grid=(B, H // hb, nq),
        in_specs=[pl.BlockSpec((tq, hb * MLA_QK_PAD), lambda b, h, i: (b * nq + i, h)),
                  pl.BlockSpec((S, hb * MLA_QK_PAD), lambda b, h, i: (b, h)),
                  pl.BlockSpec((S, hb * d), lambda b, h, i: (b, H // hb + h))],
        out_specs=pl.BlockSpec((tq, hb * d), lambda b, h, i: (b * nq + i, h)),
        out_shape=jax.ShapeDtypeStruct((M, GROUP_WIDTH), jnp.float32),
        compiler_params=_compiler_params(("parallel", "parallel", "arbitrary")),
        name="mla_attention",
    )(qn, kn, kv_up)


def _mix_kernel(a_ref, b_ref, c_ref, d_ref, g_ref, o_ref):
    o_ref[:, 0:GROUP_WIDTH] = a_ref[...].astype(o_ref.dtype)
    for n, ref in enumerate((b_ref, c_ref, d_ref)):
        x = ref[...]
        ms = jnp.mean(x * x, axis=-1, keepdims=True)
        o_ref[:, (n + 1) * GROUP_WIDTH:(n + 2) * GROUP_WIDTH] = (
            x * lax.rsqrt(ms + EPS) * g_ref[n:n + 1, :]).astype(o_ref.dtype)


def _mix_groups(o_a, o_b, o_c, o_d, g_out, *, tm=512):
    m = o_a.shape[0]
    tm = min(tm, m)
    spec = pl.BlockSpec((tm, GROUP_WIDTH), lambda i: (i, 0))
    return pl.pallas_call(
        _mix_kernel,
        grid=(m // tm,),
        in_specs=[spec, spec, spec, spec, pl.BlockSpec((3, GROUP_WIDTH), lambda i: (0, 0))],
        out_specs=pl.BlockSpec((tm, MIX_WIDTH), lambda i: (i, 0)),
        out_shape=jax.ShapeDtypeStruct((m, MIX_WIDTH), MXU_DTYPE),
        compiler_params=_compiler_params(("parallel",)),
        name="mix_groups",
    )(o_a, o_b, o_c, o_d, g_out)


def kernel(x, p, positions, w_in, w_out, g_mix, g_ffn, w_gate, w_up, w_down, w_ple_proj, w_ple_gate, g_ple,
           g_group_out, diff_g_q, diff_g_k, diff_g_sub, diff_lambda, dsa_g_q, dsa_g_k, mla_g_cq, mla_g_ckv,
           mla_w_uq, mla_w_uk, mla_w_uv, mla_g_q, mla_g_k, nsa_g_q, nsa_g_k, nsa_cmp_w1, nsa_cmp_w2, nsa_cmp_pe):
    B, S = x.shape[:2]
    M = B * S
    H = HEADS_PER_GROUP
    bf16 = MXU_DTYPE
    cc_h, ss_h = _lane_tables(positions, HEAD_DIM)
    cc_i, ss_i = _lane_tables(positions, IDX_DIM)
    cc_d, ss_d = _lane_tables(positions, DIFF_QK_DIM)
    cc_m, ss_m = _lane_tables(positions, MLA_ROPE)
    w_in_p = jnp.concatenate(
        [jnp.zeros(w_in.shape[:2] + (n,), bf16) if s is None else w_in[:, :, s:s + n].astype(bf16) for s, n in _IN_RUNS],
        axis=2)
    w_out_b, w_gate_b, w_up_b, w_down_b = (w.astype(bf16) for w in (w_out, w_gate, w_up, w_down))
    w_pg_b, w_pp_b = w_ple_gate.astype(bf16), w_ple_proj.astype(bf16)
    p_b = p.reshape(DEPTH, M, PLE_DIM).astype(bf16)
    h = x.reshape(M, D_MODEL)
    for i in range(DEPTH):
        u = _rmsnorm_rows(h, g_mix[i])
        proj = _matmul(u, w_in_p, layer=i, tm=1024, tn=512, tk=D_MODEL)
        o_a = _diff_mixer(proj, B, S, cc_d, ss_d, diff_g_q[i], diff_g_k[i], diff_g_sub[i], diff_lambda[i], i)
        o_b = _dsa_mixer(proj, B, S, cc_h, ss_h, cc_i, ss_i, dsa_g_q[i], dsa_g_k[i])
        o_c = _mla_mixer(proj, B, S, cc_m, ss_m, mla_g_cq[i], mla_g_ckv[i], mla_w_uq[i], mla_w_uk[i], mla_w_uv[i],
                         mla_g_q[i], mla_g_k[i])
        o_d = _nsa_mixer(proj, B, S, cc_h, ss_h, positions, nsa_g_q[i], nsa_g_k[i],
                         nsa_cmp_w1[i], nsa_cmp_w2[i], nsa_cmp_pe[i])
        mixed = _mix_groups(o_a, o_b, o_c, o_d, g_group_out[i])
        h = _matmul(mixed, w_out_b, layer=i, res=h, tm=1024, tn=512, tk=MIX_WIDTH)
        u = _rmsnorm_rows(h, g_ffn[i])
        act = _swiglu(u, w_gate_b, w_up_b, layer=i, tm=1024, tn=256)
        h = _matmul(act, w_down_b, layer=i, res=h, tm=512, tn=256, tk=D_FF)
        u = _rmsnorm_rows(h, g_ple[i])
        h = _ple(u, w_pg_b, p_b, w_pp_b, h, layer=i, tm=1024, tn=512)
    return h.reshape(B, S, D_MODEL)
```

```python
import functools
import math

import numpy as np
import jax
import jax.numpy as jnp
from jax import lax
from jax.experimental import pallas as pl
from jax.experimental.pallas import tpu as pltpu

D_MODEL = 4096
DEPTH = 4
HEAD_DIM = 128
N_GROUPS = 4
HEADS_PER_GROUP = D_MODEL // HEAD_DIM // N_GROUPS
GROUP_WIDTH = HEADS_PER_GROUP * HEAD_DIM
MIX_WIDTH = N_GROUPS * GROUP_WIDTH
D_FF = ((8 * D_MODEL + 3 * 256 - 1) // (3 * 256)) * 256
PLE_DIM = 256
ROPE_THETA = 10000.0
EPS = 1e-6
Q_BLOCK = 128
NEG_INF = -1e30
MASK_BIAS = -2.0 ** 100
POS_INF = 1e30

DIFF_QK_DIM = HEAD_DIM // 2
DIFF_V_DIM = HEAD_DIM
IDX_HEADS = 16
IDX_DIM = 64
DSA_TOPK = 256
MLA_Q_RANK = 1024
MLA_KV_RANK = 512
MLA_NOPE = 128
MLA_ROPE = 64
MLA_V = HEAD_DIM
NSA_KV_HEADS = 2
NSA_Q_PER_KV = HEADS_PER_GROUP // NSA_KV_HEADS
CMP_LEN = 32
CMP_STRIDE = 16
CMP_HIDDEN = HEAD_DIM
SEL_BLOCK = 64
SEL_TOPN = 16
SEL_INIT = 1
SEL_LOCAL = 2
SEL_Q_BLOCK = 32
WINDOW = 512

IN_SIZES = (
    HEADS_PER_GROUP * 2 * DIFF_QK_DIM, HEADS_PER_GROUP * 2 * DIFF_QK_DIM, HEADS_PER_GROUP * DIFF_V_DIM,
    GROUP_WIDTH, HEAD_DIM, HEAD_DIM, IDX_HEADS * IDX_DIM, IDX_DIM, IDX_HEADS,
    MLA_Q_RANK, MLA_KV_RANK, MLA_ROPE,
    GROUP_WIDTH, 3 * 2 * NSA_KV_HEADS * HEAD_DIM, 3 * HEADS_PER_GROUP,
)
IN_WIDTH = sum(IN_SIZES)

V7X_VMEM_LIMIT_BYTES = 56 * 1024 * 1024
LANE = 128
MXU_DTYPE = jnp.bfloat16


def _round_up(n, m):
    return (n + m - 1) // m * m


def _compiler_params(semantics):
    return pltpu.CompilerParams(dimension_semantics=semantics, vmem_limit_bytes=V7X_VMEM_LIMIT_BYTES)


def _rmsnorm_rows_kernel(x_ref, g_ref, o_ref):
    x = x_ref[...]
    ms = jnp.mean(x * x, axis=-1, keepdims=True)
    o_ref[...] = (x * lax.rsqrt(ms + EPS) * g_ref[...]).astype(o_ref.dtype)


def _rmsnorm_rows(x, g, *, tm=256):
    m, d = x.shape
    out_dtype = MXU_DTYPE
    return pl.pallas_call(
        _rmsnorm_rows_kernel,
        grid=(m // tm,),
        in_specs=[pl.BlockSpec((tm, d), lambda i: (i, 0)), pl.BlockSpec((1, d), lambda i: (0, 0))],
        out_specs=pl.BlockSpec((tm, d), lambda i: (i, 0)),
        out_shape=jax.ShapeDtypeStruct((m, d), out_dtype),
        compiler_params=_compiler_params(("parallel",)),
        name="rmsnorm_rows",
    )(x, g.reshape(1, d))


def _mm_kernel(a_ref, w_ref, o_ref, acc_ref, *, nk):
    k = pl.program_id(2)

    @pl.when(k == 0)
    def _():
        acc_ref[...] = jnp.zeros_like(acc_ref)

    acc_ref[...] += jnp.dot(a_ref[...], w_ref[...], preferred_element_type=jnp.float32)

    @pl.when(k == nk - 1)
    def _():
        o_ref[...] = acc_ref[...].astype(o_ref.dtype)


def _mm_res_kernel(a_ref, w_ref, r_ref, o_ref, acc_ref, *, nk):
    k = pl.program_id(2)

    @pl.when(k == 0)
    def _():
        acc_ref[...] = jnp.zeros_like(acc_ref)

    acc_ref[...] += jnp.dot(a_ref[...], w_ref[...], preferred_element_type=jnp.float32)

    @pl.when(k == nk - 1)
    def _():
        o_ref[...] = (r_ref[...] + acc_ref[...]).astype(o_ref.dtype)


def _layer_spec(arr, block, index_map, layer):
    if arr.ndim == 2:
        return pl.BlockSpec(block, index_map)
    return pl.BlockSpec((None,) + block, lambda *g: (layer,) + index_map(*g))


def _matmul(a, w, *, res=None, layer=None, tm, tn, tk, out_dtype=jnp.float32):
    m, kdim = a.shape
    n = w.shape[-1]
    nk = kdim // tk
    assert m % tm == 0 and n % tn == 0 and kdim % tk == 0
    in_specs = [pl.BlockSpec((tm, tk), lambda i, j, k: (i, k)), _layer_spec(w, (tk, tn), lambda i, j, k: (k, j), layer)]
    args = [a, w]
    if res is None:
        body = functools.partial(_mm_kernel, nk=nk)
    else:
        body = functools.partial(_mm_res_kernel, nk=nk)
        in_specs.append(pl.BlockSpec((tm, tn), lambda i, j, k: (i, j)))
        args.append(res)
    return pl.pallas_call(
        body,
        grid=(m // tm, n // tn, nk),
        in_specs=in_specs,
        out_specs=pl.BlockSpec((tm, tn), lambda i, j, k: (i, j)),
        out_shape=jax.ShapeDtypeStruct((m, n), out_dtype),
        scratch_shapes=[pltpu.VMEM((tm, tn), jnp.float32)],
        compiler_params=_compiler_params(("parallel", "parallel", "arbitrary")),
        name="matmul_res" if res is not None else "matmul",
    )(*args)


def _swiglu_kernel(a_ref, wg_ref, wu_ref, o_ref):
    a = a_ref[...]
    g = jnp.dot(a, wg_ref[...], preferred_element_type=jnp.float32)
    u = jnp.dot(a, wu_ref[...], preferred_element_type=jnp.float32)
    o_ref[...] = (g * jax.nn.sigmoid(g) * u).astype(o_ref.dtype)


def _swiglu(a, wg, wu, *, layer, tm, tn):
    m, kdim = a.shape
    n = wg.shape[-1]
    return pl.pallas_call(
        _swiglu_kernel,
        grid=(m // tm, n // tn),
        in_specs=[pl.BlockSpec((tm, kdim), lambda i, j: (i, 0)),
                  _layer_spec(wg, (kdim, tn), lambda i, j: (0, j), layer),
                  _layer_spec(wu, (kdim, tn), lambda i, j: (0, j), layer)],
        out_specs=pl.BlockSpec((tm, tn), lambda i, j: (i, j)),
        out_shape=jax.ShapeDtypeStruct((m, n), MXU_DTYPE),
        compiler_params=_compiler_params(("parallel", "parallel")),
        name="swiglu",
    )(a, wg, wu)


def _ple_kernel(a_ref, wg_ref, p_ref, wp_ref, r_ref, o_ref):
    gate = jax.nn.sigmoid(jnp.dot(a_ref[...], wg_ref[...], preferred_element_type=jnp.float32))
    emb = jnp.dot(p_ref[...], wp_ref[...], preferred_element_type=jnp.float32)
    o_ref[...] = r_ref[...] + gate * emb


def _ple(a, wg, p, wp, res, *, layer, tm, tn):
    m, kdim = a.shape
    n = wg.shape[-1]
    pdim = p.shape[-1]
    return pl.pallas_call(
        _ple_kernel,
        grid=(m // tm, n // tn),
        in_specs=[pl.BlockSpec((tm, kdim), lambda i, j: (i, 0)),
                  _layer_spec(wg, (kdim, tn), lambda i, j: (0, j), layer),
                  _layer_spec(p, (tm, pdim), lambda i, j: (i, 0), layer),
                  _layer_spec(wp, (pdim, tn), lambda i, j: (0, j), layer),
                  pl.BlockSpec((tm, tn), lambda i, j: (i, j))],
        out_specs=pl.BlockSpec((tm, tn), lambda i, j: (i, j)),
        out_shape=jax.ShapeDtypeStruct((m, n), jnp.float32),
        compiler_params=_compiler_params(("parallel", "parallel")),
        name="ple",
    )(a, wg, p, wp, res)


def _rope_tables(positions, dim):
    inv_freq = ROPE_THETA ** (-jnp.arange(0, dim, 2, dtype=jnp.float32) / dim)
    ang = positions.astype(jnp.float32)[..., None] * inv_freq
    return jnp.cos(ang), jnp.sin(ang)


_SEG_ORDER = ("a_q", "a_k", "a_v", "b_q", "b_iq", "c_q", "d_q", "d_kv", "c_kv", "b_k", "b_v")
_SEG_NAMES = ("a_q", "a_k", "a_v", "b_q", "b_k", "b_v", "b_iq", "b_ik", "b_iw", "c_q", "c_kv", "c_kr", "d_q", "d_kv", "d_g")
_ORIG_START = dict(zip(_SEG_NAMES, [int(o) for o in np.cumsum((0,) + IN_SIZES[:-1])]))
_ORIG_SIZE = dict(zip(_SEG_NAMES, IN_SIZES))
_GATES_PER_KV_GROUP = 3 * NSA_Q_PER_KV


def _build_in_layout():
    col = {}
    perm = []
    for name in _SEG_ORDER:
        col[name] = len(perm)
        perm.extend(range(_ORIG_START[name], _ORIG_START[name] + _ORIG_SIZE[name]))
    zero = IN_WIDTH
    col["narrow"] = len(perm)
    blk0 = list(range(_ORIG_START["b_ik"], _ORIG_START["b_ik"] + IDX_DIM))
    blk0 += list(range(_ORIG_START["d_g"], _ORIG_START["d_g"] + _GATES_PER_KV_GROUP))
    blk0 += list(range(_ORIG_START["b_iw"], _ORIG_START["b_iw"] + IDX_HEADS))
    blk0 += [zero] * (LANE - len(blk0))
    blk1 = list(range(_ORIG_START["c_kr"], _ORIG_START["c_kr"] + MLA_ROPE))
    blk1 += list(range(_ORIG_START["d_g"] + _GATES_PER_KV_GROUP, _ORIG_START["d_g"] + 2 * _GATES_PER_KV_GROUP))
    blk1 += [zero] * (LANE - len(blk1))
    perm.extend(blk0 + blk1)
    width = _round_up(len(perm), 512)
    perm.extend([zero] * (width - len(perm)))
    return col, np.asarray(perm, np.int32), width


_COL, _IN_PERM, IN_WIDTH_PAD = _build_in_layout()


def _runs(perm):
    runs = []
    for c in perm.tolist():
        src = None if c == IN_WIDTH else c
        if runs and ((src is None and runs[-1][0] is None)
                     or (src is not None and runs[-1][0] is not None and runs[-1][0] + runs[-1][1] == src)):
            runs[-1][1] += 1
        else:
            runs.append([src, 1])
    return [tuple(r) for r in runs]


_IN_RUNS = _runs(_IN_PERM)
_GATE_LANE0 = IDX_DIM
_IW_LANE0 = IDX_DIM + _GATES_PER_KV_GROUP


def _rope_lane_tables(cos, sin):
    return jnp.concatenate([cos, cos], axis=-1), jnp.concatenate([-sin, sin], axis=-1)


def _lane_tables(positions, dim):
    cc, ss = _rope_lane_tables(*_rope_tables(positions, dim))
    return tuple(jnp.tile(t.reshape(-1, dim), (1, LANE // dim)) for t in (cc, ss))


def _seg64_sum(y):
    r = lax.broadcasted_iota(jnp.int32, (LANE, LANE), 0) // 64
    c = lax.broadcasted_iota(jnp.int32, (LANE, LANE), 1) // 64
    bd = jnp.where(r == c, 1.0, 0.0).astype(MXU_DTYPE)
    hi = y.astype(MXU_DTYPE)
    lo = (y - hi.astype(jnp.float32)).astype(MXU_DTYPE)
    return (jnp.dot(hi, bd, preferred_element_type=jnp.float32)
            + jnp.dot(lo, bd, preferred_element_type=jnp.float32))


def _prep_kernel(x_ref, g_ref, cc_ref, ss_ref, o_ref, *, n_blocks, seg, norm, scale):
    cc, ss, g = cc_ref[...], ss_ref[...], g_ref[...]
    lane = lax.broadcasted_iota(jnp.int32, cc.shape, 1)
    for h in range(n_blocks):
        x = x_ref[:, h * LANE:(h + 1) * LANE]
        if norm:
            if seg == LANE:
                ms = jnp.mean(x * x, axis=-1, keepdims=True)
            else:
                ms = _seg64_sum(x * x) * (1.0 / seg)
            x = x * lax.rsqrt(ms + EPS) * g
        if seg == LANE:
            partner = pltpu.roll(x, LANE // 2, axis=1)
        else:
            partner = jnp.where((lane % seg) < seg // 2, pltpu.roll(x, LANE - seg // 2, axis=1),
                                pltpu.roll(x, seg // 2, axis=1))
        y = x * cc + partner * ss
        if scale != 1.0:
            y = y * scale
        o_ref[:, h * LANE:(h + 1) * LANE] = y.astype(o_ref.dtype)


def _prep_heads(proj, col0, n_blocks, gain, cc, ss, *, seg, norm, scale=1.0, tm=256):
    m = proj.shape[0]
    tm = min(tm, m)
    width = LANE * n_blocks
    cb = col0 // width
    assert cb * width == col0
    return pl.pallas_call(
        functools.partial(_prep_kernel, n_blocks=n_blocks, seg=seg, norm=norm, scale=scale),
        grid=(m // tm,),
        in_specs=[pl.BlockSpec((tm, width), lambda i: (i, cb)),
                  pl.BlockSpec((1, LANE), lambda i: (0, 0)),
                  pl.BlockSpec((tm, LANE), lambda i: (i, 0)),
                  pl.BlockSpec((tm, LANE), lambda i: (i, 0))],
        out_specs=pl.BlockSpec((tm, width), lambda i: (i, 0)),
        out_shape=jax.ShapeDtypeStruct((m, width), MXU_DTYPE),
        compiler_params=_compiler_params(("parallel",)),
        name="prep_heads",
    )(proj, gain, cc, ss)


def _dot_nt(a, b):
    return lax.dot_general(a, b, (((1,), (1,)), ((), ())), preferred_element_type=jnp.float32)


LOG2E = math.log2(math.e)


def _softmax_step(s, mask, m, l, acc, v):
    if mask is not None:
        s = jnp.where(mask, s, NEG_INF)
    m_new = jnp.maximum(m, jnp.max(s, axis=-1, keepdims=True))
    alpha = jnp.exp2(m - m_new)
    p = jnp.exp2(s - m_new)
    l = alpha * l + jnp.sum(p, axis=-1, keepdims=True)
    acc = alpha * acc + jnp.dot(p.astype(MXU_DTYPE), v, preferred_element_type=jnp.float32)
    return m_new, l, acc


def _softmax_step_t(s_t, m, l, acc_t, v_t):
    m_new = jnp.maximum(m, jnp.max(s_t, axis=0, keepdims=True))
    alpha = jnp.exp2(m - m_new)
    p = jnp.exp2(s_t - m_new)
    l = alpha * l + jnp.sum(p, axis=0, keepdims=True)
    acc_t = alpha * acc_t + jnp.dot(v_t, p.astype(MXU_DTYPE), preferred_element_type=jnp.float32)
    return m_new, l, acc_t


def _softmax_finish(l, acc):
    return acc * (1.0 / jnp.maximum(l, 1e-30))


def _nsa_cmp_kernel(x_ref, pe_ref, w1_ref, w2_ref, gk_ref, cc_ref, ss_ref, kc_ref, vc_ref):
    nch = x_ref.shape[2]
    for kv in range(2):
        x = x_ref[kv, 0]
        lo = jnp.dot((x + pe_ref[kv, 0:1, :]).astype(MXU_DTYPE), w1_ref[kv, 0], preferred_element_type=jnp.float32)
        hi = jnp.dot((x + pe_ref[kv, 1:2, :]).astype(MXU_DTYPE), w1_ref[kv, 1], preferred_element_type=jnp.float32)
        hid = lo + pltpu.roll(hi, nch - 1, axis=0)
        comp = jnp.dot(jax.nn.gelu(hid).astype(MXU_DTYPE), w2_ref[kv], preferred_element_type=jnp.float32)
        if kv == 0:
            ms = jnp.mean(comp * comp, axis=-1, keepdims=True)
            y = comp * lax.rsqrt(ms + EPS) * gk_ref[...]
            y = y * cc_ref[0] + pltpu.roll(y, LANE // 2, axis=1) * ss_ref[0]
            kc_ref[0] = y.astype(kc_ref.dtype)
        else:
            vc_ref[0] = comp.astype(vc_ref.dtype)


def _nsa_compress(craw, pe2, w1, w2, gk, cc_c, ss_c, *, n_batch):
    _, gb, nch, width = craw.shape
    return pl.pallas_call(
        _nsa_cmp_kernel,
        grid=(gb,),
        in_specs=[pl.BlockSpec((2, 1, nch, width), lambda n: (0, n, 0, 0)),
                  pl.BlockSpec((2, 2, width), lambda n: (0, 0, 0)),
                  pl.BlockSpec((2, 2, width, CMP_HIDDEN), lambda n: (0, 0, 0, 0)),
                  pl.BlockSpec((2, CMP_HIDDEN, HEAD_DIM), lambda n: (0, 0, 0)),
                  pl.BlockSpec((1, HEAD_DIM), lambda n: (0, 0)),
                  pl.BlockSpec((1, nch, HEAD_DIM), lambda n: (n % n_batch, 0, 0)),
                  pl.BlockSpec((1, nch, HEAD_DIM), lambda n: (n % n_batch, 0, 0))],
        out_specs=[pl.BlockSpec((1, nch, HEAD_DIM), lambda n: (n, 0, 0)),
                   pl.BlockSpec((1, nch, HEAD_DIM), lambda n: (n, 0, 0))],
        out_shape=[jax.ShapeDtypeStruct((gb, nch, HEAD_DIM), MXU_DTYPE)] * 2,
        compiler_params=_compiler_params(("parallel",)),
        name="nsa_compress",
    )(craw, pe2, w1, w2, gk, cc_c, ss_c)


def _nsa_attn_kernel(q_ref, kc_ref, vc_ref, ks_ref, vs_ref, kw_ref, vw_ref, gate_ref, c2s_ref, exp_ref, o_ref, *,
                     tq, tk, n_sel, n_top):
    i = pl.program_id(2)
    R = NSA_Q_PER_KV
    rows = R * tq
    nch = kc_ref.shape[1]
    bf16 = MXU_DTYPE
    q = jnp.concatenate([q_ref[:, r * HEAD_DIM:(r + 1) * HEAD_DIM] for r in range(R)], axis=0)
    t_row = i * tq + lax.broadcasted_iota(jnp.int32, (rows, 1), 0) % tq

    s = _dot_nt(q, kc_ref[0])
    blk_end = lax.broadcasted_iota(jnp.int32, (1, nch), 1) * CMP_STRIDE + (CMP_LEN - 1)
    mask = blk_end <= t_row
    zero = jnp.zeros((rows, 1), jnp.float32)
    s = jnp.where(mask, s, NEG_INF)
    p_cmp = jnp.where(mask, jnp.exp2(s - jnp.max(s, axis=-1, keepdims=True)), 0.0)
    p_cmp = p_cmp * (1.0 / jnp.maximum(jnp.sum(p_cmp, axis=-1, keepdims=True), 1e-30))
    o_cmp = jnp.dot(p_cmp.astype(bf16), vc_ref[0], preferred_element_type=jnp.float32)

    imp_rows = jnp.dot(p_cmp.astype(bf16), c2s_ref[...], preferred_element_type=jnp.float32)
    imp = imp_rows[0:tq]
    for r in range(1, R):
        imp = imp + imp_rows[r * tq:(r + 1) * tq]
    imp_t = imp.T[0:n_sel]
    blk = lax.broadcasted_iota(jnp.int32, (n_sel, tq), 0)
    t_lane = i * tq + lax.broadcasted_iota(jnp.int32, (n_sel, tq), 1)
    dist = lax.shift_right_logical(t_lane, int(math.log2(SEL_BLOCK))) - blk
    forced = (blk < SEL_INIT) | ((dist >= 0) & (dist < SEL_LOCAL))
    val = jnp.where(forced, POS_INF, jnp.where(blk * SEL_BLOCK <= t_lane, imp_t, NEG_INF))
    rank = jnp.zeros((n_sel, tq), jnp.float32)
    for mblk in range(n_sel):
        vm = val[mblk:mblk + 1, :]
        rank = rank + jnp.where((vm > val) | ((vm == val) & (blk > mblk)), 1.0, 0.0)
    drop_t = jnp.where(rank < n_top, 0.0, 1.0)
    drop_t = jnp.concatenate([drop_t, jnp.zeros((LANE - n_sel, tq), jnp.float32)], axis=0) if n_sel < LANE else drop_t
    drop_q = drop_t.T.astype(bf16)
    drop_rows = jnp.concatenate([drop_q] * R, axis=0)

    def slc_step(j, carry, diagonal):
        off = pl.multiple_of(j * tk, tk)
        k = ks_ref[pl.ds(off, tk), :]
        v = vs_ref[pl.ds(off, tk), :].astype(bf16)
        s = _dot_nt(q, k) + jnp.dot(drop_rows, exp_ref[j], preferred_element_type=jnp.float32)
        mask = (off + lax.broadcasted_iota(jnp.int32, (1, tk), 1) <= t_row) if diagonal else None
        return _softmax_step(s, mask, *carry, v)

    init = (zero + NEG_INF, zero, jnp.zeros((rows, HEAD_DIM), jnp.float32))
    n_full = (i * tq) // tk
    carry = lax.fori_loop(0, n_full, lambda j, c: slc_step(j, c, False), init)
    _, l, acc = slc_step(n_full, carry, True)
    o_slc = _softmax_finish(l, acc)

    span = WINDOW + tq
    off = pl.multiple_of(jnp.maximum(i * tq - WINDOW, 0), tq)
    key = off + lax.broadcasted_iota(jnp.int32, (1, span), 1)
    s = _dot_nt(q, kw_ref[pl.ds(off, span), :])
    s = jnp.where(key <= t_row, jnp.where(key > t_row - WINDOW, s, NEG_INF), NEG_INF)
    _, l, acc = _softmax_step(s, None, *init, vw_ref[pl.ds(off, span), :].astype(bf16))
    o_win = _softmax_finish(l, acc)

    gate = jax.nn.sigmoid(gate_ref[...])
    for r in range(R):
        c = _GATE_LANE0 + 3 * r
        rs = slice(r * tq, (r + 1) * tq)
        o_ref[:, r * HEAD_DIM:(r + 1) * HEAD_DIM] = (gate[:, c:c + 1] * o_cmp[rs]
                                                     + gate[:, c + 1:c + 2] * o_slc[rs]
                                                     + gate[:, c + 2:c + 3] * o_win[rs])


def _nsa_mixer(proj, B, S, cc_h, ss_h, positions, g_q, g_k, cmp_w1, cmp_w2, cmp_pe):
    M = B * S
    G, R, d = NSA_KV_HEADS, NSA_Q_PER_KV, HEAD_DIM
    bf16 = MXU_DTYPE
    tq = Q_BLOCK
    nq = S // tq
    nch = S // CMP_STRIDE
    n_sel = S // SEL_BLOCK
    n_top = min(SEL_TOPN, n_sel)
    kv0 = _COL["d_kv"]

    qn = _prep_heads(proj, _COL["d_q"], HEADS_PER_GROUP, g_q.reshape(1, d), cc_h, ss_h,
                     seg=LANE, norm=True, scale=d ** -0.5 * LOG2E)
    k_slc =_prep_heads(proj, kv0 + (1 * 2 + 0) * G * d, G, g_k[1].reshape(1, d), cc_h, ss_h, seg=LANE, norm=True)
    k_win = _prep_heads(proj, kv0 + (2 * 2 + 0) * G * d, G, g_k[2].reshape(1, d), cc_h, ss_h, seg=LANE, norm=True)

    craw = proj[:, kv0:kv0 + 2 * G * d].reshape(B, S, 2, G, d).transpose(2, 3, 0, 1, 4)
    craw = craw.reshape(2, G * B, nch, CMP_STRIDE * d)
    ends = np.minimum(np.arange(nch) * CMP_STRIDE + CMP_LEN - 1, S - 1)
    cc_c, ss_c = _rope_lane_tables(*_rope_tables(positions[:, ends], d))
    pe2 = cmp_pe.reshape(2, 2, CMP_STRIDE * d)
    w1 = cmp_w1.astype(bf16).reshape(2, 2, CMP_STRIDE * d, CMP_HIDDEN)
    k_cmp, v_cmp = _nsa_compress(craw, pe2, w1, cmp_w2.astype(bf16), g_k[0].reshape(1, d), cc_c, ss_c, n_batch=B)

    starts = np.arange(nch) * CMP_STRIDE
    sel_start = np.arange(LANE) * SEL_BLOCK
    c2s = ((starts[:, None] < sel_start[None, :] + SEL_BLOCK) & (starts[:, None] + CMP_LEN > sel_start[None, :])
           & (np.arange(nch)[:, None] < nch - 1) & (np.arange(LANE)[None, :] < n_sel))
    tk = min(512, S)
    expand = (np.arange(LANE)[None, :, None] == (np.arange(S // tk)[:, None, None] * tk + np.arange(tk)[None, None, :]) // SEL_BLOCK)
    c2s = jnp.asarray(c2s, bf16)
    expand = jnp.asarray(expand * MASK_BIAS, bf16)

    narrow = _COL["narrow"] // LANE
    col_vs = (kv0 + (1 * 2 + 1) * G * d) // LANE
    col_vw = (kv0 + (2 * 2 + 1) * G * d) // LANE
    return pl.pallas_call(
        functools.partial(_nsa_attn_kernel, tq=tq, tk=tk, n_sel=n_sel, n_top=n_top),
        grid=(B, G, nq),
        in_specs=[pl.BlockSpec((tq, R * d), lambda b, g, i: (b * nq + i, g)),
                  pl.BlockSpec((1, nch, d), lambda b, g, i: (g * B + b, 0, 0)),
                  pl.BlockSpec((1, nch, d), lambda b, g, i: (g * B + b, 0, 0)),
                  pl.BlockSpec((S, d), lambda b, g, i: (b, g)),
                  pl.BlockSpec((S, d), lambda b, g, i: (b, col_vs + g)),
                  pl.BlockSpec((S, d), lambda b, g, i: (b, g)),
                  pl.BlockSpec((S, d), lambda b, g, i: (b, col_vw + g)),
                  pl.BlockSpec((tq, LANE), lambda b, g, i: (b * nq + i, narrow + g)),
                  pl.BlockSpec((nch, LANE), lambda b, g, i: (0, 0)),
                  pl.BlockSpec((S // tk, LANE, tk), lambda b, g, i: (0, 0, 0))],
        out_specs=pl.BlockSpec((tq, R * d), lambda b, g, i: (b * nq + i, g)),
        out_shape=jax.ShapeDtypeStruct((M, GROUP_WIDTH), jnp.float32),
        compiler_params=_compiler_params(("parallel", "parallel", "arbitrary")),
        name="nsa_attention",
    )(qn, k_cmp, v_cmp, k_slc, proj, k_win, proj, proj, c2s, expand)


_INT_MIN = -2 ** 31


def _sortable_key(x):
    b = lax.bitcast_convert_type(x + 0.0, jnp.int32)
    return jnp.where(b >= 0, b, b ^ 0x7FFFFFFF)


def _dsa_kernel(q_ref, k_ref, v_ref, iq_ref, ik_ref, nar_ref, o_ref, key_ref, msk_ref, vt_ref, *,
                tq, tk, n_keep, idx_bits):
    i = pl.program_id(1)
    H = HEADS_PER_GROUP
    bf16 = MXU_DTYPE
    t_lane = i * tq + lax.broadcasted_iota(jnp.int32, (1, tq), 1)
    key_sub = lax.broadcasted_iota(jnp.int32, (tq, 1), 0)
    lane = lax.broadcasted_iota(jnp.int32, (tq, LANE), 1)
    iw_t = (nar_ref[...] * (IDX_HEADS ** -0.5)).T
    w_rows = [iw_t[_IW_LANE0 + hd:_IW_LANE0 + hd + 1, :] for hd in range(IDX_HEADS)]
    iq_rows = jnp.concatenate([iq_ref[:, pr * LANE:(pr + 1) * LANE] for pr in range(IDX_HEADS // 2)], axis=0)

    def score_body(j, _):
        off = pl.multiple_of(j * tq, tq)
        ik = ik_ref[pl.ds(off, tq), :]
        ik_lo = jnp.where(lane < IDX_DIM, ik, jnp.zeros_like(ik))
        ik_hi = jnp.where(lane < IDX_DIM, jnp.zeros_like(ik), pltpu.roll(ik.astype(jnp.float32), IDX_DIM, axis=1).astype(bf16))
        rel = (jnp.maximum(_dot_nt(ik_lo, iq_rows), 0.0), jnp.maximum(_dot_nt(ik_hi, iq_rows), 0.0))
        sc = jnp.zeros((tq, tq), jnp.float32)
        for hd in range(IDX_HEADS):
            sc = sc + w_rows[hd] * rel[hd % 2][:, (hd // 2) * tq:(hd // 2 + 1) * tq]
        sc = jnp.where(off + key_sub <= t_lane, sc, NEG_INF)
        key_ref[j] = _sortable_key(sc)
        return 0

    lax.fori_loop(0, i + 1, score_body, 0)

    def count(pred):
        def body(j, part):
            return part + jnp.where(pred(key_ref[j], j * tq), 1.0, 0.0)
        part = lax.fori_loop(0, i + 1, body, jnp.zeros((tq, tq), jnp.float32))
        return jnp.sum(part, axis=0, keepdims=True)

    c0 = count(lambda kk, off: kk >= 0)
    thr = jnp.where(c0 >= n_keep, 0, _INT_MIN).astype(jnp.int32)

    def thr_body(it, thr):
        cand = thr | lax.shift_left(jnp.int32(1), 30 - it)
        c = count(lambda kk, off: kk >= cand)
        return jnp.where(c >= n_keep, cand, thr)

    thr = lax.fori_loop(0, 31, thr_body, thr)

    n_ge = count(lambda kk, off: kk >= thr)

    def cut_search():
        need = n_keep - count(lambda kk, off: kk > thr)

        def cut_body(it, cut):
            cand = cut | lax.shift_left(jnp.int32(1), idx_bits - 1 - it)
            c = count(lambda kk, off: (kk == thr) & (off + key_sub < cand))
            return jnp.where(c < need, cand, cut)

        return lax.fori_loop(0, idx_bits, cut_body, jnp.zeros((1, tq), jnp.int32))

    cut = lax.cond(jnp.max(n_ge) > n_keep, cut_search, lambda: jnp.full((1, tq), 2 ** idx_bits - 1, jnp.int32))

    tpc = tk // tq
    n_chunks = (i + tpc) // tpc

    def mask_body(j, _):
        @pl.when(j <= i)
        def _():
            kk = key_ref[j]
            kidx = j * tq + key_sub
            sel = (kk > thr) | ((kk == thr) & (kidx <= cut))
            msk_ref[j] = jnp.where(kidx <= t_lane, jnp.where(sel, 0.0, NEG_INF), NEG_INF)

        @pl.when(j > i)
        def _():
            msk_ref[j] = jnp.full((tq, tq), NEG_INF, jnp.float32)

        return 0

    lax.fori_loop(0, n_chunks * tpc, mask_body, 0)

    @pl.when(i == 0)
    def _():
        for c in range(vt_ref.shape[0]):
            for u in range(tpc):
                r0 = (c * tpc + u) * tq
                vt_ref[c, :, u * tq:(u + 1) * tq] = v_ref[r0:r0 + tq, :].T.astype(bf16)

    hpp = 4
    rows = hpp * tq
    zero = jnp.zeros((1, rows), jnp.float32)
    qs = [jnp.concatenate([q_ref[:, (hc * hpp + r) * HEAD_DIM:(hc * hpp + r + 1) * HEAD_DIM] for r in range(hpp)], axis=0)
          for hc in range(H // hpp)]

    def att_body(j, carry):
        off = pl.multiple_of(j * tk, tk)
        k = k_ref[pl.ds(off, tk), :]
        bias1 = jnp.concatenate([msk_ref[j * tpc + u] for u in range(tpc)], axis=0)
        bias = jnp.concatenate([bias1] * hpp, axis=1)
        return tuple(_softmax_step_t(_dot_nt(k, q) + bias, *carry[c], vt_ref[j]) for c, q in enumerate(qs))

    init = tuple((zero + NEG_INF, zero, jnp.zeros((HEAD_DIM, rows), jnp.float32)) for _ in qs)
    res = lax.fori_loop(0, n_chunks, att_body, init)
    for hc, (_, l, acc_t) in enumerate(res):
        o = _softmax_finish(l, acc_t).T
        for r in range(hpp):
            hh = hc * hpp + r
            o_ref[:, hh * HEAD_DIM:(hh + 1) * HEAD_DIM] = o[r * tq:(r + 1) * tq]


def _dsa_mixer(proj, B, S, cc_h, ss_h, cc_i, ss_i, g_q, g_k):
    M = B * S
    d = HEAD_DIM
    tq = Q_BLOCK
    nq = S // tq
    n_keep = min(DSA_TOPK, S // 4)
    tk = min(512, S)
    qn = _prep_heads(proj, _COL["b_q"], HEADS_PER_GROUP, g_q.reshape(1, d), cc_h, ss_h, seg=LANE, norm=True,
                     scale=d ** -0.5 * LOG2E)
    kn =_prep_heads(proj, _COL["b_k"], 1, g_k.reshape(1, d), cc_h, ss_h, seg=LANE, norm=True)
    ones = jnp.ones((1, LANE), jnp.float32)
    iqn = _prep_heads(proj, _COL["b_iq"], IDX_HEADS * IDX_DIM // LANE, ones, cc_i, ss_i, seg=IDX_DIM, norm=False,
                      scale=IDX_DIM ** -0.5)
    ikn = _prep_heads(proj, _COL["narrow"], 1, ones, cc_i, ss_i, seg=IDX_DIM, norm=False)
    narrow = _COL["narrow"] // LANE
    col_v = _COL["b_v"] // LANE
    return pl.pallas_call(
        functools.partial(_dsa_kernel, tq=tq, tk=tk, n_keep=n_keep, idx_bits=int(math.log2(S))),
        grid=(B, nq),
        in_specs=[pl.BlockSpec((tq, GROUP_WIDTH), lambda b, i: (b * nq + i, 0)),
                  pl.BlockSpec((S, d), lambda b, i: (b, 0)),
                  pl.BlockSpec((S, d), lambda b, i: (b, col_v)),
                  pl.BlockSpec((tq, IDX_HEADS * IDX_DIM), lambda b, i: (b * nq + i, 0)),
                  pl.BlockSpec((S, LANE), lambda b, i: (b, 0)),
                  pl.BlockSpec((tq, LANE), lambda b, i: (b * nq + i, narrow))],
        out_specs=pl.BlockSpec((tq, GROUP_WIDTH), lambda b, i: (b * nq + i, 0)),
        out_shape=jax.ShapeDtypeStruct((M, GROUP_WIDTH), jnp.float32),
        scratch_shapes=[pltpu.VMEM((nq, tq, tq), jnp.int32), pltpu.VMEM((nq, tq, tq), jnp.float32),
                        pltpu.VMEM((S // tk, d, tk), MXU_DTYPE)],
        compiler_params=_compiler_params(("parallel", "arbitrary")),
        name="dsa_attention",
    )(qn, kn, proj, iqn, ikn, proj)


def _causal_chains(qs, k_ref, v_ref, dk, dv, t_row, t0, tk):
    rows = qs[0].shape[0]
    zero = jnp.zeros((rows, 1), jnp.float32)

    def step(j, carry, masked):
        off = pl.multiple_of(j * tk, tk)
        mask = (off + lax.broadcasted_iota(jnp.int32, (1, tk), 1) <= t_row) if masked else None
        out = []
        for c, q in enumerate(qs):
            k = k_ref[pl.ds(off, tk), c * dk:(c + 1) * dk]
            v = v_ref[pl.ds(off, tk), c * dv:(c + 1) * dv].astype(MXU_DTYPE)
            out.append(_softmax_step(_dot_nt(q, k), mask, *carry[c], v))
        return tuple(out)

    init = tuple((zero + NEG_INF, zero, jnp.zeros((rows, dv), jnp.float32)) for _ in qs)
    n_full = t0 // tk
    res = lax.fori_loop(0, n_full, lambda j, c: step(j, c, False), init)
    res = step(n_full, res, True)
    return [_softmax_finish(l, acc) for _, l, acc in res]


def _diff_attn_kernel(q_ref, k_ref, v_ref, lam_ref, g_ref, o_ref, *, tq, tk, hb, out_scale):
    i = pl.program_id(2)
    d = HEAD_DIM
    lane = lax.broadcasted_iota(jnp.int32, (tq, d), 1)
    qs = []
    for c in range(hb):
        q = q_ref[:, c * d:(c + 1) * d]
        zeros = jnp.zeros_like(q)
        qs.append(jnp.concatenate([jnp.where(lane < DIFF_QK_DIM, q, zeros), jnp.where(lane < DIFF_QK_DIM, zeros, q)], axis=0))
    t_row = i * tq + lax.broadcasted_iota(jnp.int32, (2 * tq, 1), 0) % tq
    outs = _causal_chains(qs, k_ref, v_ref, d, d, t_row, i * tq, tk)
    for c, o2 in enumerate(outs):
        o = o2[0:tq] - lam_ref[...] * o2[tq:2 * tq]
        ms = jnp.mean(o * o, axis=-1, keepdims=True)
        o_ref[:, c * d:(c + 1) * d] = o * lax.rsqrt(ms + EPS) * g_ref[...] * out_scale


def _diff_mixer(proj, B, S, cc_d, ss_d, g_q, g_k, g_sub, lam_params, layer_idx, *, tq=256, tk=512, hb=2):
    M = B * S
    d = HEAD_DIM
    nq = S // tq
    tk = min(tk, S)
    reps = LANE // DIFF_QK_DIM
    qn = _prep_heads(proj, _COL["a_q"], HEADS_PER_GROUP, jnp.tile(g_q, reps).reshape(1, d), cc_d, ss_d,
                     seg=DIFF_QK_DIM, norm=True, scale=DIFF_QK_DIM ** -0.5 * LOG2E)
    kn = _prep_heads(proj, _COL["a_k"], HEADS_PER_GROUP, jnp.tile(g_k, reps).reshape(1, d), cc_d, ss_d,
                     seg=DIFF_QK_DIM, norm=True)
    lam_init = 0.8 - 0.6 * math.exp(-0.3 * layer_idx)
    lp = lam_params.astype(jnp.float32)
    lam = jnp.exp(jnp.sum(lp[0] * lp[1])) - jnp.exp(jnp.sum(lp[2] * lp[3])) + lam_init
    col_v = _COL["a_v"] // (hb * d)
    return pl.pallas_call(
        functools.partial(_diff_attn_kernel, tq=tq, tk=tk, hb=hb, out_scale=1.0 - lam_init),
        grid=(B, HEADS_PER_GROUP // hb, nq),
        in_specs=[pl.BlockSpec((tq, hb * d), lambda b, h, i: (b * nq + i, h)),
                  pl.BlockSpec((S, hb * d), lambda b, h, i: (b, h)),
                  pl.BlockSpec((S, hb * d), lambda b, h, i: (b, col_v + h)),
                  pl.BlockSpec((1, d), lambda b, h, i: (0, 0)),
                  pl.BlockSpec((1, d), lambda b, h, i: (0, 0))],
        out_specs=pl.BlockSpec((tq, hb * d), lambda b, h, i: (b * nq + i, h)),
        out_shape=jax.ShapeDtypeStruct((M, GROUP_WIDTH), jnp.float32),
        compiler_params=_compiler_params(("parallel", "parallel", "arbitrary")),
        name="diff_attention",
    )(qn, kn, proj, jnp.full((1, d), lam, jnp.float32), g_sub.reshape(1, d))


MLA_QK_PAD = 2 * LANE


def _rmsnorm_cols_kernel(x_ref, g_ref, o_ref):
    x = x_ref[...]
    ms = jnp.mean(x * x, axis=-1, keepdims=True)
    o_ref[...] = (x * lax.rsqrt(ms + EPS) * g_ref[...]).astype(o_ref.dtype)


def _rmsnorm_cols(proj, col0, width, g, *, tm=512):
    m = proj.shape[0]
    tm = min(tm, m)
    cb = col0 // width
    assert cb * width == col0
    return pl.pallas_call(
        _rmsnorm_cols_kernel,
        grid=(m // tm,),
        in_specs=[pl.BlockSpec((tm, width), lambda i: (i, cb)), pl.BlockSpec((1, width), lambda i: (0, 0))],
        out_specs=pl.BlockSpec((tm, width), lambda i: (i, 0)),
        out_shape=jax.ShapeDtypeStruct((m, width), MXU_DTYPE),
        compiler_params=_compiler_params(("parallel",)),
        name="rmsnorm_cols",
    )(proj, g.reshape(1, width))


def _mla_prep_kernel(a_ref, b_ref, g_ref, cc_ref, ss_ref, o_ref, *, a_stride, b_col0, b_stride, scale):
    cc, ss = cc_ref[...], ss_ref[...]
    lane = lax.broadcasted_iota(jnp.int32, cc.shape, 1)
    for h in range(HEADS_PER_GROUP):
        a = a_ref[:, h * a_stride:h * a_stride + LANE]
        b = jnp.where(lane < MLA_ROPE, b_ref[:, b_col0 + h * b_stride:b_col0 + h * b_stride + LANE], 0.0)
        ms = (jnp.sum(a * a, axis=-1, keepdims=True) + jnp.sum(b * b, axis=-1, keepdims=True)) * (1.0 / (MLA_NOPE + MLA_ROPE))
        r = lax.rsqrt(ms + EPS)
        ya = a * r * g_ref[:, 0:LANE]
        yb = b * r * g_ref[:, LANE:2 * LANE]
        partner = jnp.where((lane % MLA_ROPE) < MLA_ROPE // 2, pltpu.roll(yb, LANE - MLA_ROPE // 2, axis=1),
                            pltpu.roll(yb, MLA_ROPE // 2, axis=1))
        yb = yb * cc + partner * ss
        o_ref[:, h * MLA_QK_PAD:h * MLA_QK_PAD + LANE] = (ya * scale).astype(o_ref.dtype)
        o_ref[:, h * MLA_QK_PAD + LANE:(h + 1) * MLA_QK_PAD] = (yb * scale).astype(o_ref.dtype)


def _mla_prep(a_arr, a_width, a_stride, b_arr, b_block, b_width, b_col0, b_stride, gain, cc, ss, *, scale=1.0, tm=256):
    m = a_arr.shape[0]
    tm = min(tm, m)
    g2 = jnp.concatenate([gain, jnp.zeros((MLA_QK_PAD - gain.shape[0],), gain.dtype)]).reshape(1, MLA_QK_PAD)
    return pl.pallas_call(
        functools.partial(_mla_prep_kernel, a_stride=a_stride, b_col0=b_col0, b_stride=b_stride, scale=scale),
        grid=(m // tm,),
        in_specs=[pl.BlockSpec((tm, a_width), lambda i: (i, 0)),
                  pl.BlockSpec((tm, b_width), lambda i: (i, b_block)),
                  pl.BlockSpec((1, MLA_QK_PAD), lambda i: (0, 0)),
                  pl.BlockSpec((tm, LANE), lambda i: (i, 0)),
                  pl.BlockSpec((tm, LANE), lambda i: (i, 0))],
        out_specs=pl.BlockSpec((tm, HEADS_PER_GROUP * MLA_QK_PAD), lambda i: (i, 0)),
        out_shape=jax.ShapeDtypeStruct((m, HEADS_PER_GROUP * MLA_QK_PAD), MXU_DTYPE),
        compiler_params=_compiler_params(("parallel",)),
        name="mla_prep",
    )(a_arr, b_arr, g2, cc, ss)


def _mla_attn_kernel(q_ref, k_ref, v_ref, o_ref, *, tq, tk, hb):
    i = pl.program_id(2)
    d = HEAD_DIM
    t_row = i * tq + lax.broadcasted_iota(jnp.int32, (tq, 1), 0)
    qs = [q_ref[:, c * MLA_QK_PAD:(c + 1) * MLA_QK_PAD] for c in range(hb)]
    outs = _causal_chains(qs, k_ref, v_ref, MLA_QK_PAD, d, t_row, i * tq, tk)
    for c, o in enumerate(outs):
        o_ref[:, c * d:(c + 1) * d] = o


def _mla_mixer(proj, B, S, cc_m, ss_m, g_cq, g_ckv, w_uq, w_uk, w_uv, g_q, g_k, *, tq=512, tk=512, hb=2):
    M = B * S
    H, d = HEADS_PER_GROUP, HEAD_DIM
    tq = min(tq, S)
    tk = min(tk, S)
    nq = S // tq
    dqk = MLA_NOPE + MLA_ROPE
    w_q = jnp.pad(w_uq.astype(MXU_DTYPE).reshape(MLA_Q_RANK, H, dqk), ((0, 0), (0, 0), (0, MLA_QK_PAD - dqk)))
    w_q = w_q.reshape(MLA_Q_RANK, H * MLA_QK_PAD)
    w_kv = jnp.concatenate([w_uk, w_uv], axis=1).astype(MXU_DTYPE)
    cq = _rmsnorm_cols(proj, _COL["c_q"], MLA_Q_RANK, g_cq)
    ckv = _rmsnorm_cols(proj, _COL["c_kv"], MLA_KV_RANK, g_ckv)
    q_up = _matmul(cq, w_q, tm=min(1024, M), tn=1024, tk=MLA_Q_RANK)
    kv_up = _matmul(ckv, w_kv, tm=min(1024, M), tn=1024, tk=MLA_KV_RANK)
    qn = _mla_prep(q_up, H * MLA_QK_PAD, MLA_QK_PAD, q_up, 0, H * MLA_QK_PAD, LANE, MLA_QK_PAD, g_q, cc_m, ss_m,
                   scale=dqk ** -0.5 * LOG2E)
    kn = _mla_prep(kv_up, H * MLA_NOPE, LANE, proj, _COL["narrow"] // LANE + 1, LANE, 0, 0, g_k, cc_m, ss_m)
    return pl.pallas_call(
        functools.partial(_mla_attn_kernel, tq=tq, tk=tk, hb=hb),
        grid=(B, H // hb, nq),
        in_specs=[pl.BlockSpec((tq, hb * MLA_QK_PAD), lambda b, h, i: (b * nq + i, h)),
                  pl.BlockSpec((S, hb * MLA_QK_PAD), lambda b, h, i: (b, h)),
                  pl.BlockSpec((S, hb * d), lambda b, h, i: (b, H // hb + h))],
        out_specs=pl.BlockSpec((tq, hb * d), lambda b, h, i: (b * nq + i, h)),
        out_shape=jax.ShapeDtypeStruct((M, GROUP_WIDTH), jnp.float32),
        compiler_params=_compiler_params(("parallel", "parallel", "arbitrary")),
        name="mla_attention",
    )(qn, kn, kv_up)


def _mix_kernel(a_ref, b_ref, c_ref, d_ref, g_ref, o_ref):
    o_ref[:, 0:GROUP_WIDTH] = a_ref[...].astype(o_ref.dtype)
    for n, ref in enumerate((b_ref, c_ref, d_ref)):
        x = ref[...]
        ms = jnp.mean(x * x, axis=-1, keepdims=True)
        o_ref[:, (n + 1) * GROUP_WIDTH:(n + 2) * GROUP_WIDTH] = (
            x * lax.rsqrt(ms + EPS) * g_ref[n:n + 1, :]).astype(o_ref.dtype)


def _mix_groups(o_a, o_b, o_c, o_d, g_out, *, tm=512):
    m = o_a.shape[0]
    tm = min(tm, m)
    spec = pl.BlockSpec((tm, GROUP_WIDTH), lambda i: (i, 0))
    return pl.pallas_call(
        _mix_kernel,
        grid=(m // tm,),
        in_specs=[spec, spec, spec, spec, pl.BlockSpec((3, GROUP_WIDTH), lambda i: (0, 0))],
        out_specs=pl.BlockSpec((tm, MIX_WIDTH), lambda i: (i, 0)),
        out_shape=jax.ShapeDtypeStruct((m, MIX_WIDTH), MXU_DTYPE),
        compiler_params=_compiler_params(("parallel",)),
        name="mix_groups",
    )(o_a, o_b, o_c, o_d, g_out)


def kernel(x, p, positions, w_in, w_out, g_mix, g_ffn, w_gate, w_up, w_down, w_ple_proj, w_ple_gate, g_ple,
           g_group_out, diff_g_q, diff_g_k, diff_g_sub, diff_lambda, dsa_g_q, dsa_g_k, mla_g_cq, mla_g_ckv,
           mla_w_uq, mla_w_uk, mla_w_uv, mla_g_q, mla_g_k, nsa_g_q, nsa_g_k, nsa_cmp_w1, nsa_cmp_w2, nsa_cmp_pe):
    B, S = x.shape[:2]
    M = B * S
    bf16 = MXU_DTYPE
    cc_h, ss_h = _lane_tables(positions, HEAD_DIM)
    cc_i, ss_i = _lane_tables(positions, IDX_DIM)
    cc_d, ss_d = _lane_tables(positions, DIFF_QK_DIM)
    cc_m, ss_m = _lane_tables(positions, MLA_ROPE)
    w_in_p = jnp.concatenate(
        [jnp.zeros(w_in.shape[:2] + (n,), bf16) if s is None else w_in[:, :, s:s + n].astype(bf16) for s, n in _IN_RUNS],
        axis=2)
    w_out_b, w_gate_b, w_up_b, w_down_b = (w.astype(bf16) for w in (w_out, w_gate, w_up, w_down))
    w_pg_b, w_pp_b = w_ple_gate.astype(bf16), w_ple_proj.astype(bf16)
    p_b = p.reshape(DEPTH, M, PLE_DIM).astype(bf16)
    h = x.reshape(M, D_MODEL)
    for i in range(DEPTH):
        u = _rmsnorm_rows(h, g_mix[i])
        proj = _matmul(u, w_in_p, layer=i, tm=1024, tn=512, tk=D_MODEL)
        o_a = _diff_mixer(proj, B, S, cc_d, ss_d, diff_g_q[i], diff_g_k[i], diff_g_sub[i], diff_lambda[i], i)
        o_b = _dsa_mixer(proj, B, S, cc_h, ss_h, cc_i, ss_i, dsa_g_q[i], dsa_g_k[i])
        o_c = _mla_mixer(proj, B, S, cc_m, ss_m, mla_g_cq[i], mla_g_ckv[i], mla_w_uq[i], mla_w_uk[i], mla_w_uv[i],
                         mla_g_q[i], mla_g_k[i])
        o_d = _nsa_mixer(proj, B, S, cc_h, ss_h, positions, nsa_g_q[i], nsa_g_k[i],
                         nsa_cmp_w1[i], nsa_cmp_w2[i], nsa_cmp_pe[i])
        mixed = _mix_groups(o_a, o_b, o_c, o_d, g_group_out[i])
        h = _matmul(mixed, w_out_b, layer=i, res=h, tm=1024, tn=512, tk=MIX_WIDTH)
        u = _rmsnorm_rows(h, g_ffn[i])
        act = _swiglu(u, w_gate_b, w_up_b, layer=i, tm=1024, tn=256)
        h = _matmul(act, w_down_b, layer=i, res=h, tm=512, tn=256, tk=D_FF)
        u = _rmsnorm_rows(h, g_ple[i])
        h = _ple(u, w_pg_b, p_b, w_pp_b, h, layer=i, tm=1024, tn=512)
    return h.reshape(B, S, D_MODEL)
```
